```python
import math
import jax, jax.numpy as jnp
from jax import lax
import numpy as np

D_MODEL = 1024
BATCH = 8
SEQ = 2048
DEPTH = 2
DEC_BATCH = 128
DEC_SEQ = 1
PAST_LEN = 2048
PAGE_SIZE = 128

N_EVEN = (DEPTH + 1) // 2
N_ODD = DEPTH // 2
EPS = 1e-6
CONV_K = 4

ATT_HEADS = 8
ATT_HD = 64
ATT_W = ATT_HEADS * ATT_HD
ROT_DIM = ATT_HD // 4
ROPE_THETA = 500000.0
DILATION_PATTERNS = ((128, 1), (512, 4), (2048, 16))
WIN_MAX = max(w for w, _ in DILATION_PATTERNS)

SSD_HEADDIM = 64
SSD_INNER = D_MODEL
SSD_HEADS = SSD_INNER // SSD_HEADDIM
SSD_GROUPS = 2
SSD_STATE = 128
SSD_CHUNK = 128
SSD_CONV_CH = SSD_INNER + 2 * SSD_GROUPS * SSD_STATE

EVEN_SPLITS = (ATT_W, ATT_W, ATT_W, ATT_W, SSD_INNER, SSD_CONV_CH, SSD_HEADS)
EVEN_IN = sum(EVEN_SPLITS)
EVEN_MIX = ATT_W + SSD_INNER

MLSTM_INNER = 2 * D_MODEL
MLSTM_HEADS = 8
MLSTM_HD = MLSTM_INNER // MLSTM_HEADS
MLSTM_CHUNK = 128
ODD_SPLITS = (MLSTM_INNER, MLSTM_INNER, MLSTM_INNER, MLSTM_INNER, MLSTM_HEADS, MLSTM_HEADS, MLSTM_INNER)
ODD_IN = sum(ODD_SPLITS)

kernel_name = "dilated_ssd_mlstm_hybrid_step"


def _split(u, sizes):
    idx = np.cumsum(np.array(sizes))[:-1].tolist()
    return jnp.split(u, idx, axis=-1)


def rmsnorm(x, w):
    xf = x.astype(jnp.float32)
    y = xf * lax.rsqrt(jnp.mean(xf * xf, axis=-1, keepdims=True) + EPS)
    return (y * w.astype(jnp.float32)).astype(x.dtype)


def head_norm(h, w):
    mu = jnp.mean(h, axis=-1, keepdims=True)
    var = jnp.mean(jnp.square(h - mu), axis=-1, keepdims=True)
    return (h - mu) * lax.rsqrt(var + EPS) * w.astype(jnp.float32).reshape(h.shape[2], h.shape[3])


def rotary_partial(x, pos):
    half = ROT_DIM // 2
    inv = jnp.power(jnp.float32(ROPE_THETA), -jnp.arange(half, dtype=jnp.float32) * (2.0 / ROT_DIM))
    ang = pos.astype(jnp.float32)[:, None] * inv[None, :]
    cos = jnp.cos(ang)[None, :, None, :]
    sin = jnp.sin(ang)[None, :, None, :]
    xf = x.astype(jnp.float32)
    x1 = xf[..., :half]
    x2 = xf[..., half:ROT_DIM]
    out = jnp.concatenate([x1 * cos - x2 * sin, x2 * cos + x1 * sin, xf[..., ROT_DIM:]], axis=-1)
    return out.astype(x.dtype)


def causal_conv(xin, buf, w, b):
    L = xin.shape[1]
    xp = jnp.concatenate([buf.astype(xin.dtype), xin], axis=1)
    y = b + xp[:, 0:L] * w[0]
    for j in range(1, CONV_K):
        y = y + xp[:, j:j + L] * w[j]
    return jax.nn.silu(y), xp[:, xp.shape[1] - (CONV_K - 1):]


def combine_by_denominator(outs, lses):
    wts = jax.nn.softmax(jnp.stack(lses, axis=0), axis=0)
    o = jnp.stack(outs, axis=0).astype(jnp.float32)
    return jnp.sum(wts[..., None] * o, axis=0)


def dilated_attention_prompt(q, k, v):
    b, S, H, hd = q.shape
    outs, lses = [], []
    for window, dil in DILATION_PATTERNS:
        span = window // dil
        L = S // dil
        nb = -(-L // span)
        pad = nb * span - L

        def to_blocks(t):
            t = t.reshape(b, L, dil, H, hd).transpose(0, 2, 1, 3, 4)
            t = jnp.pad(t, ((0, 0), (0, 0), (0, pad), (0, 0), (0, 0)))
            return t.reshape(b, dil, nb, span, H, hd)

        def with_prev(t):
            prev = jnp.pad(t, ((0, 0), (0, 0), (1, 0), (0, 0), (0, 0), (0, 0)))[:, :, :nb]
            return jnp.concatenate([prev, t], axis=3)

        qb = to_blocks(q)
        kk = with_prev(to_blocks(k))
        vv = with_prev(to_blocks(v))
        s = jnp.einsum('bgnqhe,bgnkhe->bgnhqk', qb, kk, preferred_element_type=jnp.float32)
        qi = jnp.arange(span)[:, None]
        kj = jnp.arange(2 * span)[None, :]
        dist = span + qi - kj
        blk = jnp.arange(nb)[:, None, None]
        valid = (dist >= 0) & (dist <= span) & ((blk > 0) | (kj[None] >= span))
        s = jnp.where(valid[None, None, :, None], s, -jnp.inf)
        lse = jax.nn.logsumexp(s, axis=-1)
        p = jnp.exp(s - lse[..., None])
        o = jnp.einsum('bgnhqk,bgnkhe->bgnqhe', p.astype(v.dtype), vv)
        o = o.reshape(b, dil, nb * span, H, hd)[:, :, :L].transpose(0, 2, 1, 3, 4).reshape(b, S, H, hd)
        lse = lse.transpose(0, 1, 2, 4, 3).reshape(b, dil, nb * span, H)[:, :, :L]
        lse = lse.transpose(0, 2, 1, 3).reshape(b, S, H)
        outs.append(o)
        lses.append(lse)
    return combine_by_denominator(outs, lses)


def dilated_attention_sample(q, k, v, k_buf, v_buf):
    b, T, H, hd = q.shape
    WB = k_buf.shape[1]
    kk = jnp.concatenate([k_buf.astype(k.dtype), k], axis=1)
    vv = jnp.concatenate([v_buf.astype(v.dtype), v], axis=1)
    t = jnp.arange(T)
    outs, lses = [], []
    for window, dil in DILATION_PATTERNS:
        span = window // dil
        idx = WB + t[:, None] - dil * jnp.arange(span + 1)[None, :]
        valid = idx >= 0
        idx = jnp.maximum(idx, 0)
        kg = kk[:, idx]
        vg = vv[:, idx]
        s = jnp.einsum('bthe,btjhe->bthj', q, kg, preferred_element_type=jnp.float32)
        s = jnp.where(valid[None, :, None, :], s, -jnp.inf)
        lse = jax.nn.logsumexp(s, axis=-1)
        p = jnp.exp(s - lse[..., None])
        outs.append(jnp.einsum('bthj,btjhe->bthe', p.astype(v.dtype), vg))
        lses.append(lse)
    return combine_by_denominator(outs, lses)


def ssd_scan(x, dt, a, bmat, cmat, d_skip, init_state):
    b, L, H, P = x.shape
    G, N = bmat.shape[2], bmat.shape[3]
    R = H // G
    cl = min(SSD_CHUNK, L)
    nc = L // cl
    xf = x.astype(jnp.float32).reshape(b, nc, cl, G, R, P)
    dtc = dt.reshape(b, nc, cl, G, R)
    Bc = bmat.astype(jnp.float32).reshape(b, nc, cl, G, N)
    Cc = cmat.astype(jnp.float32).reshape(b, nc, cl, G, N)
    acum = jnp.cumsum(dtc * a.reshape(G, R), axis=2)
    causal = jnp.tril(jnp.ones((cl, cl), dtype=bool))
    seg = acum[:, :, :, None] - acum[:, :, None, :]
    decay = jnp.exp(jnp.where(causal[:, :, None, None], seg, -jnp.inf))
    cb = jnp.einsum('bcign,bcjgn->bcijg', Cc, Bc)
    y_intra = jnp.einsum('bcijgr,bcjgrp->bcigrp', cb[..., None] * decay * dtc[:, :, None], xf)
    last = acum[:, :, -1]
    w_end = jnp.exp(last[:, :, None] - acum) * dtc
    s_local = jnp.einsum('bcjgn,bcjgr,bcjgrp->bcgrpn', Bc, w_end, xf)

    def step(state, inp):
        dec, sl = inp
        return state * jnp.exp(dec)[..., None, None] + sl, state

    init = init_state.astype(jnp.float32).reshape(b, G, R, P, N)
    final, prev = lax.scan(step, init, (jnp.moveaxis(last, 1, 0), jnp.moveaxis(s_local, 1, 0)))
    prev = jnp.moveaxis(prev, 0, 1)
    y_inter = jnp.einsum('bcign,bcgrpn->bcigrp', Cc, prev) * jnp.exp(acum)[..., None]
    y = y_intra + y_inter + d_skip.reshape(G, R)[:, :, None] * xf
    return y.reshape(b, L, H, P), final.reshape(b, H, P, N)


def mlstm_chunked(q, k, v, i_t, logf, c0, n0, m0):
    b, L, H, dh = q.shape
    cl = min(MLSTM_CHUNK, L)
    nc = L // cl
    causal = jnp.tril(jnp.ones((cl, cl), dtype=bool))

    def chunks(t):
        return jnp.moveaxis(t.astype(jnp.float32).reshape((b, nc, cl) + t.shape[2:]), 1, 0)

    def step(carry, inp):
        c_prev, n_prev, m_prev = carry
        qc, kc, vc, ic, fc = inp
        bcum = jnp.cumsum(fc, axis=1).transpose(0, 2, 1)
        ich = ic.transpose(0, 2, 1)
        dmat = jnp.where(causal, bcum[:, :, :, None] - bcum[:, :, None, :] + ich[:, :, None, :], -jnp.inf)
        inter = bcum + m_prev[:, :, None]
        m_t = jnp.maximum(inter, jnp.max(dmat, axis=-1))
        w_intra = jnp.exp(dmat - m_t[..., None])
        w_inter = jnp.exp(inter - m_t)
        att = w_intra * jnp.einsum('bthd,bshd->bhts', qc, kc)
        num = jnp.einsum('bhts,bshe->bthe', att, vc) + w_inter.transpose(0, 2, 1)[..., None] * jnp.einsum('bthd,bhde->bthe', qc, c_prev)
        den = jnp.sum(att, axis=-1) + w_inter * jnp.einsum('bthd,bhd->bht', qc, n_prev)
        h = num / jnp.maximum(jnp.abs(den), jnp.exp(-m_t)).transpose(0, 2, 1)[..., None]
        b_last = bcum[:, :, -1]
        logw = b_last[..., None] - bcum + ich
        m_new = jnp.maximum(b_last + m_prev, jnp.max(logw, axis=-1))
        ws = jnp.exp(logw - m_new[..., None])
        scale = jnp.exp(b_last + m_prev - m_new)
        c_new = scale[..., None, None] * c_prev + jnp.einsum('bhs,bshd,bshe->bhde', ws, kc, vc)
        n_new = scale[..., None] * n_prev + jnp.einsum('bhs,bshd->bhd', ws, kc)
        return (c_new, n_new, m_new), h

    init = (c0.astype(jnp.float32), n0.astype(jnp.float32), m0.astype(jnp.float32))
    (c, n, m), hs = lax.scan(step, init, (chunks(q), chunks(k), chunks(v), chunks(i_t), chunks(logf)))
    return jnp.moveaxis(hs, 0, 1).reshape(b, L, H, dh), c, n, m


def even_layer(hn, pos, w_in, w_out, conv_w, conv_b, dt_bias, a_log, d_skip, ssd_norm_w, kv_buf, conv_buf, ssd_state):
    b, L, _ = hn.shape
    u = hn @ w_in
    q, k, v, g_att, z, xbc, dt_raw = _split(u, EVEN_SPLITS)
    q = rotary_partial(q.reshape(b, L, ATT_HEADS, ATT_HD), pos) * (ATT_HD ** -0.5)
    k = rotary_partial(k.reshape(b, L, ATT_HEADS, ATT_HD), pos)
    v = v.reshape(b, L, ATT_HEADS, ATT_HD)
    if kv_buf is None:
        o = dilated_attention_prompt(q, k, v)
        keep = min(WIN_MAX, L)
        new_k, new_v = k[:, L - keep:], v[:, L - keep:]
    else:
        o = dilated_attention_sample(q, k, v, kv_buf[0], kv_buf[1])
        new_k, new_v = k, v
    att = o.reshape(b, L, ATT_W).astype(hn.dtype) * jax.nn.silu(g_att)
    if conv_buf is None:
        conv_buf = jnp.zeros((b, CONV_K - 1, SSD_CONV_CH), hn.dtype)
    if ssd_state is None:
        ssd_state = jnp.zeros((b, SSD_HEADS, SSD_HEADDIM, SSD_STATE), jnp.float32)
    xbc, new_conv = causal_conv(xbc, conv_buf, conv_w, conv_b)
    xs, bm, cm = _split(xbc, (SSD_INNER, SSD_GROUPS * SSD_STATE, SSD_GROUPS * SSD_STATE))
    dt = jax.nn.softplus(dt_raw.astype(jnp.float32) + dt_bias.astype(jnp.float32))
    a = -jnp.exp(a_log.astype(jnp.float32))
    y, new_state = ssd_scan(xs.reshape(b, L, SSD_HEADS, SSD_HEADDIM), dt, a,
                            bm.reshape(b, L, SSD_GROUPS, SSD_STATE), cm.reshape(b, L, SSD_GROUPS, SSD_STATE),
                            d_skip.astype(jnp.float32), ssd_state)
    y = rmsnorm(y.reshape(b, L, SSD_INNER) * jax.nn.silu(z.astype(jnp.float32)), ssd_norm_w).astype(hn.dtype)
    mix = jnp.concatenate([att, y], axis=-1) @ w_out
    return mix, new_k, new_v, new_conv, new_state.astype(hn.dtype)


def odd_layer(hn, w_in, w_out, conv_w, conv_b, ig_b, fg_b, norm_w, conv_buf, c0, n0, m0):
    b, L, _ = hn.shape
    u = hn @ w_in
    q_pre, k_pre, v, o_pre, i_pre, f_pre, z = _split(u, ODD_SPLITS)
    if conv_buf is None:
        conv_buf = jnp.zeros((b, CONV_K - 1, 2 * MLSTM_INNER), hn.dtype)
        c0 = jnp.zeros((b, MLSTM_HEADS, MLSTM_HD, MLSTM_HD), jnp.float32)
        n0 = jnp.zeros((b, MLSTM_HEADS, MLSTM_HD), jnp.float32)
        m0 = jnp.full((b, MLSTM_HEADS), -jnp.inf, jnp.float32)
    qk, new_conv = causal_conv(jnp.concatenate([q_pre, k_pre], axis=-1), conv_buf, conv_w, conv_b)
    q, k = _split(qk, (MLSTM_INNER, MLSTM_INNER))
    q = q.reshape(b, L, MLSTM_HEADS, MLSTM_HD)
    k = k.reshape(b, L, MLSTM_HEADS, MLSTM_HD) * (MLSTM_HD ** -0.5)
    v = v.reshape(b, L, MLSTM_HEADS, MLSTM_HD)
    i_t = i_pre.astype(jnp.float32) + ig_b.astype(jnp.float32)
    logf = jax.nn.log_sigmoid(f_pre.astype(jnp.float32) + fg_b.astype(jnp.float32))
    h, c, n, m = mlstm_chunked(q, k, v, i_t, logf, c0, n0, m0)
    h = jax.nn.sigmoid(o_pre.astype(jnp.float32)).reshape(b, L, MLSTM_HEADS, MLSTM_HD) * h
    h = head_norm(h, norm_w).reshape(b, L, MLSTM_INNER).astype(hn.dtype)
    out = (h * jax.nn.silu(z)) @ w_out
    return out, new_conv, c.astype(hn.dtype), n.astype(hn.dtype), m.astype(hn.dtype)


def setup_inputs(seed: int = 0) -> dict:
    key = jax.random.key(seed)
    ks = jax.random.split(key, 32)
    f32 = jnp.float32
    wb = min(WIN_MAX, PAST_LEN)

    def nrm(k, shape, scale):
        return scale * jax.random.normal(k, shape, f32)

    dt0 = jnp.exp(jax.random.uniform(ks[14], (N_EVEN, SSD_HEADS), f32, math.log(1e-3), math.log(1e-1)))
    return {
        "x_prompt": nrm(ks[0], (BATCH, SEQ, D_MODEL), 1.0),
        "x_sample": nrm(ks[1], (DEC_BATCH, DEC_SEQ, D_MODEL), 1.0),
        "cache_attn_k": nrm(ks[2], (N_EVEN, DEC_BATCH, wb, ATT_HEADS, ATT_HD), 1.0),
        "cache_attn_v": nrm(ks[3], (N_EVEN, DEC_BATCH, wb, ATT_HEADS, ATT_HD), 1.0),
        "state_ssd_conv": nrm(ks[4], (N_EVEN, DEC_BATCH, CONV_K - 1, SSD_CONV_CH), 1.0),
        "state_ssd": nrm(ks[5], (N_EVEN, DEC_BATCH, SSD_HEADS, SSD_HEADDIM, SSD_STATE), 0.3),
        "state_mlstm_conv": nrm(ks[6], (N_ODD, DEC_BATCH, CONV_K - 1, 2 * MLSTM_INNER), 1.0),
        "state_mlstm_c": nrm(ks[7], (N_ODD, DEC_BATCH, MLSTM_HEADS, MLSTM_HD, MLSTM_HD), 0.1),
        "state_mlstm_n": nrm(ks[8], (N_ODD, DEC_BATCH, MLSTM_HEADS, MLSTM_HD), 0.3),
        "state_mlstm_m": nrm(ks[9], (N_ODD, DEC_BATCH, MLSTM_HEADS), 1.0),
        "norm_w": 1.0 + nrm(ks[10], (DEPTH, D_MODEL), 0.02),
        "final_norm_w": 1.0 + nrm(ks[11], (D_MODEL,), 0.02),
        "w_in_even": nrm(ks[12], (N_EVEN, D_MODEL, EVEN_IN), D_MODEL ** -0.5),
        "w_out_even": nrm(ks[13], (N_EVEN, EVEN_MIX, D_MODEL), EVEN_MIX ** -0.5),
        "ssd_conv_w": nrm(ks[15], (N_EVEN, CONV_K, SSD_CONV_CH), CONV_K ** -0.5),
        "ssd_conv_b": nrm(ks[16], (N_EVEN, SSD_CONV_CH), 0.02),
        "ssd_dt_bias": dt0 + jnp.log(-jnp.expm1(-dt0)),
        "ssd_a_log": jnp.log(jax.random.uniform(ks[17], (N_EVEN, SSD_HEADS), f32, 1.0, 16.0)),
        "ssd_d": 1.0 + nrm(ks[18], (N_EVEN, SSD_HEADS), 0.1),
        "ssd_norm_w": 1.0 + nrm(ks[19], (N_EVEN, SSD_INNER), 0.02),
        "w_in_odd": nrm(ks[20], (N_ODD, D_MODEL, ODD_IN), D_MODEL ** -0.5),
        "w_out_odd": nrm(ks[21], (N_ODD, MLSTM_INNER, D_MODEL), MLSTM_INNER ** -0.5),
        "mlstm_conv_w": nrm(ks[22], (N_ODD, CONV_K, 2 * MLSTM_INNER), CONV_K ** -0.5),
        "mlstm_conv_b": nrm(ks[23], (N_ODD, 2 * MLSTM_INNER), 0.02),
        "mlstm_igate_b": nrm(ks[24], (N_ODD, MLSTM_HEADS), 0.1),
        "mlstm_fgate_b": jnp.linspace(3.0, 6.0, MLSTM_HEADS, dtype=f32)[None, :] + nrm(ks[25], (N_ODD, MLSTM_HEADS), 0.1),
        "mlstm_norm_w": 1.0 + nrm(ks[26], (N_ODD, MLSTM_INNER), 0.02),
    }


def reference(x_prompt, x_sample, cache_attn_k, cache_attn_v, state_ssd_conv, state_ssd,
              state_mlstm_conv, state_mlstm_c, state_mlstm_n, state_mlstm_m,
              norm_w, final_norm_w, w_in_even, w_out_even, ssd_conv_w, ssd_conv_b,
              ssd_dt_bias, ssd_a_log, ssd_d, ssd_norm_w, w_in_odd, w_out_odd,
              mlstm_conv_w, mlstm_conv_b, mlstm_igate_b, mlstm_fgate_b, mlstm_norm_w):
    pos_p = jnp.arange(x_prompt.shape[1])
    pos_s = PAST_LEN + jnp.arange(x_sample.shape[1])
    hp, hs = x_prompt, x_sample
    ak_p, av_p, sc_p, ss_p, mc_p, mC_p, mn_p, mm_p = [], [], [], [], [], [], [], []
    ak_s, av_s, sc_s, ss_s, mc_s, mC_s, mn_s, mm_s = [], [], [], [], [], [], [], []
    for layer in range(DEPTH):
        j = layer // 2
        np_ = rmsnorm(hp, norm_w[layer])
        ns_ = rmsnorm(hs, norm_w[layer])
        if layer % 2 == 0:
            wts = (w_in_even[j], w_out_even[j], ssd_conv_w[j], ssd_conv_b[j], ssd_dt_bias[j],
                   ssd_a_log[j], ssd_d[j], ssd_norm_w[j])
            mp, kp, vp, cp, sp = even_layer(np_, pos_p, *wts, None, None, None)
            ms, ks_, vs, cs, sst = even_layer(ns_, pos_s, *wts, (cache_attn_k[j], cache_attn_v[j]),
                                              state_ssd_conv[j], state_ssd[j])
            ak_p.append(kp); av_p.append(vp); sc_p.append(cp); ss_p.append(sp)
            ak_s.append(ks_); av_s.append(vs); sc_s.append(cs); ss_s.append(sst)
        else:
            wts = (w_in_odd[j], w_out_odd[j], mlstm_conv_w[j], mlstm_conv_b[j], mlstm_igate_b[j],
                   mlstm_fgate_b[j], mlstm_norm_w[j])
            mp, cp, Cp, Np, Mp = odd_layer(np_, *wts, None, None, None, None)
            ms, cs, Cs, Ns, Ms = odd_layer(ns_, *wts, state_mlstm_conv[j], state_mlstm_c[j],
                                           state_mlstm_n[j], state_mlstm_m[j])
            mc_p.append(cp); mC_p.append(Cp); mn_p.append(Np); mm_p.append(Mp)
            mc_s.append(cs); mC_s.append(Cs); mn_s.append(Ns); mm_s.append(Ms)
        hp = hp + mp
        hs = hs + ms
    y_prompt = rmsnorm(hp, final_norm_w)
    y_sample = rmsnorm(hs, final_norm_w)
    return (y_prompt, y_sample,
            jnp.stack(ak_p), jnp.stack(av_p), jnp.stack(sc_p), jnp.stack(ss_p),
            jnp.stack(mc_p), jnp.stack(mC_p), jnp.stack(mn_p), jnp.stack(mm_p),
            jnp.stack(ak_s), jnp.stack(av_s), jnp.stack(sc_s), jnp.stack(ss_s),
            jnp.stack(mc_s), jnp.stack(mC_s), jnp.stack(mn_s), jnp.stack(mm_s))
```

```python
import functools

import jax
import jax.numpy as jnp
from jax import lax
from jax.experimental import pallas as pl
from jax.experimental.pallas import tpu as pltpu

F32 = jnp.float32
BF16 = jnp.bfloat16
NEG_INF = float("-inf")
EPS = 1e-6
LANE = 128
CHUNK = 128
VMEM_LIMIT = 56 * 1024 * 1024

CONV_K = 4
ATT_HEADS = 8
ATT_HD = 64
ATT_W = ATT_HEADS * ATT_HD
ROT_DIM = ATT_HD // 4
ROPE_THETA = 500000.0
DILATIONS = (1, 4, 16)
PAST_LEN = 2048
SSD_HEADDIM = 64
SSD_GROUPS = 2
SSD_STATE = 128
MLSTM_HEADS = 8


def _params(*sem):
    return pltpu.CompilerParams(dimension_semantics=sem, vmem_limit_bytes=VMEM_LIMIT)


def _chunks(n, w):
    out, c = [], 0
    while c < n:
        out.append((c, min(w, n - c)))
        c += w
    return out


def _dot(a, b):
    return jnp.dot(a.astype(BF16), b.astype(BF16), preferred_element_type=F32)


def _dot_nt(a, b):
    return lax.dot_general(a.astype(BF16), b.astype(BF16), (((1,), (1,)), ((), ())),
                           preferred_element_type=F32)


def _dot_tn(a, b):
    return lax.dot_general(a.astype(BF16), b.astype(BF16), (((0,), (0,)), ((), ())),
                           preferred_element_type=F32)


def _split3(x):
    hi = x.astype(BF16)
    r1 = x - hi.astype(F32)
    mid = r1.astype(BF16)
    lo = (r1 - mid.astype(F32)).astype(BF16)
    return hi, mid, lo


def _sel_dot(sel, x):
    hi, mid, lo = _split3(x)
    d = lambda p: jnp.dot(sel, p, preferred_element_type=F32)
    return d(hi) + d(mid) + d(lo)


def _dot_sel(x, sel):
    hi, mid, lo = _split3(x)
    d = lambda p: jnp.dot(p, sel, preferred_element_type=F32)
    return d(hi) + d(mid) + d(lo)


def _tril(n):
    r = lax.broadcasted_iota(jnp.int32, (n, n), 0)
    c = lax.broadcasted_iota(jnp.int32, (n, n), 1)
    return r >= c


def _seg_matrix(rows, cols, seg, along_rows):
    r = lax.broadcasted_iota(jnp.int32, (rows, cols), 0)
    c = lax.broadcasted_iota(jnp.int32, (rows, cols), 1)
    m = (r // seg == c) if along_rows else (c // seg == r)
    return m.astype(BF16)


def _row_at(ref, b, cols=slice(None)):
    base = pl.multiple_of((b // 8) * 8, 8)
    tile = ref[pl.ds(base, 8), cols]
    sub = lax.broadcasted_iota(jnp.int32, tile.shape, 0)
    return jnp.sum(jnp.where(sub == b % 8, tile, 0.0), axis=0, keepdims=True)


def _silu(x):
    return x * (1.0 / (1.0 + jnp.exp(-x)))


def _sigmoid(x):
    return 1.0 / (1.0 + jnp.exp(-x))


def _softplus(x):
    return jnp.maximum(x, 0.0) + jnp.log1p(jnp.exp(-jnp.abs(x)))


def _rms(x, w):
    return x * lax.rsqrt(jnp.mean(x * x, axis=-1, keepdims=True) + EPS) * w


def _norm_matmul_body(x_ref, nw_ref, w_ref, o_ref, *, chunks):
    xn = _rms(x_ref[...], nw_ref[...]).astype(BF16)
    for c0, cw in chunks:
        o_ref[:, c0:c0 + cw] = jnp.dot(xn, w_ref[:, c0:c0 + cw], preferred_element_type=F32)


def _norm_matmul(x, nw, w, *, tm, panels):
    m, d = x.shape
    n = w.shape[1]
    pn = n // panels
    return pl.pallas_call(
        functools.partial(_norm_matmul_body, chunks=_chunks(pn, 512)),
        grid=(panels, m // tm),
        in_specs=[pl.BlockSpec((tm, d), lambda p, i: (i, 0)),
                  pl.BlockSpec((1, d), lambda p, i: (0, 0)),
                  pl.BlockSpec((d, pn), lambda p, i: (0, p))],
        out_specs=pl.BlockSpec((tm, pn), lambda p, i: (i, p)),
        out_shape=jax.ShapeDtypeStruct((m, n), F32),
        compiler_params=_params("arbitrary", "arbitrary"),
        name="norm_matmul")(x, nw, w)


def _out_proj_body(*refs, n_in, final):
    h_ref = refs[0]
    xs = refs[1:1 + n_in]
    ws = refs[1 + n_in:1 + 2 * n_in]
    rest = refs[1 + 2 * n_in:]
    acc = h_ref[...]
    for x_ref, w_ref in zip(xs, ws):
        acc = acc + jnp.dot(x_ref[...].astype(BF16), w_ref[...], preferred_element_type=F32)
    if final:
        fw_ref, o_ref = rest
        o_ref[...] = _rms(acc, fw_ref[...])
    else:
        (o_ref,) = rest
        o_ref[...] = acc


def _out_proj(h, xs, ws, fw=None, *, tm):
    m, d = h.shape
    n_in = len(xs)
    in_specs = [pl.BlockSpec((tm, d), lambda i: (i, 0))]
    in_specs += [pl.BlockSpec((tm, x.shape[1]), lambda i: (i, 0)) for x in xs]
    in_specs += [pl.BlockSpec(w.shape, lambda i: (0, 0)) for w in ws]
    args = [h, *xs, *ws]
    if fw is not None:
        in_specs.append(pl.BlockSpec((1, d), lambda i: (0, 0)))
        args.append(fw)
    return pl.pallas_call(
        functools.partial(_out_proj_body, n_in=n_in, final=fw is not None),
        grid=(m // tm,),
        in_specs=in_specs,
        out_specs=pl.BlockSpec((tm, d), lambda i: (i, 0)),
        out_shape=jax.ShapeDtypeStruct((m, d), F32),
        compiler_params=_params("arbitrary"),
        name="out_proj")(*args)


def _rope_tables(pos, width):
    half = ROT_DIM // 2
    inv = jnp.power(F32(ROPE_THETA), -jnp.arange(half, dtype=F32) * (2.0 / ROT_DIM))
    ang = pos.astype(F32)[:, None] * inv[None, :]
    cos, sin = jnp.cos(ang), jnp.sin(ang)
    n = pos.shape[0]
    one = jnp.ones((n, ATT_HD - ROT_DIM), F32)
    z8 = jnp.zeros((n, half), F32)
    z48 = jnp.zeros((n, ATT_HD - ROT_DIM), F32)
    c = jnp.concatenate([cos, cos, one], axis=-1)
    sa = jnp.concatenate([-sin, z8, z48], axis=-1)
    sb = jnp.concatenate([z8, sin, z48], axis=-1)
    rep = width // ATT_HD
    return tuple(jnp.tile(t, (1, rep)) for t in (c, sa, sb))


def _rotary(x, c, sa, sb):
    w = x.shape[-1]
    half = ROT_DIM // 2
    return x * c + pltpu.roll(x, w - half, 1) * sa + pltpu.roll(x, half, 1) * sb


def _attn_prompt_body(q_ref, k_ref, v_ref, g_ref, c_ref, sa_ref, sb_ref,
                      att_ref, ko_ref, vo_ref,
                      qd, kd, vd, od, mld, o_s, ml_s, *, seq):
    nblk = seq // CHUNK
    c, sa, sb = c_ref[...], sa_ref[...], sb_ref[...]
    q = _rotary(q_ref[...], c, sa, sb) * (ATT_HD ** -0.5)
    k = _rotary(k_ref[...], c, sa, sb)
    ko_ref[...] = k
    vo_ref[...] = v_ref[...]
    zero = jnp.zeros((CHUNK, LANE), F32)
    for p, d in enumerate(DILATIONS):
        kd[p, 0:CHUNK, :] = zero
        vd[p, 0:CHUNK, :] = zero
    mld[...] = jnp.zeros(mld.shape, F32)
    qd[0] = q
    kd[0, CHUNK:, :] = k
    vd[0, CHUNK:, :] = v_ref[...]
    for p, d in enumerate(DILATIONS):
        if d == 1:
            continue
        ln = seq // d
        for r in range(d):
            qd[p, r * ln:(r + 1) * ln, :] = qd[0, pl.ds(r, ln, stride=d), :]
            kd[p, CHUNK + r * ln:CHUNK + (r + 1) * ln, :] = kd[0, pl.ds(CHUNK + r, ln, stride=d), :]
            vd[p, CHUNK + r * ln:CHUNK + (r + 1) * ln, :] = vd[0, pl.ds(CHUNK + r, ln, stride=d), :]

    row = lax.broadcasted_iota(jnp.int32, (CHUNK, 2 * CHUNK), 0)
    col = lax.broadcasted_iota(jnp.int32, (CHUNK, 2 * CHUNK), 1)
    band = (col >= row) & (col <= row + CHUNK)

    for p, d in enumerate(DILATIONS):
        nb = nblk // d

        def block(t, carry, p=p, nb=nb):
            base = pl.multiple_of(t * CHUNK, CHUNK)
            first = (t % nb) == 0
            valid = band & (col >= jnp.where(first, CHUNK, 0))
            for hh in range(2):
                lo = ATT_HD * hh
                qb = qd[p, pl.ds(base, CHUNK), lo:lo + ATT_HD]
                kw = kd[p, pl.ds(base, 2 * CHUNK), lo:lo + ATT_HD]
                vw = vd[p, pl.ds(base, 2 * CHUNK), lo:lo + ATT_HD]
                s = jnp.where(valid, _dot_nt(qb, kw), NEG_INF)
                m = jnp.max(s, axis=-1, keepdims=True)
                e = jnp.exp(s - m)
                od[pl.ds(base, CHUNK), lo:lo + ATT_HD] = _dot(e, vw)
                mld[pl.ds(base, CHUNK), hh:hh + 1] = m
                mld[pl.ds(base, CHUNK), 2 + hh:3 + hh] = jnp.sum(e, axis=-1, keepdims=True)
            return carry

        lax.fori_loop(0, nblk, block, 0)
        if d == 1:
            o_s[p] = od[...]
            ml_s[p] = mld[...]
        else:
            ln = seq // d
            for r in range(d):
                o_s[p, pl.ds(r, ln, stride=d), :] = od[r * ln:(r + 1) * ln, :]
                ml_s[p, pl.ds(r, ln, stride=d), :] = mld[r * ln:(r + 1) * ln, :]

    np_ = len(DILATIONS)

    def combine(t, carry):
        base = pl.multiple_of(t * CHUNK, CHUNK)
        rows = pl.ds(base, CHUNK)
        for hh in range(2):
            lo = ATT_HD * hh
            ms = [ml_s[p, rows, hh:hh + 1] for p in range(np_)]
            ls = [ml_s[p, rows, 2 + hh:3 + hh] for p in range(np_)]
            mx = functools.reduce(jnp.maximum, ms)
            ws = [jnp.exp(mm - mx) for mm in ms]
            num = sum(w * o_s[p, rows, lo:lo + ATT_HD] for p, w in enumerate(ws))
            den = sum(w * l for w, l in zip(ws, ls))
            att_ref[rows, lo:lo + ATT_HD] = (num / den) * _silu(g_ref[rows, lo:lo + ATT_HD])
        return carry

    lax.fori_loop(0, nblk, combine, 0)


def _attn_prompt(u, cos, sa, sb, *, batch, seq):
    m = batch * seq
    npair = ATT_W // LANE
    blk = lambda off: pl.BlockSpec((seq, LANE), lambda b, hp, off=off: (b, off + hp))
    tab = pl.BlockSpec((seq, LANE), lambda b, hp: (0, 0))
    out = pl.BlockSpec((seq, LANE), lambda b, hp: (b, hp))
    np_ = len(DILATIONS)
    return pl.pallas_call(
        functools.partial(_attn_prompt_body, seq=seq),
        grid=(batch, npair),
        in_specs=[blk(0), blk(npair), blk(2 * npair), blk(3 * npair), tab, tab, tab],
        out_specs=[out, out, out],
        out_shape=[jax.ShapeDtypeStruct((m, ATT_W), F32)] * 3,
        scratch_shapes=[pltpu.VMEM((np_, seq, LANE), F32),
                        pltpu.VMEM((np_, seq + CHUNK, LANE), F32),
                        pltpu.VMEM((np_, seq + CHUNK, LANE), F32),
                        pltpu.VMEM((seq, LANE), F32),
                        pltpu.VMEM((seq, LANE), F32),
                        pltpu.VMEM((np_, seq, LANE), F32),
                        pltpu.VMEM((np_, seq, LANE), F32)],
        compiler_params=_params("arbitrary", "arbitrary"),
        name="attn_prompt")(u, u, u, u, cos, sa, sb)


def _ssd_prompt_body(xbc_ref, z_ref, dt_ref, cw_ref, cb_ref, dtb_ref, alog_ref, dsk_ref, nw_ref,
                     y_ref, conv_ref, st_ref, ext, st, ys, *, heads, nchunk):
    c = pl.program_id(1)
    inner = heads * SSD_HEADDIM
    gw = SSD_STATE
    hpg = heads // SSD_GROUPS

    @pl.when(c == 0)
    def _():
        ext[0:8, :] = jnp.zeros((8, ext.shape[1]), F32)
        st[...] = jnp.zeros(st.shape, F32)

    ext[8:8 + CHUNK, :] = xbc_ref[...]
    conv = cb_ref[...] + ext[5:5 + CHUNK, :] * cw_ref[0:1, :]
    for j in range(1, CONV_K):
        conv = conv + ext[5 + j:5 + j + CHUNK, :] * cw_ref[j:j + 1, :]
    act = _silu(conv)

    @pl.when(c == nchunk - 1)
    def _():
        conv_ref[...] = ext[CHUNK + 5:CHUNK + 8, :]

    ext[0:8, :] = ext[CHUNK:CHUNK + 8, :]

    dt = _softplus(dt_ref[...] + dtb_ref[...])
    a = -jnp.exp(alog_ref[...])
    tril = _tril(CHUNK)
    acum = _sel_dot(tril.astype(BF16), dt * a)
    acum_t = acum.T
    dt_t = dt.T
    eacum = jnp.exp(acum)
    last = acum[CHUNK - 1:CHUNK, :]
    wend = jnp.exp(last - acum) * dt
    elast = jnp.exp(last)

    for g in range(SSD_GROUPS):
        bm = act[:, inner + g * gw:inner + (g + 1) * gw]
        cm = act[:, inner + SSD_GROUPS * gw + g * gw:inner + SSD_GROUPS * gw + (g + 1) * gw]
        cb = _dot_nt(cm, bm)
        for hg in range(hpg):
            h = g * hpg + hg
            lo = h * SSD_HEADDIM
            xh = act[:, lo:lo + SSD_HEADDIM]
            seg = acum[:, h:h + 1] - acum_t[h:h + 1, :]
            decay = jnp.exp(jnp.where(tril, seg, NEG_INF))
            sh = st[lo:lo + SSD_HEADDIM, :]
            y = _dot(cb * decay * dt_t[h:h + 1, :], xh)
            y = y + _dot_nt(cm, sh) * eacum[:, h:h + 1]
            ys[:, lo:lo + SSD_HEADDIM] = y
            st[lo:lo + SSD_HEADDIM, :] = sh * elast[:, h:h + 1] + _dot_tn(xh, bm * wend[:, h:h + 1])

    xs = act[:, 0:inner]
    yt = (ys[...] + dsk_ref[...] * xs) * _silu(z_ref[...])
    y_ref[...] = _rms(yt, nw_ref[...])

    @pl.when(c == nchunk - 1)
    def _():
        st_ref[...] = st[...]


def _ssd_prompt(u, cw, cb, dtb, alog, dsk, nw, *, batch, seq, heads, col_z, col_xbc, col_dt):
    nchunk = seq // CHUNK
    inner = heads * SSD_HEADDIM
    cch = cw.shape[1]
    row = lambda b, c: b * nchunk + c
    full = lambda a: pl.BlockSpec(a.shape, lambda b, c: (0, 0))
    return pl.pallas_call(
        functools.partial(_ssd_prompt_body, heads=heads, nchunk=nchunk),
        grid=(batch, nchunk),
        in_specs=[pl.BlockSpec((CHUNK, cch), lambda b, c: (row(b, c), col_xbc // cch)),
                  pl.BlockSpec((CHUNK, inner), lambda b, c: (row(b, c), col_z // inner)),
                  pl.BlockSpec((CHUNK, LANE), lambda b, c: (row(b, c), col_dt // LANE)),
                  full(cw), full(cb), full(dtb), full(alog), full(dsk), full(nw)],
        out_specs=[pl.BlockSpec((CHUNK, inner), lambda b, c: (row(b, c), 0)),
                   pl.BlockSpec((None, CONV_K - 1, cch), lambda b, c: (b, 0, 0)),
                   pl.BlockSpec((None, inner, SSD_STATE), lambda b, c: (b, 0, 0))],
        out_shape=[jax.ShapeDtypeStruct((batch * seq, inner), F32),
                   jax.ShapeDtypeStruct((batch, CONV_K - 1, cch), F32),
                   jax.ShapeDtypeStruct((batch, inner, SSD_STATE), F32)],
        scratch_shapes=[pltpu.VMEM((CHUNK + 8, cch), F32),
                        pltpu.VMEM((inner, SSD_STATE), F32),
                        pltpu.VMEM((CHUNK, inner), F32)],
        compiler_params=_params("arbitrary", "arbitrary"),
        name="ssd_prompt")(u, u, u, cw, cb, dtb, alog, dsk, nw)


def _mlstm_prompt_body(q_ref, k_ref, v_ref, o_ref, z_ref, gi_ref, gf_ref,
                       cwq_ref, cwk_ref, cbq_ref, cbk_ref, bi_ref, bf_ref, nw_ref,
                       hz_ref, convq_ref, convk_ref, c_out, n_out, m_out,
                       extq, extk, c_s, n_s, m_s, tr_s, *, nchunk, hd):
    h = pl.program_id(1)
    c = pl.program_id(2)

    @pl.when(c == 0)
    def _():
        extq[0:8, :] = jnp.zeros((8, hd), F32)
        extk[0:8, :] = jnp.zeros((8, hd), F32)
        c_s[...] = jnp.zeros(c_s.shape, F32)
        n_s[...] = jnp.zeros(n_s.shape, F32)
        m_s[...] = jnp.full(m_s.shape, NEG_INF, F32)

    def conv(ext, x_ref, cw_ref, cb_ref, out_ref):
        ext[8:8 + CHUNK, :] = x_ref[...]
        acc = cb_ref[...] + ext[5:5 + CHUNK, :] * cw_ref[0:1, :]
        for j in range(1, CONV_K):
            acc = acc + ext[5 + j:5 + j + CHUNK, :] * cw_ref[j:j + 1, :]

        @pl.when(c == nchunk - 1)
        def _():
            out_ref[...] = ext[CHUNK + 5:CHUNK + 8, :]

        ext[0:8, :] = ext[CHUNK:CHUNK + 8, :]
        return _silu(acc)

    q = conv(extq, q_ref, cwq_ref, cbq_ref, convq_ref)
    k = conv(extk, k_ref, cwk_ref, cbk_ref, convk_ref) * (hd ** -0.5)
    v = v_ref[...]

    it = gi_ref[...] + bi_ref[...]
    logf = -_softplus(-(gf_ref[...] + bf_ref[...]))
    tril = _tril(CHUNK)
    bc = _sel_dot(tril.astype(BF16), logf)
    lane = lax.broadcasted_iota(jnp.int32, (CHUNK, LANE), 1)
    pick = lambda t: jnp.sum(jnp.where(lane == h, t, 0.0), axis=-1, keepdims=True)
    i_col = pick(it)
    b_col = pick(bc)
    tr_s[0] = it.T
    tr_s[1] = bc.T
    i_row = _row_at(tr_s.at[0], h)
    b_row = _row_at(tr_s.at[1], h)

    m_prev = m_s[...]
    dmat = jnp.where(tril, b_col - b_row + i_row, NEG_INF)
    inter = b_col + m_prev
    m_t = jnp.maximum(inter, jnp.max(dmat, axis=-1, keepdims=True))
    w_intra = jnp.exp(dmat - m_t)
    w_inter = jnp.exp(inter - m_t)
    att = w_intra * _dot_nt(q, k)
    c_prev = c_s[...]
    n_prev = n_s[...]
    num = _dot(att, v) + w_inter * _dot(q, c_prev)
    den = jnp.sum(att, axis=-1, keepdims=True) + w_inter * jnp.sum(q * n_prev, axis=-1, keepdims=True)
    hh = num / jnp.maximum(jnp.abs(den), jnp.exp(-m_t))

    b_last = b_col[CHUNK - 1:CHUNK, :]
    logw = b_last - b_col + i_col
    m_new = jnp.maximum(b_last + m_prev, jnp.max(logw, axis=0, keepdims=True))
    ws = jnp.exp(logw - m_new)
    scale = jnp.exp(b_last + m_prev - m_new)
    c_new = scale * c_prev + _dot_tn(k, ws * v)
    n_new = scale * n_prev + jnp.sum(ws * k, axis=0, keepdims=True)
    c_s[...] = c_new
    n_s[...] = n_new
    m_s[...] = m_new

    hg = _sigmoid(o_ref[...]) * hh
    mu = jnp.mean(hg, axis=-1, keepdims=True)
    var = jnp.mean(jnp.square(hg - mu), axis=-1, keepdims=True)
    hn = (hg - mu) * lax.rsqrt(var + EPS) * nw_ref[...]
    hz_ref[...] = hn * _silu(z_ref[...])

    @pl.when(c == nchunk - 1)
    def _():
        c_out[...] = c_new
        n_out[...] = n_new
        m_out[...] = m_new


def _mlstm_prompt(u, cw, cb, bi, bf, nw, *, batch, seq, heads, hd, col_gates):
    nchunk = seq // CHUNK
    inner = heads * hd
    row = lambda b, h, c: b * nchunk + c
    ublk = lambda off: pl.BlockSpec((CHUNK, hd), lambda b, h, c, off=off: (row(b, h, c), off + h))
    gblk = lambda off: pl.BlockSpec((CHUNK, LANE), lambda b, h, c, off=off: (row(b, h, c), col_gates // LANE + off))
    wblk = lambda rows, off: pl.BlockSpec((rows, hd), lambda b, h, c, off=off: (0, off + h))
    small = lambda a: pl.BlockSpec(a.shape, lambda b, h, c: (0, 0))
    return pl.pallas_call(
        functools.partial(_mlstm_prompt_body, nchunk=nchunk, hd=hd),
        grid=(batch, heads, nchunk),
        in_specs=[ublk(0), ublk(heads), ublk(2 * heads), ublk(3 * heads), ublk(4 * heads),
                  gblk(0), gblk(1),
                  wblk(CONV_K, 0), wblk(CONV_K, heads), wblk(1, 0), wblk(1, heads),
                  small(bi), small(bf), wblk(1, 0)],
        out_specs=[pl.BlockSpec((CHUNK, hd), lambda b, h, c: (row(b, h, c), h)),
                   pl.BlockSpec((None, CONV_K - 1, hd), lambda b, h, c: (b, 0, h)),
                   pl.BlockSpec((None, CONV_K - 1, hd), lambda b, h, c: (b, 0, h)),
                   pl.BlockSpec((None, None, hd, hd), lambda b, h, c: (b, h, 0, 0)),
                   pl.BlockSpec((None, None, 1, hd), lambda b, h, c: (b, h, 0, 0)),
                   pl.BlockSpec((None, None, 1, 1), lambda b, h, c: (b, h, 0, 0))],
        out_shape=[jax.ShapeDtypeStruct((batch * seq, inner), F32),
                   jax.ShapeDtypeStruct((batch, CONV_K - 1, inner), F32),
                   jax.ShapeDtypeStruct((batch, CONV_K - 1, inner), F32),
                   jax.ShapeDtypeStruct((batch, heads, hd, hd), F32),
                   jax.ShapeDtypeStruct((batch, heads, 1, hd), F32),
                   jax.ShapeDtypeStruct((batch, heads, 1, 1), F32)],
        scratch_shapes=[pltpu.VMEM((CHUNK + 8, hd), F32),
                        pltpu.VMEM((CHUNK + 8, hd), F32),
                        pltpu.VMEM((hd, hd), F32),
                        pltpu.VMEM((1, hd), F32),
                        pltpu.VMEM((1, 1), F32),
                        pltpu.VMEM((2, CHUNK, LANE), F32)],
        compiler_params=_params("arbitrary", "arbitrary", "arbitrary"),
        name="mlstm_prompt")(u, u, u, u, u, u, u, cw, cw, cb, cb, bi, bf, nw)


def _attn_sample_body(q_ref, k_ref, v_ref, g_ref, c_ref, sa_ref, sb_ref,
                      k1_ref, k4_ref, k16_ref, v1_ref, v4_ref, v16_ref,
                      att_ref, ko_ref, vo_ref, *, bb):
    c, sa, sb = c_ref[...], sa_ref[...], sb_ref[...]
    q = _rotary(q_ref[...], c, sa, sb) * (ATT_HD ** -0.5)
    k = _rotary(k_ref[...], c, sa, sb)
    v = v_ref[...]
    ko_ref[...] = k
    vo_ref[...] = v
    seg = _seg_matrix(ATT_W, LANE, ATT_HD, True)
    seg_t = _seg_matrix(LANE, ATT_W, ATT_HD, False)
    s_self = _dot_sel(q * k, seg)
    npat = len(DILATIONS)
    outs = []
    for b in range(bb):
        qb = q[b:b + 1, :]
        ss = [_dot_sel(kr[b] * qb, seg) for kr in (k1_ref, k4_ref, k16_ref)]
        sb_self = s_self[b:b + 1, :]
        mx = sb_self
        for s in ss:
            mx = jnp.maximum(mx, jnp.max(s, axis=0, keepdims=True))
        p_self = jnp.exp(sb_self - mx)
        den = npat * p_self
        acc = npat * _dot_sel(p_self, seg_t) * v[b:b + 1, :]
        for s, vr in zip(ss, (v1_ref, v4_ref, v16_ref)):
            e = jnp.exp(s - mx)
            den = den + jnp.sum(e, axis=0, keepdims=True)
            acc = acc + jnp.sum(_dot_sel(e, seg_t) * vr[b], axis=0, keepdims=True)
        outs.append(acc / _dot_sel(den, seg_t))
    att = jnp.concatenate(outs, axis=0)
    att_ref[...] = att * _silu(g_ref[...])


def _attn_sample(u, cos, sa, sb, ck, cv, *, bb):
    bs = u.shape[0]
    wb = ck.shape[1]
    ublk = lambda off: pl.BlockSpec((bb, ATT_W), lambda i, off=off: (i, off))
    tab = pl.BlockSpec((1, ATT_W), lambda i: (0, 0))
    views, specs = [], []
    for cache in (ck, cv):
        for d in DILATIONS:
            views.append(cache.reshape(bs, wb // d, d * ATT_W))
            nb = wb // d // CHUNK
            specs.append(pl.BlockSpec((bb, CHUNK, ATT_W), lambda i, nb=nb: (i, nb - 1, 0)))
    out = pl.BlockSpec((bb, ATT_W), lambda i: (i, 0))
    return pl.pallas_call(
        functools.partial(_attn_sample_body, bb=bb),
        grid=(bs // bb,),
        in_specs=[ublk(0), ublk(1), ublk(2), ublk(3), tab, tab, tab] + specs,
        out_specs=[out, out, out],
        out_shape=[jax.ShapeDtypeStruct((bs, ATT_W), F32)] * 3,
        compiler_params=_params("arbitrary"),
        name="attn_sample")(u, u, u, u, cos, sa, sb, *views)


def _ssd_sample_prep_body(xbc_ref, dt_ref, cst_ref, cw_ref, cb_ref, dtb_ref, alog_ref, aloge_ref, dsk_ref,
                          conv_ref, yloc_ref, xdtt_ref, bc_ref, da_ref, dae_ref, *, heads):
    inner = heads * SSD_HEADDIM
    cch = xbc_ref.shape[1]
    gw = SSD_STATE
    x = xbc_ref[...]
    acc = cb_ref[...] + x * cw_ref[CONV_K - 1:CONV_K, :]
    for j in range(CONV_K - 1):
        acc = acc + cst_ref[:, j * cch:(j + 1) * cch] * cw_ref[j:j + 1, :]
    for j in range(CONV_K - 2):
        conv_ref[:, j * cch:(j + 1) * cch] = cst_ref[:, (j + 1) * cch:(j + 2) * cch]
    conv_ref[:, (CONV_K - 2) * cch:(CONV_K - 1) * cch] = x
    act = _silu(acc)
    xs = act[:, 0:inner]
    bc_ref[...] = act[:, inner:]
    dt = _softplus(dt_ref[...] + dtb_ref[...])
    da_ref[...] = jnp.exp(dt * (-jnp.exp(alog_ref[...])))
    expand = _seg_matrix(LANE, inner, SSD_HEADDIM, False)
    dte = _dot_sel(dt, expand)
    dae_ref[...] = jnp.exp(dte * (-jnp.exp(aloge_ref[...])))
    xdt = xs * dte
    xdtt_ref[...] = xdt.T
    hw = inner // SSD_GROUPS
    parts = []
    for g in range(SSD_GROUPS):
        bm = act[:, inner + g * gw:inner + (g + 1) * gw]
        cm = act[:, inner + SSD_GROUPS * gw + g * gw:inner + SSD_GROUPS * gw + (g + 1) * gw]
        cbg = jnp.sum(cm * bm, axis=-1, keepdims=True)
        parts.append(cbg * xdt[:, g * hw:(g + 1) * hw])
    yloc_ref[...] = jnp.concatenate(parts, axis=-1) + dsk_ref[...] * xs


def _ssd_sample_state_body(da_ref, s_ref, xdtt_ref, bc_ref, so_ref, yi_ref, *, bb, heads):
    i = pl.program_id(0)
    bs = bc_ref.shape[0]
    inner = heads * SSD_HEADDIM
    hw = inner // SSD_GROUPS
    hpg = heads // SSD_GROUPS
    gw = SSD_STATE
    rid = lax.broadcasted_iota(jnp.int32, (bs, gw), 0)
    for t in range(bb):
        b = i * bb + t
        parts = []
        for g in range(SSD_GROUPS):
            mg = jnp.where(rid == b, bc_ref[:, g * gw:(g + 1) * gw], 0.0)
            sl = _dot(xdtt_ref[g * hw:(g + 1) * hw, :], mg)
            crow = _row_at(bc_ref, b, slice(SSD_GROUPS * gw + g * gw, SSD_GROUPS * gw + (g + 1) * gw))
            sg = s_ref[t, g * hw:(g + 1) * hw, :]
            parts.append(_dot_nt(jnp.broadcast_to(crow, (8, gw)), sg)[0:1, :])
            for hg in range(hpg):
                h = g * hpg + hg
                lo = hg * SSD_HEADDIM
                so_ref[t, h * SSD_HEADDIM:(h + 1) * SSD_HEADDIM, :] = (
                    sg[lo:lo + SSD_HEADDIM, :] * da_ref[b, h] + sl[lo:lo + SSD_HEADDIM, :])
        yi_ref[t] = jnp.concatenate(parts, axis=-1)


def _ssd_sample_finish_body(yloc_ref, yi_ref, dae_ref, z_ref, nw_ref, y_ref):
    y = (yloc_ref[...] + yi_ref[...] * dae_ref[...]) * _silu(z_ref[...])
    y_ref[...] = _rms(y, nw_ref[...])


def _ssd_sample(u, cst, state, cw, cb, dtb, alog, aloge, dsk, nw, *, heads, col_z, col_xbc, col_dt, bb):
    bs = u.shape[0]
    inner = heads * SSD_HEADDIM
    cch = cw.shape[1]
    ncs = (CONV_K - 1) * cch
    bcw = 2 * SSD_GROUPS * SSD_STATE
    full = lambda a: pl.BlockSpec(a.shape, lambda i: (0,) * a.ndim)
    conv, yloc, xdtt, bc, da, dae = pl.pallas_call(
        functools.partial(_ssd_sample_prep_body, heads=heads),
        grid=(1,),
        in_specs=[pl.BlockSpec((bs, cch), lambda i: (0, col_xbc // cch)),
                  pl.BlockSpec((bs, LANE), lambda i: (0, col_dt // LANE)),
                  full(cst), full(cw), full(cb), full(dtb), full(alog), full(aloge), full(dsk)],
        out_specs=[pl.BlockSpec((bs, ncs), lambda i: (0, 0)),
                   pl.BlockSpec((bs, inner), lambda i: (0, 0)),
                   pl.BlockSpec((inner, bs), lambda i: (0, 0)),
                   pl.BlockSpec((bs, bcw), lambda i: (0, 0)),
                   pl.BlockSpec((bs, LANE), lambda i: (0, 0)),
                   pl.BlockSpec((bs, inner), lambda i: (0, 0))],
        out_shape=[jax.ShapeDtypeStruct((bs, ncs), F32),
                   jax.ShapeDtypeStruct((bs, inner), F32),
                   jax.ShapeDtypeStruct((inner, bs), F32),
                   jax.ShapeDtypeStruct((bs, bcw), F32),
                   jax.ShapeDtypeStruct((bs, LANE), F32),
                   jax.ShapeDtypeStruct((bs, inner), F32)],
        compiler_params=_params("arbitrary"),
        name="ssd_sample_prep")(u, u, cst, cw, cb, dtb, alog, aloge, dsk)
    new_state, yi = pl.pallas_call(
        functools.partial(_ssd_sample_state_body, bb=bb, heads=heads),
        grid=(bs // bb,),
        in_specs=[pl.BlockSpec(memory_space=pltpu.SMEM),
                  pl.BlockSpec((bb, inner, SSD_STATE), lambda i: (i, 0, 0)),
                  full(xdtt), full(bc)],
        out_specs=[pl.BlockSpec((bb, inner, SSD_STATE), lambda i: (i, 0, 0)),
                   pl.BlockSpec((bb, 1, inner), lambda i: (i, 0, 0))],
        out_shape=[jax.ShapeDtypeStruct((bs, inner, SSD_STATE), F32),
                   jax.ShapeDtypeStruct((bs, 1, inner), F32)],
        compiler_params=_params("arbitrary"),
        name="ssd_sample_state")(da[:, :heads], state, xdtt, bc)
    y = pl.pallas_call(
        _ssd_sample_finish_body,
        grid=(1,),
        in_specs=[full(yloc), pl.BlockSpec((bs, inner), lambda i: (0, 0)), full(dae),
                  pl.BlockSpec((bs, inner), lambda i: (0, col_z // inner)), full(nw)],
        out_specs=pl.BlockSpec((bs, inner), lambda i: (0, 0)),
        out_shape=jax.ShapeDtypeStruct((bs, inner), F32),
        compiler_params=_params("arbitrary"),
        name="ssd_sample_finish")(yloc, yi.reshape(bs, inner), dae, u, nw)
    return y, conv, new_state


def _mlstm_sample_prep_body(q_ref, k_ref, gi_ref, gf_ref, cst_ref, cw_ref, cb_ref, bi_ref, bf_ref, m_ref,
                            conv_ref, qk_ref, kwt_ref, wi_ref, wf_ref, mt_ref, *, heads, hd):
    inner = heads * hd
    cch = 2 * inner
    x = jnp.concatenate([q_ref[...], k_ref[...]], axis=-1)
    acc = cb_ref[...] + x * cw_ref[CONV_K - 1:CONV_K, :]
    for j in range(CONV_K - 1):
        acc = acc + cst_ref[:, j * cch:(j + 1) * cch] * cw_ref[j:j + 1, :]
    for j in range(CONV_K - 2):
        conv_ref[:, j * cch:(j + 1) * cch] = cst_ref[:, (j + 1) * cch:(j + 2) * cch]
    conv_ref[:, (CONV_K - 2) * cch:(CONV_K - 1) * cch] = x
    act = _silu(acc)
    q = act[:, 0:inner]
    k = act[:, inner:] * (hd ** -0.5)
    it = gi_ref[...] + bi_ref[...]
    inter = -_softplus(-(gf_ref[...] + bf_ref[...])) + m_ref[...]
    mt = jnp.maximum(inter, it)
    wi = jnp.exp(it - mt)
    wf = jnp.exp(inter - mt)
    wi_ref[...] = wi
    wf_ref[...] = wf
    mt_ref[...] = mt
    qk_ref[:, 0:inner] = q
    qk_ref[:, inner:] = k
    kw = jnp.concatenate([k[:, h * hd:(h + 1) * hd] * wi[:, h:h + 1] for h in range(heads)], axis=-1)
    kwt_ref[...] = kw.T


def _mlstm_sample_state_body(wf_ref, c_ref, qk_ref, v_ref, kwt_ref, co_ref, qc_ref, *, heads, hd):
    b = pl.program_id(0)
    bs = v_ref.shape[0]
    rid = lax.broadcasted_iota(jnp.int32, (bs, hd), 0)
    parts = []
    for h in range(heads):
        cp = c_ref[0, h]
        qrow = _row_at(qk_ref, b, slice(h * hd, (h + 1) * hd))
        parts.append(_dot(jnp.broadcast_to(qrow, (8, hd)), cp)[0:1, :])
        mh = jnp.where(rid == b, v_ref[:, h * hd:(h + 1) * hd], 0.0)
        co_ref[0, h] = cp * wf_ref[b, h] + _dot(kwt_ref[h * hd:(h + 1) * hd, :], mh)
    qc_ref[0] = jnp.concatenate(parts, axis=-1)


def _mlstm_sample_finish_body(qk_ref, v_ref, o_ref, z_ref, qc_ref, n_ref, wi_ref, wf_ref, mt_ref, nw_ref,
                              hz_ref, no_ref, *, heads, hd):
    inner = heads * hd
    for h in range(heads):
        sl = slice(h * hd, (h + 1) * hd)
        q = qk_ref[:, sl]
        k = qk_ref[:, inner + h * hd:inner + (h + 1) * hd]
        wi = wi_ref[:, h:h + 1]
        wf = wf_ref[:, h:h + 1]
        mt = mt_ref[:, h:h + 1]
        n_prev = n_ref[:, sl]
        att = wi * jnp.sum(q * k, axis=-1, keepdims=True)
        num = att * v_ref[:, sl] + wf * qc_ref[:, sl]
        den = att + wf * jnp.sum(q * n_prev, axis=-1, keepdims=True)
        hh = num / jnp.maximum(jnp.abs(den), jnp.exp(-mt))
        hg = _sigmoid(o_ref[:, sl]) * hh
        mu = jnp.mean(hg, axis=-1, keepdims=True)
        var = jnp.mean(jnp.square(hg - mu), axis=-1, keepdims=True)
        hn = (hg - mu) * lax.rsqrt(var + EPS) * nw_ref[:, sl]
        hz_ref[:, sl] = hn * _silu(z_ref[:, sl])
        no_ref[:, sl] = wf * n_prev + wi * k


def _mlstm_sample(u, cst, c0, n0, m0p, cw, cb, bi, bf, nw, *, heads, hd, col_gates):
    bs = u.shape[0]
    inner = heads * hd
    ncs = (CONV_K - 1) * 2 * inner
    full = lambda a: pl.BlockSpec(a.shape, lambda i: (0,) * a.ndim)
    ucol = lambda j: pl.BlockSpec((bs, inner), lambda i, j=j: (0, j))
    gcol = lambda j: pl.BlockSpec((bs, LANE), lambda i, j=j: (0, col_gates // LANE + j))
    tile = jax.ShapeDtypeStruct((bs, LANE), F32)
    conv, qk, kwt, wi, wf, mt = pl.pallas_call(
        functools.partial(_mlstm_sample_prep_body, heads=heads, hd=hd),
        grid=(1,),
        in_specs=[ucol(0), ucol(1), gcol(0), gcol(1), full(cst), full(cw), full(cb), full(bi), full(bf),
                  full(m0p)],
        out_specs=[pl.BlockSpec((bs, ncs), lambda i: (0, 0)),
                   pl.BlockSpec((bs, 2 * inner), lambda i: (0, 0)),
                   pl.BlockSpec((inner, bs), lambda i: (0, 0)),
                   pl.BlockSpec((bs, LANE), lambda i: (0, 0)),
                   pl.BlockSpec((bs, LANE), lambda i: (0, 0)),
                   pl.BlockSpec((bs, LANE), lambda i: (0, 0))],
        out_shape=[jax.ShapeDtypeStruct((bs, ncs), F32),
                   jax.ShapeDtypeStruct((bs, 2 * inner), F32),
                   jax.ShapeDtypeStruct((inner, bs), F32),
                   tile, tile, tile],
        compiler_params=_params("arbitrary"),
        name="mlstm_sample_prep")(u, u, u, u, cst, cw, cb, bi, bf, m0p)
    c_new, qc = pl.pallas_call(
        functools.partial(_mlstm_sample_state_body, heads=heads, hd=hd),
        grid=(bs,),
        in_specs=[pl.BlockSpec(memory_space=pltpu.SMEM),
                  pl.BlockSpec((1, heads, hd, hd), lambda i: (i, 0, 0, 0)),
                  full(qk), pl.BlockSpec((bs, inner), lambda i: (0, 2)), full(kwt)],
        out_specs=[pl.BlockSpec((1, heads, hd, hd), lambda i: (i, 0, 0, 0)),
                   pl.BlockSpec((1, 1, inner), lambda i: (i, 0, 0))],
        out_shape=[jax.ShapeDtypeStruct((bs, heads, hd, hd), F32),
                   jax.ShapeDtypeStruct((bs, 1, inner), F32)],
        compiler_params=_params("arbitrary"),
        name="mlstm_sample_state")(wf[:, :heads], c0, qk, u, kwt)
    hz, n_new = pl.pallas_call(
        functools.partial(_mlstm_sample_finish_body, heads=heads, hd=hd),
        grid=(1,),
        in_specs=[full(qk), ucol(2), ucol(3), ucol(4), pl.BlockSpec((bs, inner), lambda i: (0, 0)),
                  full(n0), full(wi), full(wf), full(mt), full(nw)],
        out_specs=[pl.BlockSpec((bs, inner), lambda i: (0, 0)),
                   pl.BlockSpec((bs, inner), lambda i: (0, 0))],
        out_shape=[jax.ShapeDtypeStruct((bs, inner), F32)] * 2,
        compiler_params=_params("arbitrary"),
        name="mlstm_sample_finish")(qk, u, u, u, qc.reshape(bs, inner), n0, wi, wf, mt, nw)
    return hz, conv, c_new, n_new, mt[:, :heads]


def _pad_cols(w, n):
    return jnp.pad(w, ((0, 0), (0, n - w.shape[1])))


def _row(v, n=None):
    v = v.reshape(1, -1)
    return v if n is None else _pad_cols(v, n)


def kernel(x_prompt, x_sample, cache_attn_k, cache_attn_v, state_ssd_conv, state_ssd, state_mlstm_conv, state_mlstm_c, state_mlstm_n, state_mlstm_m, norm_w, final_norm_w, w_in_even, w_out_even, ssd_conv_w, ssd_conv_b, ssd_dt_bias, ssd_a_log, ssd_d, ssd_norm_w, w_in_odd, w_out_odd, mlstm_conv_w, mlstm_conv_b, mlstm_igate_b, mlstm_fgate_b, mlstm_norm_w):
    batch, seq, d_model = x_prompt.shape
    bs = x_sample.shape[0]
    ssd_heads = ssd_a_log.shape[1]
    ssd_inner = ssd_heads * SSD_HEADDIM
    ssd_cch = ssd_conv_w.shape[2]
    m_inner = mlstm_norm_w.shape[1]
    m_hd = m_inner // MLSTM_HEADS
    mp = batch * seq

    col_z = 4 * ATT_W
    col_xbc = col_z + ssd_inner
    col_dt = col_xbc + ssd_cch
    n_even = col_dt + LANE
    w_in0 = _pad_cols(w_in_even[0], n_even).astype(BF16)
    w_out0 = w_out_even[0].astype(BF16)
    nw0 = _row(norm_w[0])
    cw0, cb0 = ssd_conv_w[0], _row(ssd_conv_b[0])
    dtb = _row(ssd_dt_bias[0], LANE)
    alog = _row(ssd_a_log[0], LANE)
    aloge = _row(jnp.repeat(ssd_a_log[0], SSD_HEADDIM))
    dsk = _row(jnp.repeat(ssd_d[0], SSD_HEADDIM))
    snw = _row(ssd_norm_w[0])

    hp = x_prompt.reshape(mp, d_model)
    hs = x_sample.reshape(bs, d_model)

    up = _norm_matmul(hp, nw0, w_in0, tm=256, panels=1)
    cos_p, sa_p, sb_p = _rope_tables(jnp.arange(seq), LANE)
    att_p, k_p, v_p = _attn_prompt(up, cos_p, sa_p, sb_p, batch=batch, seq=seq)
    y_p, conv_p, st_p = _ssd_prompt(up, cw0, cb0, dtb, alog, dsk, snw, batch=batch, seq=seq,
                                    heads=ssd_heads, col_z=col_z, col_xbc=col_xbc, col_dt=col_dt)
    hp = _out_proj(hp, [att_p, y_p], [w_out0[:ATT_W], w_out0[ATT_W:]], tm=512)

    us = _norm_matmul(hs, nw0, w_in0, tm=bs, panels=1)
    cos_s, sa_s, sb_s = _rope_tables(PAST_LEN + jnp.arange(1), ATT_W)
    wb = cache_attn_k.shape[2]
    att_s, k_s, v_s = _attn_sample(us, cos_s, sa_s, sb_s,
                                   cache_attn_k[0].reshape(bs, wb, ATT_W),
                                   cache_attn_v[0].reshape(bs, wb, ATT_W), bb=8)
    y_s, conv_s, st_s = _ssd_sample(us, state_ssd_conv[0].reshape(bs, -1),
                                    state_ssd[0].reshape(bs, ssd_inner, SSD_STATE),
                                    cw0, cb0, dtb, alog, aloge, dsk, snw, heads=ssd_heads,
                                    col_z=col_z, col_xbc=col_xbc, col_dt=col_dt, bb=4)
    hs = _out_proj(hs, [att_s, y_s], [w_out0[:ATT_W], w_out0[ATT_W:]], tm=bs)

    wo = w_in_odd[0]
    gates_at = 4 * m_inner
    zcol = gates_at + 2 * MLSTM_HEADS
    col_gates = 5 * m_inner
    zpad = jnp.zeros((d_model, LANE - MLSTM_HEADS), wo.dtype)
    w_in1 = jnp.concatenate([wo[:, :gates_at], wo[:, zcol:],
                             wo[:, gates_at:gates_at + MLSTM_HEADS], zpad,
                             wo[:, gates_at + MLSTM_HEADS:zcol], zpad], axis=1).astype(BF16)
    w_out1 = w_out_odd[0].astype(BF16)
    nw1 = _row(norm_w[1])
    cw1, cb1 = mlstm_conv_w[0], _row(mlstm_conv_b[0])
    bi = _row(mlstm_igate_b[0], LANE)
    bf = _row(mlstm_fgate_b[0], LANE)
    mnw = _row(mlstm_norm_w[0])
    fnw = _row(final_norm_w)

    up1 = _norm_matmul(hp, nw1, w_in1, tm=256, panels=2)
    hz_p, cq_p, ck_p, c_p, n_p, m_p = _mlstm_prompt(up1, cw1, cb1, bi, bf, mnw, batch=batch, seq=seq,
                                                   heads=MLSTM_HEADS, hd=m_hd, col_gates=col_gates)
    y_prompt = _out_proj(hp, [hz_p], [w_out1], fnw, tm=512)

    us1 = _norm_matmul(hs, nw1, w_in1, tm=bs, panels=2)
    m0p = _pad_cols(state_mlstm_m[0], LANE)
    hz_s, mconv_s, c_s, n_s, m_s = _mlstm_sample(us1, state_mlstm_conv[0].reshape(bs, -1),
                                                 state_mlstm_c[0], state_mlstm_n[0].reshape(bs, m_inner),
                                                 m0p, cw1, cb1, bi, bf, mnw,
                                                 heads=MLSTM_HEADS, hd=m_hd, col_gates=col_gates)
    y_sample = _out_proj(hs, [hz_s], [w_out1], fnw, tm=bs)

    tmax = min(seq, CHUNK * max(DILATIONS))
    kv_shape = (1, batch, seq, ATT_HEADS, ATT_HD)
    return (
        y_prompt.reshape(batch, seq, d_model),
        y_sample.reshape(bs, 1, d_model),
        k_p.reshape(kv_shape)[:, :, seq - tmax:],
        v_p.reshape(kv_shape)[:, :, seq - tmax:],
        conv_p[None],
        st_p.reshape(1, batch, ssd_heads, SSD_HEADDIM, SSD_STATE),
        jnp.concatenate([cq_p, ck_p], axis=-1)[None],
        c_p[None],
        n_p.reshape(1, batch, MLSTM_HEADS, m_hd),
        m_p.reshape(1, batch, MLSTM_HEADS),
        k_s.reshape(1, bs, 1, ATT_HEADS, ATT_HD),
        v_s.reshape(1, bs, 1, ATT_HEADS, ATT_HD),
        conv_s.reshape(1, bs, CONV_K - 1, ssd_cch),
        st_s.reshape(1, bs, ssd_heads, SSD_HEADDIM, SSD_STATE),
        mconv_s.reshape(1, bs, CONV_K - 1, 2 * m_inner),
        c_s[None],
        n_s.reshape(1, bs, MLSTM_HEADS, m_hd),
        m_s.reshape(1, bs, MLSTM_HEADS),
    )
```

```python
import functools

import jax
import jax.numpy as jnp
from jax import lax
from jax.experimental import pallas as pl
from jax.experimental.pallas import tpu as pltpu

F32 = jnp.float32
BF16 = jnp.bfloat16
NEG_INF = float("-inf")
EPS = 1e-6
LANE = 128
CHUNK = 128
VMEM_LIMIT = 56 * 1024 * 1024

CONV_K = 4
ATT_HEADS = 8
ATT_HD = 64
ATT_W = ATT_HEADS * ATT_HD
ROT_DIM = ATT_HD // 4
ROPE_THETA = 500000.0
DILATIONS = (1, 4, 16)
PAST_LEN = 2048
SSD_HEADDIM = 64
SSD_GROUPS = 2
SSD_STATE = 128
MLSTM_HEADS = 8


def _params(*sem):
    return pltpu.CompilerParams(dimension_semantics=sem, vmem_limit_bytes=VMEM_LIMIT)


def _chunks(n, w):
    out, c = [], 0
    while c < n:
        out.append((c, min(w, n - c)))
        c += w
    return out


def _dot(a, b):
    return jnp.dot(a.astype(BF16), b.astype(BF16), preferred_element_type=F32)


def _dot_nt(a, b):
    return lax.dot_general(a.astype(BF16), b.astype(BF16), (((1,), (1,)), ((), ())),
                           preferred_element_type=F32)


def _dot_tn(a, b):
    return lax.dot_general(a.astype(BF16), b.astype(BF16), (((0,), (0,)), ((), ())),
                           preferred_element_type=F32)


def _split3(x):
    hi = x.astype(BF16)
    r1 = x - hi.astype(F32)
    mid = r1.astype(BF16)
    lo = (r1 - mid.astype(F32)).astype(BF16)
    return hi, mid, lo


def _sel_dot(sel, x):
    hi, mid, lo = _split3(x)
    d = lambda p: jnp.dot(sel, p, preferred_element_type=F32)
    return d(hi) + d(mid) + d(lo)


def _dot_sel(x, sel):
    hi, mid, lo = _split3(x)
    d = lambda p: jnp.dot(p, sel, preferred_element_type=F32)
    return d(hi) + d(mid) + d(lo)


def _tril(n):
    r = lax.broadcasted_iota(jnp.int32, (n, n), 0)
    c = lax.broadcasted_iota(jnp.int32, (n, n), 1)
    return r >= c


def _seg_matrix(rows, cols, seg, along_rows):
    r = lax.broadcasted_iota(jnp.int32, (rows, cols), 0)
    c = lax.broadcasted_iota(jnp.int32, (rows, cols), 1)
    m = (r // seg == c) if along_rows else (c // seg == r)
    return m.astype(BF16)


def _row_at(ref, b, cols=slice(None)):
    base = pl.multiple_of((b // 8) * 8, 8)
    tile = ref[pl.ds(base, 8), cols]
    sub = lax.broadcasted_iota(jnp.int32, tile.shape, 0)
    return jnp.sum(jnp.where(sub == b % 8, tile, 0.0), axis=0, keepdims=True)


def _silu(x):
    return x * (1.0 / (1.0 + jnp.exp(-x)))


def _sigmoid(x):
    return 1.0 / (1.0 + jnp.exp(-x))


def _softplus(x):
    return jnp.maximum(x, 0.0) + jnp.log1p(jnp.exp(-jnp.abs(x)))


def _rms(x, w):
    return x * lax.rsqrt(jnp.mean(x * x, axis=-1, keepdims=True) + EPS) * w


def _norm_matmul_body(x_ref, nw_ref, w_ref, o_ref, *, chunks):
    xn = _rms(x_ref[...], nw_ref[...]).astype(BF16)
    for c0, cw in chunks:
        o_ref[:, c0:c0 + cw] = jnp.dot(xn, w_ref[:, c0:c0 + cw], preferred_element_type=F32)


def _norm_matmul(x, nw, w, *, tm, panels):
    m, d = x.shape
    n = w.shape[1]
    pn = n // panels
    return pl.pallas_call(
        functools.partial(_norm_matmul_body, chunks=_chunks(pn, 512)),
        grid=(panels, m // tm),
        in_specs=[pl.BlockSpec((tm, d), lambda p, i: (i, 0)),
                  pl.BlockSpec((1, d), lambda p, i: (0, 0)),
                  pl.BlockSpec((d, pn), lambda p, i: (0, p))],
        out_specs=pl.BlockSpec((tm, pn), lambda p, i: (i, p)),
        out_shape=jax.ShapeDtypeStruct((m, n), F32),
        compiler_params=_params("arbitrary", "arbitrary"),
        name="norm_matmul")(x, nw, w)


def _out_proj_body(*refs, n_in, final):
    h_ref = refs[0]
    xs = refs[1:1 + n_in]
    ws = refs[1 + n_in:1 + 2 * n_in]
    rest = refs[1 + 2 * n_in:]
    acc = h_ref[...]
    for x_ref, w_ref in zip(xs, ws):
        acc = acc + jnp.dot(x_ref[...].astype(BF16), w_ref[...], preferred_element_type=F32)
    if final:
        fw_ref, o_ref = rest
        o_ref[...] = _rms(acc, fw_ref[...])
    else:
        (o_ref,) = rest
        o_ref[...] = acc


def _out_proj(h, xs, ws, fw=None, *, tm):
    m, d = h.shape
    n_in = len(xs)
    in_specs = [pl.BlockSpec((tm, d), lambda i: (i, 0))]
    in_specs += [pl.BlockSpec((tm, x.shape[1]), lambda i: (i, 0)) for x in xs]
    in_specs += [pl.BlockSpec(w.shape, lambda i: (0, 0)) for w in ws]
    args = [h, *xs, *ws]
    if fw is not None:
        in_specs.append(pl.BlockSpec((1, d), lambda i: (0, 0)))
        args.append(fw)
    return pl.pallas_call(
        functools.partial(_out_proj_body, n_in=n_in, final=fw is not None),
        grid=(m // tm,),
        in_specs=in_specs,
        out_specs=pl.BlockSpec((tm, d), lambda i: (i, 0)),
        out_shape=jax.ShapeDtypeStruct((m, d), F32),
        compiler_params=_params("arbitrary"),
        name="out_proj")(*args)


def _rope_tables(pos, width):
    half = ROT_DIM // 2
    inv = jnp.power(F32(ROPE_THETA), -jnp.arange(half, dtype=F32) * (2.0 / ROT_DIM))
    ang = pos.astype(F32)[:, None] * inv[None, :]
    cos, sin = jnp.cos(ang), jnp.sin(ang)
    n = pos.shape[0]
    one = jnp.ones((n, ATT_HD - ROT_DIM), F32)
    z8 = jnp.zeros((n, half), F32)
    z48 = jnp.zeros((n, ATT_HD - ROT_DIM), F32)
    c = jnp.concatenate([cos, cos, one], axis=-1)
    sa = jnp.concatenate([-sin, z8, z48], axis=-1)
    sb = jnp.concatenate([z8, sin, z48], axis=-1)
    rep = width // ATT_HD
    return tuple(jnp.tile(t, (1, rep)) for t in (c, sa, sb))


def _rotary(x, c, sa, sb):
    w = x.shape[-1]
    half = ROT_DIM // 2
    return x * c + pltpu.roll(x, w - half, 1) * sa + pltpu.roll(x, half, 1) * sb


def _attn_prompt_body(q_ref, k_ref, v_ref, g_ref, c_ref, sa_ref, sb_ref,
                      att_ref, ko_ref, vo_ref,
                      qd, kd, vd, od, mld, o_s, ml_s, *, seq):
    nblk = seq // CHUNK
    c, sa, sb = c_ref[...], sa_ref[...], sb_ref[...]
    q = _rotary(q_ref[...], c, sa, sb) * (ATT_HD ** -0.5)
    k = _rotary(k_ref[...], c, sa, sb)
    ko_ref[...] = k
    vo_ref[...] = v_ref[...]
    zero = jnp.zeros((CHUNK, LANE), F32)
    for p, d in enumerate(DILATIONS):
        kd[p, 0:CHUNK, :] = zero
        vd[p, 0:CHUNK, :] = zero
    mld[...] = jnp.zeros(mld.shape, F32)
    qd[0] = q
    kd[0, CHUNK:, :] = k
    vd[0, CHUNK:, :] = v_ref[...]
    for p, d in enumerate(DILATIONS):
        if d == 1:
            continue
        ln = seq // d
        for r in range(d):
            qd[p, r * ln:(r + 1) * ln, :] = qd[0, pl.ds(r, ln, stride=d), :]
            kd[p, CHUNK + r * ln:CHUNK + (r + 1) * ln, :] = kd[0, pl.ds(CHUNK + r, ln, stride=d), :]
            vd[p, CHUNK + r * ln:CHUNK + (r + 1) * ln, :] = vd[0, pl.ds(CHUNK + r, ln, stride=d), :]

    row = lax.broadcasted_iota(jnp.int32, (CHUNK, 2 * CHUNK), 0)
    col = lax.broadcasted_iota(jnp.int32, (CHUNK, 2 * CHUNK), 1)
    band = (col >= row) & (col <= row + CHUNK)

    for p, d in enumerate(DILATIONS):
        nb = nblk // d

        def block(t, carry, p=p, nb=nb):
            base = pl.multiple_of(t * CHUNK, CHUNK)
            first = (t % nb) == 0
            valid = band & (col >= jnp.where(first, CHUNK, 0))
            for hh in range(2):
                lo = ATT_HD * hh
                qb = qd[p, pl.ds(base, CHUNK), lo:lo + ATT_HD]
                kw = kd[p, pl.ds(base, 2 * CHUNK), lo:lo + ATT_HD]
                vw = vd[p, pl.ds(base, 2 * CHUNK), lo:lo + ATT_HD]
                s = jnp.where(valid, _dot_nt(qb, kw), NEG_INF)
                m = jnp.max(s, axis=-1, keepdims=True)
                e = jnp.exp(s - m)
                od[pl.ds(base, CHUNK), lo:lo + ATT_HD] = _dot(e, vw)
                mld[pl.ds(base, CHUNK), hh:hh + 1] = m
                mld[pl.ds(base, CHUNK), 2 + hh:3 + hh] = jnp.sum(e, axis=-1, keepdims=True)
            return carry

        lax.fori_loop(0, nblk, block, 0)
        if d == 1:
            o_s[p] = od[...]
            ml_s[p] = mld[...]
        else:
            ln = seq // d
            for r in range(d):
                o_s[p, pl.ds(r, ln, stride=d), :] = od[r * ln:(r + 1) * ln, :]
                ml_s[p, pl.ds(r, ln, stride=d), :] = mld[r * ln:(r + 1) * ln, :]

    np_ = len(DILATIONS)

    def combine(t, carry):
        base = pl.multiple_of(t * CHUNK, CHUNK)
        rows = pl.ds(base, CHUNK)
        for hh in range(2):
            lo = ATT_HD * hh
            ms = [ml_s[p, rows, hh:hh + 1] for p in range(np_)]
            ls = [ml_s[p, rows, 2 + hh:3 + hh] for p in range(np_)]
            mx = functools.reduce(jnp.maximum, ms)
            ws = [jnp.exp(mm - mx) for mm in ms]
            num = sum(w * o_s[p, rows, lo:lo + ATT_HD] for p, w in enumerate(ws))
            den = sum(w * l for w, l in zip(ws, ls))
            att_ref[rows, lo:lo + ATT_HD] = (num / den) * _silu(g_ref[rows, lo:lo + ATT_HD])
        return carry

    lax.fori_loop(0, nblk, combine, 0)


def _attn_prompt(u, cos, sa, sb, *, batch, seq):
    m = batch * seq
    npair = ATT_W // LANE
    blk = lambda off: pl.BlockSpec((seq, LANE), lambda b, hp, off=off: (b, off + hp))
    tab = pl.BlockSpec((seq, LANE), lambda b, hp: (0, 0))
    out = pl.BlockSpec((seq, LANE), lambda b, hp: (b, hp))
    np_ = len(DILATIONS)
    return pl.pallas_call(
        functools.partial(_attn_prompt_body, seq=seq),
        grid=(batch, npair),
        in_specs=[blk(0), blk(npair), blk(2 * npair), blk(3 * npair), tab, tab, tab],
        out_specs=[out, out, out],
        out_shape=[jax.ShapeDtypeStruct((m, ATT_W), F32)] * 3,
        scratch_shapes=[pltpu.VMEM((np_, seq, LANE), F32),
                        pltpu.VMEM((np_, seq + CHUNK, LANE), F32),
                        pltpu.VMEM((np_, seq + CHUNK, LANE), F32),
                        pltpu.VMEM((seq, LANE), F32),
                        pltpu.VMEM((seq, LANE), F32),
                        pltpu.VMEM((np_, seq, LANE), F32),
                        pltpu.VMEM((np_, seq, LANE), F32)],
        compiler_params=_params("arbitrary", "arbitrary"),
        name="attn_prompt")(u, u, u, u, cos, sa, sb)


def _ssd_prompt_body(xbc_ref, z_ref, dt_ref, cw_ref, cb_ref, dtb_ref, alog_ref, dsk_ref, nw_ref,
                     y_ref, conv_ref, st_ref, ext, st, ys, *, heads, nchunk):
    c = pl.program_id(1)
    inner = heads * SSD_HEADDIM
    gw = SSD_STATE
    hpg = heads // SSD_GROUPS

    @pl.when(c == 0)
    def _():
        ext[0:8, :] = jnp.zeros((8, ext.shape[1]), F32)
        st[...] = jnp.zeros(st.shape, F32)

    ext[8:8 + CHUNK, :] = xbc_ref[...]
    conv = cb_ref[...] + ext[5:5 + CHUNK, :] * cw_ref[0:1, :]
    for j in range(1, CONV_K):
        conv = conv + ext[5 + j:5 + j + CHUNK, :] * cw_ref[j:j + 1, :]
    act = _silu(conv)

    @pl.when(c == nchunk - 1)
    def _():
        conv_ref[...] = ext[CHUNK + 5:CHUNK + 8, :]

    ext[0:8, :] = ext[CHUNK:CHUNK + 8, :]

    dt = _softplus(dt_ref[...] + dtb_ref[...])
    a = -jnp.exp(alog_ref[...])
    tril = _tril(CHUNK)
    acum = _sel_dot(tril.astype(BF16), dt * a)
    acum_t = acum.T
    dt_t = dt.T
    eacum = jnp.exp(acum)
    last = acum[CHUNK - 1:CHUNK, :]
    wend = jnp.exp(last - acum) * dt
    elast = jnp.exp(last)

    for g in range(SSD_GROUPS):
        bm = act[:, inner + g * gw:inner + (g + 1) * gw]
        cm = act[:, inner + SSD_GROUPS * gw + g * gw:inner + SSD_GROUPS * gw + (g + 1) * gw]
        cb = _dot_nt(cm, bm)
        for hg in range(hpg):
            h = g * hpg + hg
            lo = h * SSD_HEADDIM
            xh = act[:, lo:lo + SSD_HEADDIM]
            seg = acum[:, h:h + 1] - acum_t[h:h + 1, :]
            decay = jnp.exp(jnp.where(tril, seg, NEG_INF))
            sh = st[lo:lo + SSD_HEADDIM, :]
            y = _dot(cb * decay * dt_t[h:h + 1, :], xh)
            y = y + _dot_nt(cm, sh) * eacum[:, h:h + 1]
            ys[:, lo:lo + SSD_HEADDIM] = y
            st[lo:lo + SSD_HEADDIM, :] = sh * elast[:, h:h + 1] + _dot_tn(xh, bm * wend[:, h:h + 1])

    xs = act[:, 0:inner]
    yt = (ys[...] + dsk_ref[...] * xs) * _silu(z_ref[...])
    y_ref[...] = _rms(yt, nw_ref[...])

    @pl.when(c == nchunk - 1)
    def _():
        st_ref[...] = st[...]


def _ssd_prompt(u, cw, cb, dtb, alog, dsk, nw, *, batch, seq, heads, col_z, col_xbc, col_dt):
    nchunk = seq // CHUNK
    inner = heads * SSD_HEADDIM
    cch = cw.shape[1]
    row = lambda b, c: b * nchunk + c
    full = lambda a: pl.BlockSpec(a.shape, lambda b, c: (0, 0))
    return pl.pallas_call(
        functools.partial(_ssd_prompt_body, heads=heads, nchunk=nchunk),
        grid=(batch, nchunk),
        in_specs=[pl.BlockSpec((CHUNK, cch), lambda b, c: (row(b, c), col_xbc // cch)),
                  pl.BlockSpec((CHUNK, inner), lambda b, c: (row(b, c), col_z // inner)),
                  pl.BlockSpec((CHUNK, LANE), lambda b, c: (row(b, c), col_dt // LANE)),
                  full(cw), full(cb), full(dtb), full(alog), full(dsk), full(nw)],
        out_specs=[pl.BlockSpec((CHUNK, inner), lambda b, c: (row(b, c), 0)),
                   pl.BlockSpec((None, CONV_K - 1, cch), lambda b, c: (b, 0, 0)),
                   pl.BlockSpec((None, inner, SSD_STATE), lambda b, c: (b, 0, 0))],
        out_shape=[jax.ShapeDtypeStruct((batch * seq, inner), F32),
                   jax.ShapeDtypeStruct((batch, CONV_K - 1, cch), F32),
                   jax.ShapeDtypeStruct((batch, inner, SSD_STATE), F32)],
        scratch_shapes=[pltpu.VMEM((CHUNK + 8, cch), F32),
                        pltpu.VMEM((inner, SSD_STATE), F32),
                        pltpu.VMEM((CHUNK, inner), F32)],
        compiler_params=_params("arbitrary", "arbitrary"),
        name="ssd_prompt")(u, u, u, cw, cb, dtb, alog, dsk, nw)


def _mlstm_prompt_body(q_ref, k_ref, v_ref, o_ref, z_ref, gi_ref, gf_ref,
                       cwq_ref, cwk_ref, cbq_ref, cbk_ref, bi_ref, bf_ref, nw_ref,
                       hz_ref, convq_ref, convk_ref, c_out, n_out, m_out,
                       extq, extk, c_s, n_s, m_s, tr_s, *, nchunk, hd):
    h = pl.program_id(1)
    c = pl.program_id(2)

    @pl.when(c == 0)
    def _():
        extq[0:8, :] = jnp.zeros((8, hd), F32)
        extk[0:8, :] = jnp.zeros((8, hd), F32)
        c_s[...] = jnp.zeros(c_s.shape, F32)
        n_s[...] = jnp.zeros(n_s.shape, F32)
        m_s[...] = jnp.full(m_s.shape, NEG_INF, F32)

    def conv(ext, x_ref, cw_ref, cb_ref, out_ref):
        ext[8:8 + CHUNK, :] = x_ref[...]
        acc = cb_ref[...] + ext[5:5 + CHUNK, :] * cw_ref[0:1, :]
        for j in range(1, CONV_K):
            acc = acc + ext[5 + j:5 + j + CHUNK, :] * cw_ref[j:j + 1, :]

        @pl.when(c == nchunk - 1)
        def _():
            out_ref[...] = ext[CHUNK + 5:CHUNK + 8, :]

        ext[0:8, :] = ext[CHUNK:CHUNK + 8, :]
        return _silu(acc)

    q = conv(extq, q_ref, cwq_ref, cbq_ref, convq_ref)
    k = conv(extk, k_ref, cwk_ref, cbk_ref, convk_ref) * (hd ** -0.5)
    v = v_ref[...]

    it = gi_ref[...] + bi_ref[...]
    logf = -_softplus(-(gf_ref[...] + bf_ref[...]))
    tril = _tril(CHUNK)
    bc = _sel_dot(tril.astype(BF16), logf)
    lane = lax.broadcasted_iota(jnp.int32, (CHUNK, LANE), 1)
    pick = lambda t: jnp.sum(jnp.where(lane == h, t, 0.0), axis=-1, keepdims=True)
    i_col = pick(it)
    b_col = pick(bc)
    tr_s[0] = it.T
    tr_s[1] = bc.T
    i_row = _row_at(tr_s.at[0], h)
    b_row = _row_at(tr_s.at[1], h)

    m_prev = m_s[...]
    dmat = jnp.where(tril, b_col - b_row + i_row, NEG_INF)
    inter = b_col + m_prev
    m_t = jnp.maximum(inter, jnp.max(dmat, axis=-1, keepdims=True))
    w_intra = jnp.exp(dmat - m_t)
    w_inter = jnp.exp(inter - m_t)
    att = w_intra * _dot_nt(q, k)
    c_prev = c_s[...]
    n_prev = n_s[...]
    num = _dot(att, v) + w_inter * _dot(q, c_prev)
    den = jnp.sum(att, axis=-1, keepdims=True) + w_inter * jnp.sum(q * n_prev, axis=-1, keepdims=True)
    hh = num / jnp.maximum(jnp.abs(den), jnp.exp(-m_t))

    b_last = b_col[CHUNK - 1:CHUNK, :]
    logw = b_last - b_col + i_col
    m_new = jnp.maximum(b_last + m_prev, jnp.max(logw, axis=0, keepdims=True))
    ws = jnp.exp(logw - m_new)
    scale = jnp.exp(b_last + m_prev - m_new)
    c_new = scale * c_prev + _dot_tn(k, ws * v)
    n_new = scale * n_prev + jnp.sum(ws * k, axis=0, keepdims=True)
    c_s[...] = c_new
    n_s[...] = n_new
    m_s[...] = m_new

    hg = _sigmoid(o_ref[...]) * hh
    mu = jnp.mean(hg, axis=-1, keepdims=True)
    var = jnp.mean(jnp.square(hg - mu), axis=-1, keepdims=True)
    hn = (hg - mu) * lax.rsqrt(var + EPS) * nw_ref[...]
    hz_ref[...] = hn * _silu(z_ref[...])

    @pl.when(c == nchunk - 1)
    def _():
        c_out[...] = c_new
        n_out[...] = n_new
        m_out[...] = m_new


def _mlstm_prompt(u, cw, cb, bi, bf, nw, *, batch, seq, heads, hd, col_gates):
    nchunk = seq // CHUNK
    inner = heads * hd
    row = lambda b, h, c: b * nchunk + c
    ublk = lambda off: pl.BlockSpec((CHUNK, hd), lambda b, h, c, off=off: (row(b, h, c), off + h))
    gblk = lambda off: pl.BlockSpec((CHUNK, LANE), lambda b, h, c, off=off: (row(b, h, c), col_gates // LANE + off))
    wblk = lambda rows, off: pl.BlockSpec((rows, hd), lambda b, h, c, off=off: (0, off + h))
    small = lambda a: pl.BlockSpec(a.shape, lambda b, h, c: (0, 0))
    return pl.pallas_call(
        functools.partial(_mlstm_prompt_body, nchunk=nchunk, hd=hd),
        grid=(batch, heads, nchunk),
        in_specs=[ublk(0), ublk(heads), ublk(2 * heads), ublk(3 * heads), ublk(4 * heads),
                  gblk(0), gblk(1),
                  wblk(CONV_K, 0), wblk(CONV_K, heads), wblk(1, 0), wblk(1, heads),
                  small(bi), small(bf), wblk(1, 0)],
        out_specs=[pl.BlockSpec((CHUNK, hd), lambda b, h, c: (row(b, h, c), h)),
                   pl.BlockSpec((None, CONV_K - 1, hd), lambda b, h, c: (b, 0, h)),
                   pl.BlockSpec((None, CONV_K - 1, hd), lambda b, h, c: (b, 0, h)),
                   pl.BlockSpec((None, None, hd, hd), lambda b, h, c: (b, h, 0, 0)),
                   pl.BlockSpec((None, None, 1, hd), lambda b, h, c: (b, h, 0, 0)),
                   pl.BlockSpec((None, None, 1, 1), lambda b, h, c: (b, h, 0, 0))],
        out_shape=[jax.ShapeDtypeStruct((batch * seq, inner), F32),
                   jax.ShapeDtypeStruct((batch, CONV_K - 1, inner), F32),
                   jax.ShapeDtypeStruct((batch, CONV_K - 1, inner), F32),
                   jax.ShapeDtypeStruct((batch, heads, hd, hd), F32),
                   jax.ShapeDtypeStruct((batch, heads, 1, hd), F32),
                   jax.ShapeDtypeStruct((batch, heads, 1, 1), F32)],
        scratch_shapes=[pltpu.VMEM((CHUNK + 8, hd), F32),
                        pltpu.VMEM((CHUNK + 8, hd), F32),
                        pltpu.VMEM((hd, hd), F32),
                        pltpu.VMEM((1, hd), F32),
                        pltpu.VMEM((1, 1), F32),
                        pltpu.VMEM((2, CHUNK, LANE), F32)],
        compiler_params=_params("arbitrary", "arbitrary", "arbitrary"),
        name="mlstm_prompt")(u, u, u, u, u, u, u, cw, cw, cb, cb, bi, bf, nw)


def _rot_sample_body(q_ref, k_ref, c_ref, sa_ref, sb_ref, qo_ref, ko_ref):
    c, sa, sb = c_ref[...], sa_ref[...], sb_ref[...]
    qo_ref[...] = _rotary(q_ref[...], c, sa, sb) * (ATT_HD ** -0.5)
    ko_ref[...] = _rotary(k_ref[...], c, sa, sb)


def _rot_sample(u, cos, sa, sb):
    bs = u.shape[0]
    ublk = lambda off: pl.BlockSpec((bs, ATT_W), lambda i, off=off: (0, off))
    tab = pl.BlockSpec((1, ATT_W), lambda i: (0, 0))
    out = pl.BlockSpec((bs, ATT_W), lambda i: (0, 0))
    return pl.pallas_call(
        _rot_sample_body,
        grid=(1,),
        in_specs=[ublk(0), ublk(1), tab, tab, tab],
        out_specs=[out, out],
        out_shape=[jax.ShapeDtypeStruct((bs, ATT_W), F32)] * 2,
        compiler_params=_params("arbitrary"),
        name="rot_sample")(u, u, cos, sa, sb)


def _attn_sample_body(q_ref, k_ref, v_ref, g_ref,
                      k1_ref, k4_ref, k16_ref, v1_ref, v4_ref, v16_ref, att_ref, *, bb):
    nh, hd = ATT_HEADS, ATT_HD
    ones = jnp.ones((hd, LANE), BF16)
    npat = len(DILATIONS)
    q, k, v = q_ref[...], k_ref[...], v_ref[...]
    s_self = _dot_sel((q * k).reshape(bb * nh, hd), ones)
    for b in range(bb):
        qt = jnp.broadcast_to(q[b][None], (CHUNK, nh, hd)).reshape(CHUNK * nh, hd)
        ss = [_dot_sel(kr[b].reshape(CHUNK * nh, hd) * qt, ones).reshape(CHUNK, nh, LANE)
              for kr in (k1_ref, k4_ref, k16_ref)]
        sb_self = s_self[b * nh:(b + 1) * nh, :]
        mx = sb_self
        for s in ss:
            mx = jnp.maximum(mx, jnp.max(s, axis=0))
        p_self = jnp.exp(sb_self - mx)
        den = npat * p_self
        acc = npat * p_self[:, :hd] * v[b]
        for s, vr in zip(ss, (v1_ref, v4_ref, v16_ref)):
            e = jnp.exp(s - mx[None])
            den = den + jnp.sum(e, axis=0)
            acc = acc + jnp.sum(e[:, :, :hd] * vr[b], axis=0)
        att_ref[b] = (acc / den[:, :hd]) * _silu(g_ref[b])


def _attn_sample(q, k, v, g, ck, cv, *, bb):
    bs, wb = ck.shape[0], ck.shape[1]
    tok = pl.BlockSpec((bb, ATT_HEADS, ATT_HD), lambda i: (i, 0, 0))
    views, specs = [], []
    for cache in (ck, cv):
        for d in DILATIONS:
            views.append(cache.reshape(bs, wb // d, d, ATT_HEADS, ATT_HD))
            nb = wb // d // CHUNK
            specs.append(pl.BlockSpec((bb, CHUNK, None, ATT_HEADS, ATT_HD),
                                      lambda i, nb=nb: (i, nb - 1, 0, 0, 0)))
    return pl.pallas_call(
        functools.partial(_attn_sample_body, bb=bb),
        grid=(bs // bb,),
        in_specs=[tok, tok, tok, tok] + specs,
        out_specs=tok,
        out_shape=jax.ShapeDtypeStruct((bs, ATT_HEADS, ATT_HD), F32),
        compiler_params=_params("arbitrary"),
        name="attn_sample")(q, k, v, g, *views)


def _ssd_sample_prep_body(xbc_ref, dt_ref, cst_ref, cw_ref, cb_ref, dtb_ref, alog_ref, aloge_ref, dsk_ref,
                          conv_ref, yloc_ref, xdtt_ref, bc_ref, da_ref, dae_ref, *, heads):
    inner = heads * SSD_HEADDIM
    cch = xbc_ref.shape[1]
    gw = SSD_STATE
    x = xbc_ref[...]
    acc = cb_ref[...] + x * cw_ref[CONV_K - 1:CONV_K, :]
    for j in range(CONV_K - 1):
        acc = acc + cst_ref[:, j * cch:(j + 1) * cch] * cw_ref[j:j + 1, :]
    for j in range(CONV_K - 2):
        conv_ref[:, j * cch:(j + 1) * cch] = cst_ref[:, (j + 1) * cch:(j + 2) * cch]
    conv_ref[:, (CONV_K - 2) * cch:(CONV_K - 1) * cch] = x
    act = _silu(acc)
    xs = act[:, 0:inner]
    bc_ref[...] = act[:, inner:]
    dt = _softplus(dt_ref[...] + dtb_ref[...])
    da_ref[...] = jnp.exp(dt * (-jnp.exp(alog_ref[...])))
    expand = _seg_matrix(LANE, inner, SSD_HEADDIM, False)
    dte = _dot_sel(dt, expand)
    dae_ref[...] = jnp.exp(dte * (-jnp.exp(aloge_ref[...])))
    xdt = xs * dte
    xdtt_ref[...] = xdt.T
    hw = inner // SSD_GROUPS
    parts = []
    for g in range(SSD_GROUPS):
        bm = act[:, inner + g * gw:inner + (g + 1) * gw]
        cm = act[:, inner + SSD_GROUPS * gw + g * gw:inner + SSD_GROUPS * gw + (g + 1) * gw]
        cbg = jnp.sum(cm * bm, axis=-1, keepdims=True)
        parts.append(cbg * xdt[:, g * hw:(g + 1) * hw])
    yloc_ref[...] = jnp.concatenate(parts, axis=-1) + dsk_ref[...] * xs


def _ssd_sample_state_body(da_ref, s_ref, xdtt_ref, bc_ref, so_ref, yi_ref, *, bb, heads):
    i = pl.program_id(0)
    bs = bc_ref.shape[0]
    inner = heads * SSD_HEADDIM
    hw = inner // SSD_GROUPS
    hpg = heads // SSD_GROUPS
    gw = SSD_STATE
    rid = lax.broadcasted_iota(jnp.int32, (bs, gw), 0)
    for t in range(bb):
        b = i * bb + t
        parts = []
        for g in range(SSD_GROUPS):
            mg = jnp.where(rid == b, bc_ref[:, g * gw:(g + 1) * gw], 0.0)
            sl = _dot(xdtt_ref[g * hw:(g + 1) * hw, :], mg)
            crow = _row_at(bc_ref, b, slice(SSD_GROUPS * gw + g * gw, SSD_GROUPS * gw + (g + 1) * gw))
            sg = s_ref[t, g * hw:(g + 1) * hw, :]
            parts.append(_dot_nt(jnp.broadcast_to(crow, (8, gw)), sg)[0:1, :])
            for hg in range(hpg):
                h = g * hpg + hg
                lo = hg * SSD_HEADDIM
                so_ref[t, h * SSD_HEADDIM:(h + 1) * SSD_HEADDIM, :] = (
                    sg[lo:lo + SSD_HEADDIM, :] * da_ref[b, h] + sl[lo:lo + SSD_HEADDIM, :])
        yi_ref[t] = jnp.concatenate(parts, axis=-1)


def _ssd_sample_finish_body(yloc_ref, yi_ref, dae_ref, z_ref, nw_ref, y_ref):
    y = (yloc_ref[...] + yi_ref[...] * dae_ref[...]) * _silu(z_ref[...])
    y_ref[...] = _rms(y, nw_ref[...])


def _ssd_sample(u, cst, state, cw, cb, dtb, alog, aloge, dsk, nw, *, heads, col_z, col_xbc, col_dt, bb):
    bs = u.shape[0]
    inner = heads * SSD_HEADDIM
    cch = cw.shape[1]
    ncs = (CONV_K - 1) * cch
    bcw = 2 * SSD_GROUPS * SSD_STATE
    full = lambda a: pl.BlockSpec(a.shape, lambda i: (0,) * a.ndim)
    conv, yloc, xdtt, bc, da, dae = pl.pallas_call(
        functools.partial(_ssd_sample_prep_body, heads=heads),
        grid=(1,),
        in_specs=[pl.BlockSpec((bs, cch), lambda i: (0, col_xbc // cch)),
                  pl.BlockSpec((bs, LANE), lambda i: (0, col_dt // LANE)),
                  full(cst), full(cw), full(cb), full(dtb), full(alog), full(aloge), full(dsk)],
        out_specs=[pl.BlockSpec((bs, ncs), lambda i: (0, 0)),
                   pl.BlockSpec((bs, inner), lambda i: (0, 0)),
                   pl.BlockSpec((inner, bs), lambda i: (0, 0)),
                   pl.BlockSpec((bs, bcw), lambda i: (0, 0)),
                   pl.BlockSpec((bs, LANE), lambda i: (0, 0)),
                   pl.BlockSpec((bs, inner), lambda i: (0, 0))],
        out_shape=[jax.ShapeDtypeStruct((bs, ncs), F32),
                   jax.ShapeDtypeStruct((bs, inner), F32),
                   jax.ShapeDtypeStruct((inner, bs), F32),
                   jax.ShapeDtypeStruct((bs, bcw), F32),
                   jax.ShapeDtypeStruct((bs, LANE), F32),
                   jax.ShapeDtypeStruct((bs, inner), F32)],
        compiler_params=_params("arbitrary"),
        name="ssd_sample_prep")(u, u, cst, cw, cb, dtb, alog, aloge, dsk)
    new_state, yi = pl.pallas_call(
        functools.partial(_ssd_sample_state_body, bb=bb, heads=heads),
        grid=(bs // bb,),
        in_specs=[pl.BlockSpec(memory_space=pltpu.SMEM),
                  pl.BlockSpec((bb, inner, SSD_STATE), lambda i: (i, 0, 0)),
                  full(xdtt), full(bc)],
        out_specs=[pl.BlockSpec((bb, inner, SSD_STATE), lambda i: (i, 0, 0)),
                   pl.BlockSpec((bb, 1, inner), lambda i: (i, 0, 0))],
        out_shape=[jax.ShapeDtypeStruct((bs, inner, SSD_STATE), F32),
                   jax.ShapeDtypeStruct((bs, 1, inner), F32)],
        compiler_params=_params("arbitrary"),
        name="ssd_sample_state")(da[:, :heads], state, xdtt, bc)
    y = pl.pallas_call(
        _ssd_sample_finish_body,
        grid=(1,),
        in_specs=[full(yloc), pl.BlockSpec((bs, inner), lambda i: (0, 0)), full(dae),
                  pl.BlockSpec((bs, inner), lambda i: (0, col_z // inner)), full(nw)],
        out_specs=pl.BlockSpec((bs, inner), lambda i: (0, 0)),
        out_shape=jax.ShapeDtypeStruct((bs, inner), F32),
        compiler_params=_params("arbitrary"),
        name="ssd_sample_finish")(yloc, yi.reshape(bs, inner), dae, u, nw)
    return y, conv, new_state


def _mlstm_sample_prep_body(q_ref, k_ref, gi_ref, gf_ref, cst_ref, cw_ref, cb_ref, bi_ref, bf_ref, m_ref,
                            conv_ref, qk_ref, kwt_ref, wi_ref, wf_ref, mt_ref, *, heads, hd):
    inner = heads * hd
    cch = 2 * inner
    x = jnp.concatenate([q_ref[...], k_ref[...]], axis=-1)
    acc = cb_ref[...] + x * cw_ref[CONV_K - 1:CONV_K, :]
    for j in range(CONV_K - 1):
        acc = acc + cst_ref[:, j * cch:(j + 1) * cch] * cw_ref[j:j + 1, :]
    for j in range(CONV_K - 2):
        conv_ref[:, j * cch:(j + 1) * cch] = cst_ref[:, (j + 1) * cch:(j + 2) * cch]
    conv_ref[:, (CONV_K - 2) * cch:(CONV_K - 1) * cch] = x
    act = _silu(acc)
    q = act[:, 0:inner]
    k = act[:, inner:] * (hd ** -0.5)
    it = gi_ref[...] + bi_ref[...]
    inter = -_softplus(-(gf_ref[...] + bf_ref[...])) + m_ref[...]
    mt = jnp.maximum(inter, it)
    wi = jnp.exp(it - mt)
    wf = jnp.exp(inter - mt)
    wi_ref[...] = wi
    wf_ref[...] = wf
    mt_ref[...] = mt
    qk_ref[:, 0:inner] = q
    qk_ref[:, inner:] = k
    kw = jnp.concatenate([k[:, h * hd:(h + 1) * hd] * wi[:, h:h + 1] for h in range(heads)], axis=-1)
    kwt_ref[...] = kw.T


def _mlstm_sample_state_body(wf_ref, c_ref, qk_ref, v_ref, kwt_ref, co_ref, qc_ref, *, heads, hd):
    b = pl.program_id(0)
    bs = v_ref.shape[0]
    rid = lax.broadcasted_iota(jnp.int32, (bs, hd), 0)
    parts = []
    for h in range(heads):
        cp = c_ref[0, h]
        qrow = _row_at(qk_ref, b, slice(h * hd, (h + 1) * hd))
        parts.append(_dot(jnp.broadcast_to(qrow, (8, hd)), cp)[0:1, :])
        mh = jnp.where(rid == b, v_ref[:, h * hd:(h + 1) * hd], 0.0)
        co_ref[0, h] = cp * wf_ref[b, h] + _dot(kwt_ref[h * hd:(h + 1) * hd, :], mh)
    qc_ref[0] = jnp.concatenate(parts, axis=-1)


def _mlstm_sample_finish_body(qk_ref, v_ref, o_ref, z_ref, qc_ref, n_ref, wi_ref, wf_ref, mt_ref, nw_ref,
                              hz_ref, no_ref, *, heads, hd):
    inner = heads * hd
    for h in range(heads):
        sl = slice(h * hd, (h + 1) * hd)
        q = qk_ref[:, sl]
        k = qk_ref[:, inner + h * hd:inner + (h + 1) * hd]
        wi = wi_ref[:, h:h + 1]
        wf = wf_ref[:, h:h + 1]
        mt = mt_ref[:, h:h + 1]
        n_prev = n_ref[:, sl]
        att = wi * jnp.sum(q * k, axis=-1, keepdims=True)
        num = att * v_ref[:, sl] + wf * qc_ref[:, sl]
        den = att + wf * jnp.sum(q * n_prev, axis=-1, keepdims=True)
        hh = num / jnp.maximum(jnp.abs(den), jnp.exp(-mt))
        hg = _sigmoid(o_ref[:, sl]) * hh
        mu = jnp.mean(hg, axis=-1, keepdims=True)
        var = jnp.mean(jnp.square(hg - mu), axis=-1, keepdims=True)
        hn = (hg - mu) * lax.rsqrt(var + EPS) * nw_ref[:, sl]
        hz_ref[:, sl] = hn * _silu(z_ref[:, sl])
        no_ref[:, sl] = wf * n_prev + wi * k


def _mlstm_sample(u, cst, c0, n0, m0p, cw, cb, bi, bf, nw, *, heads, hd, col_gates):
    bs = u.shape[0]
    inner = heads * hd
    ncs = (CONV_K - 1) * 2 * inner
    full = lambda a: pl.BlockSpec(a.shape, lambda i: (0,) * a.ndim)
    ucol = lambda j: pl.BlockSpec((bs, inner), lambda i, j=j: (0, j))
    gcol = lambda j: pl.BlockSpec((bs, LANE), lambda i, j=j: (0, col_gates // LANE + j))
    tile = jax.ShapeDtypeStruct((bs, LANE), F32)
    conv, qk, kwt, wi, wf, mt = pl.pallas_call(
        functools.partial(_mlstm_sample_prep_body, heads=heads, hd=hd),
        grid=(1,),
        in_specs=[ucol(0), ucol(1), gcol(0), gcol(1), full(cst), full(cw), full(cb), full(bi), full(bf),
                  full(m0p)],
        out_specs=[pl.BlockSpec((bs, ncs), lambda i: (0, 0)),
                   pl.BlockSpec((bs, 2 * inner), lambda i: (0, 0)),
                   pl.BlockSpec((inner, bs), lambda i: (0, 0)),
                   pl.BlockSpec((bs, LANE), lambda i: (0, 0)),
                   pl.BlockSpec((bs, LANE), lambda i: (0, 0)),
                   pl.BlockSpec((bs, LANE), lambda i: (0, 0))],
        out_shape=[jax.ShapeDtypeStruct((bs, ncs), F32),
                   jax.ShapeDtypeStruct((bs, 2 * inner), F32),
                   jax.ShapeDtypeStruct((inner, bs), F32),
                   tile, tile, tile],
        compiler_params=_params("arbitrary"),
        name="mlstm_sample_prep")(u, u, u, u, cst, cw, cb, bi, bf, m0p)
    c_new, qc = pl.pallas_call(
        functools.partial(_mlstm_sample_state_body, heads=heads, hd=hd),
        grid=(bs,),
        in_specs=[pl.BlockSpec(memory_space=pltpu.SMEM),
                  pl.BlockSpec((1, heads, hd, hd), lambda i: (i, 0, 0, 0)),
                  full(qk), pl.BlockSpec((bs, inner), lambda i: (0, 2)), full(kwt)],
        out_specs=[pl.BlockSpec((1, heads, hd, hd), lambda i: (i, 0, 0, 0)),
                   pl.BlockSpec((1, 1, inner), lambda i: (i, 0, 0))],
        out_shape=[jax.ShapeDtypeStruct((bs, heads, hd, hd), F32),
                   jax.ShapeDtypeStruct((bs, 1, inner), F32)],
        compiler_params=_params("arbitrary"),
        name="mlstm_sample_state")(wf[:, :heads], c0, qk, u, kwt)
    hz, n_new = pl.pallas_call(
        functools.partial(_mlstm_sample_finish_body, heads=heads, hd=hd),
        grid=(1,),
        in_specs=[full(qk), ucol(2), ucol(3), ucol(4), pl.BlockSpec((bs, inner), lambda i: (0, 0)),
                  full(n0), full(wi), full(wf), full(mt), full(nw)],
        out_specs=[pl.BlockSpec((bs, inner), lambda i: (0, 0)),
                   pl.BlockSpec((bs, inner), lambda i: (0, 0))],
        out_shape=[jax.ShapeDtypeStruct((bs, inner), F32)] * 2,
        compiler_params=_params("arbitrary"),
        name="mlstm_sample_finish")(qk, u, u, u, qc.reshape(bs, inner), n0, wi, wf, mt, nw)
    return hz, conv, c_new, n_new, mt[:, :heads]


def _pad_cols(w, n):
    return jnp.pad(w, ((0, 0), (0, n - w.shape[1])))


def _row(v, n=None):
    v = v.reshape(1, -1)
    return v if n is None else _pad_cols(v, n)


def kernel(x_prompt, x_sample, cache_attn_k, cache_attn_v, state_ssd_conv, state_ssd, state_mlstm_conv, state_mlstm_c, state_mlstm_n, state_mlstm_m, norm_w, final_norm_w, w_in_even, w_out_even, ssd_conv_w, ssd_conv_b, ssd_dt_bias, ssd_a_log, ssd_d, ssd_norm_w, w_in_odd, w_out_odd, mlstm_conv_w, mlstm_conv_b, mlstm_igate_b, mlstm_fgate_b, mlstm_norm_w):
    batch, seq, d_model = x_prompt.shape
    bs = x_sample.shape[0]
    ssd_heads = ssd_a_log.shape[1]
    ssd_inner = ssd_heads * SSD_HEADDIM
    ssd_cch = ssd_conv_w.shape[2]
    m_inner = mlstm_norm_w.shape[1]
    m_hd = m_inner // MLSTM_HEADS
    mp = batch * seq

    col_z = 4 * ATT_W
    col_xbc = col_z + ssd_inner
    col_dt = col_xbc + ssd_cch
    n_even = col_dt + LANE
    w_in0 = _pad_cols(w_in_even[0], n_even).astype(BF16)
    w_out0 = w_out_even[0].astype(BF16)
    nw0 = _row(norm_w[0])
    cw0, cb0 = ssd_conv_w[0], _row(ssd_conv_b[0])
    dtb = _row(ssd_dt_bias[0], LANE)
    alog = _row(ssd_a_log[0], LANE)
    aloge = _row(jnp.repeat(ssd_a_log[0], SSD_HEADDIM))
    dsk = _row(jnp.repeat(ssd_d[0], SSD_HEADDIM))
    snw = _row(ssd_norm_w[0])

    hp = x_prompt.reshape(mp, d_model)
    hs = x_sample.reshape(bs, d_model)

    up = _norm_matmul(hp, nw0, w_in0, tm=256, panels=1)
    cos_p, sa_p, sb_p = _rope_tables(jnp.arange(seq), LANE)
    att_p, k_p, v_p = _attn_prompt(up, cos_p, sa_p, sb_p, batch=batch, seq=seq)
    y_p, conv_p, st_p = _ssd_prompt(up, cw0, cb0, dtb, alog, dsk, snw, batch=batch, seq=seq,
                                    heads=ssd_heads, col_z=col_z, col_xbc=col_xbc, col_dt=col_dt)
    hp = _out_proj(hp, [att_p, y_p], [w_out0[:ATT_W], w_out0[ATT_W:]], tm=512)

    us = _norm_matmul(hs, nw0, w_in0, tm=bs, panels=1)
    cos_s, sa_s, sb_s = _rope_tables(PAST_LEN + jnp.arange(1), ATT_W)
    q_s, k_s = _rot_sample(us, cos_s, sa_s, sb_s)
    v_s = us[:, 2 * ATT_W:3 * ATT_W]
    by_head = lambda t: t.reshape(bs, ATT_HEADS, ATT_HD)
    att_s = _attn_sample(by_head(q_s), by_head(k_s), by_head(v_s), by_head(us[:, 3 * ATT_W:4 * ATT_W]),
                         cache_attn_k[0], cache_attn_v[0], bb=4).reshape(bs, ATT_W)
    y_s, conv_s, st_s = _ssd_sample(us, state_ssd_conv[0].reshape(bs, -1),
                                    state_ssd[0].reshape(bs, ssd_inner, SSD_STATE),
                                    cw0, cb0, dtb, alog, aloge, dsk, snw, heads=ssd_heads,
                                    col_z=col_z, col_xbc=col_xbc, col_dt=col_dt, bb=4)
    hs = _out_proj(hs, [att_s, y_s], [w_out0[:ATT_W], w_out0[ATT_W:]], tm=bs)

    wo = w_in_odd[0]
    gates_at = 4 * m_inner
    zcol = gates_at + 2 * MLSTM_HEADS
    col_gates = 5 * m_inner
    zpad = jnp.zeros((d_model, LANE - MLSTM_HEADS), wo.dtype)
    w_in1 = jnp.concatenate([wo[:, :gates_at], wo[:, zcol:],
                             wo[:, gates_at:gates_at + MLSTM_HEADS], zpad,
                             wo[:, gates_at + MLSTM_HEADS:zcol], zpad], axis=1).astype(BF16)
    w_out1 = w_out_odd[0].astype(BF16)
    nw1 = _row(norm_w[1])
    cw1, cb1 = mlstm_conv_w[0], _row(mlstm_conv_b[0])
    bi = _row(mlstm_igate_b[0], LANE)
    bf = _row(mlstm_fgate_b[0], LANE)
    mnw = _row(mlstm_norm_w[0])
    fnw = _row(final_norm_w)

    up1 = _norm_matmul(hp, nw1, w_in1, tm=256, panels=2)
    hz_p, cq_p, ck_p, c_p, n_p, m_p = _mlstm_prompt(up1, cw1, cb1, bi, bf, mnw, batch=batch, seq=seq,
                                                   heads=MLSTM_HEADS, hd=m_hd, col_gates=col_gates)
    y_prompt = _out_proj(hp, [hz_p], [w_out1], fnw, tm=512)

    us1 = _norm_matmul(hs, nw1, w_in1, tm=bs, panels=2)
    m0p = _pad_cols(state_mlstm_m[0], LANE)
    hz_s, mconv_s, c_s, n_s, m_s = _mlstm_sample(us1, state_mlstm_conv[0].reshape(bs, -1),
                                                 state_mlstm_c[0], state_mlstm_n[0].reshape(bs, m_inner),
                                                 m0p, cw1, cb1, bi, bf, mnw,
                                                 heads=MLSTM_HEADS, hd=m_hd, col_gates=col_gates)
    y_sample = _out_proj(hs, [hz_s], [w_out1], fnw, tm=bs)

    tmax = min(seq, CHUNK * max(DILATIONS))
    kv_shape = (1, batch, seq, ATT_HEADS, ATT_HD)
    return (
        y_prompt.reshape(batch, seq, d_model),
        y_sample.reshape(bs, 1, d_model),
        k_p.reshape(kv_shape)[:, :, seq - tmax:],
        v_p.reshape(kv_shape)[:, :, seq - tmax:],
        conv_p[None],
        st_p.reshape(1, batch, ssd_heads, SSD_HEADDIM, SSD_STATE),
        jnp.concatenate([cq_p, ck_p], axis=-1)[None],
        c_p[None],
        n_p.reshape(1, batch, MLSTM_HEADS, m_hd),
        m_p.reshape(1, batch, MLSTM_HEADS),
        k_s.reshape(1, bs, 1, ATT_HEADS, ATT_HD),
        v_s.reshape(1, bs, 1, ATT_HEADS, ATT_HD),
        conv_s.reshape(1, bs, CONV_K - 1, ssd_cch),
        st_s.reshape(1, bs, ssd_heads, SSD_HEADDIM, SSD_STATE),
        mconv_s.reshape(1, bs, CONV_K - 1, 2 * m_inner),
        c_s[None],
        n_s.reshape(1, bs, MLSTM_HEADS, m_hd),
        m_s.reshape(1, bs, MLSTM_HEADS),
    )
```

```python
import functools

import jax
import jax.numpy as jnp
from jax import lax
from jax.experimental import pallas as pl
from jax.experimental.pallas import tpu as pltpu

F32 = jnp.float32
BF16 = jnp.bfloat16
NEG_INF = float("-inf")
EPS = 1e-6
LANE = 128
CHUNK = 128
VMEM_LIMIT = 56 * 1024 * 1024

CONV_K = 4
ATT_HEADS = 8
ATT_HD = 64
ATT_W = ATT_HEADS * ATT_HD
ROT_DIM = ATT_HD // 4
ROPE_THETA = 500000.0
DILATIONS = (1, 4, 16)
PAST_LEN = 2048
SSD_HEADDIM = 64
SSD_GROUPS = 2
SSD_STATE = 128
MLSTM_HEADS = 8


def _params(*sem):
    return pltpu.CompilerParams(dimension_semantics=sem, vmem_limit_bytes=VMEM_LIMIT)


def _chunks(n, w):
    out, c = [], 0
    while c < n:
        out.append((c, min(w, n - c)))
        c += w
    return out


def _dot(a, b):
    return jnp.dot(a.astype(BF16), b.astype(BF16), preferred_element_type=F32)


def _dot_nt(a, b):
    return lax.dot_general(a.astype(BF16), b.astype(BF16), (((1,), (1,)), ((), ())),
                           preferred_element_type=F32)


def _dot_tn(a, b):
    return lax.dot_general(a.astype(BF16), b.astype(BF16), (((0,), (0,)), ((), ())),
                           preferred_element_type=F32)


def _split3(x):
    hi = x.astype(BF16)
    r1 = x - hi.astype(F32)
    mid = r1.astype(BF16)
    lo = (r1 - mid.astype(F32)).astype(BF16)
    return hi, mid, lo


def _sel_dot(sel, x):
    hi, mid, lo = _split3(x)
    d = lambda p: jnp.dot(sel, p, preferred_element_type=F32)
    return d(hi) + d(mid) + d(lo)


def _dot_sel(x, sel):
    hi, mid, lo = _split3(x)
    d = lambda p: jnp.dot(p, sel, preferred_element_type=F32)
    return d(hi) + d(mid) + d(lo)


def _tril(n):
    r = lax.broadcasted_iota(jnp.int32, (n, n), 0)
    c = lax.broadcasted_iota(jnp.int32, (n, n), 1)
    return r >= c


def _seg_matrix(rows, cols, seg, along_rows):
    r = lax.broadcasted_iota(jnp.int32, (rows, cols), 0)
    c = lax.broadcasted_iota(jnp.int32, (rows, cols), 1)
    m = (r // seg == c) if along_rows else (c // seg == r)
    return m.astype(BF16)


def _row_at(ref, b, cols=slice(None)):
    base = pl.multiple_of((b // 8) * 8, 8)
    tile = ref[pl.ds(base, 8), cols]
    sub = lax.broadcasted_iota(jnp.int32, tile.shape, 0)
    return jnp.sum(jnp.where(sub == b % 8, tile, 0.0), axis=0, keepdims=True)


def _silu(x):
    return x * (1.0 / (1.0 + jnp.exp(-x)))


def _sigmoid(x):
    return 1.0 / (1.0 + jnp.exp(-x))


def _softplus(x):
    return jnp.maximum(x, 0.0) + jnp.log1p(jnp.exp(-jnp.abs(x)))


def _rms(x, w):
    return x * lax.rsqrt(jnp.mean(x * x, axis=-1, keepdims=True) + EPS) * w


def _norm_matmul_body(x_ref, nw_ref, w_ref, o_ref, *, chunks):
    xn = _rms(x_ref[...], nw_ref[...]).astype(BF16)
    for c0, cw in chunks:
        o_ref[:, c0:c0 + cw] = jnp.dot(xn, w_ref[:, c0:c0 + cw], preferred_element_type=F32)


def _norm_matmul(x, nw, w, *, tm, panels):
    m, d = x.shape
    n = w.shape[1]
    pn = n // panels
    return pl.pallas_call(
        functools.partial(_norm_matmul_body, chunks=_chunks(pn, 512)),
        grid=(panels, m // tm),
        in_specs=[pl.BlockSpec((tm, d), lambda p, i: (i, 0)),
                  pl.BlockSpec((1, d), lambda p, i: (0, 0)),
                  pl.BlockSpec((d, pn), lambda p, i: (0, p))],
        out_specs=pl.BlockSpec((tm, pn), lambda p, i: (i, p)),
        out_shape=jax.ShapeDtypeStruct((m, n), F32),
        compiler_params=_params("arbitrary", "arbitrary"),
        name="norm_matmul")(x, nw, w)


def _out_proj_body(*refs, n_in, final):
    h_ref = refs[0]
    xs = refs[1:1 + n_in]
    ws = refs[1 + n_in:1 + 2 * n_in]
    rest = refs[1 + 2 * n_in:]
    acc = h_ref[...]
    for x_ref, w_ref in zip(xs, ws):
        acc = acc + jnp.dot(x_ref[...].astype(BF16), w_ref[...], preferred_element_type=F32)
    if final:
        fw_ref, o_ref = rest
        o_ref[...] = _rms(acc, fw_ref[...])
    else:
        (o_ref,) = rest
        o_ref[...] = acc


def _out_proj(h, xs, ws, fw=None, *, tm):
    m, d = h.shape
    n_in = len(xs)
    in_specs = [pl.BlockSpec((tm, d), lambda i: (i, 0))]
    in_specs += [pl.BlockSpec((tm, x.shape[1]), lambda i: (i, 0)) for x in xs]
    in_specs += [pl.BlockSpec(w.shape, lambda i: (0, 0)) for w in ws]
    args = [h, *xs, *ws]
    if fw is not None:
        in_specs.append(pl.BlockSpec((1, d), lambda i: (0, 0)))
        args.append(fw)
    return pl.pallas_call(
        functools.partial(_out_proj_body, n_in=n_in, final=fw is not None),
        grid=(m // tm,),
        in_specs=in_specs,
        out_specs=pl.BlockSpec((tm, d), lambda i: (i, 0)),
        out_shape=jax.ShapeDtypeStruct((m, d), F32),
        compiler_params=_params("arbitrary"),
        name="out_proj")(*args)


def _rope_tables(pos, width):
    half = ROT_DIM // 2
    inv = jnp.power(F32(ROPE_THETA), -jnp.arange(half, dtype=F32) * (2.0 / ROT_DIM))
    ang = pos.astype(F32)[:, None] * inv[None, :]
    cos, sin = jnp.cos(ang), jnp.sin(ang)
    n = pos.shape[0]
    one = jnp.ones((n, ATT_HD - ROT_DIM), F32)
    z8 = jnp.zeros((n, half), F32)
    z48 = jnp.zeros((n, ATT_HD - ROT_DIM), F32)
    c = jnp.concatenate([cos, cos, one], axis=-1)
    sa = jnp.concatenate([-sin, z8, z48], axis=-1)
    sb = jnp.concatenate([z8, sin, z48], axis=-1)
    rep = width // ATT_HD
    return tuple(jnp.tile(t, (1, rep)) for t in (c, sa, sb))


def _rotary(x, c, sa, sb):
    w = x.shape[-1]
    half = ROT_DIM // 2
    return x * c + pltpu.roll(x, w - half, 1) * sa + pltpu.roll(x, half, 1) * sb


def _attn_prompt_body(q_ref, k_ref, v_ref, g_ref, c_ref, sa_ref, sb_ref,
                      att_ref, ko_ref, vo_ref,
                      qd, kd, vd, od, mld, o_s, ml_s, *, seq):
    nblk = seq // CHUNK
    c, sa, sb = c_ref[...], sa_ref[...], sb_ref[...]
    q = _rotary(q_ref[...], c, sa, sb) * (ATT_HD ** -0.5)
    k = _rotary(k_ref[...], c, sa, sb)
    ko_ref[...] = k
    vo_ref[...] = v_ref[...]
    zero = jnp.zeros((CHUNK, LANE), F32)
    for p, d in enumerate(DILATIONS):
        kd[p, 0:CHUNK, :] = zero
        vd[p, 0:CHUNK, :] = zero
    mld[...] = jnp.zeros(mld.shape, F32)
    qd[0] = q
    kd[0, CHUNK:, :] = k
    vd[0, CHUNK:, :] = v_ref[...]
    for p, d in enumerate(DILATIONS):
        if d == 1:
            continue
        ln = seq // d
        for r in range(d):
            qd[p, r * ln:(r + 1) * ln, :] = qd[0, pl.ds(r, ln, stride=d), :]
            kd[p, CHUNK + r * ln:CHUNK + (r + 1) * ln, :] = kd[0, pl.ds(CHUNK + r, ln, stride=d), :]
            vd[p, CHUNK + r * ln:CHUNK + (r + 1) * ln, :] = vd[0, pl.ds(CHUNK + r, ln, stride=d), :]

    row = lax.broadcasted_iota(jnp.int32, (CHUNK, 2 * CHUNK), 0)
    col = lax.broadcasted_iota(jnp.int32, (CHUNK, 2 * CHUNK), 1)
    band = (col >= row) & (col <= row + CHUNK)

    for p, d in enumerate(DILATIONS):
        nb = nblk // d

        def block(t, carry, p=p, nb=nb):
            base = pl.multiple_of(t * CHUNK, CHUNK)
            first = (t % nb) == 0
            valid = band & (col >= jnp.where(first, CHUNK, 0))
            for hh in range(2):
                lo = ATT_HD * hh
                qb = qd[p, pl.ds(base, CHUNK), lo:lo + ATT_HD]
                kw = kd[p, pl.ds(base, 2 * CHUNK), lo:lo + ATT_HD]
                vw = vd[p, pl.ds(base, 2 * CHUNK), lo:lo + ATT_HD]
                s = jnp.where(valid, _dot_nt(qb, kw), NEG_INF)
                m = jnp.max(s, axis=-1, keepdims=True)
                e = jnp.exp(s - m)
                od[pl.ds(base, CHUNK), lo:lo + ATT_HD] = _dot(e, vw)
                mld[pl.ds(base, CHUNK), hh:hh + 1] = m
                mld[pl.ds(base, CHUNK), 2 + hh:3 + hh] = jnp.sum(e, axis=-1, keepdims=True)
            return carry

        lax.fori_loop(0, nblk, block, 0, unroll=4)
        if d == 1:
            o_s[p] = od[...]
            ml_s[p] = mld[...]
        else:
            ln = seq // d
            for r in range(d):
                o_s[p, pl.ds(r, ln, stride=d), :] = od[r * ln:(r + 1) * ln, :]
                ml_s[p, pl.ds(r, ln, stride=d), :] = mld[r * ln:(r + 1) * ln, :]

    np_ = len(DILATIONS)

    def combine(t, carry):
        base = pl.multiple_of(t * CHUNK, CHUNK)
        rows = pl.ds(base, CHUNK)
        for hh in range(2):
            lo = ATT_HD * hh
            ms = [ml_s[p, rows, hh:hh + 1] for p in range(np_)]
            ls = [ml_s[p, rows, 2 + hh:3 + hh] for p in range(np_)]
            mx = functools.reduce(jnp.maximum, ms)
            ws = [jnp.exp(mm - mx) for mm in ms]
            num = sum(w * o_s[p, rows, lo:lo + ATT_HD] for p, w in enumerate(ws))
            den = sum(w * l for w, l in zip(ws, ls))
            att_ref[rows, lo:lo + ATT_HD] = (num / den) * _silu(g_ref[rows, lo:lo + ATT_HD])
        return carry

    lax.fori_loop(0, nblk, combine, 0)


def _attn_prompt(u, cos, sa, sb, *, batch, seq):
    m = batch * seq
    npair = ATT_W // LANE
    blk = lambda off: pl.BlockSpec((seq, LANE), lambda b, hp, off=off: (b, off + hp))
    tab = pl.BlockSpec((seq, LANE), lambda b, hp: (0, 0))
    out = pl.BlockSpec((seq, LANE), lambda b, hp: (b, hp))
    np_ = len(DILATIONS)
    return pl.pallas_call(
        functools.partial(_attn_prompt_body, seq=seq),
        grid=(batch, npair),
        in_specs=[blk(0), blk(npair), blk(2 * npair), blk(3 * npair), tab, tab, tab],
        out_specs=[out, out, out],
        out_shape=[jax.ShapeDtypeStruct((m, ATT_W), F32)] * 3,
        scratch_shapes=[pltpu.VMEM((np_, seq, LANE), F32),
                        pltpu.VMEM((np_, seq + CHUNK, LANE), F32),
                        pltpu.VMEM((np_, seq + CHUNK, LANE), F32),
                        pltpu.VMEM((seq, LANE), F32),
                        pltpu.VMEM((seq, LANE), F32),
                        pltpu.VMEM((np_, seq, LANE), F32),
                        pltpu.VMEM((np_, seq, LANE), F32)],
        compiler_params=_params("arbitrary", "arbitrary"),
        name="attn_prompt")(u, u, u, u, cos, sa, sb)


def _ssd_prompt_body(xbc_ref, z_ref, dt_ref, cw_ref, cb_ref, dtb_ref, alog_ref, dsk_ref, nw_ref,
                     y_ref, conv_ref, st_ref, ext, st, ys, *, heads, nchunk):
    c = pl.program_id(1)
    inner = heads * SSD_HEADDIM
    gw = SSD_STATE
    hpg = heads // SSD_GROUPS

    @pl.when(c == 0)
    def _():
        ext[0:8, :] = jnp.zeros((8, ext.shape[1]), F32)
        st[...] = jnp.zeros(st.shape, F32)

    ext[8:8 + CHUNK, :] = xbc_ref[...]
    conv = cb_ref[...] + ext[5:5 + CHUNK, :] * cw_ref[0:1, :]
    for j in range(1, CONV_K):
        conv = conv + ext[5 + j:5 + j + CHUNK, :] * cw_ref[j:j + 1, :]
    act = _silu(conv)

    @pl.when(c == nchunk - 1)
    def _():
        conv_ref[...] = ext[CHUNK + 5:CHUNK + 8, :]

    ext[0:8, :] = ext[CHUNK:CHUNK + 8, :]

    dt = _softplus(dt_ref[...] + dtb_ref[...])
    a = -jnp.exp(alog_ref[...])
    tril = _tril(CHUNK)
    acum = _sel_dot(tril.astype(BF16), dt * a)
    acum_t = acum.T
    dt_t = dt.T
    eacum = jnp.exp(acum)
    last = acum[CHUNK - 1:CHUNK, :]
    wend = jnp.exp(last - acum) * dt
    elast = jnp.exp(last)

    for g in range(SSD_GROUPS):
        bm = act[:, inner + g * gw:inner + (g + 1) * gw]
        cm = act[:, inner + SSD_GROUPS * gw + g * gw:inner + SSD_GROUPS * gw + (g + 1) * gw]
        cb = _dot_nt(cm, bm)
        for hg in range(hpg):
            h = g * hpg + hg
            lo = h * SSD_HEADDIM
            xh = act[:, lo:lo + SSD_HEADDIM]
            seg = acum[:, h:h + 1] - acum_t[h:h + 1, :]
            decay = jnp.exp(jnp.where(tril, seg, NEG_INF))
            sh = st[lo:lo + SSD_HEADDIM, :]
            y = _dot(cb * decay * dt_t[h:h + 1, :], xh)
            y = y + _dot_nt(cm, sh) * eacum[:, h:h + 1]
            ys[:, lo:lo + SSD_HEADDIM] = y
            st[lo:lo + SSD_HEADDIM, :] = sh * elast[:, h:h + 1] + _dot_tn(xh, bm * wend[:, h:h + 1])

    xs = act[:, 0:inner]
    yt = (ys[...] + dsk_ref[...] * xs) * _silu(z_ref[...])
    y_ref[...] = _rms(yt, nw_ref[...])

    @pl.when(c == nchunk - 1)
    def _():
        st_ref[...] = st[...]


def _ssd_prompt(u, cw, cb, dtb, alog, dsk, nw, *, batch, seq, heads, col_z, col_xbc, col_dt):
    nchunk = seq // CHUNK
    inner = heads * SSD_HEADDIM
    cch = cw.shape[1]
    row = lambda b, c: b * nchunk + c
    full = lambda a: pl.BlockSpec(a.shape, lambda b, c: (0, 0))
    return pl.pallas_call(
        functools.partial(_ssd_prompt_body, heads=heads, nchunk=nchunk),
        grid=(batch, nchunk),
        in_specs=[pl.BlockSpec((CHUNK, cch), lambda b, c: (row(b, c), col_xbc // cch)),
                  pl.BlockSpec((CHUNK, inner), lambda b, c: (row(b, c), col_z // inner)),
                  pl.BlockSpec((CHUNK, LANE), lambda b, c: (row(b, c), col_dt // LANE)),
                  full(cw), full(cb), full(dtb), full(alog), full(dsk), full(nw)],
        out_specs=[pl.BlockSpec((CHUNK, inner), lambda b, c: (row(b, c), 0)),
                   pl.BlockSpec((None, CONV_K - 1, cch), lambda b, c: (b, 0, 0)),
                   pl.BlockSpec((None, inner, SSD_STATE), lambda b, c: (b, 0, 0))],
        out_shape=[jax.ShapeDtypeStruct((batch * seq, inner), F32),
                   jax.ShapeDtypeStruct((batch, CONV_K - 1, cch), F32),
                   jax.ShapeDtypeStruct((batch, inner, SSD_STATE), F32)],
        scratch_shapes=[pltpu.VMEM((CHUNK + 8, cch), F32),
                        pltpu.VMEM((inner, SSD_STATE), F32),
                        pltpu.VMEM((CHUNK, inner), F32)],
        compiler_params=_params("arbitrary", "arbitrary"),
        name="ssd_prompt")(u, u, u, cw, cb, dtb, alog, dsk, nw)


def _mlstm_prompt_body(q_ref, k_ref, v_ref, o_ref, z_ref, gi_ref, gf_ref,
                       cwq_ref, cwk_ref, cbq_ref, cbk_ref, bi_ref, bf_ref, nw_ref,
                       hz_ref, convq_ref, convk_ref, c_out, n_out, m_out,
                       extq, extk, c_s, n_s, m_s, tr_s, *, nchunk, hd):
    h = pl.program_id(1)
    c = pl.program_id(2)

    @pl.when(c == 0)
    def _():
        extq[0:8, :] = jnp.zeros((8, hd), F32)
        extk[0:8, :] = jnp.zeros((8, hd), F32)
        c_s[...] = jnp.zeros(c_s.shape, F32)
        n_s[...] = jnp.zeros(n_s.shape, F32)
        m_s[...] = jnp.full(m_s.shape, NEG_INF, F32)

    def conv(ext, x_ref, cw_ref, cb_ref, out_ref):
        ext[8:8 + CHUNK, :] = x_ref[...]
        acc = cb_ref[...] + ext[5:5 + CHUNK, :] * cw_ref[0:1, :]
        for j in range(1, CONV_K):
            acc = acc + ext[5 + j:5 + j + CHUNK, :] * cw_ref[j:j + 1, :]

        @pl.when(c == nchunk - 1)
        def _():
            out_ref[...] = ext[CHUNK + 5:CHUNK + 8, :]

        ext[0:8, :] = ext[CHUNK:CHUNK + 8, :]
        return _silu(acc)

    q = conv(extq, q_ref, cwq_ref, cbq_ref, convq_ref)
    k = conv(extk, k_ref, cwk_ref, cbk_ref, convk_ref) * (hd ** -0.5)
    v = v_ref[...]

    it = gi_ref[...] + bi_ref[...]
    logf = -_softplus(-(gf_ref[...] + bf_ref[...]))
    tril = _tril(CHUNK)
    bc = _sel_dot(tril.astype(BF16), logf)
    lane = lax.broadcasted_iota(jnp.int32, (CHUNK, LANE), 1)
    pick = lambda t: jnp.sum(jnp.where(lane == h, t, 0.0), axis=-1, keepdims=True)
    i_col = pick(it)
    b_col = pick(bc)
    tr_s[0] = it.T
    tr_s[1] = bc.T
    i_row = _row_at(tr_s.at[0], h)
    b_row = _row_at(tr_s.at[1], h)

    m_prev = m_s[...]
    dmat = jnp.where(tril, b_col - b_row + i_row, NEG_INF)
    inter = b_col + m_prev
    m_t = jnp.maximum(inter, jnp.max(dmat, axis=-1, keepdims=True))
    w_intra = jnp.exp(dmat - m_t)
    w_inter = jnp.exp(inter - m_t)
    att = w_intra * _dot_nt(q, k)
    c_prev = c_s[...]
    n_prev = n_s[...]
    num = _dot(att, v) + w_inter * _dot(q, c_prev)
    den = jnp.sum(att, axis=-1, keepdims=True) + w_inter * jnp.sum(q * n_prev, axis=-1, keepdims=True)
    hh = num / jnp.maximum(jnp.abs(den), jnp.exp(-m_t))

    b_last = b_col[CHUNK - 1:CHUNK, :]
    logw = b_last - b_col + i_col
    m_new = jnp.maximum(b_last + m_prev, jnp.max(logw, axis=0, keepdims=True))
    ws = jnp.exp(logw - m_new)
    scale = jnp.exp(b_last + m_prev - m_new)
    c_new = scale * c_prev + _dot_tn(k, ws * v)
    n_new = scale * n_prev + jnp.sum(ws * k, axis=0, keepdims=True)
    c_s[...] = c_new
    n_s[...] = n_new
    m_s[...] = m_new

    hg = _sigmoid(o_ref[...]) * hh
    mu = jnp.mean(hg, axis=-1, keepdims=True)
    var = jnp.mean(jnp.square(hg - mu), axis=-1, keepdims=True)
    hn = (hg - mu) * lax.rsqrt(var + EPS) * nw_ref[...]
    hz_ref[...] = hn * _silu(z_ref[...])

    @pl.when(c == nchunk - 1)
    def _():
        c_out[...] = c_new
        n_out[...] = n_new
        m_out[...] = m_new


def _mlstm_prompt(u, cw, cb, bi, bf, nw, *, batch, seq, heads, hd, col_gates):
    nchunk = seq // CHUNK
    inner = heads * hd
    row = lambda b, h, c: b * nchunk + c
    ublk = lambda off: pl.BlockSpec((CHUNK, hd), lambda b, h, c, off=off: (row(b, h, c), off + h))
    gblk = lambda off: pl.BlockSpec((CHUNK, LANE), lambda b, h, c, off=off: (row(b, h, c), col_gates // LANE + off))
    wblk = lambda rows, off: pl.BlockSpec((rows, hd), lambda b, h, c, off=off: (0, off + h))
    small = lambda a: pl.BlockSpec(a.shape, lambda b, h, c: (0, 0))
    return pl.pallas_call(
        functools.partial(_mlstm_prompt_body, nchunk=nchunk, hd=hd),
        grid=(batch, heads, nchunk),
        in_specs=[ublk(0), ublk(heads), ublk(2 * heads), ublk(3 * heads), ublk(4 * heads),
                  gblk(0), gblk(1),
                  wblk(CONV_K, 0), wblk(CONV_K, heads), wblk(1, 0), wblk(1, heads),
                  small(bi), small(bf), wblk(1, 0)],
        out_specs=[pl.BlockSpec((CHUNK, hd), lambda b, h, c: (row(b, h, c), h)),
                   pl.BlockSpec((None, CONV_K - 1, hd), lambda b, h, c: (b, 0, h)),
                   pl.BlockSpec((None, CONV_K - 1, hd), lambda b, h, c: (b, 0, h)),
                   pl.BlockSpec((None, None, hd, hd), lambda b, h, c: (b, h, 0, 0)),
                   pl.BlockSpec((None, None, 1, hd), lambda b, h, c: (b, h, 0, 0)),
                   pl.BlockSpec((None, None, 1, 1), lambda b, h, c: (b, h, 0, 0))],
        out_shape=[jax.ShapeDtypeStruct((batch * seq, inner), F32),
                   jax.ShapeDtypeStruct((batch, CONV_K - 1, inner), F32),
                   jax.ShapeDtypeStruct((batch, CONV_K - 1, inner), F32),
                   jax.ShapeDtypeStruct((batch, heads, hd, hd), F32),
                   jax.ShapeDtypeStruct((batch, heads, 1, hd), F32),
                   jax.ShapeDtypeStruct((batch, heads, 1, 1), F32)],
        scratch_shapes=[pltpu.VMEM((CHUNK + 8, hd), F32),
                        pltpu.VMEM((CHUNK + 8, hd), F32),
                        pltpu.VMEM((hd, hd), F32),
                        pltpu.VMEM((1, hd), F32),
                        pltpu.VMEM((1, 1), F32),
                        pltpu.VMEM((2, CHUNK, LANE), F32)],
        compiler_params=_params("arbitrary", "arbitrary", "arbitrary"),
        name="mlstm_prompt")(u, u, u, u, u, u, u, cw, cw, cb, cb, bi, bf, nw)


def _rot_sample_body(q_ref, k_ref, v_ref, g_ref, c_ref, sa_ref, sb_ref, xt_ref):
    c, sa, sb = c_ref[...], sa_ref[...], sb_ref[...]
    q = _rotary(q_ref[...], c, sa, sb) * (ATT_HD ** -0.5)
    k = _rotary(k_ref[...], c, sa, sb)
    xt_ref[...] = jnp.concatenate([q, k, v_ref[...], g_ref[...]], axis=-1).T


def _rot_sample(u, cos, sa, sb):
    bs = u.shape[0]
    ublk = lambda off: pl.BlockSpec((bs, ATT_W), lambda i, off=off: (0, off))
    tab = pl.BlockSpec((1, ATT_W), lambda i: (0, 0))
    return pl.pallas_call(
        _rot_sample_body,
        grid=(1,),
        in_specs=[ublk(0), ublk(1), ublk(2), ublk(3), tab, tab, tab],
        out_specs=pl.BlockSpec((4 * ATT_W, bs), lambda i: (0, 0)),
        out_shape=jax.ShapeDtypeStruct((4 * ATT_W, bs), F32),
        compiler_params=_params("arbitrary"),
        name="rot_sample")(u, u, u, u, cos, sa, sb)


def _key_multiplicity(wb):
    back = wb - lax.broadcasted_iota(jnp.int32, (1, wb), 1)
    cnt = jnp.zeros((1, wb), F32)
    for d in DILATIONS:
        cnt = cnt + ((back <= CHUNK * d) & (lax.rem(back, d) == 0)).astype(F32)
    return cnt


def _attn_sample_body(x_ref, kc_ref, vc_ref, att_ref, *, bb, wb):
    i = pl.program_id(0)
    bs = x_ref.shape[1]
    nt = wb // LANE

    @pl.when(i == 0)
    def _():
        att_ref[...] = jnp.zeros(att_ref.shape, F32)

    cnt = _key_multiplicity(wb)
    valid = cnt > 0.0
    npat = float(len(DILATIONS))
    rid = lax.broadcasted_iota(jnp.int32, (bs, LANE), 0)
    lid = lax.broadcasted_iota(jnp.int32, (ATT_HD, bs), 1)
    for t in range(bb):
        b = i * bb + t
        cols = _dot_sel(x_ref[...], (rid == b).astype(BF16))
        for h in range(ATT_HEADS):
            lo = h * ATT_HD
            qc = cols[lo:lo + ATT_HD]
            kc = cols[ATT_W + lo:ATT_W + lo + ATT_HD]
            vc = cols[2 * ATT_W + lo:2 * ATT_W + lo + ATT_HD]
            gc = cols[3 * ATT_W + lo:3 * ATT_W + lo + ATT_HD]
            kt = kc_ref[t, h]
            vt = vc_ref[t, h]
            s = jnp.sum(kt * jnp.tile(qc, (1, nt)), axis=0, keepdims=True)
            s = jnp.where(valid, s, NEG_INF)
            s_self = jnp.sum(qc * kc, axis=0, keepdims=True)[:, 0:1]
            m = jnp.maximum(jnp.max(s, axis=1, keepdims=True), s_self)
            p = cnt * jnp.exp(s - m)
            p_self = npat * jnp.exp(s_self - m)
            den = jnp.sum(p, axis=1, keepdims=True) + p_self
            acc = vt[:, 0:LANE] * p[:, 0:LANE]
            for j in range(1, nt):
                acc = acc + vt[:, j * LANE:(j + 1) * LANE] * p[:, j * LANE:(j + 1) * LANE]
            o = jnp.sum(acc, axis=1, keepdims=True) + p_self * vc[:, 0:1]
            o = o / den * _silu(gc[:, 0:1])
            att_ref[lo:lo + ATT_HD, :] = jnp.where(lid == b, o, att_ref[lo:lo + ATT_HD, :])


def _attn_sample(xt, ck, cv, *, bb):
    bs, wb = ck.shape[0], ck.shape[3]
    cache = pl.BlockSpec((bb, ATT_HEADS, ATT_HD, wb), lambda i: (i, 0, 0, 0))
    return pl.pallas_call(
        functools.partial(_attn_sample_body, bb=bb, wb=wb),
        grid=(bs // bb,),
        in_specs=[pl.BlockSpec(xt.shape, lambda i: (0, 0)), cache, cache],
        out_specs=pl.BlockSpec((ATT_W, bs), lambda i: (0, 0)),
        out_shape=jax.ShapeDtypeStruct((ATT_W, bs), F32),
        compiler_params=_params("arbitrary"),
        name="attn_sample")(xt, ck, cv)


def _ssd_sample_prep_body(xbc_ref, dt_ref, cst_ref, cw_ref, cb_ref, dtb_ref, alog_ref, aloge_ref, dsk_ref,
                          conv_ref, yloc_ref, xdtt_ref, bc_ref, da_ref, dae_ref, *, heads):
    inner = heads * SSD_HEADDIM
    cch = xbc_ref.shape[1]
    gw = SSD_STATE
    x = xbc_ref[...]
    acc = cb_ref[...] + x * cw_ref[CONV_K - 1:CONV_K, :]
    for j in range(CONV_K - 1):
        acc = acc + cst_ref[:, j * cch:(j + 1) * cch] * cw_ref[j:j + 1, :]
    for j in range(CONV_K - 2):
        conv_ref[:, j * cch:(j + 1) * cch] = cst_ref[:, (j + 1) * cch:(j + 2) * cch]
    conv_ref[:, (CONV_K - 2) * cch:(CONV_K - 1) * cch] = x
    act = _silu(acc)
    xs = act[:, 0:inner]
    bc_ref[...] = act[:, inner:]
    dt = _softplus(dt_ref[...] + dtb_ref[...])
    da_ref[...] = jnp.exp(dt * (-jnp.exp(alog_ref[...])))
    expand = _seg_matrix(LANE, inner, SSD_HEADDIM, False)
    dte = _dot_sel(dt, expand)
    dae_ref[...] = jnp.exp(dte * (-jnp.exp(aloge_ref[...])))
    xdt = xs * dte
    xdtt_ref[...] = xdt.T
    hw = inner // SSD_GROUPS
    parts = []
    for g in range(SSD_GROUPS):
        bm = act[:, inner + g * gw:inner + (g + 1) * gw]
        cm = act[:, inner + SSD_GROUPS * gw + g * gw:inner + SSD_GROUPS * gw + (g + 1) * gw]
        cbg = jnp.sum(cm * bm, axis=-1, keepdims=True)
        parts.append(cbg * xdt[:, g * hw:(g + 1) * hw])
    yloc_ref[...] = jnp.concatenate(parts, axis=-1) + dsk_ref[...] * xs


def _ssd_sample_state_body(da_ref, s_ref, xdtt_ref, bc_ref, so_ref, yi_ref, *, bb, heads):
    i = pl.program_id(0)
    bs = bc_ref.shape[0]
    inner = heads * SSD_HEADDIM
    hw = inner // SSD_GROUPS
    hpg = heads // SSD_GROUPS
    gw = SSD_STATE
    rid = lax.broadcasted_iota(jnp.int32, (bs, gw), 0)
    for t in range(bb):
        b = i * bb + t
        parts = []
        for g in range(SSD_GROUPS):
            mg = jnp.where(rid == b, bc_ref[:, g * gw:(g + 1) * gw], 0.0)
            sl = _dot(xdtt_ref[g * hw:(g + 1) * hw, :], mg)
            crow = _row_at(bc_ref, b, slice(SSD_GROUPS * gw + g * gw, SSD_GROUPS * gw + (g + 1) * gw))
            sg = s_ref[t, g * hw:(g + 1) * hw, :]
            parts.append(_dot_nt(jnp.broadcast_to(crow, (8, gw)), sg)[0:1, :])
            for hg in range(hpg):
                h = g * hpg + hg
                lo = hg * SSD_HEADDIM
                so_ref[t, h * SSD_HEADDIM:(h + 1) * SSD_HEADDIM, :] = (
                    sg[lo:lo + SSD_HEADDIM, :] * da_ref[b, h] + sl[lo:lo + SSD_HEADDIM, :])
        yi_ref[t] = jnp.concatenate(parts, axis=-1)


def _ssd_sample_finish_body(yloc_ref, yi_ref, dae_ref, z_ref, nw_ref, y_ref):
    y = (yloc_ref[...] + yi_ref[...] * dae_ref[...]) * _silu(z_ref[...])
    y_ref[...] = _rms(y, nw_ref[...])


def _ssd_sample(u, cst, state, cw, cb, dtb, alog, aloge, dsk, nw, *, heads, col_z, col_xbc, col_dt, bb):
    bs = u.shape[0]
    inner = heads * SSD_HEADDIM
    cch = cw.shape[1]
    ncs = (CONV_K - 1) * cch
    bcw = 2 * SSD_GROUPS * SSD_STATE
    full = lambda a: pl.BlockSpec(a.shape, lambda i: (0,) * a.ndim)
    conv, yloc, xdtt, bc, da, dae = pl.pallas_call(
        functools.partial(_ssd_sample_prep_body, heads=heads),
        grid=(1,),
        in_specs=[pl.BlockSpec((bs, cch), lambda i: (0, col_xbc // cch)),
                  pl.BlockSpec((bs, LANE), lambda i: (0, col_dt // LANE)),
                  full(cst), full(cw), full(cb), full(dtb), full(alog), full(aloge), full(dsk)],
        out_specs=[pl.BlockSpec((bs, ncs), lambda i: (0, 0)),
                   pl.BlockSpec((bs, inner), lambda i: (0, 0)),
                   pl.BlockSpec((inner, bs), lambda i: (0, 0)),
                   pl.BlockSpec((bs, bcw), lambda i: (0, 0)),
                   pl.BlockSpec((bs, LANE), lambda i: (0, 0)),
                   pl.BlockSpec((bs, inner), lambda i: (0, 0))],
        out_shape=[jax.ShapeDtypeStruct((bs, ncs), F32),
                   jax.ShapeDtypeStruct((bs, inner), F32),
                   jax.ShapeDtypeStruct((inner, bs), F32),
                   jax.ShapeDtypeStruct((bs, bcw), F32),
                   jax.ShapeDtypeStruct((bs, LANE), F32),
                   jax.ShapeDtypeStruct((bs, inner), F32)],
        compiler_params=_params("arbitrary"),
        name="ssd_sample_prep")(u, u, cst, cw, cb, dtb, alog, aloge, dsk)
    new_state, yi = pl.pallas_call(
        functools.partial(_ssd_sample_state_body, bb=bb, heads=heads),
        grid=(bs // bb,),
        in_specs=[pl.BlockSpec(memory_space=pltpu.SMEM),
                  pl.BlockSpec((bb, inner, SSD_STATE), lambda i: (i, 0, 0)),
                  full(xdtt), full(bc)],
        out_specs=[pl.BlockSpec((bb, inner, SSD_STATE), lambda i: (i, 0, 0)),
                   pl.BlockSpec((bb, 1, inner), lambda i: (i, 0, 0))],
        out_shape=[jax.ShapeDtypeStruct((bs, inner, SSD_STATE), F32),
                   jax.ShapeDtypeStruct((bs, 1, inner), F32)],
        compiler_params=_params("arbitrary"),
        name="ssd_sample_state")(da[:, :heads], state, xdtt, bc)
    y = pl.pallas_call(
        _ssd_sample_finish_body,
        grid=(1,),
        in_specs=[full(yloc), pl.BlockSpec((bs, inner), lambda i: (0, 0)), full(dae),
                  pl.BlockSpec((bs, inner), lambda i: (0, col_z // inner)), full(nw)],
        out_specs=pl.BlockSpec((bs, inner), lambda i: (0, 0)),
        out_shape=jax.ShapeDtypeStruct((bs, inner), F32),
        compiler_params=_params("arbitrary"),
        name="ssd_sample_finish")(yloc, yi.reshape(bs, inner), dae, u, nw)
    return y, conv, new_state


def _mlstm_sample_prep_body(q_ref, k_ref, gi_ref, gf_ref, cst_ref, cw_ref, cb_ref, bi_ref, bf_ref, m_ref,
                            conv_ref, qk_ref, kwt_ref, wi_ref, wf_ref, mt_ref, *, heads, hd):
    inner = heads * hd
    cch = 2 * inner
    x = jnp.concatenate([q_ref[...], k_ref[...]], axis=-1)
    acc = cb_ref[...] + x * cw_ref[CONV_K - 1:CONV_K, :]
    for j in range(CONV_K - 1):
        acc = acc + cst_ref[:, j * cch:(j + 1) * cch] * cw_ref[j:j + 1, :]
    for j in range(CONV_K - 2):
        conv_ref[:, j * cch:(j + 1) * cch] = cst_ref[:, (j + 1) * cch:(j + 2) * cch]
    conv_ref[:, (CONV_K - 2) * cch:(CONV_K - 1) * cch] = x
    act = _silu(acc)
    q = act[:, 0:inner]
    k = act[:, inner:] * (hd ** -0.5)
    it = gi_ref[...] + bi_ref[...]
    inter = -_softplus(-(gf_ref[...] + bf_ref[...])) + m_ref[...]
    mt = jnp.maximum(inter, it)
    wi = jnp.exp(it - mt)
    wf = jnp.exp(inter - mt)
    wi_ref[...] = wi
    wf_ref[...] = wf
    mt_ref[...] = mt
    qk_ref[:, 0:inner] = q
    qk_ref[:, inner:] = k
    kw = jnp.concatenate([k[:, h * hd:(h + 1) * hd] * wi[:, h:h + 1] for h in range(heads)], axis=-1)
    kwt_ref[...] = kw.T


def _mlstm_sample_state_body(wf_ref, c_ref, qk_ref, v_ref, kwt_ref, co_ref, qc_ref, *, heads, hd):
    b = pl.program_id(0)
    bs = v_ref.shape[0]
    rid = lax.broadcasted_iota(jnp.int32, (bs, hd), 0)
    parts = []
    for h in range(heads):
        cp = c_ref[0, h]
        qrow = _row_at(qk_ref, b, slice(h * hd, (h + 1) * hd))
        parts.append(_dot(jnp.broadcast_to(qrow, (8, hd)), cp)[0:1, :])
        mh = jnp.where(rid == b, v_ref[:, h * hd:(h + 1) * hd], 0.0)
        co_ref[0, h] = cp * wf_ref[b, h] + _dot(kwt_ref[h * hd:(h + 1) * hd, :], mh)
    qc_ref[0] = jnp.concatenate(parts, axis=-1)


def _mlstm_sample_finish_body(qk_ref, v_ref, o_ref, z_ref, qc_ref, n_ref, wi_ref, wf_ref, mt_ref, nw_ref,
                              hz_ref, no_ref, *, heads, hd):
    inner = heads * hd
    for h in range(heads):
        sl = slice(h * hd, (h + 1) * hd)
        q = qk_ref[:, sl]
        k = qk_ref[:, inner + h * hd:inner + (h + 1) * hd]
        wi = wi_ref[:, h:h + 1]
        wf = wf_ref[:, h:h + 1]
        mt = mt_ref[:, h:h + 1]
        n_prev = n_ref[:, sl]
        att = wi * jnp.sum(q * k, axis=-1, keepdims=True)
        num = att * v_ref[:, sl] + wf * qc_ref[:, sl]
        den = att + wf * jnp.sum(q * n_prev, axis=-1, keepdims=True)
        hh = num / jnp.maximum(jnp.abs(den), jnp.exp(-mt))
        hg = _sigmoid(o_ref[:, sl]) * hh
        mu = jnp.mean(hg, axis=-1, keepdims=True)
        var = jnp.mean(jnp.square(hg - mu), axis=-1, keepdims=True)
        hn = (hg - mu) * lax.rsqrt(var + EPS) * nw_ref[:, sl]
        hz_ref[:, sl] = hn * _silu(z_ref[:, sl])
        no_ref[:, sl] = wf * n_prev + wi * k


def _mlstm_sample(u, cst, c0, n0, m0p, cw, cb, bi, bf, nw, *, heads, hd, col_gates):
    bs = u.shape[0]
    inner = heads * hd
    ncs = (CONV_K - 1) * 2 * inner
    full = lambda a: pl.BlockSpec(a.shape, lambda i: (0,) * a.ndim)
    ucol = lambda j: pl.BlockSpec((bs, inner), lambda i, j=j: (0, j))
    gcol = lambda j: pl.BlockSpec((bs, LANE), lambda i, j=j: (0, col_gates // LANE + j))
    tile = jax.ShapeDtypeStruct((bs, LANE), F32)
    conv, qk, kwt, wi, wf, mt = pl.pallas_call(
        functools.partial(_mlstm_sample_prep_body, heads=heads, hd=hd),
        grid=(1,),
        in_specs=[ucol(0), ucol(1), gcol(0), gcol(1), full(cst), full(cw), full(cb), full(bi), full(bf),
                  full(m0p)],
        out_specs=[pl.BlockSpec((bs, ncs), lambda i: (0, 0)),
                   pl.BlockSpec((bs, 2 * inner), lambda i: (0, 0)),
                   pl.BlockSpec((inner, bs), lambda i: (0, 0)),
                   pl.BlockSpec((bs, LANE), lambda i: (0, 0)),
                   pl.BlockSpec((bs, LANE), lambda i: (0, 0)),
                   pl.BlockSpec((bs, LANE), lambda i: (0, 0))],
        out_shape=[jax.ShapeDtypeStruct((bs, ncs), F32),
                   jax.ShapeDtypeStruct((bs, 2 * inner), F32),
                   jax.ShapeDtypeStruct((inner, bs), F32),
                   tile, tile, tile],
        compiler_params=_params("arbitrary"),
        name="mlstm_sample_prep")(u, u, u, u, cst, cw, cb, bi, bf, m0p)
    c_new, qc = pl.pallas_call(
        functools.partial(_mlstm_sample_state_body, heads=heads, hd=hd),
        grid=(bs,),
        in_specs=[pl.BlockSpec(memory_space=pltpu.SMEM),
                  pl.BlockSpec((1, heads, hd, hd), lambda i: (i, 0, 0, 0)),
                  full(qk), pl.BlockSpec((bs, inner), lambda i: (0, 2)), full(kwt)],
        out_specs=[pl.BlockSpec((1, heads, hd, hd), lambda i: (i, 0, 0, 0)),
                   pl.BlockSpec((1, 1, inner), lambda i: (i, 0, 0))],
        out_shape=[jax.ShapeDtypeStruct((bs, heads, hd, hd), F32),
                   jax.ShapeDtypeStruct((bs, 1, inner), F32)],
        compiler_params=_params("arbitrary"),
        name="mlstm_sample_state")(wf[:, :heads], c0, qk, u, kwt)
    hz, n_new = pl.pallas_call(
        functools.partial(_mlstm_sample_finish_body, heads=heads, hd=hd),
        grid=(1,),
        in_specs=[full(qk), ucol(2), ucol(3), ucol(4), pl.BlockSpec((bs, inner), lambda i: (0, 0)),
                  full(n0), full(wi), full(wf), full(mt), full(nw)],
        out_specs=[pl.BlockSpec((bs, inner), lambda i: (0, 0)),
                   pl.BlockSpec((bs, inner), lambda i: (0, 0))],
        out_shape=[jax.ShapeDtypeStruct((bs, inner), F32)] * 2,
        compiler_params=_params("arbitrary"),
        name="mlstm_sample_finish")(qk, u, u, u, qc.reshape(bs, inner), n0, wi, wf, mt, nw)
    return hz, conv, c_new, n_new, mt[:, :heads]


def _pad_cols(w, n):
    return jnp.pad(w, ((0, 0), (0, n - w.shape[1])))


def _row(v, n=None):
    v = v.reshape(1, -1)
    return v if n is None else _pad_cols(v, n)


def kernel(x_prompt, x_sample, cache_attn_k, cache_attn_v, state_ssd_conv, state_ssd, state_mlstm_conv, state_mlstm_c, state_mlstm_n, state_mlstm_m, norm_w, final_norm_w, w_in_even, w_out_even, ssd_conv_w, ssd_conv_b, ssd_dt_bias, ssd_a_log, ssd_d, ssd_norm_w, w_in_odd, w_out_odd, mlstm_conv_w, mlstm_conv_b, mlstm_igate_b, mlstm_fgate_b, mlstm_norm_w):
    batch, seq, d_model = x_prompt.shape
    bs = x_sample.shape[0]
    ssd_heads = ssd_a_log.shape[1]
    ssd_inner = ssd_heads * SSD_HEADDIM
    ssd_cch = ssd_conv_w.shape[2]
    m_inner = mlstm_norm_w.shape[1]
    m_hd = m_inner // MLSTM_HEADS
    mp = batch * seq

    col_z = 4 * ATT_W
    col_xbc = col_z + ssd_inner
    col_dt = col_xbc + ssd_cch
    n_even = col_dt + LANE
    w_in0 = _pad_cols(w_in_even[0], n_even).astype(BF16)
    w_out0 = w_out_even[0].astype(BF16)
    nw0 = _row(norm_w[0])
    cw0, cb0 = ssd_conv_w[0], _row(ssd_conv_b[0])
    dtb = _row(ssd_dt_bias[0], LANE)
    alog = _row(ssd_a_log[0], LANE)
    aloge = _row(jnp.repeat(ssd_a_log[0], SSD_HEADDIM))
    dsk = _row(jnp.repeat(ssd_d[0], SSD_HEADDIM))
    snw = _row(ssd_norm_w[0])

    hp = x_prompt.reshape(mp, d_model)
    hs = x_sample.reshape(bs, d_model)

    up = _norm_matmul(hp, nw0, w_in0, tm=256, panels=1)
    cos_p, sa_p, sb_p = _rope_tables(jnp.arange(seq), LANE)
    att_p, k_p, v_p = _attn_prompt(up, cos_p, sa_p, sb_p, batch=batch, seq=seq)
    y_p, conv_p, st_p = _ssd_prompt(up, cw0, cb0, dtb, alog, dsk, snw, batch=batch, seq=seq,
                                    heads=ssd_heads, col_z=col_z, col_xbc=col_xbc, col_dt=col_dt)
    hp = _out_proj(hp, [att_p, y_p], [w_out0[:ATT_W], w_out0[ATT_W:]], tm=512)

    us = _norm_matmul(hs, nw0, w_in0, tm=bs, panels=1)
    cos_s, sa_s, sb_s = _rope_tables(PAST_LEN + jnp.arange(1), ATT_W)
    xt_s = _rot_sample(us, cos_s, sa_s, sb_s)
    k_s = xt_s[ATT_W:2 * ATT_W].T
    v_s = xt_s[2 * ATT_W:3 * ATT_W].T
    pos_minor = lambda cache: jnp.transpose(cache, (0, 2, 3, 1))
    att_s = _attn_sample(xt_s, pos_minor(cache_attn_k[0]), pos_minor(cache_attn_v[0]), bb=2).T
    y_s, conv_s, st_s = _ssd_sample(us, state_ssd_conv[0].reshape(bs, -1),
                                    state_ssd[0].reshape(bs, ssd_inner, SSD_STATE),
                                    cw0, cb0, dtb, alog, aloge, dsk, snw, heads=ssd_heads,
                                    col_z=col_z, col_xbc=col_xbc, col_dt=col_dt, bb=4)
    hs = _out_proj(hs, [att_s, y_s], [w_out0[:ATT_W], w_out0[ATT_W:]], tm=bs)

    wo = w_in_odd[0]
    gates_at = 4 * m_inner
    zcol = gates_at + 2 * MLSTM_HEADS
    col_gates = 5 * m_inner
    zpad = jnp.zeros((d_model, LANE - MLSTM_HEADS), wo.dtype)
    w_in1 = jnp.concatenate([wo[:, :gates_at], wo[:, zcol:],
                             wo[:, gates_at:gates_at + MLSTM_HEADS], zpad,
                             wo[:, gates_at + MLSTM_HEADS:zcol], zpad], axis=1).astype(BF16)
    w_out1 = w_out_odd[0].astype(BF16)
    nw1 = _row(norm_w[1])
    cw1, cb1 = mlstm_conv_w[0], _row(mlstm_conv_b[0])
    bi = _row(mlstm_igate_b[0], LANE)
    bf = _row(mlstm_fgate_b[0], LANE)
    mnw = _row(mlstm_norm_w[0])
    fnw = _row(final_norm_w)

    up1 = _norm_matmul(hp, nw1, w_in1, tm=256, panels=2)
    hz_p, cq_p, ck_p, c_p, n_p, m_p = _mlstm_prompt(up1, cw1, cb1, bi, bf, mnw, batch=batch, seq=seq,
                                                   heads=MLSTM_HEADS, hd=m_hd, col_gates=col_gates)
    y_prompt = _out_proj(hp, [hz_p], [w_out1], fnw, tm=512)

    us1 = _norm_matmul(hs, nw1, w_in1, tm=bs, panels=2)
    m0p = _pad_cols(state_mlstm_m[0], LANE)
    hz_s, mconv_s, c_s, n_s, m_s = _mlstm_sample(us1, state_mlstm_conv[0].reshape(bs, -1),
                                                 state_mlstm_c[0], state_mlstm_n[0].reshape(bs, m_inner),
                                                 m0p, cw1, cb1, bi, bf, mnw,
                                                 heads=MLSTM_HEADS, hd=m_hd, col_gates=col_gates)
    y_sample = _out_proj(hs, [hz_s], [w_out1], fnw, tm=bs)

    tmax = min(seq, CHUNK * max(DILATIONS))
    kv_shape = (1, batch, seq, ATT_HEADS, ATT_HD)
    return (
        y_prompt.reshape(batch, seq, d_model),
        y_sample.reshape(bs, 1, d_model),
        k_p.reshape(kv_shape)[:, :, seq - tmax:],
        v_p.reshape(kv_shape)[:, :, seq - tmax:],
        conv_p[None],
        st_p.reshape(1, batch, ssd_heads, SSD_HEADDIM, SSD_STATE),
        jnp.concatenate([cq_p, ck_p], axis=-1)[None],
        c_p[None],
        n_p.reshape(1, batch, MLSTM_HEADS, m_hd),
        m_p.reshape(1, batch, MLSTM_HEADS),
        k_s.reshape(1, bs, 1, ATT_HEADS, ATT_HD),
        v_s.reshape(1, bs, 1, ATT_HEADS, ATT_HD),
        conv_s.reshape(1, bs, CONV_K - 1, ssd_cch),
        st_s.reshape(1, bs, ssd_heads, SSD_HEADDIM, SSD_STATE),
        mconv_s.reshape(1, bs, CONV_K - 1, 2 * m_inner),
        c_s[None],
        n_s.reshape(1, bs, MLSTM_HEADS, m_hd),
        m_s.reshape(1, bs, MLSTM_HEADS),
    )
```

```python
import functools

import jax
import jax.numpy as jnp
from jax import lax
from jax.experimental import pallas as pl
from jax.experimental.pallas import tpu as pltpu

F32 = jnp.float32
BF16 = jnp.bfloat16
NEG_INF = float("-inf")
EPS = 1e-6
LANE = 128
CHUNK = 128
VMEM_LIMIT = 56 * 1024 * 1024

CONV_K = 4
ATT_HEADS = 8
ATT_HD = 64
ATT_W = ATT_HEADS * ATT_HD
ROT_DIM = ATT_HD // 4
ROPE_THETA = 500000.0
DILATIONS = (1, 4, 16)
PAST_LEN = 2048
SSD_HEADDIM = 64
SSD_GROUPS = 2
SSD_STATE = 128
MLSTM_HEADS = 8


def _params(*sem):
    return pltpu.CompilerParams(dimension_semantics=sem, vmem_limit_bytes=VMEM_LIMIT)


def _chunks(n, w):
    out, c = [], 0
    while c < n:
        out.append((c, min(w, n - c)))
        c += w
    return out


def _dot(a, b):
    return jnp.dot(a.astype(BF16), b.astype(BF16), preferred_element_type=F32)


def _dot_nt(a, b):
    return lax.dot_general(a.astype(BF16), b.astype(BF16), (((1,), (1,)), ((), ())),
                           preferred_element_type=F32)


def _dot_tn(a, b):
    return lax.dot_general(a.astype(BF16), b.astype(BF16), (((0,), (0,)), ((), ())),
                           preferred_element_type=F32)


def _split3(x):
    hi = x.astype(BF16)
    r1 = x - hi.astype(F32)
    mid = r1.astype(BF16)
    lo = (r1 - mid.astype(F32)).astype(BF16)
    return hi, mid, lo


def _sel_dot(sel, x):
    hi, mid, lo = _split3(x)
    d = lambda p: jnp.dot(sel, p, preferred_element_type=F32)
    return d(hi) + d(mid) + d(lo)


def _dot_sel(x, sel):
    hi, mid, lo = _split3(x)
    d = lambda p: jnp.dot(p, sel, preferred_element_type=F32)
    return d(hi) + d(mid) + d(lo)


def _tril(n):
    r = lax.broadcasted_iota(jnp.int32, (n, n), 0)
    c = lax.broadcasted_iota(jnp.int32, (n, n), 1)
    return r >= c


def _seg_matrix(rows, cols, seg, along_rows):
    r = lax.broadcasted_iota(jnp.int32, (rows, cols), 0)
    c = lax.broadcasted_iota(jnp.int32, (rows, cols), 1)
    m = (r // seg == c) if along_rows else (c // seg == r)
    return m.astype(BF16)


def _row_at(ref, b, cols=slice(None)):
    base = pl.multiple_of((b // 8) * 8, 8)
    tile = ref[pl.ds(base, 8), cols]
    sub = lax.broadcasted_iota(jnp.int32, tile.shape, 0)
    return jnp.sum(jnp.where(sub == b % 8, tile, 0.0), axis=0, keepdims=True)


def _silu(x):
    return x * (1.0 / (1.0 + jnp.exp(-x)))


def _sigmoid(x):
    return 1.0 / (1.0 + jnp.exp(-x))


def _softplus(x):
    return jnp.maximum(x, 0.0) + jnp.log1p(jnp.exp(-jnp.abs(x)))


def _rms(x, w):
    return x * lax.rsqrt(jnp.mean(x * x, axis=-1, keepdims=True) + EPS) * w


def _norm_matmul_body(x_ref, nw_ref, w_ref, o_ref, *, chunks):
    xn = _rms(x_ref[...], nw_ref[...]).astype(BF16)
    for c0, cw in chunks:
        o_ref[:, c0:c0 + cw] = jnp.dot(xn, w_ref[:, c0:c0 + cw], preferred_element_type=F32)


def _norm_matmul(x, nw, w, *, tm, panels):
    m, d = x.shape
    n = w.shape[1]
    pn = n // panels
    return pl.pallas_call(
        functools.partial(_norm_matmul_body, chunks=_chunks(pn, 512)),
        grid=(panels, m // tm),
        in_specs=[pl.BlockSpec((tm, d), lambda p, i: (i, 0)),
                  pl.BlockSpec((1, d), lambda p, i: (0, 0)),
                  pl.BlockSpec((d, pn), lambda p, i: (0, p))],
        out_specs=pl.BlockSpec((tm, pn), lambda p, i: (i, p)),
        out_shape=jax.ShapeDtypeStruct((m, n), F32),
        compiler_params=_params("arbitrary", "arbitrary"),
        name="norm_matmul")(x, nw, w)


def _out_proj_body(*refs, n_in, final):
    h_ref = refs[0]
    xs = refs[1:1 + n_in]
    ws = refs[1 + n_in:1 + 2 * n_in]
    rest = refs[1 + 2 * n_in:]
    acc = h_ref[...]
    for x_ref, w_ref in zip(xs, ws):
        acc = acc + jnp.dot(x_ref[...].astype(BF16), w_ref[...], preferred_element_type=F32)
    if final:
        fw_ref, o_ref = rest
        o_ref[...] = _rms(acc, fw_ref[...])
    else:
        (o_ref,) = rest
        o_ref[...] = acc


def _out_proj(h, xs, ws, fw=None, *, tm):
    m, d = h.shape
    n_in = len(xs)
    in_specs = [pl.BlockSpec((tm, d), lambda i: (i, 0))]
    in_specs += [pl.BlockSpec((tm, x.shape[1]), lambda i: (i, 0)) for x in xs]
    in_specs += [pl.BlockSpec(w.shape, lambda i: (0, 0)) for w in ws]
    args = [h, *xs, *ws]
    if fw is not None:
        in_specs.append(pl.BlockSpec((1, d), lambda i: (0, 0)))
        args.append(fw)
    return pl.pallas_call(
        functools.partial(_out_proj_body, n_in=n_in, final=fw is not None),
        grid=(m // tm,),
        in_specs=in_specs,
        out_specs=pl.BlockSpec((tm, d), lambda i: (i, 0)),
        out_shape=jax.ShapeDtypeStruct((m, d), F32),
        compiler_params=_params("arbitrary"),
        name="out_proj")(*args)


def _rope_tables(pos, width):
    half = ROT_DIM // 2
    inv = jnp.power(F32(ROPE_THETA), -jnp.arange(half, dtype=F32) * (2.0 / ROT_DIM))
    ang = pos.astype(F32)[:, None] * inv[None, :]
    cos, sin = jnp.cos(ang), jnp.sin(ang)
    n = pos.shape[0]
    one = jnp.ones((n, ATT_HD - ROT_DIM), F32)
    z8 = jnp.zeros((n, half), F32)
    z48 = jnp.zeros((n, ATT_HD - ROT_DIM), F32)
    c = jnp.concatenate([cos, cos, one], axis=-1)
    sa = jnp.concatenate([-sin, z8, z48], axis=-1)
    sb = jnp.concatenate([z8, sin, z48], axis=-1)
    rep = width // ATT_HD
    return tuple(jnp.tile(t, (1, rep)) for t in (c, sa, sb))


def _rotary(x, c, sa, sb):
    w = x.shape[-1]
    half = ROT_DIM // 2
    return x * c + pltpu.roll(x, w - half, 1) * sa + pltpu.roll(x, half, 1) * sb


def _attn_prompt_body(q_ref, k_ref, v_ref, g_ref, c_ref, sa_ref, sb_ref,
                      att_ref, ko_ref, vo_ref,
                      qd, kd, vd, od, mld, o_s, ml_s, *, seq):
    nblk = seq // CHUNK
    c, sa, sb = c_ref[...], sa_ref[...], sb_ref[...]
    q = _rotary(q_ref[...], c, sa, sb) * (ATT_HD ** -0.5)
    k = _rotary(k_ref[...], c, sa, sb)
    ko_ref[...] = k
    vo_ref[...] = v_ref[...]
    zero = jnp.zeros((CHUNK, LANE), F32)
    for p, d in enumerate(DILATIONS):
        kd[p, 0:CHUNK, :] = zero
        vd[p, 0:CHUNK, :] = zero
    mld[...] = jnp.zeros(mld.shape, F32)
    qd[0] = q
    kd[0, CHUNK:, :] = k
    vd[0, CHUNK:, :] = v_ref[...]
    for p, d in enumerate(DILATIONS):
        if d == 1:
            continue
        ln = seq // d
        for r in range(d):
            qd[p, r * ln:(r + 1) * ln, :] = qd[0, pl.ds(r, ln, stride=d), :]
            kd[p, CHUNK + r * ln:CHUNK + (r + 1) * ln, :] = kd[0, pl.ds(CHUNK + r, ln, stride=d), :]
            vd[p, CHUNK + r * ln:CHUNK + (r + 1) * ln, :] = vd[0, pl.ds(CHUNK + r, ln, stride=d), :]

    row = lax.broadcasted_iota(jnp.int32, (CHUNK, 2 * CHUNK), 0)
    col = lax.broadcasted_iota(jnp.int32, (CHUNK, 2 * CHUNK), 1)
    band = (col >= row) & (col <= row + CHUNK)

    for p, d in enumerate(DILATIONS):
        nb = nblk // d

        def block(t, carry, p=p, nb=nb):
            base = pl.multiple_of(t * CHUNK, CHUNK)
            first = (t % nb) == 0
            valid = band & (col >= jnp.where(first, CHUNK, 0))
            for hh in range(2):
                lo = ATT_HD * hh
                qb = qd[p, pl.ds(base, CHUNK), lo:lo + ATT_HD]
                kw = kd[p, pl.ds(base, 2 * CHUNK), lo:lo + ATT_HD]
                vw = vd[p, pl.ds(base, 2 * CHUNK), lo:lo + ATT_HD]
                s = jnp.where(valid, _dot_nt(qb, kw), NEG_INF)
                m = jnp.max(s, axis=-1, keepdims=True)
                e = jnp.exp(s - m)
                od[pl.ds(base, CHUNK), lo:lo + ATT_HD] = _dot(e, vw)
                mld[pl.ds(base, CHUNK), hh:hh + 1] = m
                mld[pl.ds(base, CHUNK), 2 + hh:3 + hh] = jnp.sum(e, axis=-1, keepdims=True)
            return carry

        lax.fori_loop(0, nblk, block, 0, unroll=4)
        if d == 1:
            o_s[p] = od[...]
            ml_s[p] = mld[...]
        else:
            ln = seq // d
            for r in range(d):
                o_s[p, pl.ds(r, ln, stride=d), :] = od[r * ln:(r + 1) * ln, :]
                ml_s[p, pl.ds(r, ln, stride=d), :] = mld[r * ln:(r + 1) * ln, :]

    np_ = len(DILATIONS)

    def combine(t, carry):
        base = pl.multiple_of(t * CHUNK, CHUNK)
        rows = pl.ds(base, CHUNK)
        for hh in range(2):
            lo = ATT_HD * hh
            ms = [ml_s[p, rows, hh:hh + 1] for p in range(np_)]
            ls = [ml_s[p, rows, 2 + hh:3 + hh] for p in range(np_)]
            mx = functools.reduce(jnp.maximum, ms)
            ws = [jnp.exp(mm - mx) for mm in ms]
            num = sum(w * o_s[p, rows, lo:lo + ATT_HD] for p, w in enumerate(ws))
            den = sum(w * l for w, l in zip(ws, ls))
            att_ref[rows, lo:lo + ATT_HD] = (num / den) * _silu(g_ref[rows, lo:lo + ATT_HD])
        return carry

    lax.fori_loop(0, nblk, combine, 0)


def _attn_prompt(u, cos, sa, sb, *, batch, seq):
    m = batch * seq
    npair = ATT_W // LANE
    blk = lambda off: pl.BlockSpec((seq, LANE), lambda b, hp, off=off: (b, off + hp))
    tab = pl.BlockSpec((seq, LANE), lambda b, hp: (0, 0))
    out = pl.BlockSpec((seq, LANE), lambda b, hp: (b, hp))
    np_ = len(DILATIONS)
    return pl.pallas_call(
        functools.partial(_attn_prompt_body, seq=seq),
        grid=(batch, npair),
        in_specs=[blk(0), blk(npair), blk(2 * npair), blk(3 * npair), tab, tab, tab],
        out_specs=[out, out, out],
        out_shape=[jax.ShapeDtypeStruct((m, ATT_W), F32)] * 3,
        scratch_shapes=[pltpu.VMEM((np_, seq, LANE), F32),
                        pltpu.VMEM((np_, seq + CHUNK, LANE), F32),
                        pltpu.VMEM((np_, seq + CHUNK, LANE), F32),
                        pltpu.VMEM((seq, LANE), F32),
                        pltpu.VMEM((seq, LANE), F32),
                        pltpu.VMEM((np_, seq, LANE), F32),
                        pltpu.VMEM((np_, seq, LANE), F32)],
        compiler_params=_params("arbitrary", "arbitrary"),
        name="attn_prompt")(u, u, u, u, cos, sa, sb)


def _ssd_prompt_body(xbc_ref, z_ref, dt_ref, cw_ref, cb_ref, dtb_ref, alog_ref, dsk_ref, nw_ref,
                     y_ref, conv_ref, st_ref, ext, st, ys, *, heads, nchunk):
    c = pl.program_id(1)
    inner = heads * SSD_HEADDIM
    gw = SSD_STATE
    hpg = heads // SSD_GROUPS

    @pl.when(c == 0)
    def _():
        ext[0:8, :] = jnp.zeros((8, ext.shape[1]), F32)
        st[...] = jnp.zeros(st.shape, F32)

    ext[8:8 + CHUNK, :] = xbc_ref[...]
    conv = cb_ref[...] + ext[5:5 + CHUNK, :] * cw_ref[0:1, :]
    for j in range(1, CONV_K):
        conv = conv + ext[5 + j:5 + j + CHUNK, :] * cw_ref[j:j + 1, :]
    act = _silu(conv)

    @pl.when(c == nchunk - 1)
    def _():
        conv_ref[...] = ext[CHUNK + 5:CHUNK + 8, :]

    ext[0:8, :] = ext[CHUNK:CHUNK + 8, :]

    dt = _softplus(dt_ref[...] + dtb_ref[...])
    a = -jnp.exp(alog_ref[...])
    tril = _tril(CHUNK)
    acum = _sel_dot(tril.astype(BF16), dt * a)
    acum_t = acum.T
    dt_t = dt.T
    eacum = jnp.exp(acum)
    last = acum[CHUNK - 1:CHUNK, :]
    wend = jnp.exp(last - acum) * dt
    elast = jnp.exp(last)

    for g in range(SSD_GROUPS):
        bm = act[:, inner + g * gw:inner + (g + 1) * gw]
        cm = act[:, inner + SSD_GROUPS * gw + g * gw:inner + SSD_GROUPS * gw + (g + 1) * gw]
        cb = _dot_nt(cm, bm)
        for hg in range(hpg):
            h = g * hpg + hg
            lo = h * SSD_HEADDIM
            xh = act[:, lo:lo + SSD_HEADDIM]
            seg = acum[:, h:h + 1] - acum_t[h:h + 1, :]
            decay = jnp.exp(jnp.where(tril, seg, NEG_INF))
            sh = st[lo:lo + SSD_HEADDIM, :]
            y = _dot(cb * decay * dt_t[h:h + 1, :], xh)
            y = y + _dot_nt(cm, sh) * eacum[:, h:h + 1]
            ys[:, lo:lo + SSD_HEADDIM] = y
            st[lo:lo + SSD_HEADDIM, :] = sh * elast[:, h:h + 1] + _dot_tn(xh, bm * wend[:, h:h + 1])

    xs = act[:, 0:inner]
    yt = (ys[...] + dsk_ref[...] * xs) * _silu(z_ref[...])
    y_ref[...] = _rms(yt, nw_ref[...])

    @pl.when(c == nchunk - 1)
    def _():
        st_ref[...] = st[...]


def _ssd_prompt(u, cw, cb, dtb, alog, dsk, nw, *, batch, seq, heads, col_z, col_xbc, col_dt):
    nchunk = seq // CHUNK
    inner = heads * SSD_HEADDIM
    cch = cw.shape[1]
    row = lambda b, c: b * nchunk + c
    full = lambda a: pl.BlockSpec(a.shape, lambda b, c: (0, 0))
    return pl.pallas_call(
        functools.partial(_ssd_prompt_body, heads=heads, nchunk=nchunk),
        grid=(batch, nchunk),
        in_specs=[pl.BlockSpec((CHUNK, cch), lambda b, c: (row(b, c), col_xbc // cch)),
                  pl.BlockSpec((CHUNK, inner), lambda b, c: (row(b, c), col_z // inner)),
                  pl.BlockSpec((CHUNK, LANE), lambda b, c: (row(b, c), col_dt // LANE)),
                  full(cw), full(cb), full(dtb), full(alog), full(dsk), full(nw)],
        out_specs=[pl.BlockSpec((CHUNK, inner), lambda b, c: (row(b, c), 0)),
                   pl.BlockSpec((None, CONV_K - 1, cch), lambda b, c: (b, 0, 0)),
                   pl.BlockSpec((None, inner, SSD_STATE), lambda b, c: (b, 0, 0))],
        out_shape=[jax.ShapeDtypeStruct((batch * seq, inner), F32),
                   jax.ShapeDtypeStruct((batch, CONV_K - 1, cch), F32),
                   jax.ShapeDtypeStruct((batch, inner, SSD_STATE), F32)],
        scratch_shapes=[pltpu.VMEM((CHUNK + 8, cch), F32),
                        pltpu.VMEM((inner, SSD_STATE), F32),
                        pltpu.VMEM((CHUNK, inner), F32)],
        compiler_params=_params("arbitrary", "arbitrary"),
        name="ssd_prompt")(u, u, u, cw, cb, dtb, alog, dsk, nw)


def _mlstm_prompt_body(q_ref, k_ref, v_ref, o_ref, z_ref, gi_ref, gf_ref,
                       cw_ref, cb_ref, bi_ref, bf_ref, nw_ref,
                       hz_ref, conv_ref, c_out, n_out, m_out,
                       extq, extk, qa, ka, c_s, n_s, m_s, *, nchunk, heads, hd):
    c = pl.program_id(1)
    inner = heads * hd

    @pl.when(c == 0)
    def _():
        extq[0:8, :] = jnp.zeros((8, inner), F32)
        extk[0:8, :] = jnp.zeros((8, inner), F32)
        c_s[...] = jnp.zeros(c_s.shape, F32)
        n_s[...] = jnp.zeros(n_s.shape, F32)
        m_s[...] = jnp.full(m_s.shape, NEG_INF, F32)

    def conv(ext, x_ref, col0, out, scale):
        ext[8:8 + CHUNK, :] = x_ref[...]
        cols = slice(col0, col0 + inner)
        acc = cb_ref[:, cols] + ext[5:5 + CHUNK, :] * cw_ref[0:1, cols]
        for j in range(1, CONV_K):
            acc = acc + ext[5 + j:5 + j + CHUNK, :] * cw_ref[j:j + 1, cols]
        out[...] = _silu(acc) * scale

        @pl.when(c == nchunk - 1)
        def _():
            conv_ref[:, cols] = ext[CHUNK + 5:CHUNK + 8, :]

        ext[0:8, :] = ext[CHUNK:CHUNK + 8, :]

    conv(extq, q_ref, 0, qa, 1.0)
    conv(extk, k_ref, inner, ka, hd ** -0.5)

    it = gi_ref[...] + bi_ref[...]
    logf = -_softplus(-(gf_ref[...] + bf_ref[...]))
    tril = _tril(CHUNK)
    bc = _sel_dot(tril.astype(BF16), logf)
    it_t = it.T
    bc_t = bc.T

    for h in range(heads):
        sl = slice(h * hd, (h + 1) * hd)
        q, k, v = qa[:, sl], ka[:, sl], v_ref[:, sl]
        i_col, b_col = it[:, h:h + 1], bc[:, h:h + 1]
        i_row, b_row = it_t[h:h + 1, :], bc_t[h:h + 1, :]
        m_prev = m_s[h]
        dmat = jnp.where(tril, b_col - b_row + i_row, NEG_INF)
        inter = b_col + m_prev
        m_t = jnp.maximum(inter, jnp.max(dmat, axis=-1, keepdims=True))
        w_intra = jnp.exp(dmat - m_t)
        w_inter = jnp.exp(inter - m_t)
        att = w_intra * _dot_nt(q, k)
        c_prev = c_s[h]
        n_prev = n_s[h]
        num = _dot(att, v) + w_inter * _dot(q, c_prev)
        den = jnp.sum(att, axis=-1, keepdims=True) + w_inter * jnp.sum(q * n_prev, axis=-1, keepdims=True)
        hh = num / jnp.maximum(jnp.abs(den), jnp.exp(-m_t))

        b_last = b_col[CHUNK - 1:CHUNK, :]
        logw = b_last - b_col + i_col
        m_new = jnp.maximum(b_last + m_prev, jnp.max(logw, axis=0, keepdims=True))
        ws = jnp.exp(logw - m_new)
        scale = jnp.exp(b_last + m_prev - m_new)
        c_s[h] = scale * c_prev + _dot_tn(k, ws * v)
        n_s[h] = scale * n_prev + jnp.sum(ws * k, axis=0, keepdims=True)
        m_s[h] = m_new

        hg = _sigmoid(o_ref[:, sl]) * hh
        mu = jnp.mean(hg, axis=-1, keepdims=True)
        var = jnp.mean(jnp.square(hg - mu), axis=-1, keepdims=True)
        hn = (hg - mu) * lax.rsqrt(var + EPS) * nw_ref[:, sl]
        hz_ref[:, sl] = hn * _silu(z_ref[:, sl])

    @pl.when(c == nchunk - 1)
    def _():
        c_out[...] = c_s[...]
        n_out[...] = n_s[...]
        m_out[...] = m_s[...]


def _mlstm_prompt(u, cw, cb, bi, bf, nw, *, batch, seq, heads, hd, col_gates):
    nchunk = seq // CHUNK
    inner = heads * hd
    row = lambda b, c: b * nchunk + c
    ublk = lambda j: pl.BlockSpec((CHUNK, inner), lambda b, c, j=j: (row(b, c), j))
    gblk = lambda off: pl.BlockSpec((CHUNK, LANE), lambda b, c, off=off: (row(b, c), col_gates // LANE + off))
    full = lambda a: pl.BlockSpec(a.shape, lambda b, c: (0, 0))
    return pl.pallas_call(
        functools.partial(_mlstm_prompt_body, nchunk=nchunk, heads=heads, hd=hd),
        grid=(batch, nchunk),
        in_specs=[ublk(0), ublk(1), ublk(2), ublk(3), ublk(4), gblk(0), gblk(1),
                  full(cw), full(cb), full(bi), full(bf), full(nw)],
        out_specs=[pl.BlockSpec((CHUNK, inner), lambda b, c: (row(b, c), 0)),
                   pl.BlockSpec((None, CONV_K - 1, 2 * inner), lambda b, c: (b, 0, 0)),
                   pl.BlockSpec((None, heads, hd, hd), lambda b, c: (b, 0, 0, 0)),
                   pl.BlockSpec((None, heads, 1, hd), lambda b, c: (b, 0, 0, 0)),
                   pl.BlockSpec((None, heads, 1, 1), lambda b, c: (b, 0, 0, 0))],
        out_shape=[jax.ShapeDtypeStruct((batch * seq, inner), F32),
                   jax.ShapeDtypeStruct((batch, CONV_K - 1, 2 * inner), F32),
                   jax.ShapeDtypeStruct((batch, heads, hd, hd), F32),
                   jax.ShapeDtypeStruct((batch, heads, 1, hd), F32),
                   jax.ShapeDtypeStruct((batch, heads, 1, 1), F32)],
        scratch_shapes=[pltpu.VMEM((CHUNK + 8, inner), F32),
                        pltpu.VMEM((CHUNK + 8, inner), F32),
                        pltpu.VMEM((CHUNK, inner), F32),
                        pltpu.VMEM((CHUNK, inner), F32),
                        pltpu.VMEM((heads, hd, hd), F32),
                        pltpu.VMEM((heads, 1, hd), F32),
                        pltpu.VMEM((heads, 1, 1), F32)],
        compiler_params=_params("arbitrary", "arbitrary"),
        name="mlstm_prompt")(u, u, u, u, u, u, u, cw, cb, bi, bf, nw)


def _rot_sample_body(q_ref, k_ref, v_ref, g_ref, c_ref, sa_ref, sb_ref, xt_ref):
    c, sa, sb = c_ref[...], sa_ref[...], sb_ref[...]
    q = _rotary(q_ref[...], c, sa, sb) * (ATT_HD ** -0.5)
    k = _rotary(k_ref[...], c, sa, sb)
    xt_ref[...] = jnp.concatenate([q, k, v_ref[...], g_ref[...]], axis=-1).T


def _rot_sample(u, cos, sa, sb):
    bs = u.shape[0]
    ublk = lambda off: pl.BlockSpec((bs, ATT_W), lambda i, off=off: (0, off))
    tab = pl.BlockSpec((1, ATT_W), lambda i: (0, 0))
    return pl.pallas_call(
        _rot_sample_body,
        grid=(1,),
        in_specs=[ublk(0), ublk(1), ublk(2), ublk(3), tab, tab, tab],
        out_specs=pl.BlockSpec((4 * ATT_W, bs), lambda i: (0, 0)),
        out_shape=jax.ShapeDtypeStruct((4 * ATT_W, bs), F32),
        compiler_params=_params("arbitrary"),
        name="rot_sample")(u, u, u, u, cos, sa, sb)


def _key_multiplicity(wb):
    back = wb - lax.broadcasted_iota(jnp.int32, (1, wb), 1)
    cnt = jnp.zeros((1, wb), F32)
    for d in DILATIONS:
        cnt = cnt + ((back <= CHUNK * d) & (lax.rem(back, d) == 0)).astype(F32)
    return cnt


def _attn_sample_body(x_ref, kc_ref, vc_ref, att_ref, *, bb, wb):
    i = pl.program_id(0)
    bs = x_ref.shape[1]
    nt = wb // LANE

    @pl.when(i == 0)
    def _():
        att_ref[...] = jnp.zeros(att_ref.shape, F32)

    cnt = _key_multiplicity(wb)
    valid = cnt > 0.0
    npat = float(len(DILATIONS))
    rid = lax.broadcasted_iota(jnp.int32, (bs, LANE), 0)
    lid = lax.broadcasted_iota(jnp.int32, (ATT_HD, bs), 1)
    for t in range(bb):
        b = i * bb + t
        cols = _dot_sel(x_ref[...], (rid == b).astype(BF16))
        for h in range(ATT_HEADS):
            lo = h * ATT_HD
            qc = cols[lo:lo + ATT_HD]
            kc = cols[ATT_W + lo:ATT_W + lo + ATT_HD]
            vc = cols[2 * ATT_W + lo:2 * ATT_W + lo + ATT_HD]
            gc = cols[3 * ATT_W + lo:3 * ATT_W + lo + ATT_HD]
            kt = kc_ref[t, h]
            vt = vc_ref[t, h]
            s = jnp.sum(kt * jnp.tile(qc, (1, nt)), axis=0, keepdims=True)
            s = jnp.where(valid, s, NEG_INF)
            s_self = jnp.sum(qc * kc, axis=0, keepdims=True)[:, 0:1]
            m = jnp.maximum(jnp.max(s, axis=1, keepdims=True), s_self)
            p = cnt * jnp.exp(s - m)
            p_self = npat * jnp.exp(s_self - m)
            den = jnp.sum(p, axis=1, keepdims=True) + p_self
            acc = vt[:, 0:LANE] * p[:, 0:LANE]
            for j in range(1, nt):
                acc = acc + vt[:, j * LANE:(j + 1) * LANE] * p[:, j * LANE:(j + 1) * LANE]
            o = jnp.sum(acc, axis=1, keepdims=True) + p_self * vc[:, 0:1]
            o = o / den * _silu(gc[:, 0:1])
            att_ref[lo:lo + ATT_HD, :] = jnp.where(lid == b, o, att_ref[lo:lo + ATT_HD, :])


def _attn_sample(xt, ck, cv, *, bb):
    bs, wb = ck.shape[0], ck.shape[3]
    cache = pl.BlockSpec((bb, ATT_HEADS, ATT_HD, wb), lambda i: (i, 0, 0, 0))
    return pl.pallas_call(
        functools.partial(_attn_sample_body, bb=bb, wb=wb),
        grid=(bs // bb,),
        in_specs=[pl.BlockSpec(xt.shape, lambda i: (0, 0)), cache, cache],
        out_specs=pl.BlockSpec((ATT_W, bs), lambda i: (0, 0)),
        out_shape=jax.ShapeDtypeStruct((ATT_W, bs), F32),
        compiler_params=_params("arbitrary"),
        name="attn_sample")(xt, ck, cv)


def _ssd_sample_prep_body(xbc_ref, dt_ref, cst_ref, cw_ref, cb_ref, dtb_ref, alog_ref, aloge_ref, dsk_ref,
                          conv_ref, yloc_ref, xdtt_ref, bc_ref, da_ref, dae_ref, *, heads):
    inner = heads * SSD_HEADDIM
    cch = xbc_ref.shape[1]
    gw = SSD_STATE
    x = xbc_ref[...]
    acc = cb_ref[...] + x * cw_ref[CONV_K - 1:CONV_K, :]
    for j in range(CONV_K - 1):
        acc = acc + cst_ref[:, j * cch:(j + 1) * cch] * cw_ref[j:j + 1, :]
    for j in range(CONV_K - 2):
        conv_ref[:, j * cch:(j + 1) * cch] = cst_ref[:, (j + 1) * cch:(j + 2) * cch]
    conv_ref[:, (CONV_K - 2) * cch:(CONV_K - 1) * cch] = x
    act = _silu(acc)
    xs = act[:, 0:inner]
    bc_ref[...] = act[:, inner:]
    dt = _softplus(dt_ref[...] + dtb_ref[...])
    da_ref[...] = jnp.exp(dt * (-jnp.exp(alog_ref[...])))
    expand = _seg_matrix(LANE, inner, SSD_HEADDIM, False)
    dte = _dot_sel(dt, expand)
    dae_ref[...] = jnp.exp(dte * (-jnp.exp(aloge_ref[...])))
    xdt = xs * dte
    xdtt_ref[...] = xdt.T
    hw = inner // SSD_GROUPS
    parts = []
    for g in range(SSD_GROUPS):
        bm = act[:, inner + g * gw:inner + (g + 1) * gw]
        cm = act[:, inner + SSD_GROUPS * gw + g * gw:inner + SSD_GROUPS * gw + (g + 1) * gw]
        cbg = jnp.sum(cm * bm, axis=-1, keepdims=True)
        parts.append(cbg * xdt[:, g * hw:(g + 1) * hw])
    yloc_ref[...] = jnp.concatenate(parts, axis=-1) + dsk_ref[...] * xs


def _ssd_sample_state_body(da_ref, s_ref, xdtt_ref, bc_ref, so_ref, yi_ref, *, bb, heads):
    i = pl.program_id(0)
    bs = bc_ref.shape[0]
    inner = heads * SSD_HEADDIM
    hw = inner // SSD_GROUPS
    hpg = heads // SSD_GROUPS
    gw = SSD_STATE
    rid = lax.broadcasted_iota(jnp.int32, (bs, gw), 0)
    for t in range(bb):
        b = i * bb + t
        parts = []
        for g in range(SSD_GROUPS):
            mg = jnp.where(rid == b, bc_ref[:, g * gw:(g + 1) * gw], 0.0)
            sl = _dot(xdtt_ref[g * hw:(g + 1) * hw, :], mg)
            crow = _row_at(bc_ref, b, slice(SSD_GROUPS * gw + g * gw, SSD_GROUPS * gw + (g + 1) * gw))
            sg = s_ref[t, g * hw:(g + 1) * hw, :]
            parts.append(_dot_nt(jnp.broadcast_to(crow, (8, gw)), sg)[0:1, :])
            for hg in range(hpg):
                h = g * hpg + hg
                lo = hg * SSD_HEADDIM
                so_ref[t, h * SSD_HEADDIM:(h + 1) * SSD_HEADDIM, :] = (
                    sg[lo:lo + SSD_HEADDIM, :] * da_ref[b, h] + sl[lo:lo + SSD_HEADDIM, :])
        yi_ref[t] = jnp.concatenate(parts, axis=-1)


def _ssd_sample_finish_body(yloc_ref, yi_ref, dae_ref, z_ref, nw_ref, y_ref):
    y = (yloc_ref[...] + yi_ref[...] * dae_ref[...]) * _silu(z_ref[...])
    y_ref[...] = _rms(y, nw_ref[...])


def _ssd_sample(u, cst, state, cw, cb, dtb, alog, aloge, dsk, nw, *, heads, col_z, col_xbc, col_dt, bb):
    bs = u.shape[0]
    inner = heads * SSD_HEADDIM
    cch = cw.shape[1]
    ncs = (CONV_K - 1) * cch
    bcw = 2 * SSD_GROUPS * SSD_STATE
    full = lambda a: pl.BlockSpec(a.shape, lambda i: (0,) * a.ndim)
    conv, yloc, xdtt, bc, da, dae = pl.pallas_call(
        functools.partial(_ssd_sample_prep_body, heads=heads),
        grid=(1,),
        in_specs=[pl.BlockSpec((bs, cch), lambda i: (0, col_xbc // cch)),
                  pl.BlockSpec((bs, LANE), lambda i: (0, col_dt // LANE)),
                  full(cst), full(cw), full(cb), full(dtb), full(alog), full(aloge), full(dsk)],
        out_specs=[pl.BlockSpec((bs, ncs), lambda i: (0, 0)),
                   pl.BlockSpec((bs, inner), lambda i: (0, 0)),
                   pl.BlockSpec((inner, bs), lambda i: (0, 0)),
                   pl.BlockSpec((bs, bcw), lambda i: (0, 0)),
                   pl.BlockSpec((bs, LANE), lambda i: (0, 0)),
                   pl.BlockSpec((bs, inner), lambda i: (0, 0))],
        out_shape=[jax.ShapeDtypeStruct((bs, ncs), F32),
                   jax.ShapeDtypeStruct((bs, inner), F32),
                   jax.ShapeDtypeStruct((inner, bs), F32),
                   jax.ShapeDtypeStruct((bs, bcw), F32),
                   jax.ShapeDtypeStruct((bs, LANE), F32),
                   jax.ShapeDtypeStruct((bs, inner), F32)],
        compiler_params=_params("arbitrary"),
        name="ssd_sample_prep")(u, u, cst, cw, cb, dtb, alog, aloge, dsk)
    new_state, yi = pl.pallas_call(
        functools.partial(_ssd_sample_state_body, bb=bb, heads=heads),
        grid=(bs // bb,),
        in_specs=[pl.BlockSpec(memory_space=pltpu.SMEM),
                  pl.BlockSpec((bb, inner, SSD_STATE), lambda i: (i, 0, 0)),
                  full(xdtt), full(bc)],
        out_specs=[pl.BlockSpec((bb, inner, SSD_STATE), lambda i: (i, 0, 0)),
                   pl.BlockSpec((bb, 1, inner), lambda i: (i, 0, 0))],
        out_shape=[jax.ShapeDtypeStruct((bs, inner, SSD_STATE), F32),
                   jax.ShapeDtypeStruct((bs, 1, inner), F32)],
        compiler_params=_params("arbitrary"),
        name="ssd_sample_state")(da[:, :heads], state, xdtt, bc)
    y = pl.pallas_call(
        _ssd_sample_finish_body,
        grid=(1,),
        in_specs=[full(yloc), pl.BlockSpec((bs, inner), lambda i: (0, 0)), full(dae),
                  pl.BlockSpec((bs, inner), lambda i: (0, col_z // inner)), full(nw)],
        out_specs=pl.BlockSpec((bs, inner), lambda i: (0, 0)),
        out_shape=jax.ShapeDtypeStruct((bs, inner), F32),
        compiler_params=_params("arbitrary"),
        name="ssd_sample_finish")(yloc, yi.reshape(bs, inner), dae, u, nw)
    return y, conv, new_state


def _mlstm_sample_prep_body(q_ref, k_ref, gi_ref, gf_ref, cst_ref, cw_ref, cb_ref, bi_ref, bf_ref, m_ref,
                            conv_ref, qk_ref, kwt_ref, wi_ref, wf_ref, mt_ref, *, heads, hd):
    inner = heads * hd
    cch = 2 * inner
    x = jnp.concatenate([q_ref[...], k_ref[...]], axis=-1)
    acc = cb_ref[...] + x * cw_ref[CONV_K - 1:CONV_K, :]
    for j in range(CONV_K - 1):
        acc = acc + cst_ref[:, j * cch:(j + 1) * cch] * cw_ref[j:j + 1, :]
    for j in range(CONV_K - 2):
        conv_ref[:, j * cch:(j + 1) * cch] = cst_ref[:, (j + 1) * cch:(j + 2) * cch]
    conv_ref[:, (CONV_K - 2) * cch:(CONV_K - 1) * cch] = x
    act = _silu(acc)
    q = act[:, 0:inner]
    k = act[:, inner:] * (hd ** -0.5)
    it = gi_ref[...] + bi_ref[...]
    inter = -_softplus(-(gf_ref[...] + bf_ref[...])) + m_ref[...]
    mt = jnp.maximum(inter, it)
    wi = jnp.exp(it - mt)
    wf = jnp.exp(inter - mt)
    wi_ref[...] = wi
    wf_ref[...] = wf
    mt_ref[...] = mt
    qk_ref[:, 0:inner] = q
    qk_ref[:, inner:] = k
    kw = jnp.concatenate([k[:, h * hd:(h + 1) * hd] * wi[:, h:h + 1] for h in range(heads)], axis=-1)
    kwt_ref[...] = kw.T


def _mlstm_sample_state_body(wf_ref, c_ref, qk_ref, v_ref, kwt_ref, co_ref, qc_ref, *, heads, hd):
    b = pl.program_id(0)
    bs = v_ref.shape[0]
    rid = lax.broadcasted_iota(jnp.int32, (bs, hd), 0)
    parts = []
    for h in range(heads):
        cp = c_ref[0, h]
        qrow = _row_at(qk_ref, b, slice(h * hd, (h + 1) * hd))
        parts.append(_dot(jnp.broadcast_to(qrow, (8, hd)), cp)[0:1, :])
        mh = jnp.where(rid == b, v_ref[:, h * hd:(h + 1) * hd], 0.0)
        co_ref[0, h] = cp * wf_ref[b, h] + _dot(kwt_ref[h * hd:(h + 1) * hd, :], mh)
    qc_ref[0] = jnp.concatenate(parts, axis=-1)


def _mlstm_sample_finish_body(qk_ref, v_ref, o_ref, z_ref, qc_ref, n_ref, wi_ref, wf_ref, mt_ref, nw_ref,
                              hz_ref, no_ref, *, heads, hd):
    inner = heads * hd
    for h in range(heads):
        sl = slice(h * hd, (h + 1) * hd)
        q = qk_ref[:, sl]
        k = qk_ref[:, inner + h * hd:inner + (h + 1) * hd]
        wi = wi_ref[:, h:h + 1]
        wf = wf_ref[:, h:h + 1]
        mt = mt_ref[:, h:h + 1]
        n_prev = n_ref[:, sl]
        att = wi * jnp.sum(q * k, axis=-1, keepdims=True)
        num = att * v_ref[:, sl] + wf * qc_ref[:, sl]
        den = att + wf * jnp.sum(q * n_prev, axis=-1, keepdims=True)
        hh = num / jnp.maximum(jnp.abs(den), jnp.exp(-mt))
        hg = _sigmoid(o_ref[:, sl]) * hh
        mu = jnp.mean(hg, axis=-1, keepdims=True)
        var = jnp.mean(jnp.square(hg - mu), axis=-1, keepdims=True)
        hn = (hg - mu) * lax.rsqrt(var + EPS) * nw_ref[:, sl]
        hz_ref[:, sl] = hn * _silu(z_ref[:, sl])
        no_ref[:, sl] = wf * n_prev + wi * k


def _mlstm_sample(u, cst, c0, n0, m0p, cw, cb, bi, bf, nw, *, heads, hd, col_gates):
    bs = u.shape[0]
    inner = heads * hd
    ncs = (CONV_K - 1) * 2 * inner
    full = lambda a: pl.BlockSpec(a.shape, lambda i: (0,) * a.ndim)
    ucol = lambda j: pl.BlockSpec((bs, inner), lambda i, j=j: (0, j))
    gcol = lambda j: pl.BlockSpec((bs, LANE), lambda i, j=j: (0, col_gates // LANE + j))
    tile = jax.ShapeDtypeStruct((bs, LANE), F32)
    conv, qk, kwt, wi, wf, mt = pl.pallas_call(
        functools.partial(_mlstm_sample_prep_body, heads=heads, hd=hd),
        grid=(1,),
        in_specs=[ucol(0), ucol(1), gcol(0), gcol(1), full(cst), full(cw), full(cb), full(bi), full(bf),
                  full(m0p)],
        out_specs=[pl.BlockSpec((bs, ncs), lambda i: (0, 0)),
                   pl.BlockSpec((bs, 2 * inner), lambda i: (0, 0)),
                   pl.BlockSpec((inner, bs), lambda i: (0, 0)),
                   pl.BlockSpec((bs, LANE), lambda i: (0, 0)),
                   pl.BlockSpec((bs, LANE), lambda i: (0, 0)),
                   pl.BlockSpec((bs, LANE), lambda i: (0, 0))],
        out_shape=[jax.ShapeDtypeStruct((bs, ncs), F32),
                   jax.ShapeDtypeStruct((bs, 2 * inner), F32),
                   jax.ShapeDtypeStruct((inner, bs), F32),
                   tile, tile, tile],
        compiler_params=_params("arbitrary"),
        name="mlstm_sample_prep")(u, u, u, u, cst, cw, cb, bi, bf, m0p)
    c_new, qc = pl.pallas_call(
        functools.partial(_mlstm_sample_state_body, heads=heads, hd=hd),
        grid=(bs,),
        in_specs=[pl.BlockSpec(memory_space=pltpu.SMEM),
                  pl.BlockSpec((1, heads, hd, hd), lambda i: (i, 0, 0, 0)),
                  full(qk), pl.BlockSpec((bs, inner), lambda i: (0, 2)), full(kwt)],
        out_specs=[pl.BlockSpec((1, heads, hd, hd), lambda i: (i, 0, 0, 0)),
                   pl.BlockSpec((1, 1, inner), lambda i: (i, 0, 0))],
        out_shape=[jax.ShapeDtypeStruct((bs, heads, hd, hd), F32),
                   jax.ShapeDtypeStruct((bs, 1, inner), F32)],
        compiler_params=_params("arbitrary"),
        name="mlstm_sample_state")(wf[:, :heads], c0, qk, u, kwt)
    hz, n_new = pl.pallas_call(
        functools.partial(_mlstm_sample_finish_body, heads=heads, hd=hd),
        grid=(1,),
        in_specs=[full(qk), ucol(2), ucol(3), ucol(4), pl.BlockSpec((bs, inner), lambda i: (0, 0)),
                  full(n0), full(wi), full(wf), full(mt), full(nw)],
        out_specs=[pl.BlockSpec((bs, inner), lambda i: (0, 0)),
                   pl.BlockSpec((bs, inner), lambda i: (0, 0))],
        out_shape=[jax.ShapeDtypeStruct((bs, inner), F32)] * 2,
        compiler_params=_params("arbitrary"),
        name="mlstm_sample_finish")(qk, u, u, u, qc.reshape(bs, inner), n0, wi, wf, mt, nw)
    return hz, conv, c_new, n_new, mt[:, :heads]


def _pad_cols(w, n):
    return jnp.pad(w, ((0, 0), (0, n - w.shape[1])))


def _row(v, n=None):
    v = v.reshape(1, -1)
    return v if n is None else _pad_cols(v, n)


def kernel(x_prompt, x_sample, cache_attn_k, cache_attn_v, state_ssd_conv, state_ssd, state_mlstm_conv, state_mlstm_c, state_mlstm_n, state_mlstm_m, norm_w, final_norm_w, w_in_even, w_out_even, ssd_conv_w, ssd_conv_b, ssd_dt_bias, ssd_a_log, ssd_d, ssd_norm_w, w_in_odd, w_out_odd, mlstm_conv_w, mlstm_conv_b, mlstm_igate_b, mlstm_fgate_b, mlstm_norm_w):
    batch, seq, d_model = x_prompt.shape
    bs = x_sample.shape[0]
    ssd_heads = ssd_a_log.shape[1]
    ssd_inner = ssd_heads * SSD_HEADDIM
    ssd_cch = ssd_conv_w.shape[2]
    m_inner = mlstm_norm_w.shape[1]
    m_hd = m_inner // MLSTM_HEADS
    mp = batch * seq

    col_z = 4 * ATT_W
    col_xbc = col_z + ssd_inner
    col_dt = col_xbc + ssd_cch
    n_even = col_dt + LANE
    w_in0 = _pad_cols(w_in_even[0], n_even).astype(BF16)
    w_out0 = w_out_even[0].astype(BF16)
    nw0 = _row(norm_w[0])
    cw0, cb0 = ssd_conv_w[0], _row(ssd_conv_b[0])
    dtb = _row(ssd_dt_bias[0], LANE)
    alog = _row(ssd_a_log[0], LANE)
    aloge = _row(jnp.repeat(ssd_a_log[0], SSD_HEADDIM))
    dsk = _row(jnp.repeat(ssd_d[0], SSD_HEADDIM))
    snw = _row(ssd_norm_w[0])

    hp = x_prompt.reshape(mp, d_model)
    hs = x_sample.reshape(bs, d_model)

    up = _norm_matmul(hp, nw0, w_in0, tm=256, panels=1)
    cos_p, sa_p, sb_p = _rope_tables(jnp.arange(seq), LANE)
    att_p, k_p, v_p = _attn_prompt(up, cos_p, sa_p, sb_p, batch=batch, seq=seq)
    y_p, conv_p, st_p = _ssd_prompt(up, cw0, cb0, dtb, alog, dsk, snw, batch=batch, seq=seq,
                                    heads=ssd_heads, col_z=col_z, col_xbc=col_xbc, col_dt=col_dt)
    hp = _out_proj(hp, [att_p, y_p], [w_out0[:ATT_W], w_out0[ATT_W:]], tm=512)

    us = _norm_matmul(hs, nw0, w_in0, tm=bs, panels=1)
    cos_s, sa_s, sb_s = _rope_tables(PAST_LEN + jnp.arange(1), ATT_W)
    xt_s = _rot_sample(us, cos_s, sa_s, sb_s)
    k_s = xt_s[ATT_W:2 * ATT_W].T
    v_s = xt_s[2 * ATT_W:3 * ATT_W].T
    pos_minor = lambda cache: jnp.transpose(cache, (0, 2, 3, 1))
    att_s = _attn_sample(xt_s, pos_minor(cache_attn_k[0]), pos_minor(cache_attn_v[0]), bb=2).T
    y_s, conv_s, st_s = _ssd_sample(us, state_ssd_conv[0].reshape(bs, -1),
                                    state_ssd[0].reshape(bs, ssd_inner, SSD_STATE),
                                    cw0, cb0, dtb, alog, aloge, dsk, snw, heads=ssd_heads,
                                    col_z=col_z, col_xbc=col_xbc, col_dt=col_dt, bb=4)
    hs = _out_proj(hs, [att_s, y_s], [w_out0[:ATT_W], w_out0[ATT_W:]], tm=bs)

    wo = w_in_odd[0]
    gates_at = 4 * m_inner
    zcol = gates_at + 2 * MLSTM_HEADS
    col_gates = 5 * m_inner
    zpad = jnp.zeros((d_model, LANE - MLSTM_HEADS), wo.dtype)
    w_in1 = jnp.concatenate([wo[:, :gates_at], wo[:, zcol:],
                             wo[:, gates_at:gates_at + MLSTM_HEADS], zpad,
                             wo[:, gates_at + MLSTM_HEADS:zcol], zpad], axis=1).astype(BF16)
    w_out1 = w_out_odd[0].astype(BF16)
    nw1 = _row(norm_w[1])
    cw1, cb1 = mlstm_conv_w[0], _row(mlstm_conv_b[0])
    bi = _row(mlstm_igate_b[0], LANE)
    bf = _row(mlstm_fgate_b[0], LANE)
    mnw = _row(mlstm_norm_w[0])
    fnw = _row(final_norm_w)

    up1 = _norm_matmul(hp, nw1, w_in1, tm=256, panels=2)
    hz_p, mconv_p, c_p, n_p, m_p = _mlstm_prompt(up1, cw1, cb1, bi, bf, mnw, batch=batch, seq=seq,
                                                   heads=MLSTM_HEADS, hd=m_hd, col_gates=col_gates)
    y_prompt = _out_proj(hp, [hz_p], [w_out1], fnw, tm=512)

    us1 = _norm_matmul(hs, nw1, w_in1, tm=bs, panels=2)
    m0p = _pad_cols(state_mlstm_m[0], LANE)
    hz_s, mconv_s, c_s, n_s, m_s = _mlstm_sample(us1, state_mlstm_conv[0].reshape(bs, -1),
                                                 state_mlstm_c[0], state_mlstm_n[0].reshape(bs, m_inner),
                                                 m0p, cw1, cb1, bi, bf, mnw,
                                                 heads=MLSTM_HEADS, hd=m_hd, col_gates=col_gates)
    y_sample = _out_proj(hs, [hz_s], [w_out1], fnw, tm=bs)

    tmax = min(seq, CHUNK * max(DILATIONS))
    kv_shape = (1, batch, seq, ATT_HEADS, ATT_HD)
    return (
        y_prompt.reshape(batch, seq, d_model),
        y_sample.reshape(bs, 1, d_model),
        k_p.reshape(kv_shape)[:, :, seq - tmax:],
        v_p.reshape(kv_shape)[:, :, seq - tmax:],
        conv_p[None],
        st_p.reshape(1, batch, ssd_heads, SSD_HEADDIM, SSD_STATE),
        mconv_p[None],
        c_p[None],
        n_p.reshape(1, batch, MLSTM_HEADS, m_hd),
        m_p.reshape(1, batch, MLSTM_HEADS),
        k_s.reshape(1, bs, 1, ATT_HEADS, ATT_HD),
        v_s.reshape(1, bs, 1, ATT_HEADS, ATT_HD),
        conv_s.reshape(1, bs, CONV_K - 1, ssd_cch),
        st_s.reshape(1, bs, ssd_heads, SSD_HEADDIM, SSD_STATE),
        mconv_s.reshape(1, bs, CONV_K - 1, 2 * m_inner),
        c_s[None],
        n_s.reshape(1, bs, MLSTM_HEADS, m_hd),
        m_s.reshape(1, bs, MLSTM_HEADS),
    )
```

```python
import functools

import jax
import jax.numpy as jnp
from jax import lax
from jax.experimental import pallas as pl
from jax.experimental.pallas import tpu as pltpu

F32 = jnp.float32
BF16 = jnp.bfloat16
NEG_INF = float("-inf")
EPS = 1e-6
LANE = 128
CHUNK = 128
VMEM_LIMIT = 56 * 1024 * 1024

CONV_K = 4
ATT_HEADS = 8
ATT_HD = 64
ATT_W = ATT_HEADS * ATT_HD
ROT_DIM = ATT_HD // 4
ROPE_THETA = 500000.0
DILATIONS = (1, 4, 16)
PAST_LEN = 2048
SSD_HEADDIM = 64
SSD_GROUPS = 2
SSD_STATE = 128
MLSTM_HEADS = 8


def _params(*sem):
    return pltpu.CompilerParams(dimension_semantics=sem, vmem_limit_bytes=VMEM_LIMIT)


def _chunks(n, w):
    out, c = [], 0
    while c < n:
        out.append((c, min(w, n - c)))
        c += w
    return out


def _dot(a, b):
    return jnp.dot(a.astype(BF16), b.astype(BF16), preferred_element_type=F32)


def _dot_nt(a, b):
    return lax.dot_general(a.astype(BF16), b.astype(BF16), (((1,), (1,)), ((), ())),
                           preferred_element_type=F32)


def _dot_tn(a, b):
    return lax.dot_general(a.astype(BF16), b.astype(BF16), (((0,), (0,)), ((), ())),
                           preferred_element_type=F32)


def _split3(x):
    hi = x.astype(BF16)
    r1 = x - hi.astype(F32)
    mid = r1.astype(BF16)
    lo = (r1 - mid.astype(F32)).astype(BF16)
    return hi, mid, lo


def _sel_dot(sel, x):
    hi, mid, lo = _split3(x)
    d = lambda p: jnp.dot(sel, p, preferred_element_type=F32)
    return d(hi) + d(mid) + d(lo)


def _dot_sel(x, sel):
    hi, mid, lo = _split3(x)
    d = lambda p: jnp.dot(p, sel, preferred_element_type=F32)
    return d(hi) + d(mid) + d(lo)


def _tril(n):
    r = lax.broadcasted_iota(jnp.int32, (n, n), 0)
    c = lax.broadcasted_iota(jnp.int32, (n, n), 1)
    return r >= c


def _seg_matrix(rows, cols, seg, along_rows):
    r = lax.broadcasted_iota(jnp.int32, (rows, cols), 0)
    c = lax.broadcasted_iota(jnp.int32, (rows, cols), 1)
    m = (r // seg == c) if along_rows else (c // seg == r)
    return m.astype(BF16)


def _row_at(ref, b, cols=slice(None)):
    base = pl.multiple_of((b // 8) * 8, 8)
    tile = ref[pl.ds(base, 8), cols]
    sub = lax.broadcasted_iota(jnp.int32, tile.shape, 0)
    return jnp.sum(jnp.where(sub == b % 8, tile, 0.0), axis=0, keepdims=True)


def _silu(x):
    return x * (1.0 / (1.0 + jnp.exp(-x)))


def _sigmoid(x):
    return 1.0 / (1.0 + jnp.exp(-x))


def _softplus(x):
    return jnp.maximum(x, 0.0) + jnp.log1p(jnp.exp(-jnp.abs(x)))


def _rms(x, w):
    return x * lax.rsqrt(jnp.mean(x * x, axis=-1, keepdims=True) + EPS) * w


def _norm_matmul_body(x_ref, nw_ref, w_ref, o_ref, *, chunks):
    xn = _rms(x_ref[...], nw_ref[...]).astype(BF16)
    for c0, cw in chunks:
        o_ref[:, c0:c0 + cw] = jnp.dot(xn, w_ref[:, c0:c0 + cw], preferred_element_type=F32)


def _norm_matmul(x, nw, w, *, tm, panels):
    m, d = x.shape
    n = w.shape[1]
    pn = n // panels
    return pl.pallas_call(
        functools.partial(_norm_matmul_body, chunks=_chunks(pn, 512)),
        grid=(panels, m // tm),
        in_specs=[pl.BlockSpec((tm, d), lambda p, i: (i, 0)),
                  pl.BlockSpec((1, d), lambda p, i: (0, 0)),
                  pl.BlockSpec((d, pn), lambda p, i: (0, p))],
        out_specs=pl.BlockSpec((tm, pn), lambda p, i: (i, p)),
        out_shape=jax.ShapeDtypeStruct((m, n), F32),
        compiler_params=_params("arbitrary", "arbitrary"),
        name="norm_matmul")(x, nw, w)


def _out_proj_body(*refs, n_in, final):
    h_ref = refs[0]
    xs = refs[1:1 + n_in]
    ws = refs[1 + n_in:1 + 2 * n_in]
    rest = refs[1 + 2 * n_in:]
    acc = h_ref[...]
    for x_ref, w_ref in zip(xs, ws):
        acc = acc + jnp.dot(x_ref[...].astype(BF16), w_ref[...], preferred_element_type=F32)
    if final:
        fw_ref, o_ref = rest
        o_ref[...] = _rms(acc, fw_ref[...])
    else:
        (o_ref,) = rest
        o_ref[...] = acc


def _out_proj(h, xs, ws, fw=None, *, tm):
    m, d = h.shape
    n_in = len(xs)
    in_specs = [pl.BlockSpec((tm, d), lambda i: (i, 0))]
    in_specs += [pl.BlockSpec((tm, x.shape[1]), lambda i: (i, 0)) for x in xs]
    in_specs += [pl.BlockSpec(w.shape, lambda i: (0, 0)) for w in ws]
    args = [h, *xs, *ws]
    if fw is not None:
        in_specs.append(pl.BlockSpec((1, d), lambda i: (0, 0)))
        args.append(fw)
    return pl.pallas_call(
        functools.partial(_out_proj_body, n_in=n_in, final=fw is not None),
        grid=(m // tm,),
        in_specs=in_specs,
        out_specs=pl.BlockSpec((tm, d), lambda i: (i, 0)),
        out_shape=jax.ShapeDtypeStruct((m, d), F32),
        compiler_params=_params("arbitrary"),
        name="out_proj")(*args)


def _rope_tables(pos, width):
    half = ROT_DIM // 2
    inv = jnp.power(F32(ROPE_THETA), -jnp.arange(half, dtype=F32) * (2.0 / ROT_DIM))
    ang = pos.astype(F32)[:, None] * inv[None, :]
    cos, sin = jnp.cos(ang), jnp.sin(ang)
    n = pos.shape[0]
    one = jnp.ones((n, ATT_HD - ROT_DIM), F32)
    z8 = jnp.zeros((n, half), F32)
    z48 = jnp.zeros((n, ATT_HD - ROT_DIM), F32)
    c = jnp.concatenate([cos, cos, one], axis=-1)
    sa = jnp.concatenate([-sin, z8, z48], axis=-1)
    sb = jnp.concatenate([z8, sin, z48], axis=-1)
    rep = width // ATT_HD
    return tuple(jnp.tile(t, (1, rep)) for t in (c, sa, sb))


def _rotary(x, c, sa, sb):
    w = x.shape[-1]
    half = ROT_DIM // 2
    return x * c + pltpu.roll(x, w - half, 1) * sa + pltpu.roll(x, half, 1) * sb


def _attn_prompt_body(q_ref, k_ref, v_ref, g_ref, c_ref, sa_ref, sb_ref,
                      att_ref, ko_ref, vo_ref,
                      nat, qd, kd, vd, od, std, o_s, st_s, *, seq):
    nblk = seq // CHUNK
    c, sa, sb = c_ref[...], sa_ref[...], sb_ref[...]
    q = _rotary(q_ref[...], c, sa, sb) * (ATT_HD ** -0.5)
    k = _rotary(k_ref[...], c, sa, sb)
    ko_ref[...] = k
    vo_ref[...] = v_ref[...]
    head_lane = lax.broadcasted_iota(jnp.int32, (1, LANE), 1) // ATT_HD
    nat[0] = q
    nat[1] = k
    nat[2] = v_ref[...]
    zero = jnp.zeros((CHUNK, LANE), BF16)
    for p, d in enumerate(DILATIONS):
        kd[p, 0:CHUNK, :] = zero
        vd[p, 0:CHUNK, :] = zero
        ln = seq // d
        for r in range(d):
            rows = slice(r * ln, (r + 1) * ln)
            krows = slice(CHUNK + r * ln, CHUNK + (r + 1) * ln)
            src = pl.ds(r, ln, stride=d) if d > 1 else slice(None)
            qr = nat[0, src, :]
            qd[p, 0, rows, :] = jnp.where(head_lane == 0, qr, 0.0).astype(BF16)
            qd[p, 1, rows, :] = jnp.where(head_lane == 1, qr, 0.0).astype(BF16)
            kd[p, krows, :] = nat[1, src, :].astype(BF16)
            vd[p, krows, :] = nat[2, src, :].astype(BF16)

    row = lax.broadcasted_iota(jnp.int32, (CHUNK, 2 * CHUNK), 0)
    col = lax.broadcasted_iota(jnp.int32, (CHUNK, 2 * CHUNK), 1)
    band = (col >= row) & (col <= row + CHUNK)
    first_head = lax.broadcasted_iota(jnp.int32, (CHUNK, LANE), 1) < ATT_HD

    for p, d in enumerate(DILATIONS):
        nb = nblk // d

        def block(t, carry, p=p, nb=nb):
            base = pl.multiple_of(t * CHUNK, CHUNK)
            rows = pl.ds(base, CHUNK)
            first = (t % nb) == 0
            valid = band & (col >= jnp.where(first, CHUNK, 0))
            kw = kd[p, pl.ds(base, 2 * CHUNK), :]
            vw = vd[p, pl.ds(base, 2 * CHUNK), :]
            parts = []
            for hh in range(2):
                s = lax.dot_general(qd[p, hh, rows, :], kw, (((1,), (1,)), ((), ())),
                                    preferred_element_type=F32)
                s = jnp.where(valid, s, NEG_INF)
                m = jnp.max(s, axis=-1, keepdims=True)
                e = jnp.exp(s - m)
                o = jnp.dot(e.astype(BF16), vw, preferred_element_type=F32)
                parts.append((o, m, jnp.sum(e, axis=-1, keepdims=True)))
            (o0, m0, l0), (o1, m1, l1) = parts
            wide = lambda t: jnp.broadcast_to(t, (CHUNK, LANE))
            od[rows, :] = jnp.where(first_head, o0, o1)
            std[0, rows, :] = jnp.where(first_head, wide(m0), wide(m1))
            std[1, rows, :] = jnp.where(first_head, wide(l0), wide(l1))
            return carry

        lax.fori_loop(0, nblk, block, 0, unroll=True)
        if d == 1:
            o_s[p] = od[...]
            st_s[p] = std[...]
        else:
            ln = seq // d
            for r in range(d):
                o_s[p, pl.ds(r, ln, stride=d), :] = od[r * ln:(r + 1) * ln, :]
                st_s[p, 0, pl.ds(r, ln, stride=d), :] = std[0, r * ln:(r + 1) * ln, :]
                st_s[p, 1, pl.ds(r, ln, stride=d), :] = std[1, r * ln:(r + 1) * ln, :]

    np_ = len(DILATIONS)

    def combine(t, carry):
        base = pl.multiple_of(t * CHUNK, CHUNK)
        rows = pl.ds(base, CHUNK)
        ms = [st_s[p, 0, rows, :] for p in range(np_)]
        mx = functools.reduce(jnp.maximum, ms)
        ws = [jnp.exp(mm - mx) for mm in ms]
        num = sum(w * o_s[p, rows, :] for p, w in enumerate(ws))
        den = sum(w * st_s[p, 1, rows, :] for p, w in enumerate(ws))
        att_ref[rows, :] = (num / den) * _silu(g_ref[rows, :])
        return carry

    lax.fori_loop(0, nblk, combine, 0, unroll=2)


def _attn_prompt(u, cos, sa, sb, *, batch, seq):
    m = batch * seq
    npair = ATT_W // LANE
    blk = lambda off: pl.BlockSpec((seq, LANE), lambda b, hp, off=off: (b, off + hp))
    tab = pl.BlockSpec((seq, LANE), lambda b, hp: (0, 0))
    out = pl.BlockSpec((seq, LANE), lambda b, hp: (b, hp))
    np_ = len(DILATIONS)
    return pl.pallas_call(
        functools.partial(_attn_prompt_body, seq=seq),
        grid=(batch, npair),
        in_specs=[blk(0), blk(npair), blk(2 * npair), blk(3 * npair), tab, tab, tab],
        out_specs=[out, out, out],
        out_shape=[jax.ShapeDtypeStruct((m, ATT_W), F32)] * 3,
        scratch_shapes=[pltpu.VMEM((3, seq, LANE), F32),
                        pltpu.VMEM((np_, 2, seq, LANE), BF16),
                        pltpu.VMEM((np_, seq + CHUNK, LANE), BF16),
                        pltpu.VMEM((np_, seq + CHUNK, LANE), BF16),
                        pltpu.VMEM((seq, LANE), F32),
                        pltpu.VMEM((2, seq, LANE), F32),
                        pltpu.VMEM((np_, seq, LANE), F32),
                        pltpu.VMEM((np_, 2, seq, LANE), F32)],
        compiler_params=_params("arbitrary", "arbitrary"),
        name="attn_prompt")(u, u, u, u, cos, sa, sb)


def _ssd_prompt_body(xbc_ref, z_ref, dt_ref, cw_ref, cb_ref, dtb_ref, alog_ref, dsk_ref, nw_ref,
                     y_ref, conv_ref, st_ref, ext, st, ys, *, heads, nchunk):
    c = pl.program_id(1)
    inner = heads * SSD_HEADDIM
    gw = SSD_STATE
    hpg = heads // SSD_GROUPS

    @pl.when(c == 0)
    def _():
        ext[0:8, :] = jnp.zeros((8, ext.shape[1]), F32)
        st[...] = jnp.zeros(st.shape, F32)

    ext[8:8 + CHUNK, :] = xbc_ref[...]
    conv = cb_ref[...] + ext[5:5 + CHUNK, :] * cw_ref[0:1, :]
    for j in range(1, CONV_K):
        conv = conv + ext[5 + j:5 + j + CHUNK, :] * cw_ref[j:j + 1, :]
    act = _silu(conv)

    @pl.when(c == nchunk - 1)
    def _():
        conv_ref[...] = ext[CHUNK + 5:CHUNK + 8, :]

    ext[0:8, :] = ext[CHUNK:CHUNK + 8, :]

    dt = _softplus(dt_ref[...] + dtb_ref[...])
    a = -jnp.exp(alog_ref[...])
    tril = _tril(CHUNK)
    acum = _sel_dot(tril.astype(BF16), dt * a)
    acum_t = acum.T
    dt_t = dt.T
    eacum = jnp.exp(acum)
    last = acum[CHUNK - 1:CHUNK, :]
    wend = jnp.exp(last - acum) * dt
    elast = jnp.exp(last)

    for g in range(SSD_GROUPS):
        bm = act[:, inner + g * gw:inner + (g + 1) * gw]
        cm = act[:, inner + SSD_GROUPS * gw + g * gw:inner + SSD_GROUPS * gw + (g + 1) * gw]
        cb = _dot_nt(cm, bm)
        for hg in range(hpg):
            h = g * hpg + hg
            lo = h * SSD_HEADDIM
            xh = act[:, lo:lo + SSD_HEADDIM]
            seg = acum[:, h:h + 1] - acum_t[h:h + 1, :]
            decay = jnp.exp(jnp.where(tril, seg, NEG_INF))
            sh = st[lo:lo + SSD_HEADDIM, :]
            y = _dot(cb * decay * dt_t[h:h + 1, :], xh)
            y = y + _dot_nt(cm, sh) * eacum[:, h:h + 1]
            ys[:, lo:lo + SSD_HEADDIM] = y
            st[lo:lo + SSD_HEADDIM, :] = sh * elast[:, h:h + 1] + _dot_tn(xh, bm * wend[:, h:h + 1])

    xs = act[:, 0:inner]
    yt = (ys[...] + dsk_ref[...] * xs) * _silu(z_ref[...])
    y_ref[...] = _rms(yt, nw_ref[...])

    @pl.when(c == nchunk - 1)
    def _():
        st_ref[...] = st[...]


def _ssd_prompt(u, cw, cb, dtb, alog, dsk, nw, *, batch, seq, heads, col_z, col_xbc, col_dt):
    nchunk = seq // CHUNK
    inner = heads * SSD_HEADDIM
    cch = cw.shape[1]
    row = lambda b, c: b * nchunk + c
    full = lambda a: pl.BlockSpec(a.shape, lambda b, c: (0, 0))
    return pl.pallas_call(
        functools.partial(_ssd_prompt_body, heads=heads, nchunk=nchunk),
        grid=(batch, nchunk),
        in_specs=[pl.BlockSpec((CHUNK, cch), lambda b, c: (row(b, c), col_xbc // cch)),
                  pl.BlockSpec((CHUNK, inner), lambda b, c: (row(b, c), col_z // inner)),
                  pl.BlockSpec((CHUNK, LANE), lambda b, c: (row(b, c), col_dt // LANE)),
                  full(cw), full(cb), full(dtb), full(alog), full(dsk), full(nw)],
        out_specs=[pl.BlockSpec((CHUNK, inner), lambda b, c: (row(b, c), 0)),
                   pl.BlockSpec((None, CONV_K - 1, cch), lambda b, c: (b, 0, 0)),
                   pl.BlockSpec((None, inner, SSD_STATE), lambda b, c: (b, 0, 0))],
        out_shape=[jax.ShapeDtypeStruct((batch * seq, inner), F32),
                   jax.ShapeDtypeStruct((batch, CONV_K - 1, cch), F32),
                   jax.ShapeDtypeStruct((batch, inner, SSD_STATE), F32)],
        scratch_shapes=[pltpu.VMEM((CHUNK + 8, cch), F32),
                        pltpu.VMEM((inner, SSD_STATE), F32),
                        pltpu.VMEM((CHUNK, inner), F32)],
        compiler_params=_params("arbitrary", "arbitrary"),
        name="ssd_prompt")(u, u, u, cw, cb, dtb, alog, dsk, nw)


def _mlstm_prompt_body(q_ref, k_ref, v_ref, o_ref, z_ref, gi_ref, gf_ref,
                       cw_ref, cb_ref, bi_ref, bf_ref, nw_ref,
                       hz_ref, conv_ref, c_out, n_out, m_out,
                       extq, extk, qa, ka, c_s, n_s, m_s, *, nchunk, heads, hd):
    c = pl.program_id(1)
    inner = heads * hd

    @pl.when(c == 0)
    def _():
        extq[0:8, :] = jnp.zeros((8, inner), F32)
        extk[0:8, :] = jnp.zeros((8, inner), F32)
        c_s[...] = jnp.zeros(c_s.shape, F32)
        n_s[...] = jnp.zeros(n_s.shape, F32)
        m_s[...] = jnp.full(m_s.shape, NEG_INF, F32)

    def conv(ext, x_ref, col0, out, scale):
        ext[8:8 + CHUNK, :] = x_ref[...]
        cols = slice(col0, col0 + inner)
        acc = cb_ref[:, cols] + ext[5:5 + CHUNK, :] * cw_ref[0:1, cols]
        for j in range(1, CONV_K):
            acc = acc + ext[5 + j:5 + j + CHUNK, :] * cw_ref[j:j + 1, cols]
        out[...] = _silu(acc) * scale

        @pl.when(c == nchunk - 1)
        def _():
            conv_ref[:, cols] = ext[CHUNK + 5:CHUNK + 8, :]

        ext[0:8, :] = ext[CHUNK:CHUNK + 8, :]

    conv(extq, q_ref, 0, qa, 1.0)
    conv(extk, k_ref, inner, ka, hd ** -0.5)

    it = gi_ref[...] + bi_ref[...]
    logf = -_softplus(-(gf_ref[...] + bf_ref[...]))
    tril = _tril(CHUNK)
    bc = _sel_dot(tril.astype(BF16), logf)
    it_t = it.T
    bc_t = bc.T

    for h in range(heads):
        sl = slice(h * hd, (h + 1) * hd)
        q, k, v = qa[:, sl], ka[:, sl], v_ref[:, sl]
        i_col, b_col = it[:, h:h + 1], bc[:, h:h + 1]
        i_row, b_row = it_t[h:h + 1, :], bc_t[h:h + 1, :]
        m_prev = m_s[h]
        dmat = jnp.where(tril, b_col - b_row + i_row, NEG_INF)
        inter = b_col + m_prev
        m_t = jnp.maximum(inter, jnp.max(dmat, axis=-1, keepdims=True))
        w_intra = jnp.exp(dmat - m_t)
        w_inter = jnp.exp(inter - m_t)
        att = w_intra * _dot_nt(q, k)
        c_prev = c_s[h]
        n_prev = n_s[h]
        num = _dot(att, v) + w_inter * _dot(q, c_prev)
        den = jnp.sum(att, axis=-1, keepdims=True) + w_inter * jnp.sum(q * n_prev, axis=-1, keepdims=True)
        hh = num / jnp.maximum(jnp.abs(den), jnp.exp(-m_t))

        b_last = b_col[CHUNK - 1:CHUNK, :]
        logw = b_last - b_col + i_col
        m_new = jnp.maximum(b_last + m_prev, jnp.max(logw, axis=0, keepdims=True))
        ws = jnp.exp(logw - m_new)
        scale = jnp.exp(b_last + m_prev - m_new)
        c_s[h] = scale * c_prev + _dot_tn(k, ws * v)
        n_s[h] = scale * n_prev + jnp.sum(ws * k, axis=0, keepdims=True)
        m_s[h] = m_new

        hg = _sigmoid(o_ref[:, sl]) * hh
        mu = jnp.mean(hg, axis=-1, keepdims=True)
        var = jnp.mean(jnp.square(hg - mu), axis=-1, keepdims=True)
        hn = (hg - mu) * lax.rsqrt(var + EPS) * nw_ref[:, sl]
        hz_ref[:, sl] = hn * _silu(z_ref[:, sl])

    @pl.when(c == nchunk - 1)
    def _():
        c_out[...] = c_s[...]
        n_out[...] = n_s[...]
        m_out[...] = m_s[...]


def _mlstm_prompt(u, cw, cb, bi, bf, nw, *, batch, seq, heads, hd, col_gates):
    nchunk = seq // CHUNK
    inner = heads * hd
    row = lambda b, c: b * nchunk + c
    ublk = lambda j: pl.BlockSpec((CHUNK, inner), lambda b, c, j=j: (row(b, c), j))
    gblk = lambda off: pl.BlockSpec((CHUNK, LANE), lambda b, c, off=off: (row(b, c), col_gates // LANE + off))
    full = lambda a: pl.BlockSpec(a.shape, lambda b, c: (0, 0))
    return pl.pallas_call(
        functools.partial(_mlstm_prompt_body, nchunk=nchunk, heads=heads, hd=hd),
        grid=(batch, nchunk),
        in_specs=[ublk(0), ublk(1), ublk(2), ublk(3), ublk(4), gblk(0), gblk(1),
                  full(cw), full(cb), full(bi), full(bf), full(nw)],
        out_specs=[pl.BlockSpec((CHUNK, inner), lambda b, c: (row(b, c), 0)),
                   pl.BlockSpec((None, CONV_K - 1, 2 * inner), lambda b, c: (b, 0, 0)),
                   pl.BlockSpec((None, heads, hd, hd), lambda b, c: (b, 0, 0, 0)),
                   pl.BlockSpec((None, heads, 1, hd), lambda b, c: (b, 0, 0, 0)),
                   pl.BlockSpec((None, heads, 1, 1), lambda b, c: (b, 0, 0, 0))],
        out_shape=[jax.ShapeDtypeStruct((batch * seq, inner), F32),
                   jax.ShapeDtypeStruct((batch, CONV_K - 1, 2 * inner), F32),
                   jax.ShapeDtypeStruct((batch, heads, hd, hd), F32),
                   jax.ShapeDtypeStruct((batch, heads, 1, hd), F32),
                   jax.ShapeDtypeStruct((batch, heads, 1, 1), F32)],
        scratch_shapes=[pltpu.VMEM((CHUNK + 8, inner), F32),
                        pltpu.VMEM((CHUNK + 8, inner), F32),
                        pltpu.VMEM((CHUNK, inner), F32),
                        pltpu.VMEM((CHUNK, inner), F32),
                        pltpu.VMEM((heads, hd, hd), F32),
                        pltpu.VMEM((heads, 1, hd), F32),
                        pltpu.VMEM((heads, 1, 1), F32)],
        compiler_params=_params("arbitrary", "arbitrary"),
        name="mlstm_prompt")(u, u, u, u, u, u, u, cw, cb, bi, bf, nw)


def _rot_sample_body(q_ref, k_ref, v_ref, g_ref, c_ref, sa_ref, sb_ref, xt_ref):
    c, sa, sb = c_ref[...], sa_ref[...], sb_ref[...]
    q = _rotary(q_ref[...], c, sa, sb) * (ATT_HD ** -0.5)
    k = _rotary(k_ref[...], c, sa, sb)
    xt_ref[...] = jnp.concatenate([q, k, v_ref[...], g_ref[...]], axis=-1).T


def _rot_sample(u, cos, sa, sb):
    bs = u.shape[0]
    ublk = lambda off: pl.BlockSpec((bs, ATT_W), lambda i, off=off: (0, off))
    tab = pl.BlockSpec((1, ATT_W), lambda i: (0, 0))
    return pl.pallas_call(
        _rot_sample_body,
        grid=(1,),
        in_specs=[ublk(0), ublk(1), ublk(2), ublk(3), tab, tab, tab],
        out_specs=pl.BlockSpec((4 * ATT_W, bs), lambda i: (0, 0)),
        out_shape=jax.ShapeDtypeStruct((4 * ATT_W, bs), F32),
        compiler_params=_params("arbitrary"),
        name="rot_sample")(u, u, u, u, cos, sa, sb)


def _key_multiplicity(wb):
    back = wb - lax.broadcasted_iota(jnp.int32, (1, wb), 1)
    cnt = jnp.zeros((1, wb), F32)
    for d in DILATIONS:
        cnt = cnt + ((back <= CHUNK * d) & (lax.rem(back, d) == 0)).astype(F32)
    return cnt


def _attn_sample_body(x_ref, kc_ref, vc_ref, att_ref, *, bb, wb):
    i = pl.program_id(0)
    bs = x_ref.shape[1]
    nt = wb // LANE

    @pl.when(i == 0)
    def _():
        att_ref[...] = jnp.zeros(att_ref.shape, F32)

    cnt = _key_multiplicity(wb)
    valid = cnt > 0.0
    npat = float(len(DILATIONS))
    rid = lax.broadcasted_iota(jnp.int32, (bs, LANE), 0)
    lid = lax.broadcasted_iota(jnp.int32, (ATT_HD, bs), 1)
    for t in range(bb):
        b = i * bb + t
        cols = _dot_sel(x_ref[...], (rid == b).astype(BF16))
        for h in range(ATT_HEADS):
            lo = h * ATT_HD
            qc = cols[lo:lo + ATT_HD]
            kc = cols[ATT_W + lo:ATT_W + lo + ATT_HD]
            vc = cols[2 * ATT_W + lo:2 * ATT_W + lo + ATT_HD]
            gc = cols[3 * ATT_W + lo:3 * ATT_W + lo + ATT_HD]
            kt = kc_ref[t, h]
            vt = vc_ref[t, h]
            s = jnp.sum(kt * jnp.tile(qc, (1, nt)), axis=0, keepdims=True)
            s = jnp.where(valid, s, NEG_INF)
            s_self = jnp.sum(qc * kc, axis=0, keepdims=True)[:, 0:1]
            m = jnp.maximum(jnp.max(s, axis=1, keepdims=True), s_self)
            p = cnt * jnp.exp(s - m)
            p_self = npat * jnp.exp(s_self - m)
            den = jnp.sum(p, axis=1, keepdims=True) + p_self
            acc = vt[:, 0:LANE] * p[:, 0:LANE]
            for j in range(1, nt):
                acc = acc + vt[:, j * LANE:(j + 1) * LANE] * p[:, j * LANE:(j + 1) * LANE]
            o = jnp.sum(acc, axis=1, keepdims=True) + p_self * vc[:, 0:1]
            o = o / den * _silu(gc[:, 0:1])
            att_ref[lo:lo + ATT_HD, :] = jnp.where(lid == b, o, att_ref[lo:lo + ATT_HD, :])


def _attn_sample(xt, ck, cv, *, bb):
    bs, wb = ck.shape[0], ck.shape[3]
    cache = pl.BlockSpec((bb, ATT_HEADS, ATT_HD, wb), lambda i: (i, 0, 0, 0))
    return pl.pallas_call(
        functools.partial(_attn_sample_body, bb=bb, wb=wb),
        grid=(bs // bb,),
        in_specs=[pl.BlockSpec(xt.shape, lambda i: (0, 0)), cache, cache],
        out_specs=pl.BlockSpec((ATT_W, bs), lambda i: (0, 0)),
        out_shape=jax.ShapeDtypeStruct((ATT_W, bs), F32),
        compiler_params=_params("arbitrary"),
        name="attn_sample")(xt, ck, cv)


def _ssd_sample_prep_body(xbc_ref, dt_ref, cst_ref, cw_ref, cb_ref, dtb_ref, alog_ref, aloge_ref, dsk_ref,
                          conv_ref, yloc_ref, xdtt_ref, bc_ref, da_ref, dae_ref, *, heads):
    inner = heads * SSD_HEADDIM
    cch = xbc_ref.shape[1]
    gw = SSD_STATE
    x = xbc_ref[...]
    acc = cb_ref[...] + x * cw_ref[CONV_K - 1:CONV_K, :]
    for j in range(CONV_K - 1):
        acc = acc + cst_ref[:, j * cch:(j + 1) * cch] * cw_ref[j:j + 1, :]
    for j in range(CONV_K - 2):
        conv_ref[:, j * cch:(j + 1) * cch] = cst_ref[:, (j + 1) * cch:(j + 2) * cch]
    conv_ref[:, (CONV_K - 2) * cch:(CONV_K - 1) * cch] = x
    act = _silu(acc)
    xs = act[:, 0:inner]
    bc_ref[...] = act[:, inner:]
    dt = _softplus(dt_ref[...] + dtb_ref[...])
    da_ref[...] = jnp.exp(dt * (-jnp.exp(alog_ref[...])))
    expand = _seg_matrix(LANE, inner, SSD_HEADDIM, False)
    dte = _dot_sel(dt, expand)
    dae_ref[...] = jnp.exp(dte * (-jnp.exp(aloge_ref[...])))
    xdt = xs * dte
    xdtt_ref[...] = xdt.T
    hw = inner // SSD_GROUPS
    parts = []
    for g in range(SSD_GROUPS):
        bm = act[:, inner + g * gw:inner + (g + 1) * gw]
        cm = act[:, inner + SSD_GROUPS * gw + g * gw:inner + SSD_GROUPS * gw + (g + 1) * gw]
        cbg = jnp.sum(cm * bm, axis=-1, keepdims=True)
        parts.append(cbg * xdt[:, g * hw:(g + 1) * hw])
    yloc_ref[...] = jnp.concatenate(parts, axis=-1) + dsk_ref[...] * xs


def _ssd_sample_state_body(da_ref, s_ref, xdtt_ref, bc_ref, so_ref, yi_ref, *, bb, heads):
    i = pl.program_id(0)
    bs = bc_ref.shape[0]
    inner = heads * SSD_HEADDIM
    hw = inner // SSD_GROUPS
    hpg = heads // SSD_GROUPS
    gw = SSD_STATE
    rid = lax.broadcasted_iota(jnp.int32, (bs, gw), 0)
    for t in range(bb):
        b = i * bb + t
        parts = []
        for g in range(SSD_GROUPS):
            mg = jnp.where(rid == b, bc_ref[:, g * gw:(g + 1) * gw], 0.0)
            sl = _dot(xdtt_ref[g * hw:(g + 1) * hw, :], mg)
            crow = _row_at(bc_ref, b, slice(SSD_GROUPS * gw + g * gw, SSD_GROUPS * gw + (g + 1) * gw))
            sg = s_ref[t, g * hw:(g + 1) * hw, :]
            parts.append(_dot_nt(jnp.broadcast_to(crow, (8, gw)), sg)[0:1, :])
            for hg in range(hpg):
                h = g * hpg + hg
                lo = hg * SSD_HEADDIM
                so_ref[t, h * SSD_HEADDIM:(h + 1) * SSD_HEADDIM, :] = (
                    sg[lo:lo + SSD_HEADDIM, :] * da_ref[b, h] + sl[lo:lo + SSD_HEADDIM, :])
        yi_ref[t] = jnp.concatenate(parts, axis=-1)


def _ssd_sample_finish_body(yloc_ref, yi_ref, dae_ref, z_ref, nw_ref, y_ref):
    y = (yloc_ref[...] + yi_ref[...] * dae_ref[...]) * _silu(z_ref[...])
    y_ref[...] = _rms(y, nw_ref[...])


def _ssd_sample(u, cst, state, cw, cb, dtb, alog, aloge, dsk, nw, *, heads, col_z, col_xbc, col_dt, bb):
    bs = u.shape[0]
    inner = heads * SSD_HEADDIM
    cch = cw.shape[1]
    ncs = (CONV_K - 1) * cch
    bcw = 2 * SSD_GROUPS * SSD_STATE
    full = lambda a: pl.BlockSpec(a.shape, lambda i: (0,) * a.ndim)
    conv, yloc, xdtt, bc, da, dae = pl.pallas_call(
        functools.partial(_ssd_sample_prep_body, heads=heads),
        grid=(1,),
        in_specs=[pl.BlockSpec((bs, cch), lambda i: (0, col_xbc // cch)),
                  pl.BlockSpec((bs, LANE), lambda i: (0, col_dt // LANE)),
                  full(cst), full(cw), full(cb), full(dtb), full(alog), full(aloge), full(dsk)],
        out_specs=[pl.BlockSpec((bs, ncs), lambda i: (0, 0)),
                   pl.BlockSpec((bs, inner), lambda i: (0, 0)),
                   pl.BlockSpec((inner, bs), lambda i: (0, 0)),
                   pl.BlockSpec((bs, bcw), lambda i: (0, 0)),
                   pl.BlockSpec((bs, LANE), lambda i: (0, 0)),
                   pl.BlockSpec((bs, inner), lambda i: (0, 0))],
        out_shape=[jax.ShapeDtypeStruct((bs, ncs), F32),
                   jax.ShapeDtypeStruct((bs, inner), F32),
                   jax.ShapeDtypeStruct((inner, bs), F32),
                   jax.ShapeDtypeStruct((bs, bcw), F32),
                   jax.ShapeDtypeStruct((bs, LANE), F32),
                   jax.ShapeDtypeStruct((bs, inner), F32)],
        compiler_params=_params("arbitrary"),
        name="ssd_sample_prep")(u, u, cst, cw, cb, dtb, alog, aloge, dsk)
    new_state, yi = pl.pallas_call(
        functools.partial(_ssd_sample_state_body, bb=bb, heads=heads),
        grid=(bs // bb,),
        in_specs=[pl.BlockSpec(memory_space=pltpu.SMEM),
                  pl.BlockSpec((bb, inner, SSD_STATE), lambda i: (i, 0, 0)),
                  full(xdtt), full(bc)],
        out_specs=[pl.BlockSpec((bb, inner, SSD_STATE), lambda i: (i, 0, 0)),
                   pl.BlockSpec((bb, 1, inner), lambda i: (i, 0, 0))],
        out_shape=[jax.ShapeDtypeStruct((bs, inner, SSD_STATE), F32),
                   jax.ShapeDtypeStruct((bs, 1, inner), F32)],
        compiler_params=_params("arbitrary"),
        name="ssd_sample_state")(da[:, :heads], state, xdtt, bc)
    y = pl.pallas_call(
        _ssd_sample_finish_body,
        grid=(1,),
        in_specs=[full(yloc), pl.BlockSpec((bs, inner), lambda i: (0, 0)), full(dae),
                  pl.BlockSpec((bs, inner), lambda i: (0, col_z // inner)), full(nw)],
        out_specs=pl.BlockSpec((bs, inner), lambda i: (0, 0)),
        out_shape=jax.ShapeDtypeStruct((bs, inner), F32),
        compiler_params=_params("arbitrary"),
        name="ssd_sample_finish")(yloc, yi.reshape(bs, inner), dae, u, nw)
    return y, conv, new_state


def _mlstm_sample_prep_body(q_ref, k_ref, gi_ref, gf_ref, cst_ref, cw_ref, cb_ref, bi_ref, bf_ref, m_ref,
                            conv_ref, qk_ref, kwt_ref, wi_ref, wf_ref, mt_ref, *, heads, hd):
    inner = heads * hd
    cch = 2 * inner
    x = jnp.concatenate([q_ref[...], k_ref[...]], axis=-1)
    acc = cb_ref[...] + x * cw_ref[CONV_K - 1:CONV_K, :]
    for j in range(CONV_K - 1):
        acc = acc + cst_ref[:, j * cch:(j + 1) * cch] * cw_ref[j:j + 1, :]
    for j in range(CONV_K - 2):
        conv_ref[:, j * cch:(j + 1) * cch] = cst_ref[:, (j + 1) * cch:(j + 2) * cch]
    conv_ref[:, (CONV_K - 2) * cch:(CONV_K - 1) * cch] = x
    act = _silu(acc)
    q = act[:, 0:inner]
    k = act[:, inner:] * (hd ** -0.5)
    it = gi_ref[...] + bi_ref[...]
    inter = -_softplus(-(gf_ref[...] + bf_ref[...])) + m_ref[...]
    mt = jnp.maximum(inter, it)
    wi = jnp.exp(it - mt)
    wf = jnp.exp(inter - mt)
    wi_ref[...] = wi
    wf_ref[...] = wf
    mt_ref[...] = mt
    qk_ref[:, 0:inner] = q
    qk_ref[:, inner:] = k
    kw = jnp.concatenate([k[:, h * hd:(h + 1) * hd] * wi[:, h:h + 1] for h in range(heads)], axis=-1)
    kwt_ref[...] = kw.T


def _mlstm_sample_state_body(wf_ref, c_ref, qk_ref, v_ref, kwt_ref, co_ref, qc_ref, *, heads, hd):
    b = pl.program_id(0)
    bs = v_ref.shape[0]
    rid = lax.broadcasted_iota(jnp.int32, (bs, hd), 0)
    parts = []
    for h in range(heads):
        cp = c_ref[0, h]
        qrow = _row_at(qk_ref, b, slice(h * hd, (h + 1) * hd))
        parts.append(_dot(jnp.broadcast_to(qrow, (8, hd)), cp)[0:1, :])
        mh = jnp.where(rid == b, v_ref[:, h * hd:(h + 1) * hd], 0.0)
        co_ref[0, h] = cp * wf_ref[b, h] + _dot(kwt_ref[h * hd:(h + 1) * hd, :], mh)
    qc_ref[0] = jnp.concatenate(parts, axis=-1)


def _mlstm_sample_finish_body(qk_ref, v_ref, o_ref, z_ref, qc_ref, n_ref, wi_ref, wf_ref, mt_ref, nw_ref,
                              hz_ref, no_ref, *, heads, hd):
    inner = heads * hd
    for h in range(heads):
        sl = slice(h * hd, (h + 1) * hd)
        q = qk_ref[:, sl]
        k = qk_ref[:, inner + h * hd:inner + (h + 1) * hd]
        wi = wi_ref[:, h:h + 1]
        wf = wf_ref[:, h:h + 1]
        mt = mt_ref[:, h:h + 1]
        n_prev = n_ref[:, sl]
        att = wi * jnp.sum(q * k, axis=-1, keepdims=True)
        num = att * v_ref[:, sl] + wf * qc_ref[:, sl]
        den = att + wf * jnp.sum(q * n_prev, axis=-1, keepdims=True)
        hh = num / jnp.maximum(jnp.abs(den), jnp.exp(-mt))
        hg = _sigmoid(o_ref[:, sl]) * hh
        mu = jnp.mean(hg, axis=-1, keepdims=True)
        var = jnp.mean(jnp.square(hg - mu), axis=-1, keepdims=True)
        hn = (hg - mu) * lax.rsqrt(var + EPS) * nw_ref[:, sl]
        hz_ref[:, sl] = hn * _silu(z_ref[:, sl])
        no_ref[:, sl] = wf * n_prev + wi * k


def _mlstm_sample(u, cst, c0, n0, m0p, cw, cb, bi, bf, nw, *, heads, hd, col_gates):
    bs = u.shape[0]
    inner = heads * hd
    ncs = (CONV_K - 1) * 2 * inner
    full = lambda a: pl.BlockSpec(a.shape, lambda i: (0,) * a.ndim)
    ucol = lambda j: pl.BlockSpec((bs, inner), lambda i, j=j: (0, j))
    gcol = lambda j: pl.BlockSpec((bs, LANE), lambda i, j=j: (0, col_gates // LANE + j))
    tile = jax.ShapeDtypeStruct((bs, LANE), F32)
    conv, qk, kwt, wi, wf, mt = pl.pallas_call(
        functools.partial(_mlstm_sample_prep_body, heads=heads, hd=hd),
        grid=(1,),
        in_specs=[ucol(0), ucol(1), gcol(0), gcol(1), full(cst), full(cw), full(cb), full(bi), full(bf),
                  full(m0p)],
        out_specs=[pl.BlockSpec((bs, ncs), lambda i: (0, 0)),
                   pl.BlockSpec((bs, 2 * inner), lambda i: (0, 0)),
                   pl.BlockSpec((inner, bs), lambda i: (0, 0)),
                   pl.BlockSpec((bs, LANE), lambda i: (0, 0)),
                   pl.BlockSpec((bs, LANE), lambda i: (0, 0)),
                   pl.BlockSpec((bs, LANE), lambda i: (0, 0))],
        out_shape=[jax.ShapeDtypeStruct((bs, ncs), F32),
                   jax.ShapeDtypeStruct((bs, 2 * inner), F32),
                   jax.ShapeDtypeStruct((inner, bs), F32),
                   tile, tile, tile],
        compiler_params=_params("arbitrary"),
        name="mlstm_sample_prep")(u, u, u, u, cst, cw, cb, bi, bf, m0p)
    c_new, qc = pl.pallas_call(
        functools.partial(_mlstm_sample_state_body, heads=heads, hd=hd),
        grid=(bs,),
        in_specs=[pl.BlockSpec(memory_space=pltpu.SMEM),
                  pl.BlockSpec((1, heads, hd, hd), lambda i: (i, 0, 0, 0)),
                  full(qk), pl.BlockSpec((bs, inner), lambda i: (0, 2)), full(kwt)],
        out_specs=[pl.BlockSpec((1, heads, hd, hd), lambda i: (i, 0, 0, 0)),
                   pl.BlockSpec((1, 1, inner), lambda i: (i, 0, 0))],
        out_shape=[jax.ShapeDtypeStruct((bs, heads, hd, hd), F32),
                   jax.ShapeDtypeStruct((bs, 1, inner), F32)],
        compiler_params=_params("arbitrary"),
        name="mlstm_sample_state")(wf[:, :heads], c0, qk, u, kwt)
    hz, n_new = pl.pallas_call(
        functools.partial(_mlstm_sample_finish_body, heads=heads, hd=hd),
        grid=(1,),
        in_specs=[full(qk), ucol(2), ucol(3), ucol(4), pl.BlockSpec((bs, inner), lambda i: (0, 0)),
                  full(n0), full(wi), full(wf), full(mt), full(nw)],
        out_specs=[pl.BlockSpec((bs, inner), lambda i: (0, 0)),
                   pl.BlockSpec((bs, inner), lambda i: (0, 0))],
        out_shape=[jax.ShapeDtypeStruct((bs, inner), F32)] * 2,
        compiler_params=_params("arbitrary"),
        name="mlstm_sample_finish")(qk, u, u, u, qc.reshape(bs, inner), n0, wi, wf, mt, nw)
    return hz, conv, c_new, n_new, mt[:, :heads]


def _pad_cols(w, n):
    return jnp.pad(w, ((0, 0), (0, n - w.shape[1])))


def _row(v, n=None):
    v = v.reshape(1, -1)
    return v if n is None else _pad_cols(v, n)


def kernel(x_prompt, x_sample, cache_attn_k, cache_attn_v, state_ssd_conv, state_ssd, state_mlstm_conv, state_mlstm_c, state_mlstm_n, state_mlstm_m, norm_w, final_norm_w, w_in_even, w_out_even, ssd_conv_w, ssd_conv_b, ssd_dt_bias, ssd_a_log, ssd_d, ssd_norm_w, w_in_odd, w_out_odd, mlstm_conv_w, mlstm_conv_b, mlstm_igate_b, mlstm_fgate_b, mlstm_norm_w):
    batch, seq, d_model = x_prompt.shape
    bs = x_sample.shape[0]
    ssd_heads = ssd_a_log.shape[1]
    ssd_inner = ssd_heads * SSD_HEADDIM
    ssd_cch = ssd_conv_w.shape[2]
    m_inner = mlstm_norm_w.shape[1]
    m_hd = m_inner // MLSTM_HEADS
    mp = batch * seq

    col_z = 4 * ATT_W
    col_xbc = col_z + ssd_inner
    col_dt = col_xbc + ssd_cch
    n_even = col_dt + LANE
    w_in0 = _pad_cols(w_in_even[0], n_even).astype(BF16)
    w_out0 = w_out_even[0].astype(BF16)
    nw0 = _row(norm_w[0])
    cw0, cb0 = ssd_conv_w[0], _row(ssd_conv_b[0])
    dtb = _row(ssd_dt_bias[0], LANE)
    alog = _row(ssd_a_log[0], LANE)
    aloge = _row(jnp.repeat(ssd_a_log[0], SSD_HEADDIM))
    dsk = _row(jnp.repeat(ssd_d[0], SSD_HEADDIM))
    snw = _row(ssd_norm_w[0])

    hp = x_prompt.reshape(mp, d_model)
    hs = x_sample.reshape(bs, d_model)

    up = _norm_matmul(hp, nw0, w_in0, tm=256, panels=1)
    cos_p, sa_p, sb_p = _rope_tables(jnp.arange(seq), LANE)
    att_p, k_p, v_p = _attn_prompt(up, cos_p, sa_p, sb_p, batch=batch, seq=seq)
    y_p, conv_p, st_p = _ssd_prompt(up, cw0, cb0, dtb, alog, dsk, snw, batch=batch, seq=seq,
                                    heads=ssd_heads, col_z=col_z, col_xbc=col_xbc, col_dt=col_dt)
    hp = _out_proj(hp, [att_p, y_p], [w_out0[:ATT_W], w_out0[ATT_W:]], tm=512)

    us = _norm_matmul(hs, nw0, w_in0, tm=bs, panels=1)
    cos_s, sa_s, sb_s = _rope_tables(PAST_LEN + jnp.arange(1), ATT_W)
    xt_s = _rot_sample(us, cos_s, sa_s, sb_s)
    k_s = xt_s[ATT_W:2 * ATT_W].T
    v_s = xt_s[2 * ATT_W:3 * ATT_W].T
    pos_minor = lambda cache: jnp.transpose(cache, (0, 2, 3, 1))
    att_s = _attn_sample(xt_s, pos_minor(cache_attn_k[0]), pos_minor(cache_attn_v[0]), bb=2).T
    y_s, conv_s, st_s = _ssd_sample(us, state_ssd_conv[0].reshape(bs, -1),
                                    state_ssd[0].reshape(bs, ssd_inner, SSD_STATE),
                                    cw0, cb0, dtb, alog, aloge, dsk, snw, heads=ssd_heads,
                                    col_z=col_z, col_xbc=col_xbc, col_dt=col_dt, bb=4)
    hs = _out_proj(hs, [att_s, y_s], [w_out0[:ATT_W], w_out0[ATT_W:]], tm=bs)

    wo = w_in_odd[0]
    gates_at = 4 * m_inner
    zcol = gates_at + 2 * MLSTM_HEADS
    col_gates = 5 * m_inner
    zpad = jnp.zeros((d_model, LANE - MLSTM_HEADS), wo.dtype)
    w_in1 = jnp.concatenate([wo[:, :gates_at], wo[:, zcol:],
                             wo[:, gates_at:gates_at + MLSTM_HEADS], zpad,
                             wo[:, gates_at + MLSTM_HEADS:zcol], zpad], axis=1).astype(BF16)
    w_out1 = w_out_odd[0].astype(BF16)
    nw1 = _row(norm_w[1])
    cw1, cb1 = mlstm_conv_w[0], _row(mlstm_conv_b[0])
    bi = _row(mlstm_igate_b[0], LANE)
    bf = _row(mlstm_fgate_b[0], LANE)
    mnw = _row(mlstm_norm_w[0])
    fnw = _row(final_norm_w)

    up1 = _norm_matmul(hp, nw1, w_in1, tm=256, panels=2)
    hz_p, mconv_p, c_p, n_p, m_p = _mlstm_prompt(up1, cw1, cb1, bi, bf, mnw, batch=batch, seq=seq,
                                                   heads=MLSTM_HEADS, hd=m_hd, col_gates=col_gates)
    y_prompt = _out_proj(hp, [hz_p], [w_out1], fnw, tm=512)

    us1 = _norm_matmul(hs, nw1, w_in1, tm=bs, panels=2)
    m0p = _pad_cols(state_mlstm_m[0], LANE)
    hz_s, mconv_s, c_s, n_s, m_s = _mlstm_sample(us1, state_mlstm_conv[0].reshape(bs, -1),
                                                 state_mlstm_c[0], state_mlstm_n[0].reshape(bs, m_inner),
                                                 m0p, cw1, cb1, bi, bf, mnw,
                                                 heads=MLSTM_HEADS, hd=m_hd, col_gates=col_gates)
    y_sample = _out_proj(hs, [hz_s], [w_out1], fnw, tm=bs)

    tmax = min(seq, CHUNK * max(DILATIONS))
    kv_shape = (1, batch, seq, ATT_HEADS, ATT_HD)
    return (
        y_prompt.reshape(batch, seq, d_model),
        y_sample.reshape(bs, 1, d_model),
        k_p.reshape(kv_shape)[:, :, seq - tmax:],
        v_p.reshape(kv_shape)[:, :, seq - tmax:],
        conv_p[None],
        st_p.reshape(1, batch, ssd_heads, SSD_HEADDIM, SSD_STATE),
        mconv_p[None],
        c_p[None],
        n_p.reshape(1, batch, MLSTM_HEADS, m_hd),
        m_p.reshape(1, batch, MLSTM_HEADS),
        k_s.reshape(1, bs, 1, ATT_HEADS, ATT_HD),
        v_s.reshape(1, bs, 1, ATT_HEADS, ATT_HD),
        conv_s.reshape(1, bs, CONV_K - 1, ssd_cch),
        st_s.reshape(1, bs, ssd_heads, SSD_HEADDIM, SSD_STATE),
        mconv_s.reshape(1, bs, CONV_K - 1, 2 * m_inner),
        c_s[None],
        n_s.reshape(1, bs, MLSTM_HEADS, m_hd),
        m_s.reshape(1, bs, MLSTM_HEADS),
    )
```

```python
import functools

import jax
import jax.numpy as jnp
from jax import lax
from jax.experimental import pallas as pl
from jax.experimental.pallas import tpu as pltpu

F32 = jnp.float32
BF16 = jnp.bfloat16
NEG_INF = float("-inf")
EPS = 1e-6
LANE = 128
CHUNK = 128
VMEM_LIMIT = 56 * 1024 * 1024

CONV_K = 4
ATT_HEADS = 8
ATT_HD = 64
ATT_W = ATT_HEADS * ATT_HD
ROT_DIM = ATT_HD // 4
ROPE_THETA = 500000.0
DILATIONS = (1, 4, 16)
PAST_LEN = 2048
SSD_HEADDIM = 64
SSD_GROUPS = 2
SSD_STATE = 128
MLSTM_HEADS = 8


def _params(*sem):
    return pltpu.CompilerParams(dimension_semantics=sem, vmem_limit_bytes=VMEM_LIMIT)


def _chunks(n, w):
    out, c = [], 0
    while c < n:
        out.append((c, min(w, n - c)))
        c += w
    return out


def _dot(a, b):
    return jnp.dot(a.astype(BF16), b.astype(BF16), preferred_element_type=F32)


def _dot_nt(a, b):
    return lax.dot_general(a.astype(BF16), b.astype(BF16), (((1,), (1,)), ((), ())),
                           preferred_element_type=F32)


def _dot_tn(a, b):
    return lax.dot_general(a.astype(BF16), b.astype(BF16), (((0,), (0,)), ((), ())),
                           preferred_element_type=F32)


def _split3(x):
    hi = x.astype(BF16)
    r1 = x - hi.astype(F32)
    mid = r1.astype(BF16)
    lo = (r1 - mid.astype(F32)).astype(BF16)
    return hi, mid, lo


def _sel_dot(sel, x):
    hi, mid, lo = _split3(x)
    d = lambda p: jnp.dot(sel, p, preferred_element_type=F32)
    return d(hi) + d(mid) + d(lo)


def _dot_sel(x, sel):
    hi, mid, lo = _split3(x)
    d = lambda p: jnp.dot(p, sel, preferred_element_type=F32)
    return d(hi) + d(mid) + d(lo)


def _tril(n):
    r = lax.broadcasted_iota(jnp.int32, (n, n), 0)
    c = lax.broadcasted_iota(jnp.int32, (n, n), 1)
    return r >= c


def _seg_matrix(rows, cols, seg, along_rows):
    r = lax.broadcasted_iota(jnp.int32, (rows, cols), 0)
    c = lax.broadcasted_iota(jnp.int32, (rows, cols), 1)
    m = (r // seg == c) if along_rows else (c // seg == r)
    return m.astype(BF16)


def _row_at(ref, b, cols=slice(None)):
    base = pl.multiple_of((b // 8) * 8, 8)
    tile = ref[pl.ds(base, 8), cols]
    sub = lax.broadcasted_iota(jnp.int32, tile.shape, 0)
    return jnp.sum(jnp.where(sub == b % 8, tile, 0.0), axis=0, keepdims=True)


def _silu(x):
    h = 0.5 * x
    return h + h * jnp.tanh(h)


def _sigmoid(x):
    return 0.5 + 0.5 * jnp.tanh(0.5 * x)


def _softplus(x):
    return jnp.maximum(x, 0.0) + jnp.log1p(jnp.exp(-jnp.abs(x)))


def _rms(x, w):
    return x * lax.rsqrt(jnp.mean(x * x, axis=-1, keepdims=True) + EPS) * w


def _norm_matmul_body(x_ref, nw_ref, w_ref, o_ref, *, chunks):
    xn = _rms(x_ref[...], nw_ref[...]).astype(BF16)
    for c0, cw in chunks:
        o_ref[:, c0:c0 + cw] = jnp.dot(xn, w_ref[:, c0:c0 + cw], preferred_element_type=F32)


def _norm_matmul(x, nw, w, *, tm, panels):
    m, d = x.shape
    n = w.shape[1]
    pn = n // panels
    return pl.pallas_call(
        functools.partial(_norm_matmul_body, chunks=_chunks(pn, 512)),
        grid=(panels, m // tm),
        in_specs=[pl.BlockSpec((tm, d), lambda p, i: (i, 0)),
                  pl.BlockSpec((1, d), lambda p, i: (0, 0)),
                  pl.BlockSpec((d, pn), lambda p, i: (0, p))],
        out_specs=pl.BlockSpec((tm, pn), lambda p, i: (i, p)),
        out_shape=jax.ShapeDtypeStruct((m, n), F32),
        compiler_params=_params("arbitrary", "arbitrary"),
        name="norm_matmul")(x, nw, w)


def _out_proj_body(*refs, n_in, final):
    h_ref = refs[0]
    xs = refs[1:1 + n_in]
    ws = refs[1 + n_in:1 + 2 * n_in]
    rest = refs[1 + 2 * n_in:]
    acc = h_ref[...]
    for x_ref, w_ref in zip(xs, ws):
        acc = acc + jnp.dot(x_ref[...].astype(BF16), w_ref[...], preferred_element_type=F32)
    if final:
        fw_ref, o_ref = rest
        o_ref[...] = _rms(acc, fw_ref[...])
    else:
        (o_ref,) = rest
        o_ref[...] = acc


def _out_proj(h, xs, ws, fw=None, *, tm):
    m, d = h.shape
    n_in = len(xs)
    in_specs = [pl.BlockSpec((tm, d), lambda i: (i, 0))]
    in_specs += [pl.BlockSpec((tm, x.shape[1]), lambda i: (i, 0)) for x in xs]
    in_specs += [pl.BlockSpec(w.shape, lambda i: (0, 0)) for w in ws]
    args = [h, *xs, *ws]
    if fw is not None:
        in_specs.append(pl.BlockSpec((1, d), lambda i: (0, 0)))
        args.append(fw)
    return pl.pallas_call(
        functools.partial(_out_proj_body, n_in=n_in, final=fw is not None),
        grid=(m // tm,),
        in_specs=in_specs,
        out_specs=pl.BlockSpec((tm, d), lambda i: (i, 0)),
        out_shape=jax.ShapeDtypeStruct((m, d), F32),
        compiler_params=_params("arbitrary"),
        name="out_proj")(*args)


def _rope_tables(pos, width):
    half = ROT_DIM // 2
    inv = jnp.power(F32(ROPE_THETA), -jnp.arange(half, dtype=F32) * (2.0 / ROT_DIM))
    ang = pos.astype(F32)[:, None] * inv[None, :]
    cos, sin = jnp.cos(ang), jnp.sin(ang)
    n = pos.shape[0]
    one = jnp.ones((n, ATT_HD - ROT_DIM), F32)
    z8 = jnp.zeros((n, half), F32)
    z48 = jnp.zeros((n, ATT_HD - ROT_DIM), F32)
    c = jnp.concatenate([cos, cos, one], axis=-1)
    sa = jnp.concatenate([-sin, z8, z48], axis=-1)
    sb = jnp.concatenate([z8, sin, z48], axis=-1)
    rep = width // ATT_HD
    return tuple(jnp.tile(t, (1, rep)) for t in (c, sa, sb))


def _rotary(x, c, sa, sb):
    w = x.shape[-1]
    half = ROT_DIM // 2
    return x * c + pltpu.roll(x, w - half, 1) * sa + pltpu.roll(x, half, 1) * sb


def _attn_prompt_body(q_ref, k_ref, v_ref, g_ref, c_ref, sa_ref, sb_ref,
                      att_ref, ko_ref, vo_ref,
                      nat, qd, kd, vd, od, std, o_s, st_s, *, seq):
    nblk = seq // CHUNK
    c, sa, sb = c_ref[...], sa_ref[...], sb_ref[...]
    q = _rotary(q_ref[...], c, sa, sb) * (ATT_HD ** -0.5)
    k = _rotary(k_ref[...], c, sa, sb)
    ko_ref[...] = k
    vo_ref[...] = v_ref[...]
    head_lane = lax.broadcasted_iota(jnp.int32, (1, LANE), 1) // ATT_HD
    nat[0] = q
    nat[1] = k
    nat[2] = v_ref[...]
    zero = jnp.zeros((CHUNK, LANE), BF16)
    for p, d in enumerate(DILATIONS):
        kd[p, 0:CHUNK, :] = zero
        vd[p, 0:CHUNK, :] = zero
        ln = seq // d
        for r in range(d):
            rows = slice(r * ln, (r + 1) * ln)
            krows = slice(CHUNK + r * ln, CHUNK + (r + 1) * ln)
            src = pl.ds(r, ln, stride=d) if d > 1 else slice(None)
            qr = nat[0, src, :]
            qd[p, 0, rows, :] = jnp.where(head_lane == 0, qr, 0.0).astype(BF16)
            qd[p, 1, rows, :] = jnp.where(head_lane == 1, qr, 0.0).astype(BF16)
            kd[p, krows, :] = nat[1, src, :].astype(BF16)
            vd[p, krows, :] = nat[2, src, :].astype(BF16)

    row = lax.broadcasted_iota(jnp.int32, (CHUNK, 2 * CHUNK), 0)
    col = lax.broadcasted_iota(jnp.int32, (CHUNK, 2 * CHUNK), 1)
    band = (col >= row) & (col <= row + CHUNK)
    first_head = lax.broadcasted_iota(jnp.int32, (CHUNK, LANE), 1) < ATT_HD

    for p, d in enumerate(DILATIONS):
        nb = nblk // d

        def block(t, carry, p=p, nb=nb):
            base = pl.multiple_of(t * CHUNK, CHUNK)
            rows = pl.ds(base, CHUNK)
            first = (t % nb) == 0
            valid = band & (col >= jnp.where(first, CHUNK, 0))
            kw = kd[p, pl.ds(base, 2 * CHUNK), :]
            vw = vd[p, pl.ds(base, 2 * CHUNK), :]
            parts = []
            for hh in range(2):
                s = lax.dot_general(qd[p, hh, rows, :], kw, (((1,), (1,)), ((), ())),
                                    preferred_element_type=F32)
                s = jnp.where(valid, s, NEG_INF)
                m = jnp.max(s, axis=-1, keepdims=True)
                e = jnp.exp(s - m)
                o = jnp.dot(e.astype(BF16), vw, preferred_element_type=F32)
                parts.append((o, m, jnp.sum(e, axis=-1, keepdims=True)))
            (o0, m0, l0), (o1, m1, l1) = parts
            wide = lambda t: jnp.broadcast_to(t, (CHUNK, LANE))
            od[rows, :] = jnp.where(first_head, o0, o1)
            std[0, rows, :] = jnp.where(first_head, wide(m0), wide(m1))
            std[1, rows, :] = jnp.where(first_head, wide(l0), wide(l1))
            return carry

        lax.fori_loop(0, nblk, block, 0, unroll=True)
        if d == 1:
            o_s[p] = od[...]
            st_s[p] = std[...]
        else:
            ln = seq // d
            for r in range(d):
                o_s[p, pl.ds(r, ln, stride=d), :] = od[r * ln:(r + 1) * ln, :]
                st_s[p, 0, pl.ds(r, ln, stride=d), :] = std[0, r * ln:(r + 1) * ln, :]
                st_s[p, 1, pl.ds(r, ln, stride=d), :] = std[1, r * ln:(r + 1) * ln, :]

    np_ = len(DILATIONS)

    def combine(t, carry):
        base = pl.multiple_of(t * CHUNK, CHUNK)
        rows = pl.ds(base, CHUNK)
        ms = [st_s[p, 0, rows, :] for p in range(np_)]
        mx = functools.reduce(jnp.maximum, ms)
        ws = [jnp.exp(mm - mx) for mm in ms]
        num = sum(w * o_s[p, rows, :] for p, w in enumerate(ws))
        den = sum(w * st_s[p, 1, rows, :] for p, w in enumerate(ws))
        att_ref[rows, :] = (num / den) * _silu(g_ref[rows, :])
        return carry

    lax.fori_loop(0, nblk, combine, 0, unroll=2)


def _attn_prompt(u, cos, sa, sb, *, batch, seq):
    m = batch * seq
    npair = ATT_W // LANE
    blk = lambda off: pl.BlockSpec((seq, LANE), lambda b, hp, off=off: (b, off + hp))
    tab = pl.BlockSpec((seq, LANE), lambda b, hp: (0, 0))
    out = pl.BlockSpec((seq, LANE), lambda b, hp: (b, hp))
    np_ = len(DILATIONS)
    return pl.pallas_call(
        functools.partial(_attn_prompt_body, seq=seq),
        grid=(batch, npair),
        in_specs=[blk(0), blk(npair), blk(2 * npair), blk(3 * npair), tab, tab, tab],
        out_specs=[out, out, out],
        out_shape=[jax.ShapeDtypeStruct((m, ATT_W), F32)] * 3,
        scratch_shapes=[pltpu.VMEM((3, seq, LANE), F32),
                        pltpu.VMEM((np_, 2, seq, LANE), BF16),
                        pltpu.VMEM((np_, seq + CHUNK, LANE), BF16),
                        pltpu.VMEM((np_, seq + CHUNK, LANE), BF16),
                        pltpu.VMEM((seq, LANE), F32),
                        pltpu.VMEM((2, seq, LANE), F32),
                        pltpu.VMEM((np_, seq, LANE), F32),
                        pltpu.VMEM((np_, 2, seq, LANE), F32)],
        compiler_params=_params("arbitrary", "arbitrary"),
        name="attn_prompt")(u, u, u, u, cos, sa, sb)


def _ssd_prompt_body(xbc_ref, z_ref, dt_ref, cw_ref, cb_ref, dtb_ref, alog_ref, dsk_ref, nw_ref,
                     y_ref, conv_ref, st_ref, ext, st, ys, *, heads, nchunk):
    c = pl.program_id(1)
    inner = heads * SSD_HEADDIM
    gw = SSD_STATE
    hpg = heads // SSD_GROUPS

    @pl.when(c == 0)
    def _():
        ext[0:8, :] = jnp.zeros((8, ext.shape[1]), F32)
        st[...] = jnp.zeros(st.shape, F32)

    ext[8:8 + CHUNK, :] = xbc_ref[...]
    conv = cb_ref[...] + ext[5:5 + CHUNK, :] * cw_ref[0:1, :]
    for j in range(1, CONV_K):
        conv = conv + ext[5 + j:5 + j + CHUNK, :] * cw_ref[j:j + 1, :]
    act = _silu(conv)

    @pl.when(c == nchunk - 1)
    def _():
        conv_ref[...] = ext[CHUNK + 5:CHUNK + 8, :]

    ext[0:8, :] = ext[CHUNK:CHUNK + 8, :]

    dt = _softplus(dt_ref[...] + dtb_ref[...])
    a = -jnp.exp(alog_ref[...])
    tril = _tril(CHUNK)
    acum = _sel_dot(jnp.where(tril, 1.0, 0.0).astype(BF16), dt * a)
    acum_t = acum.T
    last = acum[CHUNK - 1:CHUNK, :]
    elast = jnp.exp(last)
    expand = _seg_matrix(LANE, inner, SSD_HEADDIM, False)
    dt_e = _dot_sel(dt, expand)
    eacum_e = _dot_sel(jnp.exp(acum), expand)
    wend_e = _dot_sel(jnp.exp(last - acum) * dt, expand)
    xs = act[:, 0:inner]
    xdt = xs * dt_e
    xw = xs * wend_e
    hw = hpg * SSD_HEADDIM
    first_half = lax.broadcasted_iota(jnp.int32, (CHUNK, LANE), 1) < SSD_HEADDIM

    for g in range(SSD_GROUPS):
        bm = act[:, inner + g * gw:inner + (g + 1) * gw]
        cm = act[:, inner + SSD_GROUPS * gw + g * gw:inner + SSD_GROUPS * gw + (g + 1) * gw]
        cb = _dot_nt(cm, bm)
        sg = st[g * hw:(g + 1) * hw, :]
        y_inter = _dot_nt(cm, sg) * eacum_e[:, g * hw:(g + 1) * hw]
        s_local = _dot_tn(xw[:, g * hw:(g + 1) * hw], bm)
        for hg in range(hpg):
            h = g * hpg + hg
            rows = slice(hg * SSD_HEADDIM, (hg + 1) * SSD_HEADDIM)
            st[g * hw + hg * SSD_HEADDIM:g * hw + (hg + 1) * SSD_HEADDIM, :] = (
                sg[rows, :] * elast[:, h:h + 1] + s_local[rows, :])
        for pr in range(hpg // 2):
            lo = g * hw + pr * LANE
            xp = xdt[:, lo:lo + LANE].astype(BF16)
            halves = []
            for hh in range(2):
                h = g * hpg + 2 * pr + hh
                seg = acum[:, h:h + 1] - acum_t[h:h + 1, :]
                mh = cb * jnp.exp(jnp.where(tril, seg, NEG_INF))
                halves.append(jnp.dot(mh.astype(BF16), xp, preferred_element_type=F32))
            ys[:, lo:lo + LANE] = (jnp.where(first_half, halves[0], halves[1])
                                   + y_inter[:, pr * LANE:(pr + 1) * LANE])

    yt = (ys[...] + dsk_ref[...] * xs) * _silu(z_ref[...])
    y_ref[...] = _rms(yt, nw_ref[...])

    @pl.when(c == nchunk - 1)
    def _():
        st_ref[...] = st[...]


def _ssd_prompt(u, cw, cb, dtb, alog, dsk, nw, *, batch, seq, heads, col_z, col_xbc, col_dt):
    nchunk = seq // CHUNK
    inner = heads * SSD_HEADDIM
    cch = cw.shape[1]
    row = lambda b, c: b * nchunk + c
    full = lambda a: pl.BlockSpec(a.shape, lambda b, c: (0, 0))
    return pl.pallas_call(
        functools.partial(_ssd_prompt_body, heads=heads, nchunk=nchunk),
        grid=(batch, nchunk),
        in_specs=[pl.BlockSpec((CHUNK, cch), lambda b, c: (row(b, c), col_xbc // cch)),
                  pl.BlockSpec((CHUNK, inner), lambda b, c: (row(b, c), col_z // inner)),
                  pl.BlockSpec((CHUNK, LANE), lambda b, c: (row(b, c), col_dt // LANE)),
                  full(cw), full(cb), full(dtb), full(alog), full(dsk), full(nw)],
        out_specs=[pl.BlockSpec((CHUNK, inner), lambda b, c: (row(b, c), 0)),
                   pl.BlockSpec((None, CONV_K - 1, cch), lambda b, c: (b, 0, 0)),
                   pl.BlockSpec((None, inner, SSD_STATE), lambda b, c: (b, 0, 0))],
        out_shape=[jax.ShapeDtypeStruct((batch * seq, inner), F32),
                   jax.ShapeDtypeStruct((batch, CONV_K - 1, cch), F32),
                   jax.ShapeDtypeStruct((batch, inner, SSD_STATE), F32)],
        scratch_shapes=[pltpu.VMEM((CHUNK + 8, cch), F32),
                        pltpu.VMEM((inner, SSD_STATE), F32),
                        pltpu.VMEM((CHUNK, inner), F32)],
        compiler_params=_params("arbitrary", "arbitrary"),
        name="ssd_prompt")(u, u, u, cw, cb, dtb, alog, dsk, nw)


def _mlstm_prompt_body(q_ref, k_ref, v_ref, o_ref, z_ref, gi_ref, gf_ref,
                       cw_ref, cb_ref, bi_ref, bf_ref, nw_ref,
                       hz_ref, conv_ref, c_out, n_out, m_out,
                       histq, histk, c_s, n_s, m_s, *, nchunk, heads, hd):
    c = pl.program_id(1)
    inner = heads * hd

    @pl.when(c == 0)
    def _():
        histq[...] = jnp.zeros((8, inner), F32)
        histk[...] = jnp.zeros((8, inner), F32)
        c_s[...] = jnp.zeros(c_s.shape, F32)
        n_s[...] = jnp.zeros(n_s.shape, F32)
        m_s[...] = jnp.full(m_s.shape, NEG_INF, F32)

    def conv_head(hist, x_ref, col0, h):
        sl = slice(h * hd, (h + 1) * hd)
        cols = slice(col0 + h * hd, col0 + (h + 1) * hd)
        xp = jnp.concatenate([hist[:, sl], x_ref[:, sl]], axis=0)
        acc = cb_ref[:, cols] + xp[5:5 + CHUNK, :] * cw_ref[0:1, cols]
        for j in range(1, CONV_K):
            acc = acc + xp[5 + j:5 + j + CHUNK, :] * cw_ref[j:j + 1, cols]
        return _silu(acc)

    @pl.when(c == nchunk - 1)
    def _():
        conv_ref[:, 0:inner] = q_ref[CHUNK - (CONV_K - 1):CHUNK, :]
        conv_ref[:, inner:] = k_ref[CHUNK - (CONV_K - 1):CHUNK, :]

    it = gi_ref[...] + bi_ref[...]
    logf = -_softplus(-(gf_ref[...] + bf_ref[...]))
    tril = _tril(CHUNK)
    bc = _sel_dot(tril.astype(BF16), logf)
    it_t = it.T
    bc_t = bc.T

    for h in range(heads):
        sl = slice(h * hd, (h + 1) * hd)
        q = conv_head(histq, q_ref, 0, h)
        k = conv_head(histk, k_ref, inner, h) * (hd ** -0.5)
        v = v_ref[:, sl]
        i_col, b_col = it[:, h:h + 1], bc[:, h:h + 1]
        i_row, b_row = it_t[h:h + 1, :], bc_t[h:h + 1, :]
        m_prev = m_s[h]
        dmat = jnp.where(tril, b_col - b_row + i_row, NEG_INF)
        inter = b_col + m_prev
        m_t = jnp.maximum(inter, jnp.max(dmat, axis=-1, keepdims=True))
        w_intra = jnp.exp(dmat - m_t)
        w_inter = jnp.exp(inter - m_t)
        att = w_intra * _dot_nt(q, k)
        c_prev = c_s[h]
        n_prev = n_s[h]
        num = _dot(att, v) + w_inter * _dot(q, c_prev)
        qn = _dot_nt(q, jnp.broadcast_to(n_prev, (8, hd)))[:, 0:1]
        den = jnp.sum(att, axis=-1, keepdims=True) + w_inter * qn
        hh = num / jnp.maximum(jnp.abs(den), jnp.exp(-m_t))

        b_last = b_col[CHUNK - 1:CHUNK, :]
        logw = b_last - b_col + i_col
        m_new = jnp.maximum(b_last + m_prev, jnp.max(logw, axis=0, keepdims=True))
        ws = jnp.exp(logw - m_new)
        ws_row = jnp.exp(b_last - b_row + i_row - m_new)
        scale = jnp.exp(b_last + m_prev - m_new)
        c_s[h] = scale * c_prev + _dot_tn(k, ws * v)
        n_s[h] = scale * n_prev + _dot(jnp.broadcast_to(ws_row, (8, CHUNK)), k)[0:1, :]
        m_s[h] = m_new

        hg = _sigmoid(o_ref[:, sl]) * hh
        mu = jnp.mean(hg, axis=-1, keepdims=True)
        var = jnp.mean(jnp.square(hg - mu), axis=-1, keepdims=True)
        hn = (hg - mu) * lax.rsqrt(var + EPS) * nw_ref[:, sl]
        hz_ref[:, sl] = hn * _silu(z_ref[:, sl])

    histq[...] = q_ref[CHUNK - 8:CHUNK, :]
    histk[...] = k_ref[CHUNK - 8:CHUNK, :]

    @pl.when(c == nchunk - 1)
    def _():
        c_out[...] = c_s[...]
        n_out[...] = n_s[...]
        m_out[...] = m_s[...]


def _mlstm_prompt(u, cw, cb, bi, bf, nw, *, batch, seq, heads, hd, col_gates):
    nchunk = seq // CHUNK
    inner = heads * hd
    row = lambda b, c: b * nchunk + c
    ublk = lambda j: pl.BlockSpec((CHUNK, inner), lambda b, c, j=j: (row(b, c), j))
    gblk = lambda off: pl.BlockSpec((CHUNK, LANE), lambda b, c, off=off: (row(b, c), col_gates // LANE + off))
    full = lambda a: pl.BlockSpec(a.shape, lambda b, c: (0, 0))
    return pl.pallas_call(
        functools.partial(_mlstm_prompt_body, nchunk=nchunk, heads=heads, hd=hd),
        grid=(batch, nchunk),
        in_specs=[ublk(0), ublk(1), ublk(2), ublk(3), ublk(4), gblk(0), gblk(1),
                  full(cw), full(cb), full(bi), full(bf), full(nw)],
        out_specs=[pl.BlockSpec((CHUNK, inner), lambda b, c: (row(b, c), 0)),
                   pl.BlockSpec((None, CONV_K - 1, 2 * inner), lambda b, c: (b, 0, 0)),
                   pl.BlockSpec((None, heads, hd, hd), lambda b, c: (b, 0, 0, 0)),
                   pl.BlockSpec((None, heads, 1, hd), lambda b, c: (b, 0, 0, 0)),
                   pl.BlockSpec((None, heads, 1, 1), lambda b, c: (b, 0, 0, 0))],
        out_shape=[jax.ShapeDtypeStruct((batch * seq, inner), F32),
                   jax.ShapeDtypeStruct((batch, CONV_K - 1, 2 * inner), F32),
                   jax.ShapeDtypeStruct((batch, heads, hd, hd), F32),
                   jax.ShapeDtypeStruct((batch, heads, 1, hd), F32),
                   jax.ShapeDtypeStruct((batch, heads, 1, 1), F32)],
        scratch_shapes=[pltpu.VMEM((8, inner), F32),
                        pltpu.VMEM((8, inner), F32),
                        pltpu.VMEM((heads, hd, hd), F32),
                        pltpu.VMEM((heads, 1, hd), F32),
                        pltpu.VMEM((heads, 1, 1), F32)],
        compiler_params=_params("arbitrary", "arbitrary"),
        name="mlstm_prompt")(u, u, u, u, u, u, u, cw, cb, bi, bf, nw)


def _rot_sample_body(q_ref, k_ref, v_ref, g_ref, c_ref, sa_ref, sb_ref, xt_ref):
    c, sa, sb = c_ref[...], sa_ref[...], sb_ref[...]
    q = _rotary(q_ref[...], c, sa, sb) * (ATT_HD ** -0.5)
    k = _rotary(k_ref[...], c, sa, sb)
    xt_ref[...] = jnp.concatenate([q, k, v_ref[...], g_ref[...]], axis=-1).T


def _rot_sample(u, cos, sa, sb):
    bs = u.shape[0]
    ublk = lambda off: pl.BlockSpec((bs, ATT_W), lambda i, off=off: (0, off))
    tab = pl.BlockSpec((1, ATT_W), lambda i: (0, 0))
    return pl.pallas_call(
        _rot_sample_body,
        grid=(1,),
        in_specs=[ublk(0), ublk(1), ublk(2), ublk(3), tab, tab, tab],
        out_specs=pl.BlockSpec((4 * ATT_W, bs), lambda i: (0, 0)),
        out_shape=jax.ShapeDtypeStruct((4 * ATT_W, bs), F32),
        compiler_params=_params("arbitrary"),
        name="rot_sample")(u, u, u, u, cos, sa, sb)


def _key_multiplicity(wb):
    back = wb - lax.broadcasted_iota(jnp.int32, (1, wb), 1)
    cnt = jnp.zeros((1, wb), F32)
    for d in DILATIONS:
        cnt = cnt + ((back <= CHUNK * d) & (lax.rem(back, d) == 0)).astype(F32)
    return cnt


def _attn_sample_body(x_ref, kc_ref, vc_ref, att_ref, *, bb, wb):
    i = pl.program_id(0)
    bs = x_ref.shape[1]
    nt = wb // LANE

    @pl.when(i == 0)
    def _():
        att_ref[...] = jnp.zeros(att_ref.shape, F32)

    cnt = _key_multiplicity(wb)
    valid = cnt > 0.0
    npat = float(len(DILATIONS))
    rid = lax.broadcasted_iota(jnp.int32, (bs, LANE), 0)
    lid = lax.broadcasted_iota(jnp.int32, (ATT_HD, bs), 1)
    for t in range(bb):
        b = i * bb + t
        cols = _dot_sel(x_ref[...], (rid == b).astype(BF16))
        for h in range(ATT_HEADS):
            lo = h * ATT_HD
            qc = cols[lo:lo + ATT_HD]
            kc = cols[ATT_W + lo:ATT_W + lo + ATT_HD]
            vc = cols[2 * ATT_W + lo:2 * ATT_W + lo + ATT_HD]
            gc = cols[3 * ATT_W + lo:3 * ATT_W + lo + ATT_HD]
            kt = kc_ref[t, h]
            vt = vc_ref[t, h]
            s = jnp.sum(kt * jnp.tile(qc, (1, nt)), axis=0, keepdims=True)
            s = jnp.where(valid, s, NEG_INF)
            s_self = jnp.sum(qc * kc, axis=0, keepdims=True)[:, 0:1]
            m = jnp.maximum(jnp.max(s, axis=1, keepdims=True), s_self)
            p = cnt * jnp.exp(s - m)
            p_self = npat * jnp.exp(s_self - m)
            den = jnp.sum(p, axis=1, keepdims=True) + p_self
            acc = vt[:, 0:LANE] * p[:, 0:LANE]
            for j in range(1, nt):
                acc = acc + vt[:, j * LANE:(j + 1) * LANE] * p[:, j * LANE:(j + 1) * LANE]
            o = jnp.sum(acc, axis=1, keepdims=True) + p_self * vc[:, 0:1]
            o = o / den * _silu(gc[:, 0:1])
            att_ref[lo:lo + ATT_HD, :] = jnp.where(lid == b, o, att_ref[lo:lo + ATT_HD, :])


def _attn_sample(xt, ck, cv, *, bb):
    bs, wb = ck.shape[0], ck.shape[3]
    cache = pl.BlockSpec((bb, ATT_HEADS, ATT_HD, wb), lambda i: (i, 0, 0, 0))
    return pl.pallas_call(
        functools.partial(_attn_sample_body, bb=bb, wb=wb),
        grid=(bs // bb,),
        in_specs=[pl.BlockSpec(xt.shape, lambda i: (0, 0)), cache, cache],
        out_specs=pl.BlockSpec((ATT_W, bs), lambda i: (0, 0)),
        out_shape=jax.ShapeDtypeStruct((ATT_W, bs), F32),
        compiler_params=_params("arbitrary"),
        name="attn_sample")(xt, ck, cv)


def _ssd_sample_prep_body(xbc_ref, dt_ref, cst_ref, cw_ref, cb_ref, dtb_ref, alog_ref, aloge_ref, dsk_ref,
                          conv_ref, yloc_ref, xdtt_ref, bc_ref, da_ref, dae_ref, *, heads):
    inner = heads * SSD_HEADDIM
    cch = xbc_ref.shape[1]
    gw = SSD_STATE
    x = xbc_ref[...]
    acc = cb_ref[...] + x * cw_ref[CONV_K - 1:CONV_K, :]
    for j in range(CONV_K - 1):
        acc = acc + cst_ref[:, j * cch:(j + 1) * cch] * cw_ref[j:j + 1, :]
    for j in range(CONV_K - 2):
        conv_ref[:, j * cch:(j + 1) * cch] = cst_ref[:, (j + 1) * cch:(j + 2) * cch]
    conv_ref[:, (CONV_K - 2) * cch:(CONV_K - 1) * cch] = x
    act = _silu(acc)
    xs = act[:, 0:inner]
    bc_ref[...] = act[:, inner:]
    dt = _softplus(dt_ref[...] + dtb_ref[...])
    da_ref[...] = jnp.exp(dt * (-jnp.exp(alog_ref[...])))
    expand = _seg_matrix(LANE, inner, SSD_HEADDIM, False)
    dte = _dot_sel(dt, expand)
    dae_ref[...] = jnp.exp(dte * (-jnp.exp(aloge_ref[...])))
    xdt = xs * dte
    xdtt_ref[...] = xdt.T
    hw = inner // SSD_GROUPS
    parts = []
    for g in range(SSD_GROUPS):
        bm = act[:, inner + g * gw:inner + (g + 1) * gw]
        cm = act[:, inner + SSD_GROUPS * gw + g * gw:inner + SSD_GROUPS * gw + (g + 1) * gw]
        cbg = jnp.sum(cm * bm, axis=-1, keepdims=True)
        parts.append(cbg * xdt[:, g * hw:(g + 1) * hw])
    yloc_ref[...] = jnp.concatenate(parts, axis=-1) + dsk_ref[...] * xs


def _ssd_sample_state_body(da_ref, s_ref, xdtt_ref, bc_ref, so_ref, yi_ref, *, bb, heads):
    i = pl.program_id(0)
    bs = bc_ref.shape[0]
    inner = heads * SSD_HEADDIM
    hw = inner // SSD_GROUPS
    hpg = heads // SSD_GROUPS
    gw = SSD_STATE
    rid = lax.broadcasted_iota(jnp.int32, (bs, gw), 0)
    for t in range(bb):
        b = i * bb + t
        parts = []
        for g in range(SSD_GROUPS):
            mg = jnp.where(rid == b, bc_ref[:, g * gw:(g + 1) * gw], 0.0)
            sl = _dot(xdtt_ref[g * hw:(g + 1) * hw, :], mg)
            crow = _row_at(bc_ref, b, slice(SSD_GROUPS * gw + g * gw, SSD_GROUPS * gw + (g + 1) * gw))
            sg = s_ref[t, g * hw:(g + 1) * hw, :]
            parts.append(_dot_nt(jnp.broadcast_to(crow, (8, gw)), sg)[0:1, :])
            for hg in range(hpg):
                h = g * hpg + hg
                lo = hg * SSD_HEADDIM
                so_ref[t, h * SSD_HEADDIM:(h + 1) * SSD_HEADDIM, :] = (
                    sg[lo:lo + SSD_HEADDIM, :] * da_ref[b, h] + sl[lo:lo + SSD_HEADDIM, :])
        yi_ref[t] = jnp.concatenate(parts, axis=-1)


def _ssd_sample_finish_body(yloc_ref, yi_ref, dae_ref, z_ref, nw_ref, y_ref):
    y = (yloc_ref[...] + yi_ref[...] * dae_ref[...]) * _silu(z_ref[...])
    y_ref[...] = _rms(y, nw_ref[...])


def _ssd_sample(u, cst, state, cw, cb, dtb, alog, aloge, dsk, nw, *, heads, col_z, col_xbc, col_dt, bb):
    bs = u.shape[0]
    inner = heads * SSD_HEADDIM
    cch = cw.shape[1]
    ncs = (CONV_K - 1) * cch
    bcw = 2 * SSD_GROUPS * SSD_STATE
    full = lambda a: pl.BlockSpec(a.shape, lambda i: (0,) * a.ndim)
    conv, yloc, xdtt, bc, da, dae = pl.pallas_call(
        functools.partial(_ssd_sample_prep_body, heads=heads),
        grid=(1,),
        in_specs=[pl.BlockSpec((bs, cch), lambda i: (0, col_xbc // cch)),
                  pl.BlockSpec((bs, LANE), lambda i: (0, col_dt // LANE)),
                  full(cst), full(cw), full(cb), full(dtb), full(alog), full(aloge), full(dsk)],
        out_specs=[pl.BlockSpec((bs, ncs), lambda i: (0, 0)),
                   pl.BlockSpec((bs, inner), lambda i: (0, 0)),
                   pl.BlockSpec((inner, bs), lambda i: (0, 0)),
                   pl.BlockSpec((bs, bcw), lambda i: (0, 0)),
                   pl.BlockSpec((bs, LANE), lambda i: (0, 0)),
                   pl.BlockSpec((bs, inner), lambda i: (0, 0))],
        out_shape=[jax.ShapeDtypeStruct((bs, ncs), F32),
                   jax.ShapeDtypeStruct((bs, inner), F32),
                   jax.ShapeDtypeStruct((inner, bs), F32),
                   jax.ShapeDtypeStruct((bs, bcw), F32),
                   jax.ShapeDtypeStruct((bs, LANE), F32),
                   jax.ShapeDtypeStruct((bs, inner), F32)],
        compiler_params=_params("arbitrary"),
        name="ssd_sample_prep")(u, u, cst, cw, cb, dtb, alog, aloge, dsk)
    new_state, yi = pl.pallas_call(
        functools.partial(_ssd_sample_state_body, bb=bb, heads=heads),
        grid=(bs // bb,),
        in_specs=[pl.BlockSpec(memory_space=pltpu.SMEM),
                  pl.BlockSpec((bb, inner, SSD_STATE), lambda i: (i, 0, 0)),
                  full(xdtt), full(bc)],
        out_specs=[pl.BlockSpec((bb, inner, SSD_STATE), lambda i: (i, 0, 0)),
                   pl.BlockSpec((bb, 1, inner), lambda i: (i, 0, 0))],
        out_shape=[jax.ShapeDtypeStruct((bs, inner, SSD_STATE), F32),
                   jax.ShapeDtypeStruct((bs, 1, inner), F32)],
        compiler_params=_params("arbitrary"),
        name="ssd_sample_state")(da[:, :heads], state, xdtt, bc)
    y = pl.pallas_call(
        _ssd_sample_finish_body,
        grid=(1,),
        in_specs=[full(yloc), pl.BlockSpec((bs, inner), lambda i: (0, 0)), full(dae),
                  pl.BlockSpec((bs, inner), lambda i: (0, col_z // inner)), full(nw)],
        out_specs=pl.BlockSpec((bs, inner), lambda i: (0, 0)),
        out_shape=jax.ShapeDtypeStruct((bs, inner), F32),
        compiler_params=_params("arbitrary"),
        name="ssd_sample_finish")(yloc, yi.reshape(bs, inner), dae, u, nw)
    return y, conv, new_state


def _mlstm_sample_prep_body(q_ref, k_ref, gi_ref, gf_ref, cst_ref, cw_ref, cb_ref, bi_ref, bf_ref, m_ref,
                            conv_ref, qk_ref, kwt_ref, wi_ref, wf_ref, mt_ref, *, heads, hd):
    inner = heads * hd
    cch = 2 * inner
    x = jnp.concatenate([q_ref[...], k_ref[...]], axis=-1)
    acc = cb_ref[...] + x * cw_ref[CONV_K - 1:CONV_K, :]
    for j in range(CONV_K - 1):
        acc = acc + cst_ref[:, j * cch:(j + 1) * cch] * cw_ref[j:j + 1, :]
    for j in range(CONV_K - 2):
        conv_ref[:, j * cch:(j + 1) * cch] = cst_ref[:, (j + 1) * cch:(j + 2) * cch]
    conv_ref[:, (CONV_K - 2) * cch:(CONV_K - 1) * cch] = x
    act = _silu(acc)
    q = act[:, 0:inner]
    k = act[:, inner:] * (hd ** -0.5)
    it = gi_ref[...] + bi_ref[...]
    inter = -_softplus(-(gf_ref[...] + bf_ref[...])) + m_ref[...]
    mt = jnp.maximum(inter, it)
    wi = jnp.exp(it - mt)
    wf = jnp.exp(inter - mt)
    wi_ref[...] = wi
    wf_ref[...] = wf
    mt_ref[...] = mt
    qk_ref[:, 0:inner] = q
    qk_ref[:, inner:] = k
    kw = jnp.concatenate([k[:, h * hd:(h + 1) * hd] * wi[:, h:h + 1] for h in range(heads)], axis=-1)
    kwt_ref[...] = kw.T


def _mlstm_sample_state_body(wf_ref, c_ref, qk_ref, v_ref, kwt_ref, co_ref, qc_ref, *, heads, hd):
    b = pl.program_id(0)
    bs = v_ref.shape[0]
    rid = lax.broadcasted_iota(jnp.int32, (bs, hd), 0)
    parts = []
    for h in range(heads):
        cp = c_ref[0, h]
        qrow = _row_at(qk_ref, b, slice(h * hd, (h + 1) * hd))
        parts.append(_dot(jnp.broadcast_to(qrow, (8, hd)), cp)[0:1, :])
        mh = jnp.where(rid == b, v_ref[:, h * hd:(h + 1) * hd], 0.0)
        co_ref[0, h] = cp * wf_ref[b, h] + _dot(kwt_ref[h * hd:(h + 1) * hd, :], mh)
    qc_ref[0] = jnp.concatenate(parts, axis=-1)


def _mlstm_sample_finish_body(qk_ref, v_ref, o_ref, z_ref, qc_ref, n_ref, wi_ref, wf_ref, mt_ref, nw_ref,
                              hz_ref, no_ref, *, heads, hd):
    inner = heads * hd
    for h in range(heads):
        sl = slice(h * hd, (h + 1) * hd)
        q = qk_ref[:, sl]
        k = qk_ref[:, inner + h * hd:inner + (h + 1) * hd]
        wi = wi_ref[:, h:h + 1]
        wf = wf_ref[:, h:h + 1]
        mt = mt_ref[:, h:h + 1]
        n_prev = n_ref[:, sl]
        att = wi * jnp.sum(q * k, axis=-1, keepdims=True)
        num = att * v_ref[:, sl] + wf * qc_ref[:, sl]
        den = att + wf * jnp.sum(q * n_prev, axis=-1, keepdims=True)
        hh = num / jnp.maximum(jnp.abs(den), jnp.exp(-mt))
        hg = _sigmoid(o_ref[:, sl]) * hh
        mu = jnp.mean(hg, axis=-1, keepdims=True)
        var = jnp.mean(jnp.square(hg - mu), axis=-1, keepdims=True)
        hn = (hg - mu) * lax.rsqrt(var + EPS) * nw_ref[:, sl]
        hz_ref[:, sl] = hn * _silu(z_ref[:, sl])
        no_ref[:, sl] = wf * n_prev + wi * k


def _mlstm_sample(u, cst, c0, n0, m0p, cw, cb, bi, bf, nw, *, heads, hd, col_gates):
    bs = u.shape[0]
    inner = heads * hd
    ncs = (CONV_K - 1) * 2 * inner
    full = lambda a: pl.BlockSpec(a.shape, lambda i: (0,) * a.ndim)
    ucol = lambda j: pl.BlockSpec((bs, inner), lambda i, j=j: (0, j))
    gcol = lambda j: pl.BlockSpec((bs, LANE), lambda i, j=j: (0, col_gates // LANE + j))
    tile = jax.ShapeDtypeStruct((bs, LANE), F32)
    conv, qk, kwt, wi, wf, mt = pl.pallas_call(
        functools.partial(_mlstm_sample_prep_body, heads=heads, hd=hd),
        grid=(1,),
        in_specs=[ucol(0), ucol(1), gcol(0), gcol(1), full(cst), full(cw), full(cb), full(bi), full(bf),
                  full(m0p)],
        out_specs=[pl.BlockSpec((bs, ncs), lambda i: (0, 0)),
                   pl.BlockSpec((bs, 2 * inner), lambda i: (0, 0)),
                   pl.BlockSpec((inner, bs), lambda i: (0, 0)),
                   pl.BlockSpec((bs, LANE), lambda i: (0, 0)),
                   pl.BlockSpec((bs, LANE), lambda i: (0, 0)),
                   pl.BlockSpec((bs, LANE), lambda i: (0, 0))],
        out_shape=[jax.ShapeDtypeStruct((bs, ncs), F32),
                   jax.ShapeDtypeStruct((bs, 2 * inner), F32),
                   jax.ShapeDtypeStruct((inner, bs), F32),
                   tile, tile, tile],
        compiler_params=_params("arbitrary"),
        name="mlstm_sample_prep")(u, u, u, u, cst, cw, cb, bi, bf, m0p)
    c_new, qc = pl.pallas_call(
        functools.partial(_mlstm_sample_state_body, heads=heads, hd=hd),
        grid=(bs,),
        in_specs=[pl.BlockSpec(memory_space=pltpu.SMEM),
                  pl.BlockSpec((1, heads, hd, hd), lambda i: (i, 0, 0, 0)),
                  full(qk), pl.BlockSpec((bs, inner), lambda i: (0, 2)), full(kwt)],
        out_specs=[pl.BlockSpec((1, heads, hd, hd), lambda i: (i, 0, 0, 0)),
                   pl.BlockSpec((1, 1, inner), lambda i: (i, 0, 0))],
        out_shape=[jax.ShapeDtypeStruct((bs, heads, hd, hd), F32),
                   jax.ShapeDtypeStruct((bs, 1, inner), F32)],
        compiler_params=_params("arbitrary"),
        name="mlstm_sample_state")(wf[:, :heads], c0, qk, u, kwt)
    hz, n_new = pl.pallas_call(
        functools.partial(_mlstm_sample_finish_body, heads=heads, hd=hd),
        grid=(1,),
        in_specs=[full(qk), ucol(2), ucol(3), ucol(4), pl.BlockSpec((bs, inner), lambda i: (0, 0)),
                  full(n0), full(wi), full(wf), full(mt), full(nw)],
        out_specs=[pl.BlockSpec((bs, inner), lambda i: (0, 0)),
                   pl.BlockSpec((bs, inner), lambda i: (0, 0))],
        out_shape=[jax.ShapeDtypeStruct((bs, inner), F32)] * 2,
        compiler_params=_params("arbitrary"),
        name="mlstm_sample_finish")(qk, u, u, u, qc.reshape(bs, inner), n0, wi, wf, mt, nw)
    return hz, conv, c_new, n_new, mt[:, :heads]


def _pad_cols(w, n):
    return jnp.pad(w, ((0, 0), (0, n - w.shape[1])))


def _row(v, n=None):
    v = v.reshape(1, -1)
    return v if n is None else _pad_cols(v, n)


def kernel(x_prompt, x_sample, cache_attn_k, cache_attn_v, state_ssd_conv, state_ssd, state_mlstm_conv, state_mlstm_c, state_mlstm_n, state_mlstm_m, norm_w, final_norm_w, w_in_even, w_out_even, ssd_conv_w, ssd_conv_b, ssd_dt_bias, ssd_a_log, ssd_d, ssd_norm_w, w_in_odd, w_out_odd, mlstm_conv_w, mlstm_conv_b, mlstm_igate_b, mlstm_fgate_b, mlstm_norm_w):
    batch, seq, d_model = x_prompt.shape
    bs = x_sample.shape[0]
    ssd_heads = ssd_a_log.shape[1]
    ssd_inner = ssd_heads * SSD_HEADDIM
    ssd_cch = ssd_conv_w.shape[2]
    m_inner = mlstm_norm_w.shape[1]
    m_hd = m_inner // MLSTM_HEADS
    mp = batch * seq

    col_z = 4 * ATT_W
    col_xbc = col_z + ssd_inner
    col_dt = col_xbc + ssd_cch
    n_even = col_dt + LANE
    w_in0 = _pad_cols(w_in_even[0], n_even).astype(BF16)
    w_out0 = w_out_even[0].astype(BF16)
    nw0 = _row(norm_w[0])
    cw0, cb0 = ssd_conv_w[0], _row(ssd_conv_b[0])
    dtb = _row(ssd_dt_bias[0], LANE)
    alog = _row(ssd_a_log[0], LANE)
    aloge = _row(jnp.repeat(ssd_a_log[0], SSD_HEADDIM))
    dsk = _row(jnp.repeat(ssd_d[0], SSD_HEADDIM))
    snw = _row(ssd_norm_w[0])

    hp = x_prompt.reshape(mp, d_model)
    hs = x_sample.reshape(bs, d_model)

    up = _norm_matmul(hp, nw0, w_in0, tm=256, panels=1)
    cos_p, sa_p, sb_p = _rope_tables(jnp.arange(seq), LANE)
    att_p, k_p, v_p = _attn_prompt(up, cos_p, sa_p, sb_p, batch=batch, seq=seq)
    y_p, conv_p, st_p = _ssd_prompt(up, cw0, cb0, dtb, alog, dsk, snw, batch=batch, seq=seq,
                                    heads=ssd_heads, col_z=col_z, col_xbc=col_xbc, col_dt=col_dt)
    hp = _out_proj(hp, [att_p, y_p], [w_out0[:ATT_W], w_out0[ATT_W:]], tm=512)

    us = _norm_matmul(hs, nw0, w_in0, tm=bs, panels=1)
    cos_s, sa_s, sb_s = _rope_tables(PAST_LEN + jnp.arange(1), ATT_W)
    xt_s = _rot_sample(us, cos_s, sa_s, sb_s)
    k_s = xt_s[ATT_W:2 * ATT_W].T
    v_s = xt_s[2 * ATT_W:3 * ATT_W].T
    pos_minor = lambda cache: jnp.transpose(cache, (0, 2, 3, 1))
    att_s = _attn_sample(xt_s, pos_minor(cache_attn_k[0]), pos_minor(cache_attn_v[0]), bb=2).T
    y_s, conv_s, st_s = _ssd_sample(us, state_ssd_conv[0].reshape(bs, -1),
                                    state_ssd[0].reshape(bs, ssd_inner, SSD_STATE),
                                    cw0, cb0, dtb, alog, aloge, dsk, snw, heads=ssd_heads,
                                    col_z=col_z, col_xbc=col_xbc, col_dt=col_dt, bb=4)
    hs = _out_proj(hs, [att_s, y_s], [w_out0[:ATT_W], w_out0[ATT_W:]], tm=bs)

    wo = w_in_odd[0]
    gates_at = 4 * m_inner
    zcol = gates_at + 2 * MLSTM_HEADS
    col_gates = 5 * m_inner
    zpad = jnp.zeros((d_model, LANE - MLSTM_HEADS), wo.dtype)
    w_in1 = jnp.concatenate([wo[:, :gates_at], wo[:, zcol:],
                             wo[:, gates_at:gates_at + MLSTM_HEADS], zpad,
                             wo[:, gates_at + MLSTM_HEADS:zcol], zpad], axis=1).astype(BF16)
    w_out1 = w_out_odd[0].astype(BF16)
    nw1 = _row(norm_w[1])
    cw1, cb1 = mlstm_conv_w[0], _row(mlstm_conv_b[0])
    bi = _row(mlstm_igate_b[0], LANE)
    bf = _row(mlstm_fgate_b[0], LANE)
    mnw = _row(mlstm_norm_w[0])
    fnw = _row(final_norm_w)

    up1 = _norm_matmul(hp, nw1, w_in1, tm=256, panels=2)
    hz_p, mconv_p, c_p, n_p, m_p = _mlstm_prompt(up1, cw1, cb1, bi, bf, mnw, batch=batch, seq=seq,
                                                   heads=MLSTM_HEADS, hd=m_hd, col_gates=col_gates)
    y_prompt = _out_proj(hp, [hz_p], [w_out1], fnw, tm=512)

    us1 = _norm_matmul(hs, nw1, w_in1, tm=bs, panels=2)
    m0p = _pad_cols(state_mlstm_m[0], LANE)
    hz_s, mconv_s, c_s, n_s, m_s = _mlstm_sample(us1, state_mlstm_conv[0].reshape(bs, -1),
                                                 state_mlstm_c[0], state_mlstm_n[0].reshape(bs, m_inner),
                                                 m0p, cw1, cb1, bi, bf, mnw,
                                                 heads=MLSTM_HEADS, hd=m_hd, col_gates=col_gates)
    y_sample = _out_proj(hs, [hz_s], [w_out1], fnw, tm=bs)

    tmax = min(seq, CHUNK * max(DILATIONS))
    kv_shape = (1, batch, seq, ATT_HEADS, ATT_HD)
    return (
        y_prompt.reshape(batch, seq, d_model),
        y_sample.reshape(bs, 1, d_model),
        k_p.reshape(kv_shape)[:, :, seq - tmax:],
        v_p.reshape(kv_shape)[:, :, seq - tmax:],
        conv_p[None],
        st_p.reshape(1, batch, ssd_heads, SSD_HEADDIM, SSD_STATE),
        mconv_p[None],
        c_p[None],
        n_p.reshape(1, batch, MLSTM_HEADS, m_hd),
        m_p.reshape(1, batch, MLSTM_HEADS),
        k_s.reshape(1, bs, 1, ATT_HEADS, ATT_HD),
        v_s.reshape(1, bs, 1, ATT_HEADS, ATT_HD),
        conv_s.reshape(1, bs, CONV_K - 1, ssd_cch),
        st_s.reshape(1, bs, ssd_heads, SSD_HEADDIM, SSD_STATE),
        mconv_s.reshape(1, bs, CONV_K - 1, 2 * m_inner),
        c_s[None],
        n_s.reshape(1, bs, MLSTM_HEADS, m_hd),
        m_s.reshape(1, bs, MLSTM_HEADS),
    )
```

```python
import functools

import jax
import jax.numpy as jnp
from jax import lax
from jax.experimental import pallas as pl
from jax.experimental.pallas import tpu as pltpu

F32 = jnp.float32
BF16 = jnp.bfloat16
NEG_INF = float("-inf")
EPS = 1e-6
LANE = 128
CHUNK = 128
VMEM_LIMIT = 56 * 1024 * 1024

CONV_K = 4
ATT_HEADS = 8
ATT_HD = 64
ATT_W = ATT_HEADS * ATT_HD
ROT_DIM = ATT_HD // 4
ROPE_THETA = 500000.0
DILATIONS = (1, 4, 16)
PAST_LEN = 2048
SSD_HEADDIM = 64
SSD_GROUPS = 2
SSD_STATE = 128
MLSTM_HEADS = 8


def _params(*sem):
    return pltpu.CompilerParams(dimension_semantics=sem, vmem_limit_bytes=VMEM_LIMIT)


def _chunks(n, w):
    out, c = [], 0
    while c < n:
        out.append((c, min(w, n - c)))
        c += w
    return out


def _dot(a, b):
    return jnp.dot(a.astype(BF16), b.astype(BF16), preferred_element_type=F32)


def _dot_nt(a, b):
    return lax.dot_general(a.astype(BF16), b.astype(BF16), (((1,), (1,)), ((), ())),
                           preferred_element_type=F32)


def _dot_tn(a, b):
    return lax.dot_general(a.astype(BF16), b.astype(BF16), (((0,), (0,)), ((), ())),
                           preferred_element_type=F32)


def _split3(x):
    hi = x.astype(BF16)
    r1 = x - hi.astype(F32)
    mid = r1.astype(BF16)
    lo = (r1 - mid.astype(F32)).astype(BF16)
    return hi, mid, lo


def _sel_dot(sel, x):
    hi, mid, lo = _split3(x)
    d = lambda p: jnp.dot(sel, p, preferred_element_type=F32)
    return d(hi) + d(mid) + d(lo)


def _dot_sel(x, sel):
    hi, mid, lo = _split3(x)
    d = lambda p: jnp.dot(p, sel, preferred_element_type=F32)
    return d(hi) + d(mid) + d(lo)


def _tril(n):
    r = lax.broadcasted_iota(jnp.int32, (n, n), 0)
    c = lax.broadcasted_iota(jnp.int32, (n, n), 1)
    return r >= c


def _seg_matrix(rows, cols, seg, along_rows):
    r = lax.broadcasted_iota(jnp.int32, (rows, cols), 0)
    c = lax.broadcasted_iota(jnp.int32, (rows, cols), 1)
    m = (r // seg == c) if along_rows else (c // seg == r)
    return m.astype(BF16)


def _row_at(ref, b, cols=slice(None)):
    base = pl.multiple_of((b // 8) * 8, 8)
    tile = ref[pl.ds(base, 8), cols]
    sub = lax.broadcasted_iota(jnp.int32, tile.shape, 0)
    return jnp.sum(jnp.where(sub == b % 8, tile, 0.0), axis=0, keepdims=True)


def _silu(x):
    h = 0.5 * x
    return h + h * jnp.tanh(h)


def _sigmoid(x):
    return 0.5 + 0.5 * jnp.tanh(0.5 * x)


def _softplus(x):
    return jnp.maximum(x, 0.0) + jnp.log1p(jnp.exp(-jnp.abs(x)))


def _rms(x, w):
    return x * lax.rsqrt(jnp.mean(x * x, axis=-1, keepdims=True) + EPS) * w


def _norm_matmul_body(x_ref, nw_ref, w_ref, o_ref, *, chunks):
    xn = _rms(x_ref[...], nw_ref[...]).astype(BF16)
    for c0, cw in chunks:
        o_ref[:, c0:c0 + cw] = jnp.dot(xn, w_ref[:, c0:c0 + cw], preferred_element_type=F32)


def _norm_matmul(x, nw, w, *, tm, panels):
    m, d = x.shape
    n = w.shape[1]
    pn = n // panels
    return pl.pallas_call(
        functools.partial(_norm_matmul_body, chunks=_chunks(pn, 512)),
        grid=(panels, m // tm),
        in_specs=[pl.BlockSpec((tm, d), lambda p, i: (i, 0)),
                  pl.BlockSpec((1, d), lambda p, i: (0, 0)),
                  pl.BlockSpec((d, pn), lambda p, i: (0, p))],
        out_specs=pl.BlockSpec((tm, pn), lambda p, i: (i, p)),
        out_shape=jax.ShapeDtypeStruct((m, n), F32),
        compiler_params=_params("arbitrary", "arbitrary"),
        name="norm_matmul")(x, nw, w)


def _out_proj_body(*refs, n_in, final):
    h_ref = refs[0]
    xs = refs[1:1 + n_in]
    ws = refs[1 + n_in:1 + 2 * n_in]
    rest = refs[1 + 2 * n_in:]
    acc = h_ref[...]
    for x_ref, w_ref in zip(xs, ws):
        acc = acc + jnp.dot(x_ref[...].astype(BF16), w_ref[...], preferred_element_type=F32)
    if final:
        fw_ref, o_ref = rest
        o_ref[...] = _rms(acc, fw_ref[...])
    else:
        (o_ref,) = rest
        o_ref[...] = acc


def _out_proj(h, xs, ws, fw=None, *, tm):
    m, d = h.shape
    n_in = len(xs)
    in_specs = [pl.BlockSpec((tm, d), lambda i: (i, 0))]
    in_specs += [pl.BlockSpec((tm, x.shape[1]), lambda i: (i, 0)) for x in xs]
    in_specs += [pl.BlockSpec(w.shape, lambda i: (0, 0)) for w in ws]
    args = [h, *xs, *ws]
    if fw is not None:
        in_specs.append(pl.BlockSpec((1, d), lambda i: (0, 0)))
        args.append(fw)
    return pl.pallas_call(
        functools.partial(_out_proj_body, n_in=n_in, final=fw is not None),
        grid=(m // tm,),
        in_specs=in_specs,
        out_specs=pl.BlockSpec((tm, d), lambda i: (i, 0)),
        out_shape=jax.ShapeDtypeStruct((m, d), F32),
        compiler_params=_params("arbitrary"),
        name="out_proj")(*args)


def _rope_tables(pos, width):
    half = ROT_DIM // 2
    inv = jnp.power(F32(ROPE_THETA), -jnp.arange(half, dtype=F32) * (2.0 / ROT_DIM))
    ang = pos.astype(F32)[:, None] * inv[None, :]
    cos, sin = jnp.cos(ang), jnp.sin(ang)
    n = pos.shape[0]
    one = jnp.ones((n, ATT_HD - ROT_DIM), F32)
    z8 = jnp.zeros((n, half), F32)
    z48 = jnp.zeros((n, ATT_HD - ROT_DIM), F32)
    c = jnp.concatenate([cos, cos, one], axis=-1)
    sa = jnp.concatenate([-sin, z8, z48], axis=-1)
    sb = jnp.concatenate([z8, sin, z48], axis=-1)
    rep = width // ATT_HD
    return tuple(jnp.tile(t, (1, rep)) for t in (c, sa, sb))


def _rotary(x, c, sa, sb):
    w = x.shape[-1]
    half = ROT_DIM // 2
    return x * c + pltpu.roll(x, w - half, 1) * sa + pltpu.roll(x, half, 1) * sb


def _attn_prompt_body(q_ref, k_ref, v_ref, g_ref, c_ref, sa_ref, sb_ref,
                      att_ref, ko_ref, vo_ref,
                      nat, qd, kd, vd, od, std, o_s, st_s, *, seq):
    nblk = seq // CHUNK
    c, sa, sb = c_ref[...], sa_ref[...], sb_ref[...]
    q = _rotary(q_ref[...], c, sa, sb) * (ATT_HD ** -0.5)
    k = _rotary(k_ref[...], c, sa, sb)
    ko_ref[...] = k
    vo_ref[...] = v_ref[...]
    head_lane = lax.broadcasted_iota(jnp.int32, (1, LANE), 1) // ATT_HD
    nat[0] = q
    nat[1] = k
    nat[2] = v_ref[...]
    zero = jnp.zeros((CHUNK, LANE), BF16)
    for p, d in enumerate(DILATIONS):
        kd[p, 0:CHUNK, :] = zero
        vd[p, 0:CHUNK, :] = zero
        ln = seq // d
        for r in range(d):
            rows = slice(r * ln, (r + 1) * ln)
            krows = slice(CHUNK + r * ln, CHUNK + (r + 1) * ln)
            src = pl.ds(r, ln, stride=d) if d > 1 else slice(None)
            qr = nat[0, src, :]
            qd[p, 0, rows, :] = jnp.where(head_lane == 0, qr, 0.0).astype(BF16)
            qd[p, 1, rows, :] = jnp.where(head_lane == 1, qr, 0.0).astype(BF16)
            kd[p, krows, :] = nat[1, src, :].astype(BF16)
            vd[p, krows, :] = nat[2, src, :].astype(BF16)

    row = lax.broadcasted_iota(jnp.int32, (CHUNK, 2 * CHUNK), 0)
    col = lax.broadcasted_iota(jnp.int32, (CHUNK, 2 * CHUNK), 1)
    band = (col >= row) & (col <= row + CHUNK)
    first_head = lax.broadcasted_iota(jnp.int32, (CHUNK, LANE), 1) < ATT_HD

    for p, d in enumerate(DILATIONS):
        nb = nblk // d

        def block(t, carry, p=p, nb=nb):
            base = pl.multiple_of(t * CHUNK, CHUNK)
            rows = pl.ds(base, CHUNK)
            first = (t % nb) == 0
            valid = band & (col >= jnp.where(first, CHUNK, 0))
            kw = kd[p, pl.ds(base, 2 * CHUNK), :]
            vw = vd[p, pl.ds(base, 2 * CHUNK), :]
            parts = []
            for hh in range(2):
                s = lax.dot_general(qd[p, hh, rows, :], kw, (((1,), (1,)), ((), ())),
                                    preferred_element_type=F32)
                s = jnp.where(valid, s, NEG_INF)
                m = jnp.max(s, axis=-1, keepdims=True)
                e = jnp.exp(s - m)
                o = jnp.dot(e.astype(BF16), vw, preferred_element_type=F32)
                parts.append((o, m, jnp.sum(e, axis=-1, keepdims=True)))
            (o0, m0, l0), (o1, m1, l1) = parts
            wide = lambda t: jnp.broadcast_to(t, (CHUNK, LANE))
            od[rows, :] = jnp.where(first_head, o0, o1)
            std[0, rows, :] = jnp.where(first_head, wide(m0), wide(m1))
            std[1, rows, :] = jnp.where(first_head, wide(l0), wide(l1))
            return carry

        lax.fori_loop(0, nblk, block, 0, unroll=True)
        if d == 1:
            o_s[p] = od[...]
            st_s[p] = std[...]
        else:
            ln = seq // d
            for r in range(d):
                o_s[p, pl.ds(r, ln, stride=d), :] = od[r * ln:(r + 1) * ln, :]
                st_s[p, 0, pl.ds(r, ln, stride=d), :] = std[0, r * ln:(r + 1) * ln, :]
                st_s[p, 1, pl.ds(r, ln, stride=d), :] = std[1, r * ln:(r + 1) * ln, :]

    np_ = len(DILATIONS)

    def combine(t, carry):
        base = pl.multiple_of(t * CHUNK, CHUNK)
        rows = pl.ds(base, CHUNK)
        ms = [st_s[p, 0, rows, :] for p in range(np_)]
        mx = functools.reduce(jnp.maximum, ms)
        ws = [jnp.exp(mm - mx) for mm in ms]
        num = sum(w * o_s[p, rows, :] for p, w in enumerate(ws))
        den = sum(w * st_s[p, 1, rows, :] for p, w in enumerate(ws))
        att_ref[rows, :] = ((num / den) * _silu(g_ref[rows, :])).astype(att_ref.dtype)
        return carry

    lax.fori_loop(0, nblk, combine, 0, unroll=2)


def _attn_prompt(u, cos, sa, sb, *, batch, seq):
    m = batch * seq
    npair = ATT_W // LANE
    blk = lambda off: pl.BlockSpec((seq, LANE), lambda b, hp, off=off: (b, off + hp))
    tab = pl.BlockSpec((seq, LANE), lambda b, hp: (0, 0))
    out = pl.BlockSpec((seq, LANE), lambda b, hp: (b, hp))
    np_ = len(DILATIONS)
    return pl.pallas_call(
        functools.partial(_attn_prompt_body, seq=seq),
        grid=(batch, npair),
        in_specs=[blk(0), blk(npair), blk(2 * npair), blk(3 * npair), tab, tab, tab],
        out_specs=[out, out, out],
        out_shape=[jax.ShapeDtypeStruct((m, ATT_W), BF16),
                   jax.ShapeDtypeStruct((m, ATT_W), F32),
                   jax.ShapeDtypeStruct((m, ATT_W), F32)],
        scratch_shapes=[pltpu.VMEM((3, seq, LANE), F32),
                        pltpu.VMEM((np_, 2, seq, LANE), BF16),
                        pltpu.VMEM((np_, seq + CHUNK, LANE), BF16),
                        pltpu.VMEM((np_, seq + CHUNK, LANE), BF16),
                        pltpu.VMEM((seq, LANE), F32),
                        pltpu.VMEM((2, seq, LANE), F32),
                        pltpu.VMEM((np_, seq, LANE), F32),
                        pltpu.VMEM((np_, 2, seq, LANE), F32)],
        compiler_params=_params("arbitrary", "arbitrary"),
        name="attn_prompt")(u, u, u, u, cos, sa, sb)


def _ssd_prompt_body(xbc_ref, z_ref, dt_ref, cw_ref, cb_ref, dtb_ref, alog_ref, dsk_ref, nw_ref,
                     y_ref, conv_ref, st_ref, ext, st, ys, *, heads, nchunk):
    c = pl.program_id(1)
    inner = heads * SSD_HEADDIM
    gw = SSD_STATE
    hpg = heads // SSD_GROUPS

    @pl.when(c == 0)
    def _():
        ext[0:8, :] = jnp.zeros((8, ext.shape[1]), F32)
        st[...] = jnp.zeros(st.shape, F32)

    ext[8:8 + CHUNK, :] = xbc_ref[...]
    conv = cb_ref[...] + ext[5:5 + CHUNK, :] * cw_ref[0:1, :]
    for j in range(1, CONV_K):
        conv = conv + ext[5 + j:5 + j + CHUNK, :] * cw_ref[j:j + 1, :]
    act = _silu(conv)

    @pl.when(c == nchunk - 1)
    def _():
        conv_ref[...] = ext[CHUNK + 5:CHUNK + 8, :]

    ext[0:8, :] = ext[CHUNK:CHUNK + 8, :]

    dt = _softplus(dt_ref[...] + dtb_ref[...])
    a = -jnp.exp(alog_ref[...])
    tril = _tril(CHUNK)
    acum = _sel_dot(jnp.where(tril, 1.0, 0.0).astype(BF16), dt * a)
    acum_t = acum.T
    last = acum[CHUNK - 1:CHUNK, :]
    elast = jnp.exp(last)
    expand = _seg_matrix(LANE, inner, SSD_HEADDIM, False)
    dt_e = _dot_sel(dt, expand)
    eacum_e = _dot_sel(jnp.exp(acum), expand)
    wend_e = _dot_sel(jnp.exp(last - acum) * dt, expand)
    xs = act[:, 0:inner]
    xdt = xs * dt_e
    xw = xs * wend_e
    hw = hpg * SSD_HEADDIM
    first_half = lax.broadcasted_iota(jnp.int32, (CHUNK, LANE), 1) < SSD_HEADDIM

    for g in range(SSD_GROUPS):
        bm = act[:, inner + g * gw:inner + (g + 1) * gw]
        cm = act[:, inner + SSD_GROUPS * gw + g * gw:inner + SSD_GROUPS * gw + (g + 1) * gw]
        cb = _dot_nt(cm, bm)
        sg = st[g * hw:(g + 1) * hw, :]
        y_inter = _dot_nt(cm, sg) * eacum_e[:, g * hw:(g + 1) * hw]
        s_local = _dot_tn(xw[:, g * hw:(g + 1) * hw], bm)
        for hg in range(hpg):
            h = g * hpg + hg
            rows = slice(hg * SSD_HEADDIM, (hg + 1) * SSD_HEADDIM)
            st[g * hw + hg * SSD_HEADDIM:g * hw + (hg + 1) * SSD_HEADDIM, :] = (
                sg[rows, :] * elast[:, h:h + 1] + s_local[rows, :])
        for pr in range(hpg // 2):
            lo = g * hw + pr * LANE
            xp = xdt[:, lo:lo + LANE].astype(BF16)
            halves = []
            for hh in range(2):
                h = g * hpg + 2 * pr + hh
                seg = acum[:, h:h + 1] - acum_t[h:h + 1, :]
                mh = cb * jnp.exp(jnp.where(tril, seg, NEG_INF))
                halves.append(jnp.dot(mh.astype(BF16), xp, preferred_element_type=F32))
            ys[:, lo:lo + LANE] = (jnp.where(first_half, halves[0], halves[1])
                                   + y_inter[:, pr * LANE:(pr + 1) * LANE])

    yt = (ys[...] + dsk_ref[...] * xs) * _silu(z_ref[...])
    y_ref[...] = _rms(yt, nw_ref[...]).astype(y_ref.dtype)

    @pl.when(c == nchunk - 1)
    def _():
        st_ref[...] = st[...]


def _ssd_prompt(u, cw, cb, dtb, alog, dsk, nw, *, batch, seq, heads, col_z, col_xbc, col_dt):
    nchunk = seq // CHUNK
    inner = heads * SSD_HEADDIM
    cch = cw.shape[1]
    row = lambda b, c: b * nchunk + c
    full = lambda a: pl.BlockSpec(a.shape, lambda b, c: (0, 0))
    return pl.pallas_call(
        functools.partial(_ssd_prompt_body, heads=heads, nchunk=nchunk),
        grid=(batch, nchunk),
        in_specs=[pl.BlockSpec((CHUNK, cch), lambda b, c: (row(b, c), col_xbc // cch)),
                  pl.BlockSpec((CHUNK, inner), lambda b, c: (row(b, c), col_z // inner)),
                  pl.BlockSpec((CHUNK, LANE), lambda b, c: (row(b, c), col_dt // LANE)),
                  full(cw), full(cb), full(dtb), full(alog), full(dsk), full(nw)],
        out_specs=[pl.BlockSpec((CHUNK, inner), lambda b, c: (row(b, c), 0)),
                   pl.BlockSpec((None, CONV_K - 1, cch), lambda b, c: (b, 0, 0)),
                   pl.BlockSpec((None, inner, SSD_STATE), lambda b, c: (b, 0, 0))],
        out_shape=[jax.ShapeDtypeStruct((batch * seq, inner), BF16),
                   jax.ShapeDtypeStruct((batch, CONV_K - 1, cch), F32),
                   jax.ShapeDtypeStruct((batch, inner, SSD_STATE), F32)],
        scratch_shapes=[pltpu.VMEM((CHUNK + 8, cch), F32),
                        pltpu.VMEM((inner, SSD_STATE), F32),
                        pltpu.VMEM((CHUNK, inner), F32)],
        compiler_params=_params("arbitrary", "arbitrary"),
        name="ssd_prompt")(u, u, u, cw, cb, dtb, alog, dsk, nw)


def _mlstm_prompt_body(q_ref, k_ref, v_ref, o_ref, z_ref, gi_ref, gf_ref,
                       cw_ref, cb_ref, bi_ref, bf_ref, nw_ref,
                       hz_ref, conv_ref, c_out, n_out, m_out,
                       histq, histk, c_s, n_s, m_s, *, nchunk, heads, hd):
    c = pl.program_id(1)
    inner = heads * hd

    @pl.when(c == 0)
    def _():
        histq[...] = jnp.zeros((8, inner), F32)
        histk[...] = jnp.zeros((8, inner), F32)
        c_s[...] = jnp.zeros(c_s.shape, F32)
        n_s[...] = jnp.zeros(n_s.shape, F32)
        m_s[...] = jnp.full(m_s.shape, NEG_INF, F32)

    def conv_head(hist, x_ref, col0, h):
        sl = slice(h * hd, (h + 1) * hd)
        cols = slice(col0 + h * hd, col0 + (h + 1) * hd)
        xp = jnp.concatenate([hist[:, sl], x_ref[:, sl]], axis=0)
        acc = cb_ref[:, cols] + xp[5:5 + CHUNK, :] * cw_ref[0:1, cols]
        for j in range(1, CONV_K):
            acc = acc + xp[5 + j:5 + j + CHUNK, :] * cw_ref[j:j + 1, cols]
        return _silu(acc)

    @pl.when(c == nchunk - 1)
    def _():
        conv_ref[:, 0:inner] = q_ref[CHUNK - (CONV_K - 1):CHUNK, :]
        conv_ref[:, inner:] = k_ref[CHUNK - (CONV_K - 1):CHUNK, :]

    it = gi_ref[...] + bi_ref[...]
    logf = -_softplus(-(gf_ref[...] + bf_ref[...]))
    tril = _tril(CHUNK)
    bc = _sel_dot(jnp.where(tril, 1.0, 0.0).astype(BF16), logf)
    it_t = it.T
    bc_t = bc.T

    for h in range(heads):
        sl = slice(h * hd, (h + 1) * hd)
        q = conv_head(histq, q_ref, 0, h)
        k = conv_head(histk, k_ref, inner, h) * (hd ** -0.5)
        v = v_ref[:, sl]
        i_col, b_col = it[:, h:h + 1], bc[:, h:h + 1]
        i_row, b_row = it_t[h:h + 1, :], bc_t[h:h + 1, :]
        m_prev = m_s[h]
        dmat = jnp.where(tril, b_col - b_row + i_row, NEG_INF)
        inter = b_col + m_prev
        m_t = jnp.maximum(inter, jnp.max(dmat, axis=-1, keepdims=True))
        w_intra = jnp.exp(dmat - m_t)
        w_inter = jnp.exp(inter - m_t)
        att = w_intra * _dot_nt(q, k)
        c_prev = c_s[h]
        n_prev = n_s[h]
        num = _dot(att, v) + w_inter * _dot(q, c_prev)
        qn = _dot_nt(q, jnp.broadcast_to(n_prev, (8, hd)))[:, 0:1]
        den = jnp.sum(att, axis=-1, keepdims=True) + w_inter * qn
        hh = num / jnp.maximum(jnp.abs(den), jnp.exp(-m_t))

        b_last = b_col[CHUNK - 1:CHUNK, :]
        logw = b_last - b_col + i_col
        m_new = jnp.maximum(b_last + m_prev, jnp.max(logw, axis=0, keepdims=True))
        ws = jnp.exp(logw - m_new)
        ws_row = jnp.exp(b_last - b_row + i_row - m_new)
        scale = jnp.exp(b_last + m_prev - m_new)
        c_s[h] = scale * c_prev + _dot_tn(k, ws * v)
        n_s[h] = scale * n_prev + _dot(jnp.broadcast_to(ws_row, (8, CHUNK)), k)[0:1, :]
        m_s[h] = m_new

        hg = _sigmoid(o_ref[:, sl]) * hh
        mu = jnp.mean(hg, axis=-1, keepdims=True)
        var = jnp.mean(jnp.square(hg - mu), axis=-1, keepdims=True)
        hn = (hg - mu) * lax.rsqrt(var + EPS) * nw_ref[:, sl]
        hz_ref[:, sl] = (hn * _silu(z_ref[:, sl])).astype(hz_ref.dtype)

    histq[...] = q_ref[CHUNK - 8:CHUNK, :]
    histk[...] = k_ref[CHUNK - 8:CHUNK, :]

    @pl.when(c == nchunk - 1)
    def _():
        c_out[...] = c_s[...]
        n_out[...] = n_s[...]
        m_out[...] = m_s[...]


def _mlstm_prompt(u, cw, cb, bi, bf, nw, *, batch, seq, heads, hd, col_gates):
    nchunk = seq // CHUNK
    inner = heads * hd
    row = lambda b, c: b * nchunk + c
    ublk = lambda j: pl.BlockSpec((CHUNK, inner), lambda b, c, j=j: (row(b, c), j))
    gblk = lambda off: pl.BlockSpec((CHUNK, LANE), lambda b, c, off=off: (row(b, c), col_gates // LANE + off))
    full = lambda a: pl.BlockSpec(a.shape, lambda b, c: (0, 0))
    return pl.pallas_call(
        functools.partial(_mlstm_prompt_body, nchunk=nchunk, heads=heads, hd=hd),
        grid=(batch, nchunk),
        in_specs=[ublk(0), ublk(1), ublk(2), ublk(3), ublk(4), gblk(0), gblk(1),
                  full(cw), full(cb), full(bi), full(bf), full(nw)],
        out_specs=[pl.BlockSpec((CHUNK, inner), lambda b, c: (row(b, c), 0)),
                   pl.BlockSpec((None, CONV_K - 1, 2 * inner), lambda b, c: (b, 0, 0)),
                   pl.BlockSpec((None, heads, hd, hd), lambda b, c: (b, 0, 0, 0)),
                   pl.BlockSpec((None, heads, 1, hd), lambda b, c: (b, 0, 0, 0)),
                   pl.BlockSpec((None, heads, 1, 1), lambda b, c: (b, 0, 0, 0))],
        out_shape=[jax.ShapeDtypeStruct((batch * seq, inner), BF16),
                   jax.ShapeDtypeStruct((batch, CONV_K - 1, 2 * inner), F32),
                   jax.ShapeDtypeStruct((batch, heads, hd, hd), F32),
                   jax.ShapeDtypeStruct((batch, heads, 1, hd), F32),
                   jax.ShapeDtypeStruct((batch, heads, 1, 1), F32)],
        scratch_shapes=[pltpu.VMEM((8, inner), F32),
                        pltpu.VMEM((8, inner), F32),
                        pltpu.VMEM((heads, hd, hd), F32),
                        pltpu.VMEM((heads, 1, hd), F32),
                        pltpu.VMEM((heads, 1, 1), F32)],
        compiler_params=_params("arbitrary", "arbitrary"),
        name="mlstm_prompt")(u, u, u, u, u, u, u, cw, cb, bi, bf, nw)


def _rot_sample_body(q_ref, k_ref, v_ref, g_ref, c_ref, sa_ref, sb_ref, xt_ref):
    c, sa, sb = c_ref[...], sa_ref[...], sb_ref[...]
    q = _rotary(q_ref[...], c, sa, sb) * (ATT_HD ** -0.5)
    k = _rotary(k_ref[...], c, sa, sb)
    xt_ref[...] = jnp.concatenate([q, k, v_ref[...], g_ref[...]], axis=-1).T


def _rot_sample(u, cos, sa, sb):
    bs = u.shape[0]
    ublk = lambda off: pl.BlockSpec((bs, ATT_W), lambda i, off=off: (0, off))
    tab = pl.BlockSpec((1, ATT_W), lambda i: (0, 0))
    return pl.pallas_call(
        _rot_sample_body,
        grid=(1,),
        in_specs=[ublk(0), ublk(1), ublk(2), ublk(3), tab, tab, tab],
        out_specs=pl.BlockSpec((4 * ATT_W, bs), lambda i: (0, 0)),
        out_shape=jax.ShapeDtypeStruct((4 * ATT_W, bs), F32),
        compiler_params=_params("arbitrary"),
        name="rot_sample")(u, u, u, u, cos, sa, sb)


def _key_multiplicity(wb):
    back = wb - lax.broadcasted_iota(jnp.int32, (1, wb), 1)
    cnt = jnp.zeros((1, wb), F32)
    for d in DILATIONS:
        cnt = cnt + ((back <= CHUNK * d) & (lax.rem(back, d) == 0)).astype(F32)
    return cnt


def _attn_sample_body(x_ref, kc_ref, vc_ref, att_ref, *, bb, wb):
    i = pl.program_id(0)
    bs = x_ref.shape[1]
    nt = wb // LANE

    @pl.when(i == 0)
    def _():
        att_ref[...] = jnp.zeros(att_ref.shape, F32)

    cnt = _key_multiplicity(wb)
    valid = cnt > 0.0
    npat = float(len(DILATIONS))
    rid = lax.broadcasted_iota(jnp.int32, (bs, LANE), 0)
    lid = lax.broadcasted_iota(jnp.int32, (ATT_HD, bs), 1)
    for t in range(bb):
        b = i * bb + t
        cols = _dot_sel(x_ref[...], (rid == b).astype(BF16))
        for h in range(ATT_HEADS):
            lo = h * ATT_HD
            qc = cols[lo:lo + ATT_HD]
            kc = cols[ATT_W + lo:ATT_W + lo + ATT_HD]
            vc = cols[2 * ATT_W + lo:2 * ATT_W + lo + ATT_HD]
            gc = cols[3 * ATT_W + lo:3 * ATT_W + lo + ATT_HD]
            kt = kc_ref[t, h]
            vt = vc_ref[t, h]
            s = jnp.sum(kt * jnp.tile(qc, (1, nt)), axis=0, keepdims=True)
            s = jnp.where(valid, s, NEG_INF)
            s_self = jnp.sum(qc * kc, axis=0, keepdims=True)[:, 0:1]
            m = jnp.maximum(jnp.max(s, axis=1, keepdims=True), s_self)
            p = cnt * jnp.exp(s - m)
            p_self = npat * jnp.exp(s_self - m)
            den = jnp.sum(p, axis=1, keepdims=True) + p_self
            acc = vt[:, 0:LANE] * p[:, 0:LANE]
            for j in range(1, nt):
                acc = acc + vt[:, j * LANE:(j + 1) * LANE] * p[:, j * LANE:(j + 1) * LANE]
            o = jnp.sum(acc, axis=1, keepdims=True) + p_self * vc[:, 0:1]
            o = o / den * _silu(gc[:, 0:1])
            att_ref[lo:lo + ATT_HD, :] = jnp.where(lid == b, o, att_ref[lo:lo + ATT_HD, :])


def _attn_sample(xt, ck, cv, *, bb):
    bs, wb = ck.shape[0], ck.shape[3]
    cache = pl.BlockSpec((bb, ATT_HEADS, ATT_HD, wb), lambda i: (i, 0, 0, 0))
    return pl.pallas_call(
        functools.partial(_attn_sample_body, bb=bb, wb=wb),
        grid=(bs // bb,),
        in_specs=[pl.BlockSpec(xt.shape, lambda i: (0, 0)), cache, cache],
        out_specs=pl.BlockSpec((ATT_W, bs), lambda i: (0, 0)),
        out_shape=jax.ShapeDtypeStruct((ATT_W, bs), F32),
        compiler_params=_params("arbitrary"),
        name="attn_sample")(xt, ck, cv)


def _ssd_sample_prep_body(xbc_ref, dt_ref, cst_ref, cw_ref, cb_ref, dtb_ref, alog_ref, aloge_ref, dsk_ref,
                          conv_ref, yloc_ref, xdtt_ref, bc_ref, da_ref, dae_ref, *, heads):
    inner = heads * SSD_HEADDIM
    cch = xbc_ref.shape[1]
    gw = SSD_STATE
    x = xbc_ref[...]
    acc = cb_ref[...] + x * cw_ref[CONV_K - 1:CONV_K, :]
    for j in range(CONV_K - 1):
        acc = acc + cst_ref[:, j * cch:(j + 1) * cch] * cw_ref[j:j + 1, :]
    for j in range(CONV_K - 2):
        conv_ref[:, j * cch:(j + 1) * cch] = cst_ref[:, (j + 1) * cch:(j + 2) * cch]
    conv_ref[:, (CONV_K - 2) * cch:(CONV_K - 1) * cch] = x
    act = _silu(acc)
    xs = act[:, 0:inner]
    bc_ref[...] = act[:, inner:]
    dt = _softplus(dt_ref[...] + dtb_ref[...])
    da_ref[...] = jnp.exp(dt * (-jnp.exp(alog_ref[...])))
    expand = _seg_matrix(LANE, inner, SSD_HEADDIM, False)
    dte = _dot_sel(dt, expand)
    dae_ref[...] = jnp.exp(dte * (-jnp.exp(aloge_ref[...])))
    xdt = xs * dte
    xdtt_ref[...] = xdt.T
    hw = inner // SSD_GROUPS
    parts = []
    for g in range(SSD_GROUPS):
        bm = act[:, inner + g * gw:inner + (g + 1) * gw]
        cm = act[:, inner + SSD_GROUPS * gw + g * gw:inner + SSD_GROUPS * gw + (g + 1) * gw]
        cbg = jnp.sum(cm * bm, axis=-1, keepdims=True)
        parts.append(cbg * xdt[:, g * hw:(g + 1) * hw])
    yloc_ref[...] = jnp.concatenate(parts, axis=-1) + dsk_ref[...] * xs


def _ssd_sample_state_body(da_ref, s_ref, xdtt_ref, bc_ref, so_ref, yi_ref, *, bb, heads):
    i = pl.program_id(0)
    bs = bc_ref.shape[0]
    inner = heads * SSD_HEADDIM
    hw = inner // SSD_GROUPS
    hpg = heads // SSD_GROUPS
    gw = SSD_STATE
    rid = lax.broadcasted_iota(jnp.int32, (bs, gw), 0)
    for t in range(bb):
        b = i * bb + t
        parts = []
        for g in range(SSD_GROUPS):
            mg = jnp.where(rid == b, bc_ref[:, g * gw:(g + 1) * gw], 0.0)
            sl = _dot(xdtt_ref[g * hw:(g + 1) * hw, :], mg)
            crow = _row_at(bc_ref, b, slice(SSD_GROUPS * gw + g * gw, SSD_GROUPS * gw + (g + 1) * gw))
            sg = s_ref[t, g * hw:(g + 1) * hw, :]
            parts.append(_dot_nt(jnp.broadcast_to(crow, (8, gw)), sg)[0:1, :])
            for hg in range(hpg):
                h = g * hpg + hg
                lo = hg * SSD_HEADDIM
                so_ref[t, h * SSD_HEADDIM:(h + 1) * SSD_HEADDIM, :] = (
                    sg[lo:lo + SSD_HEADDIM, :] * da_ref[b, h] + sl[lo:lo + SSD_HEADDIM, :])
        yi_ref[t] = jnp.concatenate(parts, axis=-1)


def _ssd_sample_finish_body(yloc_ref, yi_ref, dae_ref, z_ref, nw_ref, y_ref):
    y = (yloc_ref[...] + yi_ref[...] * dae_ref[...]) * _silu(z_ref[...])
    y_ref[...] = _rms(y, nw_ref[...])


def _ssd_sample(u, cst, state, cw, cb, dtb, alog, aloge, dsk, nw, *, heads, col_z, col_xbc, col_dt, bb):
    bs = u.shape[0]
    inner = heads * SSD_HEADDIM
    cch = cw.shape[1]
    ncs = (CONV_K - 1) * cch
    bcw = 2 * SSD_GROUPS * SSD_STATE
    full = lambda a: pl.BlockSpec(a.shape, lambda i: (0,) * a.ndim)
    conv, yloc, xdtt, bc, da, dae = pl.pallas_call(
        functools.partial(_ssd_sample_prep_body, heads=heads),
        grid=(1,),
        in_specs=[pl.BlockSpec((bs, cch), lambda i: (0, col_xbc // cch)),
                  pl.BlockSpec((bs, LANE), lambda i: (0, col_dt // LANE)),
                  full(cst), full(cw), full(cb), full(dtb), full(alog), full(aloge), full(dsk)],
        out_specs=[pl.BlockSpec((bs, ncs), lambda i: (0, 0)),
                   pl.BlockSpec((bs, inner), lambda i: (0, 0)),
                   pl.BlockSpec((inner, bs), lambda i: (0, 0)),
                   pl.BlockSpec((bs, bcw), lambda i: (0, 0)),
                   pl.BlockSpec((bs, LANE), lambda i: (0, 0)),
                   pl.BlockSpec((bs, inner), lambda i: (0, 0))],
        out_shape=[jax.ShapeDtypeStruct((bs, ncs), F32),
                   jax.ShapeDtypeStruct((bs, inner), F32),
                   jax.ShapeDtypeStruct((inner, bs), F32),
                   jax.ShapeDtypeStruct((bs, bcw), F32),
                   jax.ShapeDtypeStruct((bs, LANE), F32),
                   jax.ShapeDtypeStruct((bs, inner), F32)],
        compiler_params=_params("arbitrary"),
        name="ssd_sample_prep")(u, u, cst, cw, cb, dtb, alog, aloge, dsk)
    new_state, yi = pl.pallas_call(
        functools.partial(_ssd_sample_state_body, bb=bb, heads=heads),
        grid=(bs // bb,),
        in_specs=[pl.BlockSpec(memory_space=pltpu.SMEM),
                  pl.BlockSpec((bb, inner, SSD_STATE), lambda i: (i, 0, 0)),
                  full(xdtt), full(bc)],
        out_specs=[pl.BlockSpec((bb, inner, SSD_STATE), lambda i: (i, 0, 0)),
                   pl.BlockSpec((bb, 1, inner), lambda i: (i, 0, 0))],
        out_shape=[jax.ShapeDtypeStruct((bs, inner, SSD_STATE), F32),
                   jax.ShapeDtypeStruct((bs, 1, inner), F32)],
        compiler_params=_params("arbitrary"),
        name="ssd_sample_state")(da[:, :heads], state, xdtt, bc)
    y = pl.pallas_call(
        _ssd_sample_finish_body,
        grid=(1,),
        in_specs=[full(yloc), pl.BlockSpec((bs, inner), lambda i: (0, 0)), full(dae),
                  pl.BlockSpec((bs, inner), lambda i: (0, col_z // inner)), full(nw)],
        out_specs=pl.BlockSpec((bs, inner), lambda i: (0, 0)),
        out_shape=jax.ShapeDtypeStruct((bs, inner), F32),
        compiler_params=_params("arbitrary"),
        name="ssd_sample_finish")(yloc, yi.reshape(bs, inner), dae, u, nw)
    return y, conv, new_state


def _mlstm_sample_prep_body(q_ref, k_ref, gi_ref, gf_ref, cst_ref, cw_ref, cb_ref, bi_ref, bf_ref, m_ref,
                            conv_ref, qk_ref, kwt_ref, wi_ref, wf_ref, mt_ref, *, heads, hd):
    inner = heads * hd
    cch = 2 * inner
    x = jnp.concatenate([q_ref[...], k_ref[...]], axis=-1)
    acc = cb_ref[...] + x * cw_ref[CONV_K - 1:CONV_K, :]
    for j in range(CONV_K - 1):
        acc = acc + cst_ref[:, j * cch:(j + 1) * cch] * cw_ref[j:j + 1, :]
    for j in range(CONV_K - 2):
        conv_ref[:, j * cch:(j + 1) * cch] = cst_ref[:, (j + 1) * cch:(j + 2) * cch]
    conv_ref[:, (CONV_K - 2) * cch:(CONV_K - 1) * cch] = x
    act = _silu(acc)
    q = act[:, 0:inner]
    k = act[:, inner:] * (hd ** -0.5)
    it = gi_ref[...] + bi_ref[...]
    inter = -_softplus(-(gf_ref[...] + bf_ref[...])) + m_ref[...]
    mt = jnp.maximum(inter, it)
    wi = jnp.exp(it - mt)
    wf = jnp.exp(inter - mt)
    wi_ref[...] = wi
    wf_ref[...] = wf
    mt_ref[...] = mt
    qk_ref[:, 0:inner] = q
    qk_ref[:, inner:] = k
    kw = jnp.concatenate([k[:, h * hd:(h + 1) * hd] * wi[:, h:h + 1] for h in range(heads)], axis=-1)
    kwt_ref[...] = kw.T


def _mlstm_sample_state_body(wf_ref, c_ref, qk_ref, v_ref, kwt_ref, co_ref, qc_ref, *, bb, heads, hd):
    i = pl.program_id(0)
    bs = v_ref.shape[0]
    rid = lax.broadcasted_iota(jnp.int32, (bs, hd), 0)
    for t in range(bb):
        b = i * bb + t
        parts = []
        for h in range(heads):
            cp = c_ref[t, h]
            qrow = _row_at(qk_ref, b, slice(h * hd, (h + 1) * hd))
            parts.append(_dot(jnp.broadcast_to(qrow, (8, hd)), cp)[0:1, :])
            mh = jnp.where(rid == b, v_ref[:, h * hd:(h + 1) * hd], 0.0)
            co_ref[t, h] = cp * wf_ref[b, h] + _dot(kwt_ref[h * hd:(h + 1) * hd, :], mh)
        qc_ref[t] = jnp.concatenate(parts, axis=-1)


def _mlstm_sample_finish_body(qk_ref, v_ref, o_ref, z_ref, qc_ref, n_ref, wi_ref, wf_ref, mt_ref, nw_ref,
                              hz_ref, no_ref, *, heads, hd):
    inner = heads * hd
    for h in range(heads):
        sl = slice(h * hd, (h + 1) * hd)
        q = qk_ref[:, sl]
        k = qk_ref[:, inner + h * hd:inner + (h + 1) * hd]
        wi = wi_ref[:, h:h + 1]
        wf = wf_ref[:, h:h + 1]
        mt = mt_ref[:, h:h + 1]
        n_prev = n_ref[:, sl]
        att = wi * jnp.sum(q * k, axis=-1, keepdims=True)
        num = att * v_ref[:, sl] + wf * qc_ref[:, sl]
        den = att + wf * jnp.sum(q * n_prev, axis=-1, keepdims=True)
        hh = num / jnp.maximum(jnp.abs(den), jnp.exp(-mt))
        hg = _sigmoid(o_ref[:, sl]) * hh
        mu = jnp.mean(hg, axis=-1, keepdims=True)
        var = jnp.mean(jnp.square(hg - mu), axis=-1, keepdims=True)
        hn = (hg - mu) * lax.rsqrt(var + EPS) * nw_ref[:, sl]
        hz_ref[:, sl] = hn * _silu(z_ref[:, sl])
        no_ref[:, sl] = wf * n_prev + wi * k


def _mlstm_sample(u, cst, c0, n0, m0p, cw, cb, bi, bf, nw, *, heads, hd, col_gates, bb):
    bs = u.shape[0]
    inner = heads * hd
    ncs = (CONV_K - 1) * 2 * inner
    full = lambda a: pl.BlockSpec(a.shape, lambda i: (0,) * a.ndim)
    ucol = lambda j: pl.BlockSpec((bs, inner), lambda i, j=j: (0, j))
    gcol = lambda j: pl.BlockSpec((bs, LANE), lambda i, j=j: (0, col_gates // LANE + j))
    tile = jax.ShapeDtypeStruct((bs, LANE), F32)
    conv, qk, kwt, wi, wf, mt = pl.pallas_call(
        functools.partial(_mlstm_sample_prep_body, heads=heads, hd=hd),
        grid=(1,),
        in_specs=[ucol(0), ucol(1), gcol(0), gcol(1), full(cst), full(cw), full(cb), full(bi), full(bf),
                  full(m0p)],
        out_specs=[pl.BlockSpec((bs, ncs), lambda i: (0, 0)),
                   pl.BlockSpec((bs, 2 * inner), lambda i: (0, 0)),
                   pl.BlockSpec((inner, bs), lambda i: (0, 0)),
                   pl.BlockSpec((bs, LANE), lambda i: (0, 0)),
                   pl.BlockSpec((bs, LANE), lambda i: (0, 0)),
                   pl.BlockSpec((bs, LANE), lambda i: (0, 0))],
        out_shape=[jax.ShapeDtypeStruct((bs, ncs), F32),
                   jax.ShapeDtypeStruct((bs, 2 * inner), F32),
                   jax.ShapeDtypeStruct((inner, bs), F32),
                   tile, tile, tile],
        compiler_params=_params("arbitrary"),
        name="mlstm_sample_prep")(u, u, u, u, cst, cw, cb, bi, bf, m0p)
    c_new, qc = pl.pallas_call(
        functools.partial(_mlstm_sample_state_body, bb=bb, heads=heads, hd=hd),
        grid=(bs // bb,),
        in_specs=[pl.BlockSpec(memory_space=pltpu.SMEM),
                  pl.BlockSpec((bb, heads, hd, hd), lambda i: (i, 0, 0, 0)),
                  full(qk), pl.BlockSpec((bs, inner), lambda i: (0, 2)), full(kwt)],
        out_specs=[pl.BlockSpec((bb, heads, hd, hd), lambda i: (i, 0, 0, 0)),
                   pl.BlockSpec((bb, 1, inner), lambda i: (i, 0, 0))],
        out_shape=[jax.ShapeDtypeStruct((bs, heads, hd, hd), F32),
                   jax.ShapeDtypeStruct((bs, 1, inner), F32)],
        compiler_params=_params("arbitrary"),
        name="mlstm_sample_state")(wf[:, :heads], c0, qk, u, kwt)
    hz, n_new = pl.pallas_call(
        functools.partial(_mlstm_sample_finish_body, heads=heads, hd=hd),
        grid=(1,),
        in_specs=[full(qk), ucol(2), ucol(3), ucol(4), pl.BlockSpec((bs, inner), lambda i: (0, 0)),
                  full(n0), full(wi), full(wf), full(mt), full(nw)],
        out_specs=[pl.BlockSpec((bs, inner), lambda i: (0, 0)),
                   pl.BlockSpec((bs, inner), lambda i: (0, 0))],
        out_shape=[jax.ShapeDtypeStruct((bs, inner), F32)] * 2,
        compiler_params=_params("arbitrary"),
        name="mlstm_sample_finish")(qk, u, u, u, qc.reshape(bs, inner), n0, wi, wf, mt, nw)
    return hz, conv, c_new, n_new, mt[:, :heads]


def _pad_cols(w, n):
    return jnp.pad(w, ((0, 0), (0, n - w.shape[1])))


def _row(v, n=None):
    v = v.reshape(1, -1)
    return v if n is None else _pad_cols(v, n)


def kernel(x_prompt, x_sample, cache_attn_k, cache_attn_v, state_ssd_conv, state_ssd, state_mlstm_conv, state_mlstm_c, state_mlstm_n, state_mlstm_m, norm_w, final_norm_w, w_in_even, w_out_even, ssd_conv_w, ssd_conv_b, ssd_dt_bias, ssd_a_log, ssd_d, ssd_norm_w, w_in_odd, w_out_odd, mlstm_conv_w, mlstm_conv_b, mlstm_igate_b, mlstm_fgate_b, mlstm_norm_w):
    batch, seq, d_model = x_prompt.shape
    bs = x_sample.shape[0]
    ssd_heads = ssd_a_log.shape[1]
    ssd_inner = ssd_heads * SSD_HEADDIM
    ssd_cch = ssd_conv_w.shape[2]
    m_inner = mlstm_norm_w.shape[1]
    m_hd = m_inner // MLSTM_HEADS
    mp = batch * seq

    col_z = 4 * ATT_W
    col_xbc = col_z + ssd_inner
    col_dt = col_xbc + ssd_cch
    n_even = col_dt + LANE
    w_in0 = _pad_cols(w_in_even[0], n_even).astype(BF16)
    w_out0 = w_out_even[0].astype(BF16)
    nw0 = _row(norm_w[0])
    cw0, cb0 = ssd_conv_w[0], _row(ssd_conv_b[0])
    dtb = _row(ssd_dt_bias[0], LANE)
    alog = _row(ssd_a_log[0], LANE)
    aloge = _row(jnp.repeat(ssd_a_log[0], SSD_HEADDIM))
    dsk = _row(jnp.repeat(ssd_d[0], SSD_HEADDIM))
    snw = _row(ssd_norm_w[0])

    hp = x_prompt.reshape(mp, d_model)
    hs = x_sample.reshape(bs, d_model)

    up = _norm_matmul(hp, nw0, w_in0, tm=512, panels=1)
    cos_p, sa_p, sb_p = _rope_tables(jnp.arange(seq), LANE)
    att_p, k_p, v_p = _attn_prompt(up, cos_p, sa_p, sb_p, batch=batch, seq=seq)
    y_p, conv_p, st_p = _ssd_prompt(up, cw0, cb0, dtb, alog, dsk, snw, batch=batch, seq=seq,
                                    heads=ssd_heads, col_z=col_z, col_xbc=col_xbc, col_dt=col_dt)
    hp = _out_proj(hp, [att_p, y_p], [w_out0[:ATT_W], w_out0[ATT_W:]], tm=512)

    us = _norm_matmul(hs, nw0, w_in0, tm=bs, panels=1)
    cos_s, sa_s, sb_s = _rope_tables(PAST_LEN + jnp.arange(1), ATT_W)
    xt_s = _rot_sample(us, cos_s, sa_s, sb_s)
    k_s = xt_s[ATT_W:2 * ATT_W].T
    v_s = xt_s[2 * ATT_W:3 * ATT_W].T
    pos_minor = lambda cache: jnp.transpose(cache, (0, 2, 3, 1))
    att_s = _attn_sample(xt_s, pos_minor(cache_attn_k[0]), pos_minor(cache_attn_v[0]), bb=2).T
    y_s, conv_s, st_s = _ssd_sample(us, state_ssd_conv[0].reshape(bs, -1),
                                    state_ssd[0].reshape(bs, ssd_inner, SSD_STATE),
                                    cw0, cb0, dtb, alog, aloge, dsk, snw, heads=ssd_heads,
                                    col_z=col_z, col_xbc=col_xbc, col_dt=col_dt, bb=4)
    hs = _out_proj(hs, [att_s, y_s], [w_out0[:ATT_W], w_out0[ATT_W:]], tm=bs)

    wo = w_in_odd[0]
    gates_at = 4 * m_inner
    zcol = gates_at + 2 * MLSTM_HEADS
    col_gates = 5 * m_inner
    zpad = jnp.zeros((d_model, LANE - MLSTM_HEADS), wo.dtype)
    w_in1 = jnp.concatenate([wo[:, :gates_at], wo[:, zcol:],
                             wo[:, gates_at:gates_at + MLSTM_HEADS], zpad,
                             wo[:, gates_at + MLSTM_HEADS:zcol], zpad], axis=1).astype(BF16)
    w_out1 = w_out_odd[0].astype(BF16)
    nw1 = _row(norm_w[1])
    cw1, cb1 = mlstm_conv_w[0], _row(mlstm_conv_b[0])
    bi = _row(mlstm_igate_b[0], LANE)
    bf = _row(mlstm_fgate_b[0], LANE)
    mnw = _row(mlstm_norm_w[0])
    fnw = _row(final_norm_w)

    up1 = _norm_matmul(hp, nw1, w_in1, tm=512, panels=2)
    hz_p, mconv_p, c_p, n_p, m_p = _mlstm_prompt(up1, cw1, cb1, bi, bf, mnw, batch=batch, seq=seq,
                                                   heads=MLSTM_HEADS, hd=m_hd, col_gates=col_gates)
    y_prompt = _out_proj(hp, [hz_p], [w_out1], fnw, tm=512)

    us1 = _norm_matmul(hs, nw1, w_in1, tm=bs, panels=2)
    m0p = _pad_cols(state_mlstm_m[0], LANE)
    hz_s, mconv_s, c_s, n_s, m_s = _mlstm_sample(us1, state_mlstm_conv[0].reshape(bs, -1),
                                                 state_mlstm_c[0], state_mlstm_n[0].reshape(bs, m_inner),
                                                 m0p, cw1, cb1, bi, bf, mnw,
                                                 heads=MLSTM_HEADS, hd=m_hd, col_gates=col_gates, bb=2)
    y_sample = _out_proj(hs, [hz_s], [w_out1], fnw, tm=bs)

    tmax = min(seq, CHUNK * max(DILATIONS))
    kv_shape = (1, batch, seq, ATT_HEADS, ATT_HD)
    return (
        y_prompt.reshape(batch, seq, d_model),
        y_sample.reshape(bs, 1, d_model),
        k_p.reshape(kv_shape)[:, :, seq - tmax:],
        v_p.reshape(kv_shape)[:, :, seq - tmax:],
        conv_p[None],
        st_p.reshape(1, batch, ssd_heads, SSD_HEADDIM, SSD_STATE),
        mconv_p[None],
        c_p[None],
        n_p.reshape(1, batch, MLSTM_HEADS, m_hd),
        m_p.reshape(1, batch, MLSTM_HEADS),
        k_s.reshape(1, bs, 1, ATT_HEADS, ATT_HD),
        v_s.reshape(1, bs, 1, ATT_HEADS, ATT_HD),
        conv_s.reshape(1, bs, CONV_K - 1, ssd_cch),
        st_s.reshape(1, bs, ssd_heads, SSD_HEADDIM, SSD_STATE),
        mconv_s.reshape(1, bs, CONV_K - 1, 2 * m_inner),
        c_s[None],
        n_s.reshape(1, bs, MLSTM_HEADS, m_hd),
        m_s.reshape(1, bs, MLSTM_HEADS),
    )
```

```python
import functools

import jax
import jax.numpy as jnp
from jax import lax
from jax.experimental import pallas as pl
from jax.experimental.pallas import tpu as pltpu

F32 = jnp.float32
BF16 = jnp.bfloat16
NEG_INF = float("-inf")
EPS = 1e-6
LANE = 128
CHUNK = 128
VMEM_LIMIT = 56 * 1024 * 1024

CONV_K = 4
ATT_HEADS = 8
ATT_HD = 64
ATT_W = ATT_HEADS * ATT_HD
ROT_DIM = ATT_HD // 4
ROPE_THETA = 500000.0
DILATIONS = (1, 4, 16)
PAST_LEN = 2048
SSD_HEADDIM = 64
SSD_GROUPS = 2
SSD_STATE = 128
MLSTM_HEADS = 8


def _params(*sem):
    return pltpu.CompilerParams(dimension_semantics=sem, vmem_limit_bytes=VMEM_LIMIT)


def _chunks(n, w):
    out, c = [], 0
    while c < n:
        out.append((c, min(w, n - c)))
        c += w
    return out


def _dot(a, b):
    return jnp.dot(a.astype(BF16), b.astype(BF16), preferred_element_type=F32)


def _dot_nt(a, b):
    return lax.dot_general(a.astype(BF16), b.astype(BF16), (((1,), (1,)), ((), ())),
                           preferred_element_type=F32)


def _dot_tn(a, b):
    return lax.dot_general(a.astype(BF16), b.astype(BF16), (((0,), (0,)), ((), ())),
                           preferred_element_type=F32)


def _split3(x):
    hi = x.astype(BF16)
    r1 = x - hi.astype(F32)
    mid = r1.astype(BF16)
    lo = (r1 - mid.astype(F32)).astype(BF16)
    return hi, mid, lo


def _sel_dot(sel, x):
    hi, mid, lo = _split3(x)
    d = lambda p: jnp.dot(sel, p, preferred_element_type=F32)
    return d(hi) + d(mid) + d(lo)


def _dot_sel(x, sel):
    hi, mid, lo = _split3(x)
    d = lambda p: jnp.dot(p, sel, preferred_element_type=F32)
    return d(hi) + d(mid) + d(lo)


def _tril(n):
    r = lax.broadcasted_iota(jnp.int32, (n, n), 0)
    c = lax.broadcasted_iota(jnp.int32, (n, n), 1)
    return r >= c


def _seg_matrix(rows, cols, seg, along_rows):
    r = lax.broadcasted_iota(jnp.int32, (rows, cols), 0)
    c = lax.broadcasted_iota(jnp.int32, (rows, cols), 1)
    m = (r // seg == c) if along_rows else (c // seg == r)
    return m.astype(BF16)


def _row_at(ref, b, cols=slice(None)):
    base = pl.multiple_of((b // 8) * 8, 8)
    tile = ref[pl.ds(base, 8), cols]
    sub = lax.broadcasted_iota(jnp.int32, tile.shape, 0)
    return jnp.sum(jnp.where(sub == b % 8, tile, 0.0), axis=0, keepdims=True)


def _silu(x):
    h = 0.5 * x
    return h + h * jnp.tanh(h)


def _sigmoid(x):
    return 0.5 + 0.5 * jnp.tanh(0.5 * x)


def _softplus(x):
    return jnp.maximum(x, 0.0) + jnp.log1p(jnp.exp(-jnp.abs(x)))


def _rms(x, w):
    return x * lax.rsqrt(jnp.mean(x * x, axis=-1, keepdims=True) + EPS) * w


def _norm_matmul_body(x_ref, nw_ref, w_ref, o_ref, *, chunks):
    xn = _rms(x_ref[...], nw_ref[...]).astype(BF16)
    for c0, cw in chunks:
        o_ref[:, c0:c0 + cw] = jnp.dot(xn, w_ref[:, c0:c0 + cw], preferred_element_type=F32)


def _norm_matmul(x, nw, w, *, tm, panels):
    m, d = x.shape
    n = w.shape[1]
    pn = n // panels
    return pl.pallas_call(
        functools.partial(_norm_matmul_body, chunks=_chunks(pn, 512)),
        grid=(panels, m // tm),
        in_specs=[pl.BlockSpec((tm, d), lambda p, i: (i, 0)),
                  pl.BlockSpec((1, d), lambda p, i: (0, 0)),
                  pl.BlockSpec((d, pn), lambda p, i: (0, p))],
        out_specs=pl.BlockSpec((tm, pn), lambda p, i: (i, p)),
        out_shape=jax.ShapeDtypeStruct((m, n), F32),
        compiler_params=_params("arbitrary", "arbitrary"),
        name="norm_matmul")(x, nw, w)


def _out_proj_body(*refs, n_in, final):
    h_ref = refs[0]
    xs = refs[1:1 + n_in]
    ws = refs[1 + n_in:1 + 2 * n_in]
    rest = refs[1 + 2 * n_in:]
    acc = h_ref[...]
    for x_ref, w_ref in zip(xs, ws):
        acc = acc + jnp.dot(x_ref[...].astype(BF16), w_ref[...], preferred_element_type=F32)
    if final:
        fw_ref, o_ref = rest
        o_ref[...] = _rms(acc, fw_ref[...])
    else:
        (o_ref,) = rest
        o_ref[...] = acc


def _out_proj(h, xs, ws, fw=None, *, tm):
    m, d = h.shape
    n_in = len(xs)
    in_specs = [pl.BlockSpec((tm, d), lambda i: (i, 0))]
    in_specs += [pl.BlockSpec((tm, x.shape[1]), lambda i: (i, 0)) for x in xs]
    in_specs += [pl.BlockSpec(w.shape, lambda i: (0, 0)) for w in ws]
    args = [h, *xs, *ws]
    if fw is not None:
        in_specs.append(pl.BlockSpec((1, d), lambda i: (0, 0)))
        args.append(fw)
    return pl.pallas_call(
        functools.partial(_out_proj_body, n_in=n_in, final=fw is not None),
        grid=(m // tm,),
        in_specs=in_specs,
        out_specs=pl.BlockSpec((tm, d), lambda i: (i, 0)),
        out_shape=jax.ShapeDtypeStruct((m, d), F32),
        compiler_params=_params("arbitrary"),
        name="out_proj")(*args)


def _rope_tables(pos, width):
    half = ROT_DIM // 2
    inv = jnp.power(F32(ROPE_THETA), -jnp.arange(half, dtype=F32) * (2.0 / ROT_DIM))
    ang = pos.astype(F32)[:, None] * inv[None, :]
    cos, sin = jnp.cos(ang), jnp.sin(ang)
    n = pos.shape[0]
    one = jnp.ones((n, ATT_HD - ROT_DIM), F32)
    z8 = jnp.zeros((n, half), F32)
    z48 = jnp.zeros((n, ATT_HD - ROT_DIM), F32)
    c = jnp.concatenate([cos, cos, one], axis=-1)
    sa = jnp.concatenate([-sin, z8, z48], axis=-1)
    sb = jnp.concatenate([z8, sin, z48], axis=-1)
    rep = width // ATT_HD
    return tuple(jnp.tile(t, (1, rep)) for t in (c, sa, sb))


def _rotary(x, c, sa, sb):
    w = x.shape[-1]
    half = ROT_DIM // 2
    return x * c + pltpu.roll(x, w - half, 1) * sa + pltpu.roll(x, half, 1) * sb


def _attn_prompt_body(q_ref, k_ref, v_ref, g_ref, c_ref, sa_ref, sb_ref,
                      att_ref, ko_ref, vo_ref,
                      nat, qd, kd, vd, od, std, o_s, st_s, *, seq):
    nblk = seq // CHUNK
    c, sa, sb = c_ref[...], sa_ref[...], sb_ref[...]
    q = _rotary(q_ref[...], c, sa, sb) * (ATT_HD ** -0.5)
    k = _rotary(k_ref[...], c, sa, sb)
    ko_ref[...] = k.T
    vo_ref[...] = v_ref[...].T
    head_lane = lax.broadcasted_iota(jnp.int32, (1, LANE), 1) // ATT_HD
    nat[0] = q
    nat[1] = k
    nat[2] = v_ref[...]
    zero = jnp.zeros((CHUNK, LANE), BF16)
    for p, d in enumerate(DILATIONS):
        kd[p, 0:CHUNK, :] = zero
        vd[p, 0:CHUNK, :] = zero
        ln = seq // d
        for r in range(d):
            rows = slice(r * ln, (r + 1) * ln)
            krows = slice(CHUNK + r * ln, CHUNK + (r + 1) * ln)
            src = pl.ds(r, ln, stride=d) if d > 1 else slice(None)
            qr = nat[0, src, :]
            qd[p, 0, rows, :] = jnp.where(head_lane == 0, qr, 0.0).astype(BF16)
            qd[p, 1, rows, :] = jnp.where(head_lane == 1, qr, 0.0).astype(BF16)
            kd[p, krows, :] = nat[1, src, :].astype(BF16)
            vd[p, krows, :] = nat[2, src, :].astype(BF16)

    row = lax.broadcasted_iota(jnp.int32, (CHUNK, 2 * CHUNK), 0)
    col = lax.broadcasted_iota(jnp.int32, (CHUNK, 2 * CHUNK), 1)
    band = (col >= row) & (col <= row + CHUNK)
    first_head = lax.broadcasted_iota(jnp.int32, (CHUNK, LANE), 1) < ATT_HD

    for p, d in enumerate(DILATIONS):
        nb = nblk // d

        def block(t, carry, p=p, nb=nb):
            base = pl.multiple_of(t * CHUNK, CHUNK)
            rows = pl.ds(base, CHUNK)
            first = (t % nb) == 0
            valid = band & (col >= jnp.where(first, CHUNK, 0))
            kw = kd[p, pl.ds(base, 2 * CHUNK), :]
            vw = vd[p, pl.ds(base, 2 * CHUNK), :]
            parts = []
            for hh in range(2):
                s = lax.dot_general(qd[p, hh, rows, :], kw, (((1,), (1,)), ((), ())),
                                    preferred_element_type=F32)
                s = jnp.where(valid, s, NEG_INF)
                m = jnp.max(s, axis=-1, keepdims=True)
                e = jnp.exp(s - m)
                o = jnp.dot(e.astype(BF16), vw, preferred_element_type=F32)
                parts.append((o, m, jnp.sum(e, axis=-1, keepdims=True)))
            (o0, m0, l0), (o1, m1, l1) = parts
            wide = lambda t: jnp.broadcast_to(t, (CHUNK, LANE))
            od[rows, :] = jnp.where(first_head, o0, o1)
            std[0, rows, :] = jnp.where(first_head, wide(m0), wide(m1))
            std[1, rows, :] = jnp.where(first_head, wide(l0), wide(l1))
            return carry

        lax.fori_loop(0, nblk, block, 0, unroll=True)
        if d == 1:
            o_s[p] = od[...]
            st_s[p] = std[...]
        else:
            ln = seq // d
            for r in range(d):
                o_s[p, pl.ds(r, ln, stride=d), :] = od[r * ln:(r + 1) * ln, :]
                st_s[p, 0, pl.ds(r, ln, stride=d), :] = std[0, r * ln:(r + 1) * ln, :]
                st_s[p, 1, pl.ds(r, ln, stride=d), :] = std[1, r * ln:(r + 1) * ln, :]

    np_ = len(DILATIONS)

    def combine(t, carry):
        base = pl.multiple_of(t * CHUNK, CHUNK)
        rows = pl.ds(base, CHUNK)
        ms = [st_s[p, 0, rows, :] for p in range(np_)]
        mx = functools.reduce(jnp.maximum, ms)
        ws = [jnp.exp(mm - mx) for mm in ms]
        num = sum(w * o_s[p, rows, :] for p, w in enumerate(ws))
        den = sum(w * st_s[p, 1, rows, :] for p, w in enumerate(ws))
        att_ref[rows, :] = ((num / den) * _silu(g_ref[rows, :])).astype(att_ref.dtype)
        return carry

    lax.fori_loop(0, nblk, combine, 0, unroll=2)


def _attn_prompt(u, cos, sa, sb, *, batch, seq):
    m = batch * seq
    npair = ATT_W // LANE
    blk = lambda off: pl.BlockSpec((seq, LANE), lambda b, hp, off=off: (b, off + hp))
    tab = pl.BlockSpec((seq, LANE), lambda b, hp: (0, 0))
    out = pl.BlockSpec((seq, LANE), lambda b, hp: (b, hp))
    out_t = pl.BlockSpec((None, LANE, seq), lambda b, hp: (b, hp, 0))
    np_ = len(DILATIONS)
    return pl.pallas_call(
        functools.partial(_attn_prompt_body, seq=seq),
        grid=(batch, npair),
        in_specs=[blk(0), blk(npair), blk(2 * npair), blk(3 * npair), tab, tab, tab],
        out_specs=[out, out_t, out_t],
        out_shape=[jax.ShapeDtypeStruct((m, ATT_W), BF16),
                   jax.ShapeDtypeStruct((batch, ATT_W, seq), F32),
                   jax.ShapeDtypeStruct((batch, ATT_W, seq), F32)],
        scratch_shapes=[pltpu.VMEM((3, seq, LANE), F32),
                        pltpu.VMEM((np_, 2, seq, LANE), BF16),
                        pltpu.VMEM((np_, seq + CHUNK, LANE), BF16),
                        pltpu.VMEM((np_, seq + CHUNK, LANE), BF16),
                        pltpu.VMEM((seq, LANE), F32),
                        pltpu.VMEM((2, seq, LANE), F32),
                        pltpu.VMEM((np_, seq, LANE), F32),
                        pltpu.VMEM((np_, 2, seq, LANE), F32)],
        compiler_params=_params("arbitrary", "arbitrary"),
        name="attn_prompt")(u, u, u, u, cos, sa, sb)


def _ssd_prompt_body(xbc_ref, z_ref, dt_ref, cw_ref, cb_ref, dtb_ref, alog_ref, dsk_ref, nw_ref,
                     y_ref, conv_ref, st_ref, ext, st, ys, *, heads, nchunk):
    c = pl.program_id(1)
    inner = heads * SSD_HEADDIM
    gw = SSD_STATE
    hpg = heads // SSD_GROUPS

    @pl.when(c == 0)
    def _():
        ext[0:8, :] = jnp.zeros((8, ext.shape[1]), F32)
        st[...] = jnp.zeros(st.shape, F32)

    ext[8:8 + CHUNK, :] = xbc_ref[...]
    conv = cb_ref[...] + ext[5:5 + CHUNK, :] * cw_ref[0:1, :]
    for j in range(1, CONV_K):
        conv = conv + ext[5 + j:5 + j + CHUNK, :] * cw_ref[j:j + 1, :]
    act = _silu(conv)

    @pl.when(c == nchunk - 1)
    def _():
        conv_ref[...] = ext[CHUNK + 5:CHUNK + 8, :]

    ext[0:8, :] = ext[CHUNK:CHUNK + 8, :]

    dt = _softplus(dt_ref[...] + dtb_ref[...])
    a = -jnp.exp(alog_ref[...])
    tril = _tril(CHUNK)
    acum = _sel_dot(jnp.where(tril, 1.0, 0.0).astype(BF16), dt * a)
    acum_t = acum.T
    last = acum[CHUNK - 1:CHUNK, :]
    elast = jnp.exp(last)
    expand = _seg_matrix(LANE, inner, SSD_HEADDIM, False)
    dt_e = _dot_sel(dt, expand)
    eacum_e = _dot_sel(jnp.exp(acum), expand)
    wend_e = _dot_sel(jnp.exp(last - acum) * dt, expand)
    xs = act[:, 0:inner]
    xdt = xs * dt_e
    xw = xs * wend_e
    hw = hpg * SSD_HEADDIM
    first_half = lax.broadcasted_iota(jnp.int32, (CHUNK, LANE), 1) < SSD_HEADDIM

    for g in range(SSD_GROUPS):
        bm = act[:, inner + g * gw:inner + (g + 1) * gw]
        cm = act[:, inner + SSD_GROUPS * gw + g * gw:inner + SSD_GROUPS * gw + (g + 1) * gw]
        cb = _dot_nt(cm, bm)
        sg = st[g * hw:(g + 1) * hw, :]
        y_inter = _dot_nt(cm, sg) * eacum_e[:, g * hw:(g + 1) * hw]
        s_local = _dot_tn(xw[:, g * hw:(g + 1) * hw], bm)
        for hg in range(hpg):
            h = g * hpg + hg
            rows = slice(hg * SSD_HEADDIM, (hg + 1) * SSD_HEADDIM)
            st[g * hw + hg * SSD_HEADDIM:g * hw + (hg + 1) * SSD_HEADDIM, :] = (
                sg[rows, :] * elast[:, h:h + 1] + s_local[rows, :])
        for pr in range(hpg // 2):
            lo = g * hw + pr * LANE
            xp = xdt[:, lo:lo + LANE].astype(BF16)
            halves = []
            for hh in range(2):
                h = g * hpg + 2 * pr + hh
                seg = acum[:, h:h + 1] - acum_t[h:h + 1, :]
                mh = cb * jnp.exp(jnp.where(tril, seg, NEG_INF))
                halves.append(jnp.dot(mh.astype(BF16), xp, preferred_element_type=F32))
            ys[:, lo:lo + LANE] = (jnp.where(first_half, halves[0], halves[1])
                                   + y_inter[:, pr * LANE:(pr + 1) * LANE])

    yt = (ys[...] + dsk_ref[...] * xs) * _silu(z_ref[...])
    y_ref[...] = _rms(yt, nw_ref[...]).astype(y_ref.dtype)

    @pl.when(c == nchunk - 1)
    def _():
        st_ref[...] = st[...]


def _ssd_prompt(u, cw, cb, dtb, alog, dsk, nw, *, batch, seq, heads, col_z, col_xbc, col_dt):
    nchunk = seq // CHUNK
    inner = heads * SSD_HEADDIM
    cch = cw.shape[1]
    row = lambda b, c: b * nchunk + c
    full = lambda a: pl.BlockSpec(a.shape, lambda b, c: (0, 0))
    return pl.pallas_call(
        functools.partial(_ssd_prompt_body, heads=heads, nchunk=nchunk),
        grid=(batch, nchunk),
        in_specs=[pl.BlockSpec((CHUNK, cch), lambda b, c: (row(b, c), col_xbc // cch)),
                  pl.BlockSpec((CHUNK, inner), lambda b, c: (row(b, c), col_z // inner)),
                  pl.BlockSpec((CHUNK, LANE), lambda b, c: (row(b, c), col_dt // LANE)),
                  full(cw), full(cb), full(dtb), full(alog), full(dsk), full(nw)],
        out_specs=[pl.BlockSpec((CHUNK, inner), lambda b, c: (row(b, c), 0)),
                   pl.BlockSpec((None, CONV_K - 1, cch), lambda b, c: (b, 0, 0)),
                   pl.BlockSpec((None, inner, SSD_STATE), lambda b, c: (b, 0, 0))],
        out_shape=[jax.ShapeDtypeStruct((batch * seq, inner), BF16),
                   jax.ShapeDtypeStruct((batch, CONV_K - 1, cch), F32),
                   jax.ShapeDtypeStruct((batch, inner, SSD_STATE), F32)],
        scratch_shapes=[pltpu.VMEM((CHUNK + 8, cch), F32),
                        pltpu.VMEM((inner, SSD_STATE), F32),
                        pltpu.VMEM((CHUNK, inner), F32)],
        compiler_params=_params("arbitrary", "arbitrary"),
        name="ssd_prompt")(u, u, u, cw, cb, dtb, alog, dsk, nw)


def _mlstm_prompt_body(q_ref, k_ref, v_ref, o_ref, z_ref, gi_ref, gf_ref,
                       cw_ref, cb_ref, bi_ref, bf_ref, nw_ref,
                       hz_ref, conv_ref, c_out, n_out, m_out,
                       histq, histk, c_s, n_s, m_s, *, nchunk, heads, hd):
    c = pl.program_id(1)
    inner = heads * hd

    @pl.when(c == 0)
    def _():
        histq[...] = jnp.zeros((8, inner), F32)
        histk[...] = jnp.zeros((8, inner), F32)
        c_s[...] = jnp.zeros(c_s.shape, F32)
        n_s[...] = jnp.zeros(n_s.shape, F32)
        m_s[...] = jnp.full(m_s.shape, NEG_INF, F32)

    def conv_head(hist, x_ref, col0, h):
        sl = slice(h * hd, (h + 1) * hd)
        cols = slice(col0 + h * hd, col0 + (h + 1) * hd)
        xp = jnp.concatenate([hist[:, sl], x_ref[:, sl]], axis=0)
        acc = cb_ref[:, cols] + xp[5:5 + CHUNK, :] * cw_ref[0:1, cols]
        for j in range(1, CONV_K):
            acc = acc + xp[5 + j:5 + j + CHUNK, :] * cw_ref[j:j + 1, cols]
        return _silu(acc)

    @pl.when(c == nchunk - 1)
    def _():
        conv_ref[:, 0:inner] = q_ref[CHUNK - (CONV_K - 1):CHUNK, :]
        conv_ref[:, inner:] = k_ref[CHUNK - (CONV_K - 1):CHUNK, :]

    it = gi_ref[...] + bi_ref[...]
    logf = -_softplus(-(gf_ref[...] + bf_ref[...]))
    tril = _tril(CHUNK)
    bc = _sel_dot(jnp.where(tril, 1.0, 0.0).astype(BF16), logf)
    it_t = it.T
    bc_t = bc.T

    for h in range(heads):
        sl = slice(h * hd, (h + 1) * hd)
        q = conv_head(histq, q_ref, 0, h)
        k = conv_head(histk, k_ref, inner, h) * (hd ** -0.5)
        v = v_ref[:, sl]
        i_col, b_col = it[:, h:h + 1], bc[:, h:h + 1]
        i_row, b_row = it_t[h:h + 1, :], bc_t[h:h + 1, :]
        m_prev = m_s[h]
        dmat = jnp.where(tril, b_col - b_row + i_row, NEG_INF)
        inter = b_col + m_prev
        m_t = jnp.maximum(inter, jnp.max(dmat, axis=-1, keepdims=True))
        w_intra = jnp.exp(dmat - m_t)
        w_inter = jnp.exp(inter - m_t)
        att = w_intra * _dot_nt(q, k)
        c_prev = c_s[h]
        n_prev = n_s[h]
        num = _dot(att, v) + w_inter * _dot(q, c_prev)
        qn = _dot_nt(q, jnp.broadcast_to(n_prev, (8, hd)))[:, 0:1]
        den = jnp.sum(att, axis=-1, keepdims=True) + w_inter * qn
        hh = num / jnp.maximum(jnp.abs(den), jnp.exp(-m_t))

        b_last = b_col[CHUNK - 1:CHUNK, :]
        logw = b_last - b_col + i_col
        m_new = jnp.maximum(b_last + m_prev, jnp.max(logw, axis=0, keepdims=True))
        ws = jnp.exp(logw - m_new)
        ws_row = jnp.exp(b_last - b_row + i_row - m_new)
        scale = jnp.exp(b_last + m_prev - m_new)
        c_s[h] = scale * c_prev + _dot_tn(k, ws * v)
        n_s[h] = scale * n_prev + _dot(jnp.broadcast_to(ws_row, (8, CHUNK)), k)[0:1, :]
        m_s[h] = m_new

        hg = _sigmoid(o_ref[:, sl]) * hh
        mu = jnp.mean(hg, axis=-1, keepdims=True)
        var = jnp.mean(jnp.square(hg - mu), axis=-1, keepdims=True)
        hn = (hg - mu) * lax.rsqrt(var + EPS) * nw_ref[:, sl]
        hz_ref[:, sl] = (hn * _silu(z_ref[:, sl])).astype(hz_ref.dtype)

    histq[...] = q_ref[CHUNK - 8:CHUNK, :]
    histk[...] = k_ref[CHUNK - 8:CHUNK, :]

    @pl.when(c == nchunk - 1)
    def _():
        c_out[...] = c_s[...]
        n_out[...] = n_s[...]
        m_out[...] = m_s[...]


def _mlstm_prompt(u, cw, cb, bi, bf, nw, *, batch, seq, heads, hd, col_gates):
    nchunk = seq // CHUNK
    inner = heads * hd
    row = lambda b, c: b * nchunk + c
    ublk = lambda j: pl.BlockSpec((CHUNK, inner), lambda b, c, j=j: (row(b, c), j))
    gblk = lambda off: pl.BlockSpec((CHUNK, LANE), lambda b, c, off=off: (row(b, c), col_gates // LANE + off))
    full = lambda a: pl.BlockSpec(a.shape, lambda b, c: (0, 0))
    return pl.pallas_call(
        functools.partial(_mlstm_prompt_body, nchunk=nchunk, heads=heads, hd=hd),
        grid=(batch, nchunk),
        in_specs=[ublk(0), ublk(1), ublk(2), ublk(3), ublk(4), gblk(0), gblk(1),
                  full(cw), full(cb), full(bi), full(bf), full(nw)],
        out_specs=[pl.BlockSpec((CHUNK, inner), lambda b, c: (row(b, c), 0)),
                   pl.BlockSpec((None, CONV_K - 1, 2 * inner), lambda b, c: (b, 0, 0)),
                   pl.BlockSpec((None, heads, hd, hd), lambda b, c: (b, 0, 0, 0)),
                   pl.BlockSpec((None, heads, 1, hd), lambda b, c: (b, 0, 0, 0)),
                   pl.BlockSpec((None, heads, 1, 1), lambda b, c: (b, 0, 0, 0))],
        out_shape=[jax.ShapeDtypeStruct((batch * seq, inner), BF16),
                   jax.ShapeDtypeStruct((batch, CONV_K - 1, 2 * inner), F32),
                   jax.ShapeDtypeStruct((batch, heads, hd, hd), F32),
                   jax.ShapeDtypeStruct((batch, heads, 1, hd), F32),
                   jax.ShapeDtypeStruct((batch, heads, 1, 1), F32)],
        scratch_shapes=[pltpu.VMEM((8, inner), F32),
                        pltpu.VMEM((8, inner), F32),
                        pltpu.VMEM((heads, hd, hd), F32),
                        pltpu.VMEM((heads, 1, hd), F32),
                        pltpu.VMEM((heads, 1, 1), F32)],
        compiler_params=_params("arbitrary", "arbitrary"),
        name="mlstm_prompt")(u, u, u, u, u, u, u, cw, cb, bi, bf, nw)


def _rot_sample_body(q_ref, k_ref, v_ref, g_ref, c_ref, sa_ref, sb_ref, xt_ref):
    c, sa, sb = c_ref[...], sa_ref[...], sb_ref[...]
    q = _rotary(q_ref[...], c, sa, sb) * (ATT_HD ** -0.5)
    k = _rotary(k_ref[...], c, sa, sb)
    xt_ref[...] = jnp.concatenate([q, k, v_ref[...], g_ref[...]], axis=-1).T


def _rot_sample(u, cos, sa, sb):
    bs = u.shape[0]
    ublk = lambda off: pl.BlockSpec((bs, ATT_W), lambda i, off=off: (0, off))
    tab = pl.BlockSpec((1, ATT_W), lambda i: (0, 0))
    return pl.pallas_call(
        _rot_sample_body,
        grid=(1,),
        in_specs=[ublk(0), ublk(1), ublk(2), ublk(3), tab, tab, tab],
        out_specs=pl.BlockSpec((4 * ATT_W, bs), lambda i: (0, 0)),
        out_shape=jax.ShapeDtypeStruct((4 * ATT_W, bs), F32),
        compiler_params=_params("arbitrary"),
        name="rot_sample")(u, u, u, u, cos, sa, sb)


def _key_multiplicity(wb):
    back = wb - lax.broadcasted_iota(jnp.int32, (1, wb), 1)
    cnt = jnp.zeros((1, wb), F32)
    for d in DILATIONS:
        cnt = cnt + ((back <= CHUNK * d) & (lax.rem(back, d) == 0)).astype(F32)
    return cnt


def _attn_sample_body(x_ref, kc_ref, vc_ref, att_ref, *, bb, wb):
    i = pl.program_id(0)
    bs = x_ref.shape[1]
    nt = wb // LANE

    @pl.when(i == 0)
    def _():
        att_ref[...] = jnp.zeros(att_ref.shape, F32)

    cnt = _key_multiplicity(wb)
    valid = cnt > 0.0
    npat = float(len(DILATIONS))
    rid = lax.broadcasted_iota(jnp.int32, (bs, LANE), 0)
    lid = lax.broadcasted_iota(jnp.int32, (ATT_HD, bs), 1)
    for t in range(bb):
        b = i * bb + t
        cols = _dot_sel(x_ref[...], (rid == b).astype(BF16))
        for h in range(ATT_HEADS):
            lo = h * ATT_HD
            qc = cols[lo:lo + ATT_HD]
            kc = cols[ATT_W + lo:ATT_W + lo + ATT_HD]
            vc = cols[2 * ATT_W + lo:2 * ATT_W + lo + ATT_HD]
            gc = cols[3 * ATT_W + lo:3 * ATT_W + lo + ATT_HD]
            kt = kc_ref[t, h]
            vt = vc_ref[t, h]
            s = jnp.sum(kt * jnp.tile(qc, (1, nt)), axis=0, keepdims=True)
            s = jnp.where(valid, s, NEG_INF)
            s_self = jnp.sum(qc * kc, axis=0, keepdims=True)[:, 0:1]
            m = jnp.maximum(jnp.max(s, axis=1, keepdims=True), s_self)
            p = cnt * jnp.exp(s - m)
            p_self = npat * jnp.exp(s_self - m)
            den = jnp.sum(p, axis=1, keepdims=True) + p_self
            acc = vt[:, 0:LANE] * p[:, 0:LANE]
            for j in range(1, nt):
                acc = acc + vt[:, j * LANE:(j + 1) * LANE] * p[:, j * LANE:(j + 1) * LANE]
            o = jnp.sum(acc, axis=1, keepdims=True) + p_self * vc[:, 0:1]
            o = o / den * _silu(gc[:, 0:1])
            att_ref[lo:lo + ATT_HD, :] = jnp.where(lid == b, o, att_ref[lo:lo + ATT_HD, :])


def _attn_sample(xt, ck, cv, *, bb):
    bs, wb = ck.shape[0], ck.shape[3]
    cache = pl.BlockSpec((bb, ATT_HEADS, ATT_HD, wb), lambda i: (i, 0, 0, 0))
    return pl.pallas_call(
        functools.partial(_attn_sample_body, bb=bb, wb=wb),
        grid=(bs // bb,),
        in_specs=[pl.BlockSpec(xt.shape, lambda i: (0, 0)), cache, cache],
        out_specs=pl.BlockSpec((ATT_W, bs), lambda i: (0, 0)),
        out_shape=jax.ShapeDtypeStruct((ATT_W, bs), F32),
        compiler_params=_params("arbitrary"),
        name="attn_sample")(xt, ck, cv)


def _ssd_sample_prep_body(xbc_ref, dt_ref, cst_ref, cw_ref, cb_ref, dtb_ref, alog_ref, aloge_ref, dsk_ref,
                          conv_ref, yloc_ref, xdtt_ref, bc_ref, da_ref, dae_ref, *, heads):
    inner = heads * SSD_HEADDIM
    cch = xbc_ref.shape[1]
    gw = SSD_STATE
    x = xbc_ref[...]
    acc = cb_ref[...] + x * cw_ref[CONV_K - 1:CONV_K, :]
    for j in range(CONV_K - 1):
        acc = acc + cst_ref[:, j * cch:(j + 1) * cch] * cw_ref[j:j + 1, :]
    for j in range(CONV_K - 2):
        conv_ref[:, j * cch:(j + 1) * cch] = cst_ref[:, (j + 1) * cch:(j + 2) * cch]
    conv_ref[:, (CONV_K - 2) * cch:(CONV_K - 1) * cch] = x
    act = _silu(acc)
    xs = act[:, 0:inner]
    bc_ref[...] = act[:, inner:]
    dt = _softplus(dt_ref[...] + dtb_ref[...])
    da_ref[...] = jnp.exp(dt * (-jnp.exp(alog_ref[...])))
    expand = _seg_matrix(LANE, inner, SSD_HEADDIM, False)
    dte = _dot_sel(dt, expand)
    dae_ref[...] = jnp.exp(dte * (-jnp.exp(aloge_ref[...])))
    xdt = xs * dte
    xdtt_ref[...] = xdt.T
    hw = inner // SSD_GROUPS
    parts = []
    for g in range(SSD_GROUPS):
        bm = act[:, inner + g * gw:inner + (g + 1) * gw]
        cm = act[:, inner + SSD_GROUPS * gw + g * gw:inner + SSD_GROUPS * gw + (g + 1) * gw]
        cbg = jnp.sum(cm * bm, axis=-1, keepdims=True)
        parts.append(cbg * xdt[:, g * hw:(g + 1) * hw])
    yloc_ref[...] = jnp.concatenate(parts, axis=-1) + dsk_ref[...] * xs


def _ssd_sample_state_body(da_ref, s_ref, xdtt_ref, bc_ref, so_ref, yi_ref, *, bb, heads):
    i = pl.program_id(0)
    bs = bc_ref.shape[0]
    inner = heads * SSD_HEADDIM
    hw = inner // SSD_GROUPS
    hpg = heads // SSD_GROUPS
    gw = SSD_STATE
    rid = lax.broadcasted_iota(jnp.int32, (bs, gw), 0)
    for t in range(bb):
        b = i * bb + t
        parts = []
        for g in range(SSD_GROUPS):
            mg = jnp.where(rid == b, bc_ref[:, g * gw:(g + 1) * gw], 0.0)
            sl = _dot(xdtt_ref[g * hw:(g + 1) * hw, :], mg)
            crow = _row_at(bc_ref, b, slice(SSD_GROUPS * gw + g * gw, SSD_GROUPS * gw + (g + 1) * gw))
            sg = s_ref[t, g * hw:(g + 1) * hw, :]
            parts.append(_dot_nt(jnp.broadcast_to(crow, (8, gw)), sg)[0:1, :])
            for hg in range(hpg):
                h = g * hpg + hg
                lo = hg * SSD_HEADDIM
                so_ref[t, h * SSD_HEADDIM:(h + 1) * SSD_HEADDIM, :] = (
                    sg[lo:lo + SSD_HEADDIM, :] * da_ref[b, h] + sl[lo:lo + SSD_HEADDIM, :])
        yi_ref[t] = jnp.concatenate(parts, axis=-1)


def _ssd_sample_finish_body(yloc_ref, yi_ref, dae_ref, z_ref, nw_ref, y_ref):
    y = (yloc_ref[...] + yi_ref[...] * dae_ref[...]) * _silu(z_ref[...])
    y_ref[...] = _rms(y, nw_ref[...])


def _ssd_sample(u, cst, state, cw, cb, dtb, alog, aloge, dsk, nw, *, heads, col_z, col_xbc, col_dt, bb):
    bs = u.shape[0]
    inner = heads * SSD_HEADDIM
    cch = cw.shape[1]
    ncs = (CONV_K - 1) * cch
    bcw = 2 * SSD_GROUPS * SSD_STATE
    full = lambda a: pl.BlockSpec(a.shape, lambda i: (0,) * a.ndim)
    conv, yloc, xdtt, bc, da, dae = pl.pallas_call(
        functools.partial(_ssd_sample_prep_body, heads=heads),
        grid=(1,),
        in_specs=[pl.BlockSpec((bs, cch), lambda i: (0, col_xbc // cch)),
                  pl.BlockSpec((bs, LANE), lambda i: (0, col_dt // LANE)),
                  full(cst), full(cw), full(cb), full(dtb), full(alog), full(aloge), full(dsk)],
        out_specs=[pl.BlockSpec((bs, ncs), lambda i: (0, 0)),
                   pl.BlockSpec((bs, inner), lambda i: (0, 0)),
                   pl.BlockSpec((inner, bs), lambda i: (0, 0)),
                   pl.BlockSpec((bs, bcw), lambda i: (0, 0)),
                   pl.BlockSpec((bs, LANE), lambda i: (0, 0)),
                   pl.BlockSpec((bs, inner), lambda i: (0, 0))],
        out_shape=[jax.ShapeDtypeStruct((bs, ncs), F32),
                   jax.ShapeDtypeStruct((bs, inner), F32),
                   jax.ShapeDtypeStruct((inner, bs), F32),
                   jax.ShapeDtypeStruct((bs, bcw), F32),
                   jax.ShapeDtypeStruct((bs, LANE), F32),
                   jax.ShapeDtypeStruct((bs, inner), F32)],
        compiler_params=_params("arbitrary"),
        name="ssd_sample_prep")(u, u, cst, cw, cb, dtb, alog, aloge, dsk)
    new_state, yi = pl.pallas_call(
        functools.partial(_ssd_sample_state_body, bb=bb, heads=heads),
        grid=(bs // bb,),
        in_specs=[pl.BlockSpec(memory_space=pltpu.SMEM),
                  pl.BlockSpec((bb, inner, SSD_STATE), lambda i: (i, 0, 0)),
                  full(xdtt), full(bc)],
        out_specs=[pl.BlockSpec((bb, inner, SSD_STATE), lambda i: (i, 0, 0)),
                   pl.BlockSpec((bb, 1, inner), lambda i: (i, 0, 0))],
        out_shape=[jax.ShapeDtypeStruct((bs, inner, SSD_STATE), F32),
                   jax.ShapeDtypeStruct((bs, 1, inner), F32)],
        compiler_params=_params("arbitrary"),
        name="ssd_sample_state")(da[:, :heads], state, xdtt, bc)
    y = pl.pallas_call(
        _ssd_sample_finish_body,
        grid=(1,),
        in_specs=[full(yloc), pl.BlockSpec((bs, inner), lambda i: (0, 0)), full(dae),
                  pl.BlockSpec((bs, inner), lambda i: (0, col_z // inner)), full(nw)],
        out_specs=pl.BlockSpec((bs, inner), lambda i: (0, 0)),
        out_shape=jax.ShapeDtypeStruct((bs, inner), F32),
        compiler_params=_params("arbitrary"),
        name="ssd_sample_finish")(yloc, yi.reshape(bs, inner), dae, u, nw)
    return y, conv, new_state


def _mlstm_sample_prep_body(q_ref, k_ref, gi_ref, gf_ref, cst_ref, cw_ref, cb_ref, bi_ref, bf_ref, m_ref,
                            conv_ref, qk_ref, kwt_ref, wi_ref, wf_ref, mt_ref, *, heads, hd):
    inner = heads * hd
    cch = 2 * inner
    x = jnp.concatenate([q_ref[...], k_ref[...]], axis=-1)
    acc = cb_ref[...] + x * cw_ref[CONV_K - 1:CONV_K, :]
    for j in range(CONV_K - 1):
        acc = acc + cst_ref[:, j * cch:(j + 1) * cch] * cw_ref[j:j + 1, :]
    for j in range(CONV_K - 2):
        conv_ref[:, j * cch:(j + 1) * cch] = cst_ref[:, (j + 1) * cch:(j + 2) * cch]
    conv_ref[:, (CONV_K - 2) * cch:(CONV_K - 1) * cch] = x
    act = _silu(acc)
    q = act[:, 0:inner]
    k = act[:, inner:] * (hd ** -0.5)
    it = gi_ref[...] + bi_ref[...]
    inter = -_softplus(-(gf_ref[...] + bf_ref[...])) + m_ref[...]
    mt = jnp.maximum(inter, it)
    wi = jnp.exp(it - mt)
    wf = jnp.exp(inter - mt)
    wi_ref[...] = wi
    wf_ref[...] = wf
    mt_ref[...] = mt
    qk_ref[:, 0:inner] = q
    qk_ref[:, inner:] = k
    kw = jnp.concatenate([k[:, h * hd:(h + 1) * hd] * wi[:, h:h + 1] for h in range(heads)], axis=-1)
    kwt_ref[...] = kw.T


def _mlstm_sample_state_body(wf_ref, c_ref, qk_ref, v_ref, kwt_ref, co_ref, qc_ref, *, bb, heads, hd):
    i = pl.program_id(0)
    bs = v_ref.shape[0]
    rid = lax.broadcasted_iota(jnp.int32, (bs, hd), 0)
    for t in range(bb):
        b = i * bb + t
        parts = []
        for h in range(heads):
            cp = c_ref[t, h]
            qrow = _row_at(qk_ref, b, slice(h * hd, (h + 1) * hd))
            parts.append(_dot(jnp.broadcast_to(qrow, (8, hd)), cp)[0:1, :])
            mh = jnp.where(rid == b, v_ref[:, h * hd:(h + 1) * hd], 0.0)
            co_ref[t, h] = cp * wf_ref[b, h] + _dot(kwt_ref[h * hd:(h + 1) * hd, :], mh)
        qc_ref[t] = jnp.concatenate(parts, axis=-1)


def _mlstm_sample_finish_body(qk_ref, v_ref, o_ref, z_ref, qc_ref, n_ref, wi_ref, wf_ref, mt_ref, nw_ref,
                              hz_ref, no_ref, *, heads, hd):
    inner = heads * hd
    for h in range(heads):
        sl = slice(h * hd, (h + 1) * hd)
        q = qk_ref[:, sl]
        k = qk_ref[:, inner + h * hd:inner + (h + 1) * hd]
        wi = wi_ref[:, h:h + 1]
        wf = wf_ref[:, h:h + 1]
        mt = mt_ref[:, h:h + 1]
        n_prev = n_ref[:, sl]
        att = wi * jnp.sum(q * k, axis=-1, keepdims=True)
        num = att * v_ref[:, sl] + wf * qc_ref[:, sl]
        den = att + wf * jnp.sum(q * n_prev, axis=-1, keepdims=True)
        hh = num / jnp.maximum(jnp.abs(den), jnp.exp(-mt))
        hg = _sigmoid(o_ref[:, sl]) * hh
        mu = jnp.mean(hg, axis=-1, keepdims=True)
        var = jnp.mean(jnp.square(hg - mu), axis=-1, keepdims=True)
        hn = (hg - mu) * lax.rsqrt(var + EPS) * nw_ref[:, sl]
        hz_ref[:, sl] = hn * _silu(z_ref[:, sl])
        no_ref[:, sl] = wf * n_prev + wi * k


def _mlstm_sample(u, cst, c0, n0, m0p, cw, cb, bi, bf, nw, *, heads, hd, col_gates, bb):
    bs = u.shape[0]
    inner = heads * hd
    ncs = (CONV_K - 1) * 2 * inner
    full = lambda a: pl.BlockSpec(a.shape, lambda i: (0,) * a.ndim)
    ucol = lambda j: pl.BlockSpec((bs, inner), lambda i, j=j: (0, j))
    gcol = lambda j: pl.BlockSpec((bs, LANE), lambda i, j=j: (0, col_gates // LANE + j))
    tile = jax.ShapeDtypeStruct((bs, LANE), F32)
    conv, qk, kwt, wi, wf, mt = pl.pallas_call(
        functools.partial(_mlstm_sample_prep_body, heads=heads, hd=hd),
        grid=(1,),
        in_specs=[ucol(0), ucol(1), gcol(0), gcol(1), full(cst), full(cw), full(cb), full(bi), full(bf),
                  full(m0p)],
        out_specs=[pl.BlockSpec((bs, ncs), lambda i: (0, 0)),
                   pl.BlockSpec((bs, 2 * inner), lambda i: (0, 0)),
                   pl.BlockSpec((inner, bs), lambda i: (0, 0)),
                   pl.BlockSpec((bs, LANE), lambda i: (0, 0)),
                   pl.BlockSpec((bs, LANE), lambda i: (0, 0)),
                   pl.BlockSpec((bs, LANE), lambda i: (0, 0))],
        out_shape=[jax.ShapeDtypeStruct((bs, ncs), F32),
                   jax.ShapeDtypeStruct((bs, 2 * inner), F32),
                   jax.ShapeDtypeStruct((inner, bs), F32),
                   tile, tile, tile],
        compiler_params=_params("arbitrary"),
        name="mlstm_sample_prep")(u, u, u, u, cst, cw, cb, bi, bf, m0p)
    c_new, qc = pl.pallas_call(
        functools.partial(_mlstm_sample_state_body, bb=bb, heads=heads, hd=hd),
        grid=(bs // bb,),
        in_specs=[pl.BlockSpec(memory_space=pltpu.SMEM),
                  pl.BlockSpec((bb, heads, hd, hd), lambda i: (i, 0, 0, 0)),
                  full(qk), pl.BlockSpec((bs, inner), lambda i: (0, 2)), full(kwt)],
        out_specs=[pl.BlockSpec((bb, heads, hd, hd), lambda i: (i, 0, 0, 0)),
                   pl.BlockSpec((bb, 1, inner), lambda i: (i, 0, 0))],
        out_shape=[jax.ShapeDtypeStruct((bs, heads, hd, hd), F32),
                   jax.ShapeDtypeStruct((bs, 1, inner), F32)],
        compiler_params=_params("arbitrary"),
        name="mlstm_sample_state")(wf[:, :heads], c0, qk, u, kwt)
    hz, n_new = pl.pallas_call(
        functools.partial(_mlstm_sample_finish_body, heads=heads, hd=hd),
        grid=(1,),
        in_specs=[full(qk), ucol(2), ucol(3), ucol(4), pl.BlockSpec((bs, inner), lambda i: (0, 0)),
                  full(n0), full(wi), full(wf), full(mt), full(nw)],
        out_specs=[pl.BlockSpec((bs, inner), lambda i: (0, 0)),
                   pl.BlockSpec((bs, inner), lambda i: (0, 0))],
        out_shape=[jax.ShapeDtypeStruct((bs, inner), F32)] * 2,
        compiler_params=_params("arbitrary"),
        name="mlstm_sample_finish")(qk, u, u, u, qc.reshape(bs, inner), n0, wi, wf, mt, nw)
    return hz, conv, c_new, n_new, mt[:, :heads]


def _pad_cols(w, n):
    return jnp.pad(w, ((0, 0), (0, n - w.shape[1])))


def _row(v, n=None):
    v = v.reshape(1, -1)
    return v if n is None else _pad_cols(v, n)


def kernel(x_prompt, x_sample, cache_attn_k, cache_attn_v, state_ssd_conv, state_ssd, state_mlstm_conv, state_mlstm_c, state_mlstm_n, state_mlstm_m, norm_w, final_norm_w, w_in_even, w_out_even, ssd_conv_w, ssd_conv_b, ssd_dt_bias, ssd_a_log, ssd_d, ssd_norm_w, w_in_odd, w_out_odd, mlstm_conv_w, mlstm_conv_b, mlstm_igate_b, mlstm_fgate_b, mlstm_norm_w):
    batch, seq, d_model = x_prompt.shape
    bs = x_sample.shape[0]
    ssd_heads = ssd_a_log.shape[1]
    ssd_inner = ssd_heads * SSD_HEADDIM
    ssd_cch = ssd_conv_w.shape[2]
    m_inner = mlstm_norm_w.shape[1]
    m_hd = m_inner // MLSTM_HEADS
    mp = batch * seq

    col_z = 4 * ATT_W
    col_xbc = col_z + ssd_inner
    col_dt = col_xbc + ssd_cch
    n_even = col_dt + LANE
    w_in0 = _pad_cols(w_in_even[0], n_even).astype(BF16)
    w_out0 = w_out_even[0].astype(BF16)
    nw0 = _row(norm_w[0])
    cw0, cb0 = ssd_conv_w[0], _row(ssd_conv_b[0])
    dtb = _row(ssd_dt_bias[0], LANE)
    alog = _row(ssd_a_log[0], LANE)
    aloge = _row(jnp.repeat(ssd_a_log[0], SSD_HEADDIM))
    dsk = _row(jnp.repeat(ssd_d[0], SSD_HEADDIM))
    snw = _row(ssd_norm_w[0])

    hp = x_prompt.reshape(mp, d_model)
    hs = x_sample.reshape(bs, d_model)

    up = _norm_matmul(hp, nw0, w_in0, tm=512, panels=1)
    cos_p, sa_p, sb_p = _rope_tables(jnp.arange(seq), LANE)
    att_p, k_p, v_p = _attn_prompt(up, cos_p, sa_p, sb_p, batch=batch, seq=seq)
    y_p, conv_p, st_p = _ssd_prompt(up, cw0, cb0, dtb, alog, dsk, snw, batch=batch, seq=seq,
                                    heads=ssd_heads, col_z=col_z, col_xbc=col_xbc, col_dt=col_dt)
    hp = _out_proj(hp, [att_p, y_p], [w_out0[:ATT_W], w_out0[ATT_W:]], tm=512)

    us = _norm_matmul(hs, nw0, w_in0, tm=bs, panels=1)
    cos_s, sa_s, sb_s = _rope_tables(PAST_LEN + jnp.arange(1), ATT_W)
    xt_s = _rot_sample(us, cos_s, sa_s, sb_s)
    k_s = xt_s[ATT_W:2 * ATT_W].T
    v_s = xt_s[2 * ATT_W:3 * ATT_W].T
    pos_minor = lambda cache: jnp.transpose(cache, (0, 2, 3, 1))
    att_s = _attn_sample(xt_s, pos_minor(cache_attn_k[0]), pos_minor(cache_attn_v[0]), bb=2).T
    y_s, conv_s, st_s = _ssd_sample(us, state_ssd_conv[0].reshape(bs, -1),
                                    state_ssd[0].reshape(bs, ssd_inner, SSD_STATE),
                                    cw0, cb0, dtb, alog, aloge, dsk, snw, heads=ssd_heads,
                                    col_z=col_z, col_xbc=col_xbc, col_dt=col_dt, bb=4)
    hs = _out_proj(hs, [att_s, y_s], [w_out0[:ATT_W], w_out0[ATT_W:]], tm=bs)

    wo = w_in_odd[0]
    gates_at = 4 * m_inner
    zcol = gates_at + 2 * MLSTM_HEADS
    col_gates = 5 * m_inner
    zpad = jnp.zeros((d_model, LANE - MLSTM_HEADS), wo.dtype)
    w_in1 = jnp.concatenate([wo[:, :gates_at], wo[:, zcol:],
                             wo[:, gates_at:gates_at + MLSTM_HEADS], zpad,
                             wo[:, gates_at + MLSTM_HEADS:zcol], zpad], axis=1).astype(BF16)
    w_out1 = w_out_odd[0].astype(BF16)
    nw1 = _row(norm_w[1])
    cw1, cb1 = mlstm_conv_w[0], _row(mlstm_conv_b[0])
    bi = _row(mlstm_igate_b[0], LANE)
    bf = _row(mlstm_fgate_b[0], LANE)
    mnw = _row(mlstm_norm_w[0])
    fnw = _row(final_norm_w)

    up1 = _norm_matmul(hp, nw1, w_in1, tm=512, panels=2)
    hz_p, mconv_p, c_p, n_p, m_p = _mlstm_prompt(up1, cw1, cb1, bi, bf, mnw, batch=batch, seq=seq,
                                                   heads=MLSTM_HEADS, hd=m_hd, col_gates=col_gates)
    y_prompt = _out_proj(hp, [hz_p], [w_out1], fnw, tm=512)

    us1 = _norm_matmul(hs, nw1, w_in1, tm=bs, panels=2)
    m0p = _pad_cols(state_mlstm_m[0], LANE)
    hz_s, mconv_s, c_s, n_s, m_s = _mlstm_sample(us1, state_mlstm_conv[0].reshape(bs, -1),
                                                 state_mlstm_c[0], state_mlstm_n[0].reshape(bs, m_inner),
                                                 m0p, cw1, cb1, bi, bf, mnw,
                                                 heads=MLSTM_HEADS, hd=m_hd, col_gates=col_gates, bb=2)
    y_sample = _out_proj(hs, [hz_s], [w_out1], fnw, tm=bs)

    tmax = min(seq, CHUNK * max(DILATIONS))
    kv_out = lambda t: jnp.transpose(t.reshape(batch, ATT_HEADS, ATT_HD, seq), (0, 3, 1, 2))[None, :, seq - tmax:]
    return (
        y_prompt.reshape(batch, seq, d_model),
        y_sample.reshape(bs, 1, d_model),
        kv_out(k_p),
        kv_out(v_p),
        conv_p[None],
        st_p.reshape(1, batch, ssd_heads, SSD_HEADDIM, SSD_STATE),
        mconv_p[None],
        c_p[None],
        n_p.reshape(1, batch, MLSTM_HEADS, m_hd),
        m_p.reshape(1, batch, MLSTM_HEADS),
        k_s.reshape(1, bs, 1, ATT_HEADS, ATT_HD),
        v_s.reshape(1, bs, 1, ATT_HEADS, ATT_HD),
        conv_s.reshape(1, bs, CONV_K - 1, ssd_cch),
        st_s.reshape(1, bs, ssd_heads, SSD_HEADDIM, SSD_STATE),
        mconv_s.reshape(1, bs, CONV_K - 1, 2 * m_inner),
        c_s[None],
        n_s.reshape(1, bs, MLSTM_HEADS, m_hd),
        m_s.reshape(1, bs, MLSTM_HEADS),
    )
```

```python
import functools

import jax
import jax.numpy as jnp
from jax import lax
from jax.experimental import pallas as pl
from jax.experimental.pallas import tpu as pltpu

F32 = jnp.float32
BF16 = jnp.bfloat16
NEG_INF = float("-inf")
EPS = 1e-6
LANE = 128
CHUNK = 128
VMEM_LIMIT = 56 * 1024 * 1024

CONV_K = 4
ATT_HEADS = 8
ATT_HD = 64
ATT_W = ATT_HEADS * ATT_HD
ROT_DIM = ATT_HD // 4
ROPE_THETA = 500000.0
DILATIONS = (1, 4, 16)
PAST_LEN = 2048
SSD_HEADDIM = 64
SSD_GROUPS = 2
SSD_STATE = 128
MLSTM_HEADS = 8


def _params(*sem):
    return pltpu.CompilerParams(dimension_semantics=sem, vmem_limit_bytes=VMEM_LIMIT)


def _chunks(n, w):
    out, c = [], 0
    while c < n:
        out.append((c, min(w, n - c)))
        c += w
    return out


def _dot(a, b):
    return jnp.dot(a.astype(BF16), b.astype(BF16), preferred_element_type=F32)


def _dot_nt(a, b):
    return lax.dot_general(a.astype(BF16), b.astype(BF16), (((1,), (1,)), ((), ())),
                           preferred_element_type=F32)


def _dot_tn(a, b):
    return lax.dot_general(a.astype(BF16), b.astype(BF16), (((0,), (0,)), ((), ())),
                           preferred_element_type=F32)


def _split3(x):
    hi = x.astype(BF16)
    r1 = x - hi.astype(F32)
    mid = r1.astype(BF16)
    lo = (r1 - mid.astype(F32)).astype(BF16)
    return hi, mid, lo


def _sel_dot(sel, x):
    hi, mid, lo = _split3(x)
    d = lambda p: jnp.dot(sel, p, preferred_element_type=F32)
    return d(hi) + d(mid) + d(lo)


def _dot_sel(x, sel):
    hi, mid, lo = _split3(x)
    d = lambda p: jnp.dot(p, sel, preferred_element_type=F32)
    return d(hi) + d(mid) + d(lo)


def _tril(n):
    r = lax.broadcasted_iota(jnp.int32, (n, n), 0)
    c = lax.broadcasted_iota(jnp.int32, (n, n), 1)
    return r >= c


def _seg_matrix(rows, cols, seg, along_rows):
    r = lax.broadcasted_iota(jnp.int32, (rows, cols), 0)
    c = lax.broadcasted_iota(jnp.int32, (rows, cols), 1)
    m = (r // seg == c) if along_rows else (c // seg == r)
    return m.astype(BF16)


def _row_at(ref, b, cols=slice(None)):
    base = pl.multiple_of((b // 8) * 8, 8)
    tile = ref[pl.ds(base, 8), cols]
    sub = lax.broadcasted_iota(jnp.int32, tile.shape, 0)
    return jnp.sum(jnp.where(sub == b % 8, tile, 0.0), axis=0, keepdims=True)


def _silu(x):
    h = 0.5 * x
    return h + h * jnp.tanh(h)


def _sigmoid(x):
    return 0.5 + 0.5 * jnp.tanh(0.5 * x)


def _softplus(x):
    return jnp.maximum(x, 0.0) + jnp.log1p(jnp.exp(-jnp.abs(x)))


def _rms(x, w):
    return x * lax.rsqrt(jnp.mean(x * x, axis=-1, keepdims=True) + EPS) * w


def _norm_matmul_body(x_ref, nw_ref, w_ref, o_ref, *, chunks):
    xn = _rms(x_ref[...], nw_ref[...]).astype(BF16)
    for c0, cw in chunks:
        o_ref[:, c0:c0 + cw] = jnp.dot(xn, w_ref[:, c0:c0 + cw], preferred_element_type=F32)


def _norm_matmul(x, nw, w, *, tm, panels):
    m, d = x.shape
    n = w.shape[1]
    pn = n // panels
    return pl.pallas_call(
        functools.partial(_norm_matmul_body, chunks=_chunks(pn, 512)),
        grid=(panels, m // tm),
        in_specs=[pl.BlockSpec((tm, d), lambda p, i: (i, 0)),
                  pl.BlockSpec((1, d), lambda p, i: (0, 0)),
                  pl.BlockSpec((d, pn), lambda p, i: (0, p))],
        out_specs=pl.BlockSpec((tm, pn), lambda p, i: (i, p)),
        out_shape=jax.ShapeDtypeStruct((m, n), F32),
        compiler_params=_params("arbitrary", "arbitrary"),
        name="norm_matmul")(x, nw, w)


def _out_proj_body(*refs, n_in, final):
    h_ref = refs[0]
    xs = refs[1:1 + n_in]
    ws = refs[1 + n_in:1 + 2 * n_in]
    rest = refs[1 + 2 * n_in:]
    acc = h_ref[...]
    for x_ref, w_ref in zip(xs, ws):
        acc = acc + jnp.dot(x_ref[...].astype(BF16), w_ref[...], preferred_element_type=F32)
    if final:
        fw_ref, o_ref = rest
        o_ref[...] = _rms(acc, fw_ref[...])
    else:
        (o_ref,) = rest
        o_ref[...] = acc


def _out_proj(h, xs, ws, fw=None, *, tm):
    m, d = h.shape
    n_in = len(xs)
    in_specs = [pl.BlockSpec((tm, d), lambda i: (i, 0))]
    in_specs += [pl.BlockSpec((tm, x.shape[1]), lambda i: (i, 0)) for x in xs]
    in_specs += [pl.BlockSpec(w.shape, lambda i: (0, 0)) for w in ws]
    args = [h, *xs, *ws]
    if fw is not None:
        in_specs.append(pl.BlockSpec((1, d), lambda i: (0, 0)))
        args.append(fw)
    return pl.pallas_call(
        functools.partial(_out_proj_body, n_in=n_in, final=fw is not None),
        grid=(m // tm,),
        in_specs=in_specs,
        out_specs=pl.BlockSpec((tm, d), lambda i: (i, 0)),
        out_shape=jax.ShapeDtypeStruct((m, d), F32),
        compiler_params=_params("arbitrary"),
        name="out_proj")(*args)


def _rope_tables(pos, width):
    half = ROT_DIM // 2
    inv = jnp.power(F32(ROPE_THETA), -jnp.arange(half, dtype=F32) * (2.0 / ROT_DIM))
    ang = pos.astype(F32)[:, None] * inv[None, :]
    cos, sin = jnp.cos(ang), jnp.sin(ang)
    n = pos.shape[0]
    one = jnp.ones((n, ATT_HD - ROT_DIM), F32)
    z8 = jnp.zeros((n, half), F32)
    z48 = jnp.zeros((n, ATT_HD - ROT_DIM), F32)
    c = jnp.concatenate([cos, cos, one], axis=-1)
    sa = jnp.concatenate([-sin, z8, z48], axis=-1)
    sb = jnp.concatenate([z8, sin, z48], axis=-1)
    rep = width // ATT_HD
    return tuple(jnp.tile(t, (1, rep)) for t in (c, sa, sb))


def _rotary(x, c, sa, sb):
    w = x.shape[-1]
    half = ROT_DIM // 2
    return x * c + pltpu.roll(x, w - half, 1) * sa + pltpu.roll(x, half, 1) * sb


def _attn_prompt_body(q_ref, k_ref, v_ref, g_ref, c_ref, sa_ref, sb_ref,
                      att_ref, ko_ref, vo_ref,
                      nat, qd, kd, vd, od, std, o_s, st_s, *, seq):
    nblk = seq // CHUNK
    c, sa, sb = c_ref[...], sa_ref[...], sb_ref[...]
    q = _rotary(q_ref[...], c, sa, sb) * (ATT_HD ** -0.5)
    k = _rotary(k_ref[...], c, sa, sb)
    ko_ref[...] = k.T
    vo_ref[...] = v_ref[...].T
    head_lane = lax.broadcasted_iota(jnp.int32, (1, LANE), 1) // ATT_HD
    nat[0] = q
    nat[1] = k
    nat[2] = v_ref[...]
    zero = jnp.zeros((CHUNK, LANE), BF16)
    for p, d in enumerate(DILATIONS):
        kd[p, 0:CHUNK, :] = zero
        vd[p, 0:CHUNK, :] = zero
        ln = seq // d
        for r in range(d):
            rows = slice(r * ln, (r + 1) * ln)
            krows = slice(CHUNK + r * ln, CHUNK + (r + 1) * ln)
            src = pl.ds(r, ln, stride=d) if d > 1 else slice(None)
            qr = nat[0, src, :]
            qd[p, 0, rows, :] = jnp.where(head_lane == 0, qr, 0.0).astype(BF16)
            qd[p, 1, rows, :] = jnp.where(head_lane == 1, qr, 0.0).astype(BF16)
            kd[p, krows, :] = nat[1, src, :].astype(BF16)
            vd[p, krows, :] = nat[2, src, :].astype(BF16)

    row = lax.broadcasted_iota(jnp.int32, (CHUNK, 2 * CHUNK), 0)
    col = lax.broadcasted_iota(jnp.int32, (CHUNK, 2 * CHUNK), 1)
    band = (col >= row) & (col <= row + CHUNK)
    first_head = lax.broadcasted_iota(jnp.int32, (CHUNK, LANE), 1) < ATT_HD

    for p, d in enumerate(DILATIONS):
        nb = nblk // d

        def block(t, carry, p=p, nb=nb):
            base = pl.multiple_of(t * CHUNK, CHUNK)
            rows = pl.ds(base, CHUNK)
            first = (t % nb) == 0
            valid = band & (col >= jnp.where(first, CHUNK, 0))
            kw = kd[p, pl.ds(base, 2 * CHUNK), :]
            vw = vd[p, pl.ds(base, 2 * CHUNK), :]
            parts = []
            for hh in range(2):
                s = lax.dot_general(qd[p, hh, rows, :], kw, (((1,), (1,)), ((), ())),
                                    preferred_element_type=F32)
                s = jnp.where(valid, s, NEG_INF)
                m = jnp.max(s, axis=-1, keepdims=True)
                e = jnp.exp(s - m)
                o = jnp.dot(e.astype(BF16), vw, preferred_element_type=F32)
                parts.append((o, m, jnp.sum(e, axis=-1, keepdims=True)))
            (o0, m0, l0), (o1, m1, l1) = parts
            wide = lambda t: jnp.broadcast_to(t, (CHUNK, LANE))
            o_dst, st_dst = (o_s.at[p], st_s.at[p]) if nb == nblk else (od, std)
            o_dst[rows, :] = jnp.where(first_head, o0, o1)
            st_dst[0, rows, :] = jnp.where(first_head, wide(m0), wide(m1))
            st_dst[1, rows, :] = jnp.where(first_head, wide(l0), wide(l1))
            return carry

        lax.fori_loop(0, nblk, block, 0, unroll=True)
        if d > 1:
            ln = seq // d
            for r in range(d):
                o_s[p, pl.ds(r, ln, stride=d), :] = od[r * ln:(r + 1) * ln, :]
                st_s[p, 0, pl.ds(r, ln, stride=d), :] = std[0, r * ln:(r + 1) * ln, :]
                st_s[p, 1, pl.ds(r, ln, stride=d), :] = std[1, r * ln:(r + 1) * ln, :]

    np_ = len(DILATIONS)

    def combine(t, carry):
        base = pl.multiple_of(t * CHUNK, CHUNK)
        rows = pl.ds(base, CHUNK)
        ms = [st_s[p, 0, rows, :] for p in range(np_)]
        mx = functools.reduce(jnp.maximum, ms)
        ws = [jnp.exp(mm - mx) for mm in ms]
        num = sum(w * o_s[p, rows, :] for p, w in enumerate(ws))
        den = sum(w * st_s[p, 1, rows, :] for p, w in enumerate(ws))
        att_ref[rows, :] = ((num / den) * _silu(g_ref[rows, :])).astype(att_ref.dtype)
        return carry

    lax.fori_loop(0, nblk, combine, 0, unroll=2)


def _attn_prompt(u, cos, sa, sb, *, batch, seq):
    m = batch * seq
    npair = ATT_W // LANE
    blk = lambda off: pl.BlockSpec((seq, LANE), lambda b, hp, off=off: (b, off + hp))
    tab = pl.BlockSpec((seq, LANE), lambda b, hp: (0, 0))
    out = pl.BlockSpec((seq, LANE), lambda b, hp: (b, hp))
    out_t = pl.BlockSpec((None, LANE, seq), lambda b, hp: (b, hp, 0))
    np_ = len(DILATIONS)
    return pl.pallas_call(
        functools.partial(_attn_prompt_body, seq=seq),
        grid=(batch, npair),
        in_specs=[blk(0), blk(npair), blk(2 * npair), blk(3 * npair), tab, tab, tab],
        out_specs=[out, out_t, out_t],
        out_shape=[jax.ShapeDtypeStruct((m, ATT_W), BF16),
                   jax.ShapeDtypeStruct((batch, ATT_W, seq), F32),
                   jax.ShapeDtypeStruct((batch, ATT_W, seq), F32)],
        scratch_shapes=[pltpu.VMEM((3, seq, LANE), F32),
                        pltpu.VMEM((np_, 2, seq, LANE), BF16),
                        pltpu.VMEM((np_, seq + CHUNK, LANE), BF16),
                        pltpu.VMEM((np_, seq + CHUNK, LANE), BF16),
                        pltpu.VMEM((seq, LANE), F32),
                        pltpu.VMEM((2, seq, LANE), F32),
                        pltpu.VMEM((np_, seq, LANE), F32),
                        pltpu.VMEM((np_, 2, seq, LANE), F32)],
        compiler_params=_params("arbitrary", "arbitrary"),
        name="attn_prompt")(u, u, u, u, cos, sa, sb)


def _ssd_prompt_body(xbc_ref, z_ref, dt_ref, cw_ref, cb_ref, dtb_ref, alog_ref, dsk_ref, nw_ref,
                     y_ref, conv_ref, st_ref, ext, st, ys, *, heads, nchunk):
    c = pl.program_id(1)
    inner = heads * SSD_HEADDIM
    gw = SSD_STATE
    hpg = heads // SSD_GROUPS

    @pl.when(c == 0)
    def _():
        ext[0:8, :] = jnp.zeros((8, ext.shape[1]), F32)
        st[...] = jnp.zeros(st.shape, F32)

    ext[8:8 + CHUNK, :] = xbc_ref[...]
    conv = cb_ref[...] + ext[5:5 + CHUNK, :] * cw_ref[0:1, :]
    for j in range(1, CONV_K):
        conv = conv + ext[5 + j:5 + j + CHUNK, :] * cw_ref[j:j + 1, :]
    act = _silu(conv)

    @pl.when(c == nchunk - 1)
    def _():
        conv_ref[...] = ext[CHUNK + 5:CHUNK + 8, :]

    ext[0:8, :] = ext[CHUNK:CHUNK + 8, :]

    dt = _softplus(dt_ref[...] + dtb_ref[...])
    a = -jnp.exp(alog_ref[...])
    tril = _tril(CHUNK)
    acum = _sel_dot(jnp.where(tril, 1.0, 0.0).astype(BF16), dt * a)
    acum_t = acum.T
    last = acum[CHUNK - 1:CHUNK, :]
    elast = jnp.exp(last)
    expand = _seg_matrix(LANE, inner, SSD_HEADDIM, False)
    dt_e = _dot_sel(dt, expand)
    eacum_e = _dot_sel(jnp.exp(acum), expand)
    wend_e = _dot_sel(jnp.exp(last - acum) * dt, expand)
    xs = act[:, 0:inner]
    xdt = xs * dt_e
    xw = xs * wend_e
    hw = hpg * SSD_HEADDIM
    first_half = lax.broadcasted_iota(jnp.int32, (CHUNK, LANE), 1) < SSD_HEADDIM

    for g in range(SSD_GROUPS):
        bm = act[:, inner + g * gw:inner + (g + 1) * gw]
        cm = act[:, inner + SSD_GROUPS * gw + g * gw:inner + SSD_GROUPS * gw + (g + 1) * gw]
        cb = _dot_nt(cm, bm)
        sg = st[g * hw:(g + 1) * hw, :]
        y_inter = _dot_nt(cm, sg) * eacum_e[:, g * hw:(g + 1) * hw]
        s_local = _dot_tn(xw[:, g * hw:(g + 1) * hw], bm)
        for hg in range(hpg):
            h = g * hpg + hg
            rows = slice(hg * SSD_HEADDIM, (hg + 1) * SSD_HEADDIM)
            st[g * hw + hg * SSD_HEADDIM:g * hw + (hg + 1) * SSD_HEADDIM, :] = (
                sg[rows, :] * elast[:, h:h + 1] + s_local[rows, :])
        for pr in range(hpg // 2):
            lo = g * hw + pr * LANE
            xp = xdt[:, lo:lo + LANE].astype(BF16)
            halves = []
            for hh in range(2):
                h = g * hpg + 2 * pr + hh
                seg = acum[:, h:h + 1] - acum_t[h:h + 1, :]
                mh = cb * jnp.exp(jnp.where(tril, seg, NEG_INF))
                halves.append(jnp.dot(mh.astype(BF16), xp, preferred_element_type=F32))
            ys[:, lo:lo + LANE] = (jnp.where(first_half, halves[0], halves[1])
                                   + y_inter[:, pr * LANE:(pr + 1) * LANE])

    yt = (ys[...] + dsk_ref[...] * xs) * _silu(z_ref[...])
    y_ref[...] = _rms(yt, nw_ref[...]).astype(y_ref.dtype)

    @pl.when(c == nchunk - 1)
    def _():
        st_ref[...] = st[...]


def _ssd_prompt(u, cw, cb, dtb, alog, dsk, nw, *, batch, seq, heads, col_z, col_xbc, col_dt):
    nchunk = seq // CHUNK
    inner = heads * SSD_HEADDIM
    cch = cw.shape[1]
    row = lambda b, c: b * nchunk + c
    full = lambda a: pl.BlockSpec(a.shape, lambda b, c: (0, 0))
    return pl.pallas_call(
        functools.partial(_ssd_prompt_body, heads=heads, nchunk=nchunk),
        grid=(batch, nchunk),
        in_specs=[pl.BlockSpec((CHUNK, cch), lambda b, c: (row(b, c), col_xbc // cch)),
                  pl.BlockSpec((CHUNK, inner), lambda b, c: (row(b, c), col_z // inner)),
                  pl.BlockSpec((CHUNK, LANE), lambda b, c: (row(b, c), col_dt // LANE)),
                  full(cw), full(cb), full(dtb), full(alog), full(dsk), full(nw)],
        out_specs=[pl.BlockSpec((CHUNK, inner), lambda b, c: (row(b, c), 0)),
                   pl.BlockSpec((None, CONV_K - 1, cch), lambda b, c: (b, 0, 0)),
                   pl.BlockSpec((None, inner, SSD_STATE), lambda b, c: (b, 0, 0))],
        out_shape=[jax.ShapeDtypeStruct((batch * seq, inner), BF16),
                   jax.ShapeDtypeStruct((batch, CONV_K - 1, cch), F32),
                   jax.ShapeDtypeStruct((batch, inner, SSD_STATE), F32)],
        scratch_shapes=[pltpu.VMEM((CHUNK + 8, cch), F32),
                        pltpu.VMEM((inner, SSD_STATE), F32),
                        pltpu.VMEM((CHUNK, inner), F32)],
        compiler_params=_params("arbitrary", "arbitrary"),
        name="ssd_prompt")(u, u, u, cw, cb, dtb, alog, dsk, nw)


def _mlstm_prompt_body(q_ref, k_ref, v_ref, o_ref, z_ref, gi_ref, gf_ref,
                       cw_ref, cb_ref, bi_ref, bf_ref, nw_ref,
                       hz_ref, conv_ref, c_out, n_out, m_out,
                       histq, histk, c_s, n_s, m_s, gate_s, *, nchunk, heads, hd):
    c = pl.program_id(1)
    inner = heads * hd

    @pl.when(c == 0)
    def _():
        histq[...] = jnp.zeros((8, inner), F32)
        histk[...] = jnp.zeros((8, inner), F32)
        c_s[...] = jnp.zeros(c_s.shape, F32)
        n_s[...] = jnp.zeros(n_s.shape, F32)
        m_s[...] = jnp.full(m_s.shape, NEG_INF, F32)

    def conv_head(hist, x_ref, col0, h):
        sl = slice(h * hd, (h + 1) * hd)
        cols = slice(col0 + h * hd, col0 + (h + 1) * hd)
        xp = jnp.concatenate([hist[:, sl], x_ref[:, sl]], axis=0)
        acc = cb_ref[:, cols] + xp[5:5 + CHUNK, :] * cw_ref[0:1, cols]
        for j in range(1, CONV_K):
            acc = acc + xp[5 + j:5 + j + CHUNK, :] * cw_ref[j:j + 1, cols]
        return _silu(acc)

    @pl.when(c == nchunk - 1)
    def _():
        conv_ref[:, 0:inner] = q_ref[CHUNK - (CONV_K - 1):CHUNK, :]
        conv_ref[:, inner:] = k_ref[CHUNK - (CONV_K - 1):CHUNK, :]

    it = gi_ref[...] + bi_ref[...]
    logf = -_softplus(-(gf_ref[...] + bf_ref[...]))
    tril = _tril(CHUNK)
    bc = _sel_dot(jnp.where(tril, 1.0, 0.0).astype(BF16), logf)
    gate_s[0] = it
    gate_s[1] = bc
    gate_s[2] = it.T
    gate_s[3] = bc.T

    for h in range(heads):
        sl = slice(h * hd, (h + 1) * hd)
        q = conv_head(histq, q_ref, 0, h)
        k = conv_head(histk, k_ref, inner, h) * (hd ** -0.5)
        v = v_ref[:, sl]
        i_col, b_col = gate_s[0, :, h:h + 1], gate_s[1, :, h:h + 1]
        i_row, b_row = gate_s[2, h:h + 1, :], gate_s[3, h:h + 1, :]
        m_prev = m_s[h]
        dmat = jnp.where(tril, b_col - b_row + i_row, NEG_INF)
        inter = b_col + m_prev
        m_t = jnp.maximum(inter, jnp.max(dmat, axis=-1, keepdims=True))
        w_intra = jnp.exp(dmat - m_t)
        w_inter = jnp.exp(inter - m_t)
        att = w_intra * _dot_nt(q, k)
        c_prev = c_s[h]
        n_prev = n_s[h]
        num = _dot(att, v) + w_inter * _dot(q, c_prev)
        qn = _dot_nt(q, jnp.broadcast_to(n_prev, (8, hd)))[:, 0:1]
        den = jnp.sum(att, axis=-1, keepdims=True) + w_inter * qn
        hh = num / jnp.maximum(jnp.abs(den), jnp.exp(-m_t))

        b_last = b_col[CHUNK - 1:CHUNK, :]
        logw = b_last - b_col + i_col
        m_new = jnp.maximum(b_last + m_prev, jnp.max(logw, axis=0, keepdims=True))
        ws = jnp.exp(logw - m_new)
        ws_row = jnp.exp(b_last - b_row + i_row - m_new)
        scale = jnp.exp(b_last + m_prev - m_new)
        c_s[h] = scale * c_prev + _dot_tn(k, ws * v)
        n_s[h] = scale * n_prev + _dot(jnp.broadcast_to(ws_row, (8, CHUNK)), k)[0:1, :]
        m_s[h] = m_new

        hg = _sigmoid(o_ref[:, sl]) * hh
        mu = jnp.mean(hg, axis=-1, keepdims=True)
        var = jnp.mean(jnp.square(hg - mu), axis=-1, keepdims=True)
        hn = (hg - mu) * lax.rsqrt(var + EPS) * nw_ref[:, sl]
        hz_ref[:, sl] = (hn * _silu(z_ref[:, sl])).astype(hz_ref.dtype)

    histq[...] = q_ref[CHUNK - 8:CHUNK, :]
    histk[...] = k_ref[CHUNK - 8:CHUNK, :]

    @pl.when(c == nchunk - 1)
    def _():
        c_out[...] = c_s[...]
        n_out[...] = n_s[...]
        m_out[...] = m_s[...]


def _mlstm_prompt(u, cw, cb, bi, bf, nw, *, batch, seq, heads, hd, col_gates):
    nchunk = seq // CHUNK
    inner = heads * hd
    row = lambda b, c: b * nchunk + c
    ublk = lambda j: pl.BlockSpec((CHUNK, inner), lambda b, c, j=j: (row(b, c), j))
    gblk = lambda off: pl.BlockSpec((CHUNK, LANE), lambda b, c, off=off: (row(b, c), col_gates // LANE + off))
    full = lambda a: pl.BlockSpec(a.shape, lambda b, c: (0, 0))
    return pl.pallas_call(
        functools.partial(_mlstm_prompt_body, nchunk=nchunk, heads=heads, hd=hd),
        grid=(batch, nchunk),
        in_specs=[ublk(0), ublk(1), ublk(2), ublk(3), ublk(4), gblk(0), gblk(1),
                  full(cw), full(cb), full(bi), full(bf), full(nw)],
        out_specs=[pl.BlockSpec((CHUNK, inner), lambda b, c: (row(b, c), 0)),
                   pl.BlockSpec((None, CONV_K - 1, 2 * inner), lambda b, c: (b, 0, 0)),
                   pl.BlockSpec((None, heads, hd, hd), lambda b, c: (b, 0, 0, 0)),
                   pl.BlockSpec((None, heads, 1, hd), lambda b, c: (b, 0, 0, 0)),
                   pl.BlockSpec((None, heads, 1, 1), lambda b, c: (b, 0, 0, 0))],
        out_shape=[jax.ShapeDtypeStruct((batch * seq, inner), BF16),
                   jax.ShapeDtypeStruct((batch, CONV_K - 1, 2 * inner), F32),
                   jax.ShapeDtypeStruct((batch, heads, hd, hd), F32),
                   jax.ShapeDtypeStruct((batch, heads, 1, hd), F32),
                   jax.ShapeDtypeStruct((batch, heads, 1, 1), F32)],
        scratch_shapes=[pltpu.VMEM((8, inner), F32),
                        pltpu.VMEM((8, inner), F32),
                        pltpu.VMEM((heads, hd, hd), F32),
                        pltpu.VMEM((heads, 1, hd), F32),
                        pltpu.VMEM((heads, 1, 1), F32),
                        pltpu.VMEM((4, CHUNK, LANE), F32)],
        compiler_params=_params("arbitrary", "arbitrary"),
        name="mlstm_prompt")(u, u, u, u, u, u, u, cw, cb, bi, bf, nw)


def _rot_sample_body(q_ref, k_ref, v_ref, g_ref, c_ref, sa_ref, sb_ref, xt_ref):
    c, sa, sb = c_ref[...], sa_ref[...], sb_ref[...]
    q = _rotary(q_ref[...], c, sa, sb) * (ATT_HD ** -0.5)
    k = _rotary(k_ref[...], c, sa, sb)
    xt_ref[...] = jnp.concatenate([q, k, v_ref[...], g_ref[...]], axis=-1).T


def _rot_sample(u, cos, sa, sb):
    bs = u.shape[0]
    ublk = lambda off: pl.BlockSpec((bs, ATT_W), lambda i, off=off: (0, off))
    tab = pl.BlockSpec((1, ATT_W), lambda i: (0, 0))
    return pl.pallas_call(
        _rot_sample_body,
        grid=(1,),
        in_specs=[ublk(0), ublk(1), ublk(2), ublk(3), tab, tab, tab],
        out_specs=pl.BlockSpec((4 * ATT_W, bs), lambda i: (0, 0)),
        out_shape=jax.ShapeDtypeStruct((4 * ATT_W, bs), F32),
        compiler_params=_params("arbitrary"),
        name="rot_sample")(u, u, u, u, cos, sa, sb)


def _key_multiplicity(wb):
    back = wb - lax.broadcasted_iota(jnp.int32, (1, wb), 1)
    cnt = jnp.zeros((1, wb), F32)
    for d in DILATIONS:
        cnt = cnt + ((back <= CHUNK * d) & (lax.rem(back, d) == 0)).astype(F32)
    return cnt


def _attn_sample_body(x_ref, kc_ref, vc_ref, att_ref, *, bb, wb):
    i = pl.program_id(0)
    bs = x_ref.shape[1]
    nt = wb // LANE

    @pl.when(i == 0)
    def _():
        att_ref[...] = jnp.zeros(att_ref.shape, F32)

    cnt = _key_multiplicity(wb)
    valid = cnt > 0.0
    npat = float(len(DILATIONS))
    rid = lax.broadcasted_iota(jnp.int32, (bs, LANE), 0)
    lid = lax.broadcasted_iota(jnp.int32, (ATT_HD, bs), 1)
    for t in range(bb):
        b = i * bb + t
        cols = _dot_sel(x_ref[...], (rid == b).astype(BF16))
        for h in range(ATT_HEADS):
            lo = h * ATT_HD
            qc = cols[lo:lo + ATT_HD]
            kc = cols[ATT_W + lo:ATT_W + lo + ATT_HD]
            vc = cols[2 * ATT_W + lo:2 * ATT_W + lo + ATT_HD]
            gc = cols[3 * ATT_W + lo:3 * ATT_W + lo + ATT_HD]
            kt = kc_ref[t, h]
            vt = vc_ref[t, h]
            s = jnp.sum(kt * jnp.tile(qc, (1, nt)), axis=0, keepdims=True)
            s = jnp.where(valid, s, NEG_INF)
            s_self = jnp.sum(qc * kc, axis=0, keepdims=True)[:, 0:1]
            m = jnp.maximum(jnp.max(s, axis=1, keepdims=True), s_self)
            p = cnt * jnp.exp(s - m)
            p_self = npat * jnp.exp(s_self - m)
            den = jnp.sum(p, axis=1, keepdims=True) + p_self
            acc = vt[:, 0:LANE] * p[:, 0:LANE]
            for j in range(1, nt):
                acc = acc + vt[:, j * LANE:(j + 1) * LANE] * p[:, j * LANE:(j + 1) * LANE]
            o = jnp.sum(acc, axis=1, keepdims=True) + p_self * vc[:, 0:1]
            o = o / den * _silu(gc[:, 0:1])
            att_ref[lo:lo + ATT_HD, :] = jnp.where(lid == b, o, att_ref[lo:lo + ATT_HD, :])


def _attn_sample(xt, ck, cv, *, bb):
    bs, wb = ck.shape[0], ck.shape[3]
    cache = pl.BlockSpec((bb, ATT_HEADS, ATT_HD, wb), lambda i: (i, 0, 0, 0))
    return pl.pallas_call(
        functools.partial(_attn_sample_body, bb=bb, wb=wb),
        grid=(bs // bb,),
        in_specs=[pl.BlockSpec(xt.shape, lambda i: (0, 0)), cache, cache],
        out_specs=pl.BlockSpec((ATT_W, bs), lambda i: (0, 0)),
        out_shape=jax.ShapeDtypeStruct((ATT_W, bs), F32),
        compiler_params=_params("arbitrary"),
        name="attn_sample")(xt, ck, cv)


def _ssd_sample_prep_body(xbc_ref, dt_ref, cst_ref, cw_ref, cb_ref, dtb_ref, alog_ref, aloge_ref, dsk_ref,
                          conv_ref, yloc_ref, xdtt_ref, bc_ref, da_ref, dae_ref, *, heads):
    inner = heads * SSD_HEADDIM
    cch = xbc_ref.shape[1]
    gw = SSD_STATE
    x = xbc_ref[...]
    acc = cb_ref[...] + x * cw_ref[CONV_K - 1:CONV_K, :]
    for j in range(CONV_K - 1):
        acc = acc + cst_ref[:, j * cch:(j + 1) * cch] * cw_ref[j:j + 1, :]
    for j in range(CONV_K - 2):
        conv_ref[:, j * cch:(j + 1) * cch] = cst_ref[:, (j + 1) * cch:(j + 2) * cch]
    conv_ref[:, (CONV_K - 2) * cch:(CONV_K - 1) * cch] = x
    act = _silu(acc)
    xs = act[:, 0:inner]
    bc_ref[...] = act[:, inner:]
    dt = _softplus(dt_ref[...] + dtb_ref[...])
    da_ref[...] = jnp.exp(dt * (-jnp.exp(alog_ref[...])))
    expand = _seg_matrix(LANE, inner, SSD_HEADDIM, False)
    dte = _dot_sel(dt, expand)
    dae_ref[...] = jnp.exp(dte * (-jnp.exp(aloge_ref[...])))
    xdt = xs * dte
    xdtt_ref[...] = xdt.T
    hw = inner // SSD_GROUPS
    parts = []
    for g in range(SSD_GROUPS):
        bm = act[:, inner + g * gw:inner + (g + 1) * gw]
        cm = act[:, inner + SSD_GROUPS * gw + g * gw:inner + SSD_GROUPS * gw + (g + 1) * gw]
        cbg = jnp.sum(cm * bm, axis=-1, keepdims=True)
        parts.append(cbg * xdt[:, g * hw:(g + 1) * hw])
    yloc_ref[...] = jnp.concatenate(parts, axis=-1) + dsk_ref[...] * xs


def _ssd_sample_state_body(da_ref, s_ref, xdtt_ref, bc_ref, so_ref, yi_ref, *, bb, heads):
    i = pl.program_id(0)
    bs = bc_ref.shape[0]
    inner = heads * SSD_HEADDIM
    hw = inner // SSD_GROUPS
    hpg = heads // SSD_GROUPS
    gw = SSD_STATE
    rid = lax.broadcasted_iota(jnp.int32, (bs, gw), 0)
    for t in range(bb):
        b = i * bb + t
        parts = []
        for g in range(SSD_GROUPS):
            mg = jnp.where(rid == b, bc_ref[:, g * gw:(g + 1) * gw], 0.0)
            sl = _dot(xdtt_ref[g * hw:(g + 1) * hw, :], mg)
            crow = _row_at(bc_ref, b, slice(SSD_GROUPS * gw + g * gw, SSD_GROUPS * gw + (g + 1) * gw))
            sg = s_ref[t, g * hw:(g + 1) * hw, :]
            parts.append(_dot_nt(jnp.broadcast_to(crow, (8, gw)), sg)[0:1, :])
            for hg in range(hpg):
                h = g * hpg + hg
                lo = hg * SSD_HEADDIM
                so_ref[t, h * SSD_HEADDIM:(h + 1) * SSD_HEADDIM, :] = (
                    sg[lo:lo + SSD_HEADDIM, :] * da_ref[b, h] + sl[lo:lo + SSD_HEADDIM, :])
        yi_ref[t] = jnp.concatenate(parts, axis=-1)


def _ssd_sample_finish_body(yloc_ref, yi_ref, dae_ref, z_ref, nw_ref, y_ref):
    y = (yloc_ref[...] + yi_ref[...] * dae_ref[...]) * _silu(z_ref[...])
    y_ref[...] = _rms(y, nw_ref[...])


def _ssd_sample(u, cst, state, cw, cb, dtb, alog, aloge, dsk, nw, *, heads, col_z, col_xbc, col_dt, bb):
    bs = u.shape[0]
    inner = heads * SSD_HEADDIM
    cch = cw.shape[1]
    ncs = (CONV_K - 1) * cch
    bcw = 2 * SSD_GROUPS * SSD_STATE
    full = lambda a: pl.BlockSpec(a.shape, lambda i: (0,) * a.ndim)
    conv, yloc, xdtt, bc, da, dae = pl.pallas_call(
        functools.partial(_ssd_sample_prep_body, heads=heads),
        grid=(1,),
        in_specs=[pl.BlockSpec((bs, cch), lambda i: (0, col_xbc // cch)),
                  pl.BlockSpec((bs, LANE), lambda i: (0, col_dt // LANE)),
                  full(cst), full(cw), full(cb), full(dtb), full(alog), full(aloge), full(dsk)],
        out_specs=[pl.BlockSpec((bs, ncs), lambda i: (0, 0)),
                   pl.BlockSpec((bs, inner), lambda i: (0, 0)),
                   pl.BlockSpec((inner, bs), lambda i: (0, 0)),
                   pl.BlockSpec((bs, bcw), lambda i: (0, 0)),
                   pl.BlockSpec((bs, LANE), lambda i: (0, 0)),
                   pl.BlockSpec((bs, inner), lambda i: (0, 0))],
        out_shape=[jax.ShapeDtypeStruct((bs, ncs), F32),
                   jax.ShapeDtypeStruct((bs, inner), F32),
                   jax.ShapeDtypeStruct((inner, bs), F32),
                   jax.ShapeDtypeStruct((bs, bcw), F32),
                   jax.ShapeDtypeStruct((bs, LANE), F32),
                   jax.ShapeDtypeStruct((bs, inner), F32)],
        compiler_params=_params("arbitrary"),
        name="ssd_sample_prep")(u, u, cst, cw, cb, dtb, alog, aloge, dsk)
    new_state, yi = pl.pallas_call(
        functools.partial(_ssd_sample_state_body, bb=bb, heads=heads),
        grid=(bs // bb,),
        in_specs=[pl.BlockSpec(memory_space=pltpu.SMEM),
                  pl.BlockSpec((bb, inner, SSD_STATE), lambda i: (i, 0, 0)),
                  full(xdtt), full(bc)],
        out_specs=[pl.BlockSpec((bb, inner, SSD_STATE), lambda i: (i, 0, 0)),
                   pl.BlockSpec((bb, 1, inner), lambda i: (i, 0, 0))],
        out_shape=[jax.ShapeDtypeStruct((bs, inner, SSD_STATE), F32),
                   jax.ShapeDtypeStruct((bs, 1, inner), F32)],
        compiler_params=_params("arbitrary"),
        name="ssd_sample_state")(da[:, :heads], state, xdtt, bc)
    y = pl.pallas_call(
        _ssd_sample_finish_body,
        grid=(1,),
        in_specs=[full(yloc), pl.BlockSpec((bs, inner), lambda i: (0, 0)), full(dae),
                  pl.BlockSpec((bs, inner), lambda i: (0, col_z // inner)), full(nw)],
        out_specs=pl.BlockSpec((bs, inner), lambda i: (0, 0)),
        out_shape=jax.ShapeDtypeStruct((bs, inner), F32),
        compiler_params=_params("arbitrary"),
        name="ssd_sample_finish")(yloc, yi.reshape(bs, inner), dae, u, nw)
    return y, conv, new_state


def _mlstm_sample_prep_body(q_ref, k_ref, gi_ref, gf_ref, cst_ref, cw_ref, cb_ref, bi_ref, bf_ref, m_ref,
                            conv_ref, qk_ref, kwt_ref, wi_ref, wf_ref, mt_ref, *, heads, hd):
    inner = heads * hd
    cch = 2 * inner
    x = jnp.concatenate([q_ref[...], k_ref[...]], axis=-1)
    acc = cb_ref[...] + x * cw_ref[CONV_K - 1:CONV_K, :]
    for j in range(CONV_K - 1):
        acc = acc + cst_ref[:, j * cch:(j + 1) * cch] * cw_ref[j:j + 1, :]
    for j in range(CONV_K - 2):
        conv_ref[:, j * cch:(j + 1) * cch] = cst_ref[:, (j + 1) * cch:(j + 2) * cch]
    conv_ref[:, (CONV_K - 2) * cch:(CONV_K - 1) * cch] = x
    act = _silu(acc)
    q = act[:, 0:inner]
    k = act[:, inner:] * (hd ** -0.5)
    it = gi_ref[...] + bi_ref[...]
    inter = -_softplus(-(gf_ref[...] + bf_ref[...])) + m_ref[...]
    mt = jnp.maximum(inter, it)
    wi = jnp.exp(it - mt)
    wf = jnp.exp(inter - mt)
    wi_ref[...] = wi
    wf_ref[...] = wf
    mt_ref[...] = mt
    qk_ref[:, 0:inner] = q
    qk_ref[:, inner:] = k
    kw = jnp.concatenate([k[:, h * hd:(h + 1) * hd] * wi[:, h:h + 1] for h in range(heads)], axis=-1)
    kwt_ref[...] = kw.T


def _mlstm_sample_state_body(wf_ref, c_ref, qk_ref, v_ref, kwt_ref, co_ref, qc_ref, *, bb, heads, hd):
    i = pl.program_id(0)
    bs = v_ref.shape[0]
    rid = lax.broadcasted_iota(jnp.int32, (bs, hd), 0)
    for t in range(bb):
        b = i * bb + t
        parts = []
        for h in range(heads):
            cp = c_ref[t, h]
            qrow = _row_at(qk_ref, b, slice(h * hd, (h + 1) * hd))
            parts.append(_dot(jnp.broadcast_to(qrow, (8, hd)), cp)[0:1, :])
            mh = jnp.where(rid == b, v_ref[:, h * hd:(h + 1) * hd], 0.0)
            co_ref[t, h] = cp * wf_ref[b, h] + _dot(kwt_ref[h * hd:(h + 1) * hd, :], mh)
        qc_ref[t] = jnp.concatenate(parts, axis=-1)


def _mlstm_sample_finish_body(qk_ref, v_ref, o_ref, z_ref, qc_ref, n_ref, wi_ref, wf_ref, mt_ref, nw_ref,
                              hz_ref, no_ref, *, heads, hd):
    inner = heads * hd
    for h in range(heads):
        sl = slice(h * hd, (h + 1) * hd)
        q = qk_ref[:, sl]
        k = qk_ref[:, inner + h * hd:inner + (h + 1) * hd]
        wi = wi_ref[:, h:h + 1]
        wf = wf_ref[:, h:h + 1]
        mt = mt_ref[:, h:h + 1]
        n_prev = n_ref[:, sl]
        att = wi * jnp.sum(q * k, axis=-1, keepdims=True)
        num = att * v_ref[:, sl] + wf * qc_ref[:, sl]
        den = att + wf * jnp.sum(q * n_prev, axis=-1, keepdims=True)
        hh = num / jnp.maximum(jnp.abs(den), jnp.exp(-mt))
        hg = _sigmoid(o_ref[:, sl]) * hh
        mu = jnp.mean(hg, axis=-1, keepdims=True)
        var = jnp.mean(jnp.square(hg - mu), axis=-1, keepdims=True)
        hn = (hg - mu) * lax.rsqrt(var + EPS) * nw_ref[:, sl]
        hz_ref[:, sl] = hn * _silu(z_ref[:, sl])
        no_ref[:, sl] = wf * n_prev + wi * k


def _mlstm_sample(u, cst, c0, n0, m0p, cw, cb, bi, bf, nw, *, heads, hd, col_gates, bb):
    bs = u.shape[0]
    inner = heads * hd
    ncs = (CONV_K - 1) * 2 * inner
    full = lambda a: pl.BlockSpec(a.shape, lambda i: (0,) * a.ndim)
    ucol = lambda j: pl.BlockSpec((bs, inner), lambda i, j=j: (0, j))
    gcol = lambda j: pl.BlockSpec((bs, LANE), lambda i, j=j: (0, col_gates // LANE + j))
    tile = jax.ShapeDtypeStruct((bs, LANE), F32)
    conv, qk, kwt, wi, wf, mt = pl.pallas_call(
        functools.partial(_mlstm_sample_prep_body, heads=heads, hd=hd),
        grid=(1,),
        in_specs=[ucol(0), ucol(1), gcol(0), gcol(1), full(cst), full(cw), full(cb), full(bi), full(bf),
                  full(m0p)],
        out_specs=[pl.BlockSpec((bs, ncs), lambda i: (0, 0)),
                   pl.BlockSpec((bs, 2 * inner), lambda i: (0, 0)),
                   pl.BlockSpec((inner, bs), lambda i: (0, 0)),
                   pl.BlockSpec((bs, LANE), lambda i: (0, 0)),
                   pl.BlockSpec((bs, LANE), lambda i: (0, 0)),
                   pl.BlockSpec((bs, LANE), lambda i: (0, 0))],
        out_shape=[jax.ShapeDtypeStruct((bs, ncs), F32),
                   jax.ShapeDtypeStruct((bs, 2 * inner), F32),
                   jax.ShapeDtypeStruct((inner, bs), F32),
                   tile, tile, tile],
        compiler_params=_params("arbitrary"),
        name="mlstm_sample_prep")(u, u, u, u, cst, cw, cb, bi, bf, m0p)
    c_new, qc = pl.pallas_call(
        functools.partial(_mlstm_sample_state_body, bb=bb, heads=heads, hd=hd),
        grid=(bs // bb,),
        in_specs=[pl.BlockSpec(memory_space=pltpu.SMEM),
                  pl.BlockSpec((bb, heads, hd, hd), lambda i: (i, 0, 0, 0)),
                  full(qk), pl.BlockSpec((bs, inner), lambda i: (0, 2)), full(kwt)],
        out_specs=[pl.BlockSpec((bb, heads, hd, hd), lambda i: (i, 0, 0, 0)),
                   pl.BlockSpec((bb, 1, inner), lambda i: (i, 0, 0))],
        out_shape=[jax.ShapeDtypeStruct((bs, heads, hd, hd), F32),
                   jax.ShapeDtypeStruct((bs, 1, inner), F32)],
        compiler_params=_params("arbitrary"),
        name="mlstm_sample_state")(wf[:, :heads], c0, qk, u, kwt)
    hz, n_new = pl.pallas_call(
        functools.partial(_mlstm_sample_finish_body, heads=heads, hd=hd),
        grid=(1,),
        in_specs=[full(qk), ucol(2), ucol(3), ucol(4), pl.BlockSpec((bs, inner), lambda i: (0, 0)),
                  full(n0), full(wi), full(wf), full(mt), full(nw)],
        out_specs=[pl.BlockSpec((bs, inner), lambda i: (0, 0)),
                   pl.BlockSpec((bs, inner), lambda i: (0, 0))],
        out_shape=[jax.ShapeDtypeStruct((bs, inner), F32)] * 2,
        compiler_params=_params("arbitrary"),
        name="mlstm_sample_finish")(qk, u, u, u, qc.reshape(bs, inner), n0, wi, wf, mt, nw)
    return hz, conv, c_new, n_new, mt[:, :heads]


def _pad_cols(w, n):
    return jnp.pad(w, ((0, 0), (0, n - w.shape[1])))


def _row(v, n=None):
    v = v.reshape(1, -1)
    return v if n is None else _pad_cols(v, n)


def kernel(x_prompt, x_sample, cache_attn_k, cache_attn_v, state_ssd_conv, state_ssd, state_mlstm_conv, state_mlstm_c, state_mlstm_n, state_mlstm_m, norm_w, final_norm_w, w_in_even, w_out_even, ssd_conv_w, ssd_conv_b, ssd_dt_bias, ssd_a_log, ssd_d, ssd_norm_w, w_in_odd, w_out_odd, mlstm_conv_w, mlstm_conv_b, mlstm_igate_b, mlstm_fgate_b, mlstm_norm_w):
    batch, seq, d_model = x_prompt.shape
    bs = x_sample.shape[0]
    ssd_heads = ssd_a_log.shape[1]
    ssd_inner = ssd_heads * SSD_HEADDIM
    ssd_cch = ssd_conv_w.shape[2]
    m_inner = mlstm_norm_w.shape[1]
    m_hd = m_inner // MLSTM_HEADS
    mp = batch * seq

    col_z = 4 * ATT_W
    col_xbc = col_z + ssd_inner
    col_dt = col_xbc + ssd_cch
    n_even = col_dt + LANE
    w_in0 = _pad_cols(w_in_even[0], n_even).astype(BF16)
    w_out0 = w_out_even[0].astype(BF16)
    nw0 = _row(norm_w[0])
    cw0, cb0 = ssd_conv_w[0], _row(ssd_conv_b[0])
    dtb = _row(ssd_dt_bias[0], LANE)
    alog = _row(ssd_a_log[0], LANE)
    aloge = _row(jnp.repeat(ssd_a_log[0], SSD_HEADDIM))
    dsk = _row(jnp.repeat(ssd_d[0], SSD_HEADDIM))
    snw = _row(ssd_norm_w[0])

    hp = x_prompt.reshape(mp, d_model)
    hs = x_sample.reshape(bs, d_model)

    up = _norm_matmul(hp, nw0, w_in0, tm=512, panels=1)
    cos_p, sa_p, sb_p = _rope_tables(jnp.arange(seq), LANE)
    att_p, k_p, v_p = _attn_prompt(up, cos_p, sa_p, sb_p, batch=batch, seq=seq)
    y_p, conv_p, st_p = _ssd_prompt(up, cw0, cb0, dtb, alog, dsk, snw, batch=batch, seq=seq,
                                    heads=ssd_heads, col_z=col_z, col_xbc=col_xbc, col_dt=col_dt)
    hp = _out_proj(hp, [att_p, y_p], [w_out0[:ATT_W], w_out0[ATT_W:]], tm=512)

    us = _norm_matmul(hs, nw0, w_in0, tm=bs, panels=1)
    cos_s, sa_s, sb_s = _rope_tables(PAST_LEN + jnp.arange(1), ATT_W)
    xt_s = _rot_sample(us, cos_s, sa_s, sb_s)
    k_s = xt_s[ATT_W:2 * ATT_W].T
    v_s = xt_s[2 * ATT_W:3 * ATT_W].T
    pos_minor = lambda cache: jnp.transpose(cache, (0, 2, 3, 1))
    att_s = _attn_sample(xt_s, pos_minor(cache_attn_k[0]), pos_minor(cache_attn_v[0]), bb=2).T
    y_s, conv_s, st_s = _ssd_sample(us, state_ssd_conv[0].reshape(bs, -1),
                                    state_ssd[0].reshape(bs, ssd_inner, SSD_STATE),
                                    cw0, cb0, dtb, alog, aloge, dsk, snw, heads=ssd_heads,
                                    col_z=col_z, col_xbc=col_xbc, col_dt=col_dt, bb=4)
    hs = _out_proj(hs, [att_s, y_s], [w_out0[:ATT_W], w_out0[ATT_W:]], tm=bs)

    wo = w_in_odd[0]
    gates_at = 4 * m_inner
    zcol = gates_at + 2 * MLSTM_HEADS
    col_gates = 5 * m_inner
    zpad = jnp.zeros((d_model, LANE - MLSTM_HEADS), wo.dtype)
    w_in1 = jnp.concatenate([wo[:, :gates_at], wo[:, zcol:],
                             wo[:, gates_at:gates_at + MLSTM_HEADS], zpad,
                             wo[:, gates_at + MLSTM_HEADS:zcol], zpad], axis=1).astype(BF16)
    w_out1 = w_out_odd[0].astype(BF16)
    nw1 = _row(norm_w[1])
    cw1, cb1 = mlstm_conv_w[0], _row(mlstm_conv_b[0])
    bi = _row(mlstm_igate_b[0], LANE)
    bf = _row(mlstm_fgate_b[0], LANE)
    mnw = _row(mlstm_norm_w[0])
    fnw = _row(final_norm_w)

    up1 = _norm_matmul(hp, nw1, w_in1, tm=512, panels=2)
    hz_p, mconv_p, c_p, n_p, m_p = _mlstm_prompt(up1, cw1, cb1, bi, bf, mnw, batch=batch, seq=seq,
                                                   heads=MLSTM_HEADS, hd=m_hd, col_gates=col_gates)
    y_prompt = _out_proj(hp, [hz_p], [w_out1], fnw, tm=512)

    us1 = _norm_matmul(hs, nw1, w_in1, tm=bs, panels=2)
    m0p = _pad_cols(state_mlstm_m[0], LANE)
    hz_s, mconv_s, c_s, n_s, m_s = _mlstm_sample(us1, state_mlstm_conv[0].reshape(bs, -1),
                                                 state_mlstm_c[0], state_mlstm_n[0].reshape(bs, m_inner),
                                                 m0p, cw1, cb1, bi, bf, mnw,
                                                 heads=MLSTM_HEADS, hd=m_hd, col_gates=col_gates, bb=4)
    y_sample = _out_proj(hs, [hz_s], [w_out1], fnw, tm=bs)

    tmax = min(seq, CHUNK * max(DILATIONS))
    kv_out = lambda t: jnp.transpose(t.reshape(batch, ATT_HEADS, ATT_HD, seq), (0, 3, 1, 2))[None, :, seq - tmax:]
    return (
        y_prompt.reshape(batch, seq, d_model),
        y_sample.reshape(bs, 1, d_model),
        kv_out(k_p),
        kv_out(v_p),
        conv_p[None],
        st_p.reshape(1, batch, ssd_heads, SSD_HEADDIM, SSD_STATE),
        mconv_p[None],
        c_p[None],
        n_p.reshape(1, batch, MLSTM_HEADS, m_hd),
        m_p.reshape(1, batch, MLSTM_HEADS),
        k_s.reshape(1, bs, 1, ATT_HEADS, ATT_HD),
        v_s.reshape(1, bs, 1, ATT_HEADS, ATT_HD),
        conv_s.reshape(1, bs, CONV_K - 1, ssd_cch),
        st_s.reshape(1, bs, ssd_heads, SSD_HEADDIM, SSD_STATE),
        mconv_s.reshape(1, bs, CONV_K - 1, 2 * m_inner),
        c_s[None],
        n_s.reshape(1, bs, MLSTM_HEADS, m_hd),
        m_s.reshape(1, bs, MLSTM_HEADS),
    )
```

```python
import functools

import jax
import jax.numpy as jnp
from jax import lax
from jax.experimental import pallas as pl
from jax.experimental.pallas import tpu as pltpu

F32 = jnp.float32
BF16 = jnp.bfloat16
NEG_INF = float("-inf")
EPS = 1e-6
LANE = 128
SUBLANE = 8
CHUNK = 128
VMEM_LIMIT = 56 * 1024 * 1024

CONV_K = 4
HIST = SUBLANE - (CONV_K - 1)
ATT_HEADS = 8
ATT_HD = 64
ATT_W = ATT_HEADS * ATT_HD
ROT_DIM = ATT_HD // 4
ROPE_THETA = 500000.0
DILATIONS = (1, 4, 16)
PAST_LEN = 2048
SSD_HEADDIM = 64
SSD_GROUPS = 2
SSD_STATE = 128
MLSTM_HEADS = 8


def _params(*sem):
    return pltpu.CompilerParams(dimension_semantics=sem, vmem_limit_bytes=VMEM_LIMIT)


def _chunks(n, w):
    out, c = [], 0
    while c < n:
        out.append((c, min(w, n - c)))
        c += w
    return out


def _dot(a, b):
    return jnp.dot(a.astype(BF16), b.astype(BF16), preferred_element_type=F32)


def _dot_nt(a, b):
    return lax.dot_general(a.astype(BF16), b.astype(BF16), (((1,), (1,)), ((), ())),
                           preferred_element_type=F32)


def _dot_tn(a, b):
    return lax.dot_general(a.astype(BF16), b.astype(BF16), (((0,), (0,)), ((), ())),
                           preferred_element_type=F32)


def _split3(x):
    hi = x.astype(BF16)
    r1 = x - hi.astype(F32)
    mid = r1.astype(BF16)
    lo = (r1 - mid.astype(F32)).astype(BF16)
    return hi, mid, lo


def _sel_dot(sel, x):
    hi, mid, lo = _split3(x)
    d = lambda p: jnp.dot(sel, p, preferred_element_type=F32)
    return d(hi) + d(mid) + d(lo)


def _dot_sel(x, sel):
    hi, mid, lo = _split3(x)
    d = lambda p: jnp.dot(p, sel, preferred_element_type=F32)
    return d(hi) + d(mid) + d(lo)


def _tril(n):
    r = lax.broadcasted_iota(jnp.int32, (n, n), 0)
    c = lax.broadcasted_iota(jnp.int32, (n, n), 1)
    return r >= c


def _seg_matrix(rows, cols, seg, along_rows):
    r = lax.broadcasted_iota(jnp.int32, (rows, cols), 0)
    c = lax.broadcasted_iota(jnp.int32, (rows, cols), 1)
    m = (r // seg == c) if along_rows else (c // seg == r)
    return m.astype(BF16)


def _row_at(ref, b, cols=slice(None)):
    base = pl.multiple_of((b // SUBLANE) * SUBLANE, SUBLANE)
    tile = ref[pl.ds(base, SUBLANE), cols]
    sub = lax.broadcasted_iota(jnp.int32, tile.shape, 0)
    return jnp.sum(jnp.where(sub == b % SUBLANE, tile, 0.0), axis=0, keepdims=True)


def _silu(x):
    h = 0.5 * x
    return h + h * jnp.tanh(h)


def _sigmoid(x):
    return 0.5 + 0.5 * jnp.tanh(0.5 * x)


def _softplus(x):
    return jnp.maximum(x, 0.0) + jnp.log1p(jnp.exp(-jnp.abs(x)))


def _rms(x, w):
    return x * lax.rsqrt(jnp.mean(x * x, axis=-1, keepdims=True) + EPS) * w


def _norm_matmul_body(x_ref, nw_ref, w_ref, o_ref, *, chunks):
    xn = _rms(x_ref[...], nw_ref[...]).astype(BF16)
    for c0, cw in chunks:
        o_ref[:, c0:c0 + cw] = jnp.dot(xn, w_ref[:, c0:c0 + cw], preferred_element_type=F32)


def _norm_matmul(x, nw, w, *, tm, panels):
    m, d = x.shape
    n = w.shape[1]
    pn = n // panels
    return pl.pallas_call(
        functools.partial(_norm_matmul_body, chunks=_chunks(pn, 512)),
        grid=(panels, m // tm),
        in_specs=[pl.BlockSpec((tm, d), lambda p, i: (i, 0)),
                  pl.BlockSpec((1, d), lambda p, i: (0, 0)),
                  pl.BlockSpec((d, pn), lambda p, i: (0, p))],
        out_specs=pl.BlockSpec((tm, pn), lambda p, i: (i, p)),
        out_shape=jax.ShapeDtypeStruct((m, n), F32),
        compiler_params=_params("arbitrary", "arbitrary"),
        name="norm_matmul")(x, nw, w)


def _out_proj_body(*refs, n_in, final):
    h_ref = refs[0]
    xs = refs[1:1 + n_in]
    ws = refs[1 + n_in:1 + 2 * n_in]
    rest = refs[1 + 2 * n_in:]
    acc = h_ref[...]
    for x_ref, w_ref in zip(xs, ws):
        acc = acc + jnp.dot(x_ref[...].astype(BF16), w_ref[...], preferred_element_type=F32)
    if final:
        fw_ref, o_ref = rest
        o_ref[...] = _rms(acc, fw_ref[...])
    else:
        (o_ref,) = rest
        o_ref[...] = acc


def _out_proj(h, xs, ws, fw=None, *, tm):
    m, d = h.shape
    n_in = len(xs)
    in_specs = [pl.BlockSpec((tm, d), lambda i: (i, 0))]
    in_specs += [pl.BlockSpec((tm, x.shape[1]), lambda i: (i, 0)) for x in xs]
    in_specs += [pl.BlockSpec(w.shape, lambda i: (0, 0)) for w in ws]
    args = [h, *xs, *ws]
    if fw is not None:
        in_specs.append(pl.BlockSpec((1, d), lambda i: (0, 0)))
        args.append(fw)
    return pl.pallas_call(
        functools.partial(_out_proj_body, n_in=n_in, final=fw is not None),
        grid=(m // tm,),
        in_specs=in_specs,
        out_specs=pl.BlockSpec((tm, d), lambda i: (i, 0)),
        out_shape=jax.ShapeDtypeStruct((m, d), F32),
        compiler_params=_params("arbitrary"),
        name="out_proj")(*args)


def _rope_tables(pos, width):
    half = ROT_DIM // 2
    inv = jnp.power(F32(ROPE_THETA), -jnp.arange(half, dtype=F32) * (2.0 / ROT_DIM))
    ang = pos.astype(F32)[:, None] * inv[None, :]
    cos, sin = jnp.cos(ang), jnp.sin(ang)
    n = pos.shape[0]
    one = jnp.ones((n, ATT_HD - ROT_DIM), F32)
    z8 = jnp.zeros((n, half), F32)
    z48 = jnp.zeros((n, ATT_HD - ROT_DIM), F32)
    c = jnp.concatenate([cos, cos, one], axis=-1)
    sa = jnp.concatenate([-sin, z8, z48], axis=-1)
    sb = jnp.concatenate([z8, sin, z48], axis=-1)
    rep = width // ATT_HD
    return tuple(jnp.tile(t, (1, rep)) for t in (c, sa, sb))


def _rotary(x, c, sa, sb):
    w = x.shape[-1]
    half = ROT_DIM // 2
    return x * c + pltpu.roll(x, w - half, 1) * sa + pltpu.roll(x, half, 1) * sb


def _attn_prompt_body(q_ref, k_ref, v_ref, g_ref, c_ref, sa_ref, sb_ref,
                      att_ref, ko_ref, vo_ref,
                      nat, qd, kd, vd, od, std, o_s, st_s, *, seq):
    nblk = seq // CHUNK
    c, sa, sb = c_ref[...], sa_ref[...], sb_ref[...]
    q = _rotary(q_ref[...], c, sa, sb) * (ATT_HD ** -0.5)
    k = _rotary(k_ref[...], c, sa, sb)
    ko_ref[...] = k.T
    vo_ref[...] = v_ref[...].T
    head_lane = lax.broadcasted_iota(jnp.int32, (1, LANE), 1) // ATT_HD
    nat[0] = q
    nat[1] = k
    nat[2] = v_ref[...]
    zero = jnp.zeros((CHUNK, LANE), BF16)
    for p, d in enumerate(DILATIONS):
        kd[p, 0:CHUNK, :] = zero
        vd[p, 0:CHUNK, :] = zero
        ln = seq // d
        for r in range(d):
            rows = slice(r * ln, (r + 1) * ln)
            krows = slice(CHUNK + r * ln, CHUNK + (r + 1) * ln)
            src = pl.ds(r, ln, stride=d) if d > 1 else slice(None)
            qr = nat[0, src, :]
            qd[p, 0, rows, :] = jnp.where(head_lane == 0, qr, 0.0).astype(BF16)
            qd[p, 1, rows, :] = jnp.where(head_lane == 1, qr, 0.0).astype(BF16)
            kd[p, krows, :] = nat[1, src, :].astype(BF16)
            vd[p, krows, :] = nat[2, src, :].astype(BF16)

    row = lax.broadcasted_iota(jnp.int32, (CHUNK, 2 * CHUNK), 0)
    col = lax.broadcasted_iota(jnp.int32, (CHUNK, 2 * CHUNK), 1)
    band = (col >= row) & (col <= row + CHUNK)
    first_head = lax.broadcasted_iota(jnp.int32, (CHUNK, LANE), 1) < ATT_HD

    for p, d in enumerate(DILATIONS):
        nb = nblk // d

        def block(t, carry, p=p, nb=nb):
            base = pl.multiple_of(t * CHUNK, CHUNK)
            rows = pl.ds(base, CHUNK)
            first = (t % nb) == 0
            valid = band & (col >= jnp.where(first, CHUNK, 0))
            kw = kd[p, pl.ds(base, 2 * CHUNK), :]
            vw = vd[p, pl.ds(base, 2 * CHUNK), :]
            parts = []
            for hh in range(2):
                s = lax.dot_general(qd[p, hh, rows, :], kw, (((1,), (1,)), ((), ())),
                                    preferred_element_type=F32)
                s = jnp.where(valid, s, NEG_INF)
                m = jnp.max(s, axis=-1, keepdims=True)
                e = jnp.exp(s - m)
                o = jnp.dot(e.astype(BF16), vw, preferred_element_type=F32)
                parts.append((o, m, jnp.sum(e, axis=-1, keepdims=True)))
            (o0, m0, l0), (o1, m1, l1) = parts
            wide = lambda t: jnp.broadcast_to(t, (CHUNK, LANE))
            o_dst, st_dst = (o_s.at[p], st_s.at[p]) if nb == nblk else (od, std)
            o_dst[rows, :] = jnp.where(first_head, o0, o1)
            st_dst[0, rows, :] = jnp.where(first_head, wide(m0), wide(m1))
            st_dst[1, rows, :] = jnp.where(first_head, wide(l0), wide(l1))
            return carry

        lax.fori_loop(0, nblk, block, 0, unroll=True)
        if d > 1:
            ln = seq // d
            for r in range(d):
                o_s[p, pl.ds(r, ln, stride=d), :] = od[r * ln:(r + 1) * ln, :]
                st_s[p, 0, pl.ds(r, ln, stride=d), :] = std[0, r * ln:(r + 1) * ln, :]
                st_s[p, 1, pl.ds(r, ln, stride=d), :] = std[1, r * ln:(r + 1) * ln, :]

    np_ = len(DILATIONS)

    def combine(t, carry):
        base = pl.multiple_of(t * CHUNK, CHUNK)
        rows = pl.ds(base, CHUNK)
        ms = [st_s[p, 0, rows, :] for p in range(np_)]
        mx = functools.reduce(jnp.maximum, ms)
        ws = [jnp.exp(mm - mx) for mm in ms]
        num = sum(w * o_s[p, rows, :] for p, w in enumerate(ws))
        den = sum(w * st_s[p, 1, rows, :] for p, w in enumerate(ws))
        att_ref[rows, :] = ((num / den) * _silu(g_ref[rows, :])).astype(att_ref.dtype)
        return carry

    lax.fori_loop(0, nblk, combine, 0, unroll=2)


def _attn_prompt(u, cos, sa, sb, *, batch, seq):
    m = batch * seq
    npair = ATT_W // LANE
    blk = lambda off: pl.BlockSpec((seq, LANE), lambda b, hp, off=off: (b, off + hp))
    tab = pl.BlockSpec((seq, LANE), lambda b, hp: (0, 0))
    out = pl.BlockSpec((seq, LANE), lambda b, hp: (b, hp))
    out_t = pl.BlockSpec((None, LANE, seq), lambda b, hp: (b, hp, 0))
    np_ = len(DILATIONS)
    return pl.pallas_call(
        functools.partial(_attn_prompt_body, seq=seq),
        grid=(batch, npair),
        in_specs=[blk(0), blk(npair), blk(2 * npair), blk(3 * npair), tab, tab, tab],
        out_specs=[out, out_t, out_t],
        out_shape=[jax.ShapeDtypeStruct((m, ATT_W), BF16),
                   jax.ShapeDtypeStruct((batch, ATT_W, seq), F32),
                   jax.ShapeDtypeStruct((batch, ATT_W, seq), F32)],
        scratch_shapes=[pltpu.VMEM((3, seq, LANE), F32),
                        pltpu.VMEM((np_, 2, seq, LANE), BF16),
                        pltpu.VMEM((np_, seq + CHUNK, LANE), BF16),
                        pltpu.VMEM((np_, seq + CHUNK, LANE), BF16),
                        pltpu.VMEM((seq, LANE), F32),
                        pltpu.VMEM((2, seq, LANE), F32),
                        pltpu.VMEM((np_, seq, LANE), F32),
                        pltpu.VMEM((np_, 2, seq, LANE), F32)],
        compiler_params=_params("arbitrary", "arbitrary"),
        name="attn_prompt")(u, u, u, u, cos, sa, sb)


def _ssd_prompt_body(xbc_ref, z_ref, dt_ref, cw_ref, cb_ref, dtb_ref, alog_ref, dsk_ref, nw_ref,
                     y_ref, conv_ref, st_ref, ext, st, ys, *, heads, nchunk):
    c = pl.program_id(1)
    inner = heads * SSD_HEADDIM
    gw = SSD_STATE
    hpg = heads // SSD_GROUPS

    @pl.when(c == 0)
    def _():
        ext[0:SUBLANE, :] = jnp.zeros((SUBLANE, ext.shape[1]), F32)
        st[...] = jnp.zeros(st.shape, F32)

    ext[SUBLANE:SUBLANE + CHUNK, :] = xbc_ref[...]
    conv = cb_ref[...] + ext[HIST:HIST + CHUNK, :] * cw_ref[0:1, :]
    for j in range(1, CONV_K):
        conv = conv + ext[HIST + j:HIST + j + CHUNK, :] * cw_ref[j:j + 1, :]
    act = _silu(conv)

    @pl.when(c == nchunk - 1)
    def _():
        conv_ref[...] = ext[CHUNK + HIST:CHUNK + SUBLANE, :]

    ext[0:SUBLANE, :] = ext[CHUNK:CHUNK + SUBLANE, :]

    dt = _softplus(dt_ref[...] + dtb_ref[...])
    a = -jnp.exp(alog_ref[...])
    tril = _tril(CHUNK)
    acum = _sel_dot(jnp.where(tril, 1.0, 0.0).astype(BF16), dt * a)
    acum_t = acum.T
    last = acum[CHUNK - 1:CHUNK, :]
    elast = jnp.exp(last)
    expand = _seg_matrix(LANE, inner, SSD_HEADDIM, False)
    dt_e = _dot_sel(dt, expand)
    eacum_e = _dot_sel(jnp.exp(acum), expand)
    wend_e = _dot_sel(jnp.exp(last - acum) * dt, expand)
    xs = act[:, 0:inner]
    xdt = xs * dt_e
    xw = xs * wend_e
    hw = hpg * SSD_HEADDIM
    first_half = lax.broadcasted_iota(jnp.int32, (CHUNK, LANE), 1) < SSD_HEADDIM

    for g in range(SSD_GROUPS):
        bm = act[:, inner + g * gw:inner + (g + 1) * gw]
        cm = act[:, inner + SSD_GROUPS * gw + g * gw:inner + SSD_GROUPS * gw + (g + 1) * gw]
        cb = _dot_nt(cm, bm)
        sg = st[g * hw:(g + 1) * hw, :]
        y_inter = _dot_nt(cm, sg) * eacum_e[:, g * hw:(g + 1) * hw]
        s_local = _dot_tn(xw[:, g * hw:(g + 1) * hw], bm)
        for hg in range(hpg):
            h = g * hpg + hg
            rows = slice(hg * SSD_HEADDIM, (hg + 1) * SSD_HEADDIM)
            st[g * hw + hg * SSD_HEADDIM:g * hw + (hg + 1) * SSD_HEADDIM, :] = (
                sg[rows, :] * elast[:, h:h + 1] + s_local[rows, :])
        for pr in range(hpg // 2):
            lo = g * hw + pr * LANE
            xp = xdt[:, lo:lo + LANE].astype(BF16)
            halves = []
            for hh in range(2):
                h = g * hpg + 2 * pr + hh
                seg = acum[:, h:h + 1] - acum_t[h:h + 1, :]
                mh = cb * jnp.exp(jnp.where(tril, seg, NEG_INF))
                halves.append(jnp.dot(mh.astype(BF16), xp, preferred_element_type=F32))
            ys[:, lo:lo + LANE] = (jnp.where(first_half, halves[0], halves[1])
                                   + y_inter[:, pr * LANE:(pr + 1) * LANE])

    yt = (ys[...] + dsk_ref[...] * xs) * _silu(z_ref[...])
    y_ref[...] = _rms(yt, nw_ref[...]).astype(y_ref.dtype)

    @pl.when(c == nchunk - 1)
    def _():
        st_ref[...] = st[...]


def _ssd_prompt(u, cw, cb, dtb, alog, dsk, nw, *, batch, seq, heads, col_z, col_xbc, col_dt):
    nchunk = seq // CHUNK
    inner = heads * SSD_HEADDIM
    cch = cw.shape[1]
    row = lambda b, c: b * nchunk + c
    full = lambda a: pl.BlockSpec(a.shape, lambda b, c: (0, 0))
    return pl.pallas_call(
        functools.partial(_ssd_prompt_body, heads=heads, nchunk=nchunk),
        grid=(batch, nchunk),
        in_specs=[pl.BlockSpec((CHUNK, cch), lambda b, c: (row(b, c), col_xbc // cch)),
                  pl.BlockSpec((CHUNK, inner), lambda b, c: (row(b, c), col_z // inner)),
                  pl.BlockSpec((CHUNK, LANE), lambda b, c: (row(b, c), col_dt // LANE)),
                  full(cw), full(cb), full(dtb), full(alog), full(dsk), full(nw)],
        out_specs=[pl.BlockSpec((CHUNK, inner), lambda b, c: (row(b, c), 0)),
                   pl.BlockSpec((None, CONV_K - 1, cch), lambda b, c: (b, 0, 0)),
                   pl.BlockSpec((None, inner, SSD_STATE), lambda b, c: (b, 0, 0))],
        out_shape=[jax.ShapeDtypeStruct((batch * seq, inner), BF16),
                   jax.ShapeDtypeStruct((batch, CONV_K - 1, cch), F32),
                   jax.ShapeDtypeStruct((batch, inner, SSD_STATE), F32)],
        scratch_shapes=[pltpu.VMEM((CHUNK + SUBLANE, cch), F32),
                        pltpu.VMEM((inner, SSD_STATE), F32),
                        pltpu.VMEM((CHUNK, inner), F32)],
        compiler_params=_params("arbitrary", "arbitrary"),
        name="ssd_prompt")(u, u, u, cw, cb, dtb, alog, dsk, nw)


def _mlstm_prompt_body(q_ref, k_ref, v_ref, o_ref, z_ref, gi_ref, gf_ref,
                       cw_ref, cb_ref, bi_ref, bf_ref, nw_ref,
                       hz_ref, conv_ref, c_out, n_out, m_out,
                       histq, histk, c_s, n_s, m_s, gate_s, *, nchunk, heads, hd):
    c = pl.program_id(1)
    inner = heads * hd

    @pl.when(c == 0)
    def _():
        histq[...] = jnp.zeros((SUBLANE, inner), F32)
        histk[...] = jnp.zeros((SUBLANE, inner), F32)
        c_s[...] = jnp.zeros(c_s.shape, F32)
        n_s[...] = jnp.zeros(n_s.shape, F32)
        m_s[...] = jnp.full(m_s.shape, NEG_INF, F32)

    def conv_head(hist, x_ref, col0, h):
        sl = slice(h * hd, (h + 1) * hd)
        cols = slice(col0 + h * hd, col0 + (h + 1) * hd)
        xp = jnp.concatenate([hist[:, sl], x_ref[:, sl]], axis=0)
        acc = cb_ref[:, cols] + xp[HIST:HIST + CHUNK, :] * cw_ref[0:1, cols]
        for j in range(1, CONV_K):
            acc = acc + xp[HIST + j:HIST + j + CHUNK, :] * cw_ref[j:j + 1, cols]
        return _silu(acc)

    @pl.when(c == nchunk - 1)
    def _():
        conv_ref[:, 0:inner] = q_ref[CHUNK - (CONV_K - 1):CHUNK, :]
        conv_ref[:, inner:] = k_ref[CHUNK - (CONV_K - 1):CHUNK, :]

    it = gi_ref[...] + bi_ref[...]
    logf = -_softplus(-(gf_ref[...] + bf_ref[...]))
    tril = _tril(CHUNK)
    bc = _sel_dot(jnp.where(tril, 1.0, 0.0).astype(BF16), logf)
    gate_s[0] = it
    gate_s[1] = bc
    gate_s[2] = it.T
    gate_s[3] = bc.T

    for h in range(heads):
        sl = slice(h * hd, (h + 1) * hd)
        q = conv_head(histq, q_ref, 0, h)
        k = conv_head(histk, k_ref, inner, h) * (hd ** -0.5)
        v = v_ref[:, sl]
        i_col, b_col = gate_s[0, :, h:h + 1], gate_s[1, :, h:h + 1]
        i_row, b_row = gate_s[2, h:h + 1, :], gate_s[3, h:h + 1, :]
        m_prev = m_s[h]
        dmat = jnp.where(tril, b_col - b_row + i_row, NEG_INF)
        inter = b_col + m_prev
        m_t = jnp.maximum(inter, jnp.max(dmat, axis=-1, keepdims=True))
        w_intra = jnp.exp(dmat - m_t)
        w_inter = jnp.exp(inter - m_t)
        att = w_intra * _dot_nt(q, k)
        c_prev = c_s[h]
        n_prev = n_s[h]
        num = _dot(att, v) + w_inter * _dot(q, c_prev)
        qn = _dot_nt(q, jnp.broadcast_to(n_prev, (SUBLANE, hd)))[:, 0:1]
        den = jnp.sum(att, axis=-1, keepdims=True) + w_inter * qn
        hh = num / jnp.maximum(jnp.abs(den), jnp.exp(-m_t))

        b_last = b_col[CHUNK - 1:CHUNK, :]
        logw = b_last - b_col + i_col
        m_new = jnp.maximum(b_last + m_prev, jnp.max(logw, axis=0, keepdims=True))
        ws = jnp.exp(logw - m_new)
        ws_row = jnp.exp(b_last - b_row + i_row - m_new)
        scale = jnp.exp(b_last + m_prev - m_new)
        c_s[h] = scale * c_prev + _dot_tn(k, ws * v)
        n_s[h] = scale * n_prev + _dot(jnp.broadcast_to(ws_row, (SUBLANE, CHUNK)), k)[0:1, :]
        m_s[h] = m_new

        hg = _sigmoid(o_ref[:, sl]) * hh
        mu = jnp.mean(hg, axis=-1, keepdims=True)
        var = jnp.mean(jnp.square(hg - mu), axis=-1, keepdims=True)
        hn = (hg - mu) * lax.rsqrt(var + EPS) * nw_ref[:, sl]
        hz_ref[:, sl] = (hn * _silu(z_ref[:, sl])).astype(hz_ref.dtype)

    histq[...] = q_ref[CHUNK - SUBLANE:CHUNK, :]
    histk[...] = k_ref[CHUNK - SUBLANE:CHUNK, :]

    @pl.when(c == nchunk - 1)
    def _():
        c_out[...] = c_s[...]
        n_out[...] = n_s[...]
        m_out[...] = m_s[...]


def _mlstm_prompt(u, cw, cb, bi, bf, nw, *, batch, seq, heads, hd, col_gates):
    nchunk = seq // CHUNK
    inner = heads * hd
    row = lambda b, c: b * nchunk + c
    ublk = lambda j: pl.BlockSpec((CHUNK, inner), lambda b, c, j=j: (row(b, c), j))
    gblk = lambda off: pl.BlockSpec((CHUNK, LANE), lambda b, c, off=off: (row(b, c), col_gates // LANE + off))
    full = lambda a: pl.BlockSpec(a.shape, lambda b, c: (0, 0))
    return pl.pallas_call(
        functools.partial(_mlstm_prompt_body, nchunk=nchunk, heads=heads, hd=hd),
        grid=(batch, nchunk),
        in_specs=[ublk(0), ublk(1), ublk(2), ublk(3), ublk(4), gblk(0), gblk(1),
                  full(cw), full(cb), full(bi), full(bf), full(nw)],
        out_specs=[pl.BlockSpec((CHUNK, inner), lambda b, c: (row(b, c), 0)),
                   pl.BlockSpec((None, CONV_K - 1, 2 * inner), lambda b, c: (b, 0, 0)),
                   pl.BlockSpec((None, heads, hd, hd), lambda b, c: (b, 0, 0, 0)),
                   pl.BlockSpec((None, heads, 1, hd), lambda b, c: (b, 0, 0, 0)),
                   pl.BlockSpec((None, heads, 1, 1), lambda b, c: (b, 0, 0, 0))],
        out_shape=[jax.ShapeDtypeStruct((batch * seq, inner), BF16),
                   jax.ShapeDtypeStruct((batch, CONV_K - 1, 2 * inner), F32),
                   jax.ShapeDtypeStruct((batch, heads, hd, hd), F32),
                   jax.ShapeDtypeStruct((batch, heads, 1, hd), F32),
                   jax.ShapeDtypeStruct((batch, heads, 1, 1), F32)],
        scratch_shapes=[pltpu.VMEM((SUBLANE, inner), F32),
                        pltpu.VMEM((SUBLANE, inner), F32),
                        pltpu.VMEM((heads, hd, hd), F32),
                        pltpu.VMEM((heads, 1, hd), F32),
                        pltpu.VMEM((heads, 1, 1), F32),
                        pltpu.VMEM((4, CHUNK, LANE), F32)],
        compiler_params=_params("arbitrary", "arbitrary"),
        name="mlstm_prompt")(u, u, u, u, u, u, u, cw, cb, bi, bf, nw)


def _rot_sample_body(q_ref, k_ref, v_ref, g_ref, c_ref, sa_ref, sb_ref, xt_ref):
    c, sa, sb = c_ref[...], sa_ref[...], sb_ref[...]
    q = _rotary(q_ref[...], c, sa, sb) * (ATT_HD ** -0.5)
    k = _rotary(k_ref[...], c, sa, sb)
    xt_ref[...] = jnp.concatenate([q, k, v_ref[...], g_ref[...]], axis=-1).T


def _rot_sample(u, cos, sa, sb):
    bs = u.shape[0]
    ublk = lambda off: pl.BlockSpec((bs, ATT_W), lambda i, off=off: (0, off))
    tab = pl.BlockSpec((1, ATT_W), lambda i: (0, 0))
    return pl.pallas_call(
        _rot_sample_body,
        grid=(1,),
        in_specs=[ublk(0), ublk(1), ublk(2), ublk(3), tab, tab, tab],
        out_specs=pl.BlockSpec((4 * ATT_W, bs), lambda i: (0, 0)),
        out_shape=jax.ShapeDtypeStruct((4 * ATT_W, bs), F32),
        compiler_params=_params("arbitrary"),
        name="rot_sample")(u, u, u, u, cos, sa, sb)


def _key_multiplicity(wb):
    back = wb - lax.broadcasted_iota(jnp.int32, (1, wb), 1)
    cnt = jnp.zeros((1, wb), F32)
    for d in DILATIONS:
        cnt = cnt + ((back <= CHUNK * d) & (lax.rem(back, d) == 0)).astype(F32)
    return cnt


def _attn_sample_body(x_ref, kc_ref, vc_ref, att_ref, *, bb, wb):
    i = pl.program_id(0)
    bs = x_ref.shape[1]
    nt = wb // LANE

    @pl.when(i == 0)
    def _():
        att_ref[...] = jnp.zeros(att_ref.shape, F32)

    cnt = _key_multiplicity(wb)
    valid = cnt > 0.0
    npat = float(len(DILATIONS))
    rid = lax.broadcasted_iota(jnp.int32, (bs, LANE), 0)
    lid = lax.broadcasted_iota(jnp.int32, (ATT_HD, bs), 1)
    for t in range(bb):
        b = i * bb + t
        cols = _dot_sel(x_ref[...], (rid == b).astype(BF16))
        for h in range(ATT_HEADS):
            lo = h * ATT_HD
            qc = cols[lo:lo + ATT_HD]
            kc = cols[ATT_W + lo:ATT_W + lo + ATT_HD]
            vc = cols[2 * ATT_W + lo:2 * ATT_W + lo + ATT_HD]
            gc = cols[3 * ATT_W + lo:3 * ATT_W + lo + ATT_HD]
            kt = kc_ref[t, h]
            vt = vc_ref[t, h]
            s = jnp.sum(kt * jnp.tile(qc, (1, nt)), axis=0, keepdims=True)
            s = jnp.where(valid, s, NEG_INF)
            s_self = jnp.sum(qc * kc, axis=0, keepdims=True)[:, 0:1]
            m = jnp.maximum(jnp.max(s, axis=1, keepdims=True), s_self)
            p = cnt * jnp.exp(s - m)
            p_self = npat * jnp.exp(s_self - m)
            den = jnp.sum(p, axis=1, keepdims=True) + p_self
            acc = vt[:, 0:LANE] * p[:, 0:LANE]
            for j in range(1, nt):
                acc = acc + vt[:, j * LANE:(j + 1) * LANE] * p[:, j * LANE:(j + 1) * LANE]
            o = jnp.sum(acc, axis=1, keepdims=True) + p_self * vc[:, 0:1]
            o = o / den * _silu(gc[:, 0:1])
            att_ref[lo:lo + ATT_HD, :] = jnp.where(lid == b, o, att_ref[lo:lo + ATT_HD, :])


def _attn_sample(xt, ck, cv, *, bb):
    bs, wb = ck.shape[0], ck.shape[3]
    cache = pl.BlockSpec((bb, ATT_HEADS, ATT_HD, wb), lambda i: (i, 0, 0, 0))
    return pl.pallas_call(
        functools.partial(_attn_sample_body, bb=bb, wb=wb),
        grid=(bs // bb,),
        in_specs=[pl.BlockSpec(xt.shape, lambda i: (0, 0)), cache, cache],
        out_specs=pl.BlockSpec((ATT_W, bs), lambda i: (0, 0)),
        out_shape=jax.ShapeDtypeStruct((ATT_W, bs), F32),
        compiler_params=_params("arbitrary"),
        name="attn_sample")(xt, ck, cv)


def _ssd_sample_prep_body(xbc_ref, dt_ref, cst_ref, cw_ref, cb_ref, dtb_ref, alog_ref, aloge_ref, dsk_ref,
                          conv_ref, yloc_ref, xdtt_ref, bc_ref, da_ref, dae_ref, *, heads):
    inner = heads * SSD_HEADDIM
    cch = xbc_ref.shape[1]
    gw = SSD_STATE
    x = xbc_ref[...]
    acc = cb_ref[...] + x * cw_ref[CONV_K - 1:CONV_K, :]
    for j in range(CONV_K - 1):
        acc = acc + cst_ref[:, j * cch:(j + 1) * cch] * cw_ref[j:j + 1, :]
    for j in range(CONV_K - 2):
        conv_ref[:, j * cch:(j + 1) * cch] = cst_ref[:, (j + 1) * cch:(j + 2) * cch]
    conv_ref[:, (CONV_K - 2) * cch:(CONV_K - 1) * cch] = x
    act = _silu(acc)
    xs = act[:, 0:inner]
    bc_ref[...] = act[:, inner:]
    dt = _softplus(dt_ref[...] + dtb_ref[...])
    da_ref[...] = jnp.exp(dt * (-jnp.exp(alog_ref[...])))
    expand = _seg_matrix(LANE, inner, SSD_HEADDIM, False)
    dte = _dot_sel(dt, expand)
    dae_ref[...] = jnp.exp(dte * (-jnp.exp(aloge_ref[...])))
    xdt = xs * dte
    xdtt_ref[...] = xdt.T
    hw = inner // SSD_GROUPS
    parts = []
    for g in range(SSD_GROUPS):
        bm = act[:, inner + g * gw:inner + (g + 1) * gw]
        cm = act[:, inner + SSD_GROUPS * gw + g * gw:inner + SSD_GROUPS * gw + (g + 1) * gw]
        cbg = jnp.sum(cm * bm, axis=-1, keepdims=True)
        parts.append(cbg * xdt[:, g * hw:(g + 1) * hw])
    yloc_ref[...] = jnp.concatenate(parts, axis=-1) + dsk_ref[...] * xs


def _ssd_sample_state_body(da_ref, s_ref, xdtt_ref, bc_ref, so_ref, yi_ref, *, bb, heads):
    i = pl.program_id(0)
    bs = bc_ref.shape[0]
    inner = heads * SSD_HEADDIM
    hw = inner // SSD_GROUPS
    hpg = heads // SSD_GROUPS
    gw = SSD_STATE
    rid = lax.broadcasted_iota(jnp.int32, (bs, gw), 0)
    for t in range(bb):
        b = i * bb + t
        parts = []
        for g in range(SSD_GROUPS):
            mg = jnp.where(rid == b, bc_ref[:, g * gw:(g + 1) * gw], 0.0)
            sl = _dot(xdtt_ref[g * hw:(g + 1) * hw, :], mg)
            crow = _row_at(bc_ref, b, slice(SSD_GROUPS * gw + g * gw, SSD_GROUPS * gw + (g + 1) * gw))
            sg = s_ref[t, g * hw:(g + 1) * hw, :]
            parts.append(_dot_nt(jnp.broadcast_to(crow, (SUBLANE, gw)), sg)[0:1, :])
            for hg in range(hpg):
                h = g * hpg + hg
                lo = hg * SSD_HEADDIM
                so_ref[t, h * SSD_HEADDIM:(h + 1) * SSD_HEADDIM, :] = (
                    sg[lo:lo + SSD_HEADDIM, :] * da_ref[b, h] + sl[lo:lo + SSD_HEADDIM, :])
        yi_ref[t] = jnp.concatenate(parts, axis=-1)


def _ssd_sample_finish_body(yloc_ref, yi_ref, dae_ref, z_ref, nw_ref, y_ref):
    y = (yloc_ref[...] + yi_ref[...] * dae_ref[...]) * _silu(z_ref[...])
    y_ref[...] = _rms(y, nw_ref[...])


def _ssd_sample(u, cst, state, cw, cb, dtb, alog, aloge, dsk, nw, *, heads, col_z, col_xbc, col_dt, bb):
    bs = u.shape[0]
    inner = heads * SSD_HEADDIM
    cch = cw.shape[1]
    ncs = (CONV_K - 1) * cch
    bcw = 2 * SSD_GROUPS * SSD_STATE
    full = lambda a: pl.BlockSpec(a.shape, lambda i: (0,) * a.ndim)
    conv, yloc, xdtt, bc, da, dae = pl.pallas_call(
        functools.partial(_ssd_sample_prep_body, heads=heads),
        grid=(1,),
        in_specs=[pl.BlockSpec((bs, cch), lambda i: (0, col_xbc // cch)),
                  pl.BlockSpec((bs, LANE), lambda i: (0, col_dt // LANE)),
                  full(cst), full(cw), full(cb), full(dtb), full(alog), full(aloge), full(dsk)],
        out_specs=[pl.BlockSpec((bs, ncs), lambda i: (0, 0)),
                   pl.BlockSpec((bs, inner), lambda i: (0, 0)),
                   pl.BlockSpec((inner, bs), lambda i: (0, 0)),
                   pl.BlockSpec((bs, bcw), lambda i: (0, 0)),
                   pl.BlockSpec((bs, LANE), lambda i: (0, 0)),
                   pl.BlockSpec((bs, inner), lambda i: (0, 0))],
        out_shape=[jax.ShapeDtypeStruct((bs, ncs), F32),
                   jax.ShapeDtypeStruct((bs, inner), F32),
                   jax.ShapeDtypeStruct((inner, bs), F32),
                   jax.ShapeDtypeStruct((bs, bcw), F32),
                   jax.ShapeDtypeStruct((bs, LANE), F32),
                   jax.ShapeDtypeStruct((bs, inner), F32)],
        compiler_params=_params("arbitrary"),
        name="ssd_sample_prep")(u, u, cst, cw, cb, dtb, alog, aloge, dsk)
    new_state, yi = pl.pallas_call(
        functools.partial(_ssd_sample_state_body, bb=bb, heads=heads),
        grid=(bs // bb,),
        in_specs=[pl.BlockSpec(memory_space=pltpu.SMEM),
                  pl.BlockSpec((bb, inner, SSD_STATE), lambda i: (i, 0, 0)),
                  full(xdtt), full(bc)],
        out_specs=[pl.BlockSpec((bb, inner, SSD_STATE), lambda i: (i, 0, 0)),
                   pl.BlockSpec((bb, 1, inner), lambda i: (i, 0, 0))],
        out_shape=[jax.ShapeDtypeStruct((bs, inner, SSD_STATE), F32),
                   jax.ShapeDtypeStruct((bs, 1, inner), F32)],
        compiler_params=_params("arbitrary"),
        name="ssd_sample_state")(da[:, :heads], state, xdtt, bc)
    y = pl.pallas_call(
        _ssd_sample_finish_body,
        grid=(1,),
        in_specs=[full(yloc), pl.BlockSpec((bs, inner), lambda i: (0, 0)), full(dae),
                  pl.BlockSpec((bs, inner), lambda i: (0, col_z // inner)), full(nw)],
        out_specs=pl.BlockSpec((bs, inner), lambda i: (0, 0)),
        out_shape=jax.ShapeDtypeStruct((bs, inner), F32),
        compiler_params=_params("arbitrary"),
        name="ssd_sample_finish")(yloc, yi.reshape(bs, inner), dae, u, nw)
    return y, conv, new_state


def _mlstm_sample_prep_body(q_ref, k_ref, gi_ref, gf_ref, cst_ref, cw_ref, cb_ref, bi_ref, bf_ref, m_ref,
                            conv_ref, qk_ref, kwt_ref, wi_ref, wf_ref, mt_ref, *, heads, hd):
    inner = heads * hd
    cch = 2 * inner
    x = jnp.concatenate([q_ref[...], k_ref[...]], axis=-1)
    acc = cb_ref[...] + x * cw_ref[CONV_K - 1:CONV_K, :]
    for j in range(CONV_K - 1):
        acc = acc + cst_ref[:, j * cch:(j + 1) * cch] * cw_ref[j:j + 1, :]
    for j in range(CONV_K - 2):
        conv_ref[:, j * cch:(j + 1) * cch] = cst_ref[:, (j + 1) * cch:(j + 2) * cch]
    conv_ref[:, (CONV_K - 2) * cch:(CONV_K - 1) * cch] = x
    act = _silu(acc)
    q = act[:, 0:inner]
    k = act[:, inner:] * (hd ** -0.5)
    it = gi_ref[...] + bi_ref[...]
    inter = -_softplus(-(gf_ref[...] + bf_ref[...])) + m_ref[...]
    mt = jnp.maximum(inter, it)
    wi = jnp.exp(it - mt)
    wf = jnp.exp(inter - mt)
    wi_ref[...] = wi
    wf_ref[...] = wf
    mt_ref[...] = mt
    qk_ref[:, 0:inner] = q
    qk_ref[:, inner:] = k
    kw = jnp.concatenate([k[:, h * hd:(h + 1) * hd] * wi[:, h:h + 1] for h in range(heads)], axis=-1)
    kwt_ref[...] = kw.T


def _mlstm_sample_state_body(wf_ref, c_ref, qk_ref, v_ref, kwt_ref, co_ref, qc_ref, *, bb, heads, hd):
    i = pl.program_id(0)
    bs = v_ref.shape[0]
    rid = lax.broadcasted_iota(jnp.int32, (bs, hd), 0)
    for t in range(bb):
        b = i * bb + t
        parts = []
        for h in range(heads):
            cp = c_ref[t, h]
            qrow = _row_at(qk_ref, b, slice(h * hd, (h + 1) * hd))
            parts.append(_dot(jnp.broadcast_to(qrow, (SUBLANE, hd)), cp)[0:1, :])
            mh = jnp.where(rid == b, v_ref[:, h * hd:(h + 1) * hd], 0.0)
            co_ref[t, h] = cp * wf_ref[b, h] + _dot(kwt_ref[h * hd:(h + 1) * hd, :], mh)
        qc_ref[t] = jnp.concatenate(parts, axis=-1)


def _mlstm_sample_finish_body(qk_ref, v_ref, o_ref, z_ref, qc_ref, n_ref, wi_ref, wf_ref, mt_ref, nw_ref,
                              hz_ref, no_ref, *, heads, hd):
    inner = heads * hd
    for h in range(heads):
        sl = slice(h * hd, (h + 1) * hd)
        q = qk_ref[:, sl]
        k = qk_ref[:, inner + h * hd:inner + (h + 1) * hd]
        wi = wi_ref[:, h:h + 1]
        wf = wf_ref[:, h:h + 1]
        mt = mt_ref[:, h:h + 1]
        n_prev = n_ref[:, sl]
        att = wi * jnp.sum(q * k, axis=-1, keepdims=True)
        num = att * v_ref[:, sl] + wf * qc_ref[:, sl]
        den = att + wf * jnp.sum(q * n_prev, axis=-1, keepdims=True)
        hh = num / jnp.maximum(jnp.abs(den), jnp.exp(-mt))
        hg = _sigmoid(o_ref[:, sl]) * hh
        mu = jnp.mean(hg, axis=-1, keepdims=True)
        var = jnp.mean(jnp.square(hg - mu), axis=-1, keepdims=True)
        hn = (hg - mu) * lax.rsqrt(var + EPS) * nw_ref[:, sl]
        hz_ref[:, sl] = hn * _silu(z_ref[:, sl])
        no_ref[:, sl] = wf * n_prev + wi * k


def _mlstm_sample(u, cst, c0, n0, m0p, cw, cb, bi, bf, nw, *, heads, hd, col_gates, bb):
    bs = u.shape[0]
    inner = heads * hd
    ncs = (CONV_K - 1) * 2 * inner
    full = lambda a: pl.BlockSpec(a.shape, lambda i: (0,) * a.ndim)
    ucol = lambda j: pl.BlockSpec((bs, inner), lambda i, j=j: (0, j))
    gcol = lambda j: pl.BlockSpec((bs, LANE), lambda i, j=j: (0, col_gates // LANE + j))
    tile = jax.ShapeDtypeStruct((bs, LANE), F32)
    conv, qk, kwt, wi, wf, mt = pl.pallas_call(
        functools.partial(_mlstm_sample_prep_body, heads=heads, hd=hd),
        grid=(1,),
        in_specs=[ucol(0), ucol(1), gcol(0), gcol(1), full(cst), full(cw), full(cb), full(bi), full(bf),
                  full(m0p)],
        out_specs=[pl.BlockSpec((bs, ncs), lambda i: (0, 0)),
                   pl.BlockSpec((bs, 2 * inner), lambda i: (0, 0)),
                   pl.BlockSpec((inner, bs), lambda i: (0, 0)),
                   pl.BlockSpec((bs, LANE), lambda i: (0, 0)),
                   pl.BlockSpec((bs, LANE), lambda i: (0, 0)),
                   pl.BlockSpec((bs, LANE), lambda i: (0, 0))],
        out_shape=[jax.ShapeDtypeStruct((bs, ncs), F32),
                   jax.ShapeDtypeStruct((bs, 2 * inner), F32),
                   jax.ShapeDtypeStruct((inner, bs), F32),
                   tile, tile, tile],
        compiler_params=_params("arbitrary"),
        name="mlstm_sample_prep")(u, u, u, u, cst, cw, cb, bi, bf, m0p)
    c_new, qc = pl.pallas_call(
        functools.partial(_mlstm_sample_state_body, bb=bb, heads=heads, hd=hd),
        grid=(bs // bb,),
        in_specs=[pl.BlockSpec(memory_space=pltpu.SMEM),
                  pl.BlockSpec((bb, heads, hd, hd), lambda i: (i, 0, 0, 0)),
                  full(qk), pl.BlockSpec((bs, inner), lambda i: (0, 2)), full(kwt)],
        out_specs=[pl.BlockSpec((bb, heads, hd, hd), lambda i: (i, 0, 0, 0)),
                   pl.BlockSpec((bb, 1, inner), lambda i: (i, 0, 0))],
        out_shape=[jax.ShapeDtypeStruct((bs, heads, hd, hd), F32),
                   jax.ShapeDtypeStruct((bs, 1, inner), F32)],
        compiler_params=_params("arbitrary"),
        name="mlstm_sample_state")(wf[:, :heads], c0, qk, u, kwt)
    hz, n_new = pl.pallas_call(
        functools.partial(_mlstm_sample_finish_body, heads=heads, hd=hd),
        grid=(1,),
        in_specs=[full(qk), ucol(2), ucol(3), ucol(4), pl.BlockSpec((bs, inner), lambda i: (0, 0)),
                  full(n0), full(wi), full(wf), full(mt), full(nw)],
        out_specs=[pl.BlockSpec((bs, inner), lambda i: (0, 0)),
                   pl.BlockSpec((bs, inner), lambda i: (0, 0))],
        out_shape=[jax.ShapeDtypeStruct((bs, inner), F32)] * 2,
        compiler_params=_params("arbitrary"),
        name="mlstm_sample_finish")(qk, u, u, u, qc.reshape(bs, inner), n0, wi, wf, mt, nw)
    return hz, conv, c_new, n_new, mt[:, :heads]


def _pad_cols(w, n):
    return jnp.pad(w, ((0, 0), (0, n - w.shape[1])))


def _row(v, n=None):
    v = v.reshape(1, -1)
    return v if n is None else _pad_cols(v, n)


def kernel(x_prompt, x_sample, cache_attn_k, cache_attn_v, state_ssd_conv, state_ssd, state_mlstm_conv, state_mlstm_c, state_mlstm_n, state_mlstm_m, norm_w, final_norm_w, w_in_even, w_out_even, ssd_conv_w, ssd_conv_b, ssd_dt_bias, ssd_a_log, ssd_d, ssd_norm_w, w_in_odd, w_out_odd, mlstm_conv_w, mlstm_conv_b, mlstm_igate_b, mlstm_fgate_b, mlstm_norm_w):
    batch, seq, d_model = x_prompt.shape
    bs = x_sample.shape[0]
    ssd_heads = ssd_a_log.shape[1]
    ssd_inner = ssd_heads * SSD_HEADDIM
    ssd_cch = ssd_conv_w.shape[2]
    m_inner = mlstm_norm_w.shape[1]
    m_hd = m_inner // MLSTM_HEADS
    mp = batch * seq

    col_z = 4 * ATT_W
    col_xbc = col_z + ssd_inner
    col_dt = col_xbc + ssd_cch
    n_even = col_dt + LANE
    w_in0 = _pad_cols(w_in_even[0], n_even).astype(BF16)
    w_out0 = w_out_even[0].astype(BF16)
    nw0 = _row(norm_w[0])
    cw0, cb0 = ssd_conv_w[0], _row(ssd_conv_b[0])
    dtb = _row(ssd_dt_bias[0], LANE)
    alog = _row(ssd_a_log[0], LANE)
    aloge = _row(jnp.repeat(ssd_a_log[0], SSD_HEADDIM))
    dsk = _row(jnp.repeat(ssd_d[0], SSD_HEADDIM))
    snw = _row(ssd_norm_w[0])

    hp = x_prompt.reshape(mp, d_model)
    hs = x_sample.reshape(bs, d_model)

    up = _norm_matmul(hp, nw0, w_in0, tm=512, panels=1)
    cos_p, sa_p, sb_p = _rope_tables(jnp.arange(seq), LANE)
    att_p, k_p, v_p = _attn_prompt(up, cos_p, sa_p, sb_p, batch=batch, seq=seq)
    y_p, conv_p, st_p = _ssd_prompt(up, cw0, cb0, dtb, alog, dsk, snw, batch=batch, seq=seq,
                                    heads=ssd_heads, col_z=col_z, col_xbc=col_xbc, col_dt=col_dt)
    hp = _out_proj(hp, [att_p, y_p], [w_out0[:ATT_W], w_out0[ATT_W:]], tm=1024)

    us = _norm_matmul(hs, nw0, w_in0, tm=bs, panels=1)
    cos_s, sa_s, sb_s = _rope_tables(PAST_LEN + jnp.arange(1), ATT_W)
    xt_s = _rot_sample(us, cos_s, sa_s, sb_s)
    k_s = xt_s[ATT_W:2 * ATT_W].T
    v_s = xt_s[2 * ATT_W:3 * ATT_W].T
    pos_minor = lambda cache: jnp.transpose(cache, (0, 2, 3, 1))
    att_s = _attn_sample(xt_s, pos_minor(cache_attn_k[0]), pos_minor(cache_attn_v[0]), bb=2).T
    y_s, conv_s, st_s = _ssd_sample(us, state_ssd_conv[0].reshape(bs, -1),
                                    state_ssd[0].reshape(bs, ssd_inner, SSD_STATE),
                                    cw0, cb0, dtb, alog, aloge, dsk, snw, heads=ssd_heads,
                                    col_z=col_z, col_xbc=col_xbc, col_dt=col_dt, bb=8)
    hs = _out_proj(hs, [att_s, y_s], [w_out0[:ATT_W], w_out0[ATT_W:]], tm=bs)

    wo = w_in_odd[0]
    gates_at = 4 * m_inner
    zcol = gates_at + 2 * MLSTM_HEADS
    col_gates = 5 * m_inner
    zpad = jnp.zeros((d_model, LANE - MLSTM_HEADS), wo.dtype)
    w_in1 = jnp.concatenate([wo[:, :gates_at], wo[:, zcol:],
                             wo[:, gates_at:gates_at + MLSTM_HEADS], zpad,
                             wo[:, gates_at + MLSTM_HEADS:zcol], zpad], axis=1).astype(BF16)
    w_out1 = w_out_odd[0].astype(BF16)
    nw1 = _row(norm_w[1])
    cw1, cb1 = mlstm_conv_w[0], _row(mlstm_conv_b[0])
    bi = _row(mlstm_igate_b[0], LANE)
    bf = _row(mlstm_fgate_b[0], LANE)
    mnw = _row(mlstm_norm_w[0])
    fnw = _row(final_norm_w)

    up1 = _norm_matmul(hp, nw1, w_in1, tm=512, panels=2)
    hz_p, mconv_p, c_p, n_p, m_p = _mlstm_prompt(up1, cw1, cb1, bi, bf, mnw, batch=batch, seq=seq,
                                                   heads=MLSTM_HEADS, hd=m_hd, col_gates=col_gates)
    y_prompt = _out_proj(hp, [hz_p], [w_out1], fnw, tm=1024)

    us1 = _norm_matmul(hs, nw1, w_in1, tm=bs, panels=2)
    m0p = _pad_cols(state_mlstm_m[0], LANE)
    hz_s, mconv_s, c_s, n_s, m_s = _mlstm_sample(us1, state_mlstm_conv[0].reshape(bs, -1),
                                                 state_mlstm_c[0], state_mlstm_n[0].reshape(bs, m_inner),
                                                 m0p, cw1, cb1, bi, bf, mnw,
                                                 heads=MLSTM_HEADS, hd=m_hd, col_gates=col_gates, bb=4)
    y_sample = _out_proj(hs, [hz_s], [w_out1], fnw, tm=bs)

    tmax = min(seq, CHUNK * max(DILATIONS))
    kv_out = lambda t: jnp.transpose(t.reshape(batch, ATT_HEADS, ATT_HD, seq), (0, 3, 1, 2))[None, :, seq - tmax:]
    return (
        y_prompt.reshape(batch, seq, d_model),
        y_sample.reshape(bs, 1, d_model),
        kv_out(k_p),
        kv_out(v_p),
        conv_p[None],
        st_p.reshape(1, batch, ssd_heads, SSD_HEADDIM, SSD_STATE),
        mconv_p[None],
        c_p[None],
        n_p.reshape(1, batch, MLSTM_HEADS, m_hd),
        m_p.reshape(1, batch, MLSTM_HEADS),
        k_s.reshape(1, bs, 1, ATT_HEADS, ATT_HD),
        v_s.reshape(1, bs, 1, ATT_HEADS, ATT_HD),
        conv_s.reshape(1, bs, CONV_K - 1, ssd_cch),
        st_s.reshape(1, bs, ssd_heads, SSD_HEADDIM, SSD_STATE),
        mconv_s.reshape(1, bs, CONV_K - 1, 2 * m_inner),
        c_s[None],
        n_s.reshape(1, bs, MLSTM_HEADS, m_hd),
        m_s.reshape(1, bs, MLSTM_HEADS),
    )
```

```python
import functools

import jax
import jax.numpy as jnp
from jax import lax
from jax.experimental import pallas as pl
from jax.experimental.pallas import tpu as pltpu

F32 = jnp.float32
BF16 = jnp.bfloat16
NEG_INF = float("-inf")
EPS = 1e-6
LANE = 128
SUBLANE = 8
CHUNK = 128
VMEM_LIMIT = 56 * 1024 * 1024

CONV_K = 4
HIST = SUBLANE - (CONV_K - 1)
ATT_HEADS = 8
ATT_HD = 64
ATT_W = ATT_HEADS * ATT_HD
ROT_DIM = ATT_HD // 4
ROPE_THETA = 500000.0
DILATIONS = (1, 4, 16)
PAST_LEN = 2048
SSD_HEADDIM = 64
SSD_GROUPS = 2
SSD_STATE = 128
MLSTM_HEADS = 8


def _params(*sem):
    return pltpu.CompilerParams(dimension_semantics=sem, vmem_limit_bytes=VMEM_LIMIT)


def _chunks(n, w):
    out, c = [], 0
    while c < n:
        out.append((c, min(w, n - c)))
        c += w
    return out


def _dot(a, b):
    return jnp.dot(a.astype(BF16), b.astype(BF16), preferred_element_type=F32)


def _dot_nt(a, b):
    return lax.dot_general(a.astype(BF16), b.astype(BF16), (((1,), (1,)), ((), ())),
                           preferred_element_type=F32)


def _dot_tn(a, b):
    return lax.dot_general(a.astype(BF16), b.astype(BF16), (((0,), (0,)), ((), ())),
                           preferred_element_type=F32)


def _split3(x):
    hi = x.astype(BF16)
    r1 = x - hi.astype(F32)
    mid = r1.astype(BF16)
    lo = (r1 - mid.astype(F32)).astype(BF16)
    return hi, mid, lo


def _sel_dot(sel, x):
    hi, mid, lo = _split3(x)
    d = lambda p: jnp.dot(sel, p, preferred_element_type=F32)
    return d(hi) + d(mid) + d(lo)


def _dot_sel(x, sel):
    hi, mid, lo = _split3(x)
    d = lambda p: jnp.dot(p, sel, preferred_element_type=F32)
    return d(hi) + d(mid) + d(lo)


def _tril(n):
    r = lax.broadcasted_iota(jnp.int32, (n, n), 0)
    c = lax.broadcasted_iota(jnp.int32, (n, n), 1)
    return r >= c


def _seg_matrix(rows, cols, seg, along_rows):
    r = lax.broadcasted_iota(jnp.int32, (rows, cols), 0)
    c = lax.broadcasted_iota(jnp.int32, (rows, cols), 1)
    m = (r // seg == c) if along_rows else (c // seg == r)
    return m.astype(BF16)


def _row_at(ref, b, cols=slice(None)):
    base = pl.multiple_of((b // SUBLANE) * SUBLANE, SUBLANE)
    tile = ref[pl.ds(base, SUBLANE), cols]
    sub = lax.broadcasted_iota(jnp.int32, tile.shape, 0)
    return jnp.sum(jnp.where(sub == b % SUBLANE, tile, 0.0), axis=0, keepdims=True)


def _silu(x):
    h = 0.5 * x
    return h + h * jnp.tanh(h)


def _sigmoid(x):
    return 0.5 + 0.5 * jnp.tanh(0.5 * x)


def _softplus(x):
    return jnp.maximum(x, 0.0) + jnp.log1p(jnp.exp(-jnp.abs(x)))


def _rms(x, w):
    return x * lax.rsqrt(jnp.mean(x * x, axis=-1, keepdims=True) + EPS) * w


def _norm_matmul_body(x_ref, nw_ref, w_ref, o_ref, *, chunks):
    xn = _rms(x_ref[...], nw_ref[...]).astype(BF16)
    for c0, cw in chunks:
        o_ref[:, c0:c0 + cw] = jnp.dot(xn, w_ref[:, c0:c0 + cw], preferred_element_type=F32)


def _norm_matmul(x, nw, w, *, tm, panels):
    m, d = x.shape
    n = w.shape[1]
    pn = n // panels
    return pl.pallas_call(
        functools.partial(_norm_matmul_body, chunks=_chunks(pn, 512)),
        grid=(panels, m // tm),
        in_specs=[pl.BlockSpec((tm, d), lambda p, i: (i, 0)),
                  pl.BlockSpec((1, d), lambda p, i: (0, 0)),
                  pl.BlockSpec((d, pn), lambda p, i: (0, p))],
        out_specs=pl.BlockSpec((tm, pn), lambda p, i: (i, p)),
        out_shape=jax.ShapeDtypeStruct((m, n), F32),
        compiler_params=_params("arbitrary", "arbitrary"),
        name="norm_matmul")(x, nw, w)


def _out_proj_body(*refs, n_in, final):
    h_ref = refs[0]
    xs = refs[1:1 + n_in]
    ws = refs[1 + n_in:1 + 2 * n_in]
    rest = refs[1 + 2 * n_in:]
    acc = h_ref[...]
    for x_ref, w_ref in zip(xs, ws):
        acc = acc + jnp.dot(x_ref[...].astype(BF16), w_ref[...], preferred_element_type=F32)
    if final:
        fw_ref, o_ref = rest
        o_ref[...] = _rms(acc, fw_ref[...])
    else:
        (o_ref,) = rest
        o_ref[...] = acc


def _out_proj(h, xs, ws, fw=None, *, tm):
    m, d = h.shape
    n_in = len(xs)
    in_specs = [pl.BlockSpec((tm, d), lambda i: (i, 0))]
    in_specs += [pl.BlockSpec((tm, x.shape[1]), lambda i: (i, 0)) for x in xs]
    in_specs += [pl.BlockSpec(w.shape, lambda i: (0, 0)) for w in ws]
    args = [h, *xs, *ws]
    if fw is not None:
        in_specs.append(pl.BlockSpec((1, d), lambda i: (0, 0)))
        args.append(fw)
    return pl.pallas_call(
        functools.partial(_out_proj_body, n_in=n_in, final=fw is not None),
        grid=(m // tm,),
        in_specs=in_specs,
        out_specs=pl.BlockSpec((tm, d), lambda i: (i, 0)),
        out_shape=jax.ShapeDtypeStruct((m, d), F32),
        compiler_params=_params("arbitrary"),
        name="out_proj")(*args)


def _rope_tables(pos, width):
    half = ROT_DIM // 2
    inv = jnp.power(F32(ROPE_THETA), -jnp.arange(half, dtype=F32) * (2.0 / ROT_DIM))
    ang = pos.astype(F32)[:, None] * inv[None, :]
    cos, sin = jnp.cos(ang), jnp.sin(ang)
    n = pos.shape[0]
    one = jnp.ones((n, ATT_HD - ROT_DIM), F32)
    z8 = jnp.zeros((n, half), F32)
    z48 = jnp.zeros((n, ATT_HD - ROT_DIM), F32)
    c = jnp.concatenate([cos, cos, one], axis=-1)
    sa = jnp.concatenate([-sin, z8, z48], axis=-1)
    sb = jnp.concatenate([z8, sin, z48], axis=-1)
    rep = width // ATT_HD
    return tuple(jnp.tile(t, (1, rep)) for t in (c, sa, sb))


def _rotary(x, c, sa, sb):
    w = x.shape[-1]
    half = ROT_DIM // 2
    return x * c + pltpu.roll(x, w - half, 1) * sa + pltpu.roll(x, half, 1) * sb


def _attn_prompt_body(q_ref, k_ref, v_ref, g_ref, c_ref, sa_ref, sb_ref,
                      att_ref, ko_ref, vo_ref,
                      nat, qd, kd, vd, od, std, o_s, st_s, *, seq):
    nblk = seq // CHUNK
    c, sa, sb = c_ref[...], sa_ref[...], sb_ref[...]
    q = _rotary(q_ref[...], c, sa, sb) * (ATT_HD ** -0.5)
    k = _rotary(k_ref[...], c, sa, sb)
    ko_ref[...] = k.T
    vo_ref[...] = v_ref[...].T
    head_lane = lax.broadcasted_iota(jnp.int32, (1, LANE), 1) // ATT_HD
    nat[0] = q
    nat[1] = k
    nat[2] = v_ref[...]
    zero = jnp.zeros((CHUNK, LANE), BF16)
    for p, d in enumerate(DILATIONS):
        kd[p, 0:CHUNK, :] = zero
        vd[p, 0:CHUNK, :] = zero
        ln = seq // d
        for r in range(d):
            rows = slice(r * ln, (r + 1) * ln)
            krows = slice(CHUNK + r * ln, CHUNK + (r + 1) * ln)
            src = pl.ds(r, ln, stride=d) if d > 1 else slice(None)
            qr = nat[0, src, :]
            qd[p, 0, rows, :] = jnp.where(head_lane == 0, qr, 0.0).astype(BF16)
            qd[p, 1, rows, :] = jnp.where(head_lane == 1, qr, 0.0).astype(BF16)
            kd[p, krows, :] = nat[1, src, :].astype(BF16)
            vd[p, krows, :] = nat[2, src, :].astype(BF16)

    row = lax.broadcasted_iota(jnp.int32, (CHUNK, 2 * CHUNK), 0)
    col = lax.broadcasted_iota(jnp.int32, (CHUNK, 2 * CHUNK), 1)
    band = (col >= row) & (col <= row + CHUNK)
    first_head = lax.broadcasted_iota(jnp.int32, (CHUNK, LANE), 1) < ATT_HD

    for p, d in enumerate(DILATIONS):
        nb = nblk // d

        def block(t, carry, p=p, nb=nb):
            base = pl.multiple_of(t * CHUNK, CHUNK)
            rows = pl.ds(base, CHUNK)
            first = (t % nb) == 0
            valid = band & (col >= jnp.where(first, CHUNK, 0))
            kw = kd[p, pl.ds(base, 2 * CHUNK), :]
            vw = vd[p, pl.ds(base, 2 * CHUNK), :]
            parts = []
            for hh in range(2):
                s = lax.dot_general(qd[p, hh, rows, :], kw, (((1,), (1,)), ((), ())),
                                    preferred_element_type=F32)
                s = jnp.where(valid, s, NEG_INF)
                m = jnp.max(s, axis=-1, keepdims=True)
                e = jnp.exp(s - m)
                o = jnp.dot(e.astype(BF16), vw, preferred_element_type=F32)
                parts.append((o, m, jnp.sum(e, axis=-1, keepdims=True)))
            (o0, m0, l0), (o1, m1, l1) = parts
            wide = lambda t: jnp.broadcast_to(t, (CHUNK, LANE))
            o_dst, st_dst = (o_s.at[p], st_s.at[p]) if nb == nblk else (od, std)
            o_dst[rows, :] = jnp.where(first_head, o0, o1)
            st_dst[0, rows, :] = jnp.where(first_head, wide(m0), wide(m1))
            st_dst[1, rows, :] = jnp.where(first_head, wide(l0), wide(l1))
            return carry

        lax.fori_loop(0, nblk, block, 0, unroll=True)
        if d > 1:
            ln = seq // d
            for r in range(d):
                o_s[p, pl.ds(r, ln, stride=d), :] = od[r * ln:(r + 1) * ln, :]
                st_s[p, 0, pl.ds(r, ln, stride=d), :] = std[0, r * ln:(r + 1) * ln, :]
                st_s[p, 1, pl.ds(r, ln, stride=d), :] = std[1, r * ln:(r + 1) * ln, :]

    np_ = len(DILATIONS)

    def combine(t, carry):
        base = pl.multiple_of(t * CHUNK, CHUNK)
        rows = pl.ds(base, CHUNK)
        ms = [st_s[p, 0, rows, :] for p in range(np_)]
        mx = functools.reduce(jnp.maximum, ms)
        ws = [jnp.exp(mm - mx) for mm in ms]
        num = sum(w * o_s[p, rows, :] for p, w in enumerate(ws))
        den = sum(w * st_s[p, 1, rows, :] for p, w in enumerate(ws))
        att_ref[rows, :] = ((num / den) * _silu(g_ref[rows, :])).astype(att_ref.dtype)
        return carry

    lax.fori_loop(0, nblk, combine, 0, unroll=2)


def _attn_prompt(u, cos, sa, sb, *, batch, seq):
    m = batch * seq
    npair = ATT_W // LANE
    blk = lambda off: pl.BlockSpec((seq, LANE), lambda b, hp, off=off: (b, off + hp))
    tab = pl.BlockSpec((seq, LANE), lambda b, hp: (0, 0))
    out = pl.BlockSpec((seq, LANE), lambda b, hp: (b, hp))
    out_t = pl.BlockSpec((None, LANE, seq), lambda b, hp: (b, hp, 0))
    np_ = len(DILATIONS)
    return pl.pallas_call(
        functools.partial(_attn_prompt_body, seq=seq),
        grid=(batch, npair),
        in_specs=[blk(0), blk(npair), blk(2 * npair), blk(3 * npair), tab, tab, tab],
        out_specs=[out, out_t, out_t],
        out_shape=[jax.ShapeDtypeStruct((m, ATT_W), BF16),
                   jax.ShapeDtypeStruct((batch, ATT_W, seq), F32),
                   jax.ShapeDtypeStruct((batch, ATT_W, seq), F32)],
        scratch_shapes=[pltpu.VMEM((3, seq, LANE), F32),
                        pltpu.VMEM((np_, 2, seq, LANE), BF16),
                        pltpu.VMEM((np_, seq + CHUNK, LANE), BF16),
                        pltpu.VMEM((np_, seq + CHUNK, LANE), BF16),
                        pltpu.VMEM((seq, LANE), F32),
                        pltpu.VMEM((2, seq, LANE), F32),
                        pltpu.VMEM((np_, seq, LANE), F32),
                        pltpu.VMEM((np_, 2, seq, LANE), F32)],
        compiler_params=_params("arbitrary", "arbitrary"),
        name="attn_prompt")(u, u, u, u, cos, sa, sb)


def _ssd_prompt_body(xbc_ref, z_ref, dt_ref, cw_ref, cb_ref, dtb_ref, alog_ref, dsk_ref, nw_ref,
                     y_ref, conv_ref, st_ref, ext, st, ys, *, heads, nchunk):
    c = pl.program_id(1)
    inner = heads * SSD_HEADDIM
    gw = SSD_STATE
    hpg = heads // SSD_GROUPS

    @pl.when(c == 0)
    def _():
        ext[0:SUBLANE, :] = jnp.zeros((SUBLANE, ext.shape[1]), F32)
        st[...] = jnp.zeros(st.shape, F32)

    ext[SUBLANE:SUBLANE + CHUNK, :] = xbc_ref[...]
    xp = ext[...]
    conv = cb_ref[...] + xp[SUBLANE:, :] * cw_ref[CONV_K - 1:CONV_K, :]
    for j in range(CONV_K - 1):
        conv = conv + pltpu.roll(xp, CONV_K - 1 - j, 0)[SUBLANE:, :] * cw_ref[j:j + 1, :]
    act = _silu(conv)

    @pl.when(c == nchunk - 1)
    def _():
        conv_ref[...] = ext[CHUNK + HIST:CHUNK + SUBLANE, :]

    ext[0:SUBLANE, :] = ext[CHUNK:CHUNK + SUBLANE, :]

    dt = _softplus(dt_ref[...] + dtb_ref[...])
    a = -jnp.exp(alog_ref[...])
    tril = _tril(CHUNK)
    acum = _sel_dot(jnp.where(tril, 1.0, 0.0).astype(BF16), dt * a)
    acum_t = acum.T
    last = acum[CHUNK - 1:CHUNK, :]
    elast = jnp.exp(last)
    expand = _seg_matrix(LANE, inner, SSD_HEADDIM, False)
    dt_e = _dot_sel(dt, expand)
    eacum_e = _dot_sel(jnp.exp(acum), expand)
    wend_e = _dot_sel(jnp.exp(last - acum) * dt, expand)
    xs = act[:, 0:inner]
    xdt = xs * dt_e
    xw = xs * wend_e
    hw = hpg * SSD_HEADDIM
    first_half = lax.broadcasted_iota(jnp.int32, (CHUNK, LANE), 1) < SSD_HEADDIM

    for g in range(SSD_GROUPS):
        bm = act[:, inner + g * gw:inner + (g + 1) * gw]
        cm = act[:, inner + SSD_GROUPS * gw + g * gw:inner + SSD_GROUPS * gw + (g + 1) * gw]
        cb = _dot_nt(cm, bm)
        sg = st[g * hw:(g + 1) * hw, :]
        y_inter = _dot_nt(cm, sg) * eacum_e[:, g * hw:(g + 1) * hw]
        s_local = _dot_tn(xw[:, g * hw:(g + 1) * hw], bm)
        for hg in range(hpg):
            h = g * hpg + hg
            rows = slice(hg * SSD_HEADDIM, (hg + 1) * SSD_HEADDIM)
            st[g * hw + hg * SSD_HEADDIM:g * hw + (hg + 1) * SSD_HEADDIM, :] = (
                sg[rows, :] * elast[:, h:h + 1] + s_local[rows, :])
        for pr in range(hpg // 2):
            lo = g * hw + pr * LANE
            xp = xdt[:, lo:lo + LANE].astype(BF16)
            halves = []
            for hh in range(2):
                h = g * hpg + 2 * pr + hh
                seg = acum[:, h:h + 1] - acum_t[h:h + 1, :]
                mh = cb * jnp.exp(jnp.where(tril, seg, NEG_INF))
                halves.append(jnp.dot(mh.astype(BF16), xp, preferred_element_type=F32))
            ys[:, lo:lo + LANE] = (jnp.where(first_half, halves[0], halves[1])
                                   + y_inter[:, pr * LANE:(pr + 1) * LANE])

    yt = (ys[...] + dsk_ref[...] * xs) * _silu(z_ref[...])
    y_ref[...] = _rms(yt, nw_ref[...]).astype(y_ref.dtype)

    @pl.when(c == nchunk - 1)
    def _():
        st_ref[...] = st[...]


def _ssd_prompt(u, cw, cb, dtb, alog, dsk, nw, *, batch, seq, heads, col_z, col_xbc, col_dt):
    nchunk = seq // CHUNK
    inner = heads * SSD_HEADDIM
    cch = cw.shape[1]
    row = lambda b, c: b * nchunk + c
    full = lambda a: pl.BlockSpec(a.shape, lambda b, c: (0, 0))
    return pl.pallas_call(
        functools.partial(_ssd_prompt_body, heads=heads, nchunk=nchunk),
        grid=(batch, nchunk),
        in_specs=[pl.BlockSpec((CHUNK, cch), lambda b, c: (row(b, c), col_xbc // cch)),
                  pl.BlockSpec((CHUNK, inner), lambda b, c: (row(b, c), col_z // inner)),
                  pl.BlockSpec((CHUNK, LANE), lambda b, c: (row(b, c), col_dt // LANE)),
                  full(cw), full(cb), full(dtb), full(alog), full(dsk), full(nw)],
        out_specs=[pl.BlockSpec((CHUNK, inner), lambda b, c: (row(b, c), 0)),
                   pl.BlockSpec((None, CONV_K - 1, cch), lambda b, c: (b, 0, 0)),
                   pl.BlockSpec((None, inner, SSD_STATE), lambda b, c: (b, 0, 0))],
        out_shape=[jax.ShapeDtypeStruct((batch * seq, inner), BF16),
                   jax.ShapeDtypeStruct((batch, CONV_K - 1, cch), F32),
                   jax.ShapeDtypeStruct((batch, inner, SSD_STATE), F32)],
        scratch_shapes=[pltpu.VMEM((CHUNK + SUBLANE, cch), F32),
                        pltpu.VMEM((inner, SSD_STATE), F32),
                        pltpu.VMEM((CHUNK, inner), F32)],
        compiler_params=_params("arbitrary", "arbitrary"),
        name="ssd_prompt")(u, u, u, cw, cb, dtb, alog, dsk, nw)


def _mlstm_prompt_body(q_ref, k_ref, v_ref, o_ref, z_ref, gi_ref, gf_ref,
                       cw_ref, cb_ref, bi_ref, bf_ref, nw_ref,
                       hz_ref, conv_ref, c_out, n_out, m_out,
                       histq, histk, c_s, n_s, m_s, gate_s, *, nchunk, heads, hd):
    c = pl.program_id(1)
    inner = heads * hd

    @pl.when(c == 0)
    def _():
        histq[...] = jnp.zeros((SUBLANE, inner), F32)
        histk[...] = jnp.zeros((SUBLANE, inner), F32)
        c_s[...] = jnp.zeros(c_s.shape, F32)
        n_s[...] = jnp.zeros(n_s.shape, F32)
        m_s[...] = jnp.full(m_s.shape, NEG_INF, F32)

    def conv_head(hist, x_ref, col0, h):
        sl = slice(h * hd, (h + 1) * hd)
        cols = slice(col0 + h * hd, col0 + (h + 1) * hd)
        xp = jnp.concatenate([hist[:, sl], x_ref[:, sl]], axis=0)
        acc = cb_ref[:, cols] + xp[SUBLANE:, :] * cw_ref[CONV_K - 1:CONV_K, cols]
        for j in range(CONV_K - 1):
            acc = acc + pltpu.roll(xp, CONV_K - 1 - j, 0)[SUBLANE:, :] * cw_ref[j:j + 1, cols]
        return _silu(acc)

    @pl.when(c == nchunk - 1)
    def _():
        conv_ref[:, 0:inner] = q_ref[CHUNK - (CONV_K - 1):CHUNK, :]
        conv_ref[:, inner:] = k_ref[CHUNK - (CONV_K - 1):CHUNK, :]

    it = gi_ref[...] + bi_ref[...]
    logf = -_softplus(-(gf_ref[...] + bf_ref[...]))
    tril = _tril(CHUNK)
    bc = _sel_dot(jnp.where(tril, 1.0, 0.0).astype(BF16), logf)
    gate_s[0] = it
    gate_s[1] = bc
    gate_s[2] = it.T
    gate_s[3] = bc.T

    for h in range(heads):
        sl = slice(h * hd, (h + 1) * hd)
        q = conv_head(histq, q_ref, 0, h)
        k = conv_head(histk, k_ref, inner, h) * (hd ** -0.5)
        v = v_ref[:, sl]
        i_col, b_col = gate_s[0, :, h:h + 1], gate_s[1, :, h:h + 1]
        i_row, b_row = gate_s[2, h:h + 1, :], gate_s[3, h:h + 1, :]
        m_prev = m_s[h]
        dmat = jnp.where(tril, b_col - b_row + i_row, NEG_INF)
        inter = b_col + m_prev
        m_t = jnp.maximum(inter, jnp.max(dmat, axis=-1, keepdims=True))
        w_intra = jnp.exp(dmat - m_t)
        w_inter = jnp.exp(inter - m_t)
        att = w_intra * _dot_nt(q, k)
        c_prev = c_s[h]
        n_prev = n_s[h]
        num = _dot(att, v) + w_inter * _dot(q, c_prev)
        qn = _dot_nt(q, jnp.broadcast_to(n_prev, (SUBLANE, hd)))[:, 0:1]
        den = jnp.sum(att, axis=-1, keepdims=True) + w_inter * qn
        hh = num / jnp.maximum(jnp.abs(den), jnp.exp(-m_t))

        b_last = b_col[CHUNK - 1:CHUNK, :]
        logw = b_last - b_col + i_col
        m_new = jnp.maximum(b_last + m_prev, jnp.max(logw, axis=0, keepdims=True))
        ws = jnp.exp(logw - m_new)
        ws_row = jnp.exp(b_last - b_row + i_row - m_new)
        scale = jnp.exp(b_last + m_prev - m_new)
        c_s[h] = scale * c_prev + _dot_tn(k, ws * v)
        n_s[h] = scale * n_prev + _dot(jnp.broadcast_to(ws_row, (SUBLANE, CHUNK)), k)[0:1, :]
        m_s[h] = m_new

        hg = _sigmoid(o_ref[:, sl]) * hh
        mu = jnp.mean(hg, axis=-1, keepdims=True)
        var = jnp.mean(jnp.square(hg - mu), axis=-1, keepdims=True)
        hn = (hg - mu) * lax.rsqrt(var + EPS) * nw_ref[:, sl]
        hz_ref[:, sl] = (hn * _silu(z_ref[:, sl])).astype(hz_ref.dtype)

    histq[...] = q_ref[CHUNK - SUBLANE:CHUNK, :]
    histk[...] = k_ref[CHUNK - SUBLANE:CHUNK, :]

    @pl.when(c == nchunk - 1)
    def _():
        c_out[...] = c_s[...]
        n_out[...] = n_s[...]
        m_out[...] = m_s[...]


def _mlstm_prompt(u, cw, cb, bi, bf, nw, *, batch, seq, heads, hd, col_gates):
    nchunk = seq // CHUNK
    inner = heads * hd
    row = lambda b, c: b * nchunk + c
    ublk = lambda j: pl.BlockSpec((CHUNK, inner), lambda b, c, j=j: (row(b, c), j))
    gblk = lambda off: pl.BlockSpec((CHUNK, LANE), lambda b, c, off=off: (row(b, c), col_gates // LANE + off))
    full = lambda a: pl.BlockSpec(a.shape, lambda b, c: (0, 0))
    return pl.pallas_call(
        functools.partial(_mlstm_prompt_body, nchunk=nchunk, heads=heads, hd=hd),
        grid=(batch, nchunk),
        in_specs=[ublk(0), ublk(1), ublk(2), ublk(3), ublk(4), gblk(0), gblk(1),
                  full(cw), full(cb), full(bi), full(bf), full(nw)],
        out_specs=[pl.BlockSpec((CHUNK, inner), lambda b, c: (row(b, c), 0)),
                   pl.BlockSpec((None, CONV_K - 1, 2 * inner), lambda b, c: (b, 0, 0)),
                   pl.BlockSpec((None, heads, hd, hd), lambda b, c: (b, 0, 0, 0)),
                   pl.BlockSpec((None, heads, 1, hd), lambda b, c: (b, 0, 0, 0)),
                   pl.BlockSpec((None, heads, 1, 1), lambda b, c: (b, 0, 0, 0))],
        out_shape=[jax.ShapeDtypeStruct((batch * seq, inner), BF16),
                   jax.ShapeDtypeStruct((batch, CONV_K - 1, 2 * inner), F32),
                   jax.ShapeDtypeStruct((batch, heads, hd, hd), F32),
                   jax.ShapeDtypeStruct((batch, heads, 1, hd), F32),
                   jax.ShapeDtypeStruct((batch, heads, 1, 1), F32)],
        scratch_shapes=[pltpu.VMEM((SUBLANE, inner), F32),
                        pltpu.VMEM((SUBLANE, inner), F32),
                        pltpu.VMEM((heads, hd, hd), F32),
                        pltpu.VMEM((heads, 1, hd), F32),
                        pltpu.VMEM((heads, 1, 1), F32),
                        pltpu.VMEM((4, CHUNK, LANE), F32)],
        compiler_params=_params("arbitrary", "arbitrary"),
        name="mlstm_prompt")(u, u, u, u, u, u, u, cw, cb, bi, bf, nw)


def _rot_sample_body(q_ref, k_ref, v_ref, g_ref, c_ref, sa_ref, sb_ref, xt_ref):
    c, sa, sb = c_ref[...], sa_ref[...], sb_ref[...]
    q = _rotary(q_ref[...], c, sa, sb) * (ATT_HD ** -0.5)
    k = _rotary(k_ref[...], c, sa, sb)
    xt_ref[...] = jnp.concatenate([q, k, v_ref[...], g_ref[...]], axis=-1).T


def _rot_sample(u, cos, sa, sb):
    bs = u.shape[0]
    ublk = lambda off: pl.BlockSpec((bs, ATT_W), lambda i, off=off: (0, off))
    tab = pl.BlockSpec((1, ATT_W), lambda i: (0, 0))
    return pl.pallas_call(
        _rot_sample_body,
        grid=(1,),
        in_specs=[ublk(0), ublk(1), ublk(2), ublk(3), tab, tab, tab],
        out_specs=pl.BlockSpec((4 * ATT_W, bs), lambda i: (0, 0)),
        out_shape=jax.ShapeDtypeStruct((4 * ATT_W, bs), F32),
        compiler_params=_params("arbitrary"),
        name="rot_sample")(u, u, u, u, cos, sa, sb)


def _key_multiplicity(wb):
    back = wb - lax.broadcasted_iota(jnp.int32, (1, wb), 1)
    cnt = jnp.zeros((1, wb), F32)
    for d in DILATIONS:
        cnt = cnt + ((back <= CHUNK * d) & (lax.rem(back, d) == 0)).astype(F32)
    return cnt


def _attn_sample_body(x_ref, kc_ref, vc_ref, att_ref, *, bb, wb):
    i = pl.program_id(0)
    bs = x_ref.shape[1]
    nt = wb // LANE

    @pl.when(i == 0)
    def _():
        att_ref[...] = jnp.zeros(att_ref.shape, F32)

    cnt = _key_multiplicity(wb)
    valid = cnt > 0.0
    npat = float(len(DILATIONS))
    rid = lax.broadcasted_iota(jnp.int32, (bs, LANE), 0)
    lid = lax.broadcasted_iota(jnp.int32, (ATT_HD, bs), 1)
    for t in range(bb):
        b = i * bb + t
        cols = _dot_sel(x_ref[...], (rid == b).astype(BF16))
        for h in range(ATT_HEADS):
            lo = h * ATT_HD
            qc = cols[lo:lo + ATT_HD]
            kc = cols[ATT_W + lo:ATT_W + lo + ATT_HD]
            vc = cols[2 * ATT_W + lo:2 * ATT_W + lo + ATT_HD]
            gc = cols[3 * ATT_W + lo:3 * ATT_W + lo + ATT_HD]
            kt = kc_ref[t, h]
            vt = vc_ref[t, h]
            s = jnp.sum(kt * jnp.tile(qc, (1, nt)), axis=0, keepdims=True)
            s = jnp.where(valid, s, NEG_INF)
            s_self = jnp.sum(qc * kc, axis=0, keepdims=True)[:, 0:1]
            m = jnp.maximum(jnp.max(s, axis=1, keepdims=True), s_self)
            p = cnt * jnp.exp(s - m)
            p_self = npat * jnp.exp(s_self - m)
            den = jnp.sum(p, axis=1, keepdims=True) + p_self
            acc = vt[:, 0:LANE] * p[:, 0:LANE]
            for j in range(1, nt):
                acc = acc + vt[:, j * LANE:(j + 1) * LANE] * p[:, j * LANE:(j + 1) * LANE]
            o = jnp.sum(acc, axis=1, keepdims=True) + p_self * vc[:, 0:1]
            o = o / den * _silu(gc[:, 0:1])
            att_ref[lo:lo + ATT_HD, :] = jnp.where(lid == b, o, att_ref[lo:lo + ATT_HD, :])


def _attn_sample(xt, ck, cv, *, bb):
    bs, wb = ck.shape[0], ck.shape[3]
    cache = pl.BlockSpec((bb, ATT_HEADS, ATT_HD, wb), lambda i: (i, 0, 0, 0))
    return pl.pallas_call(
        functools.partial(_attn_sample_body, bb=bb, wb=wb),
        grid=(bs // bb,),
        in_specs=[pl.BlockSpec(xt.shape, lambda i: (0, 0)), cache, cache],
        out_specs=pl.BlockSpec((ATT_W, bs), lambda i: (0, 0)),
        out_shape=jax.ShapeDtypeStruct((ATT_W, bs), F32),
        compiler_params=_params("arbitrary"),
        name="attn_sample")(xt, ck, cv)


def _ssd_sample_prep_body(xbc_ref, dt_ref, cst_ref, cw_ref, cb_ref, dtb_ref, alog_ref, aloge_ref, dsk_ref,
                          conv_ref, yloc_ref, xdtt_ref, bc_ref, da_ref, dae_ref, *, heads):
    inner = heads * SSD_HEADDIM
    cch = xbc_ref.shape[1]
    gw = SSD_STATE
    x = xbc_ref[...]
    acc = cb_ref[...] + x * cw_ref[CONV_K - 1:CONV_K, :]
    for j in range(CONV_K - 1):
        acc = acc + cst_ref[:, j * cch:(j + 1) * cch] * cw_ref[j:j + 1, :]
    for j in range(CONV_K - 2):
        conv_ref[:, j * cch:(j + 1) * cch] = cst_ref[:, (j + 1) * cch:(j + 2) * cch]
    conv_ref[:, (CONV_K - 2) * cch:(CONV_K - 1) * cch] = x
    act = _silu(acc)
    xs = act[:, 0:inner]
    bc_ref[...] = act[:, inner:]
    dt = _softplus(dt_ref[...] + dtb_ref[...])
    da_ref[...] = jnp.exp(dt * (-jnp.exp(alog_ref[...])))
    expand = _seg_matrix(LANE, inner, SSD_HEADDIM, False)
    dte = _dot_sel(dt, expand)
    dae_ref[...] = jnp.exp(dte * (-jnp.exp(aloge_ref[...])))
    xdt = xs * dte
    xdtt_ref[...] = xdt.T
    hw = inner // SSD_GROUPS
    parts = []
    for g in range(SSD_GROUPS):
        bm = act[:, inner + g * gw:inner + (g + 1) * gw]
        cm = act[:, inner + SSD_GROUPS * gw + g * gw:inner + SSD_GROUPS * gw + (g + 1) * gw]
        cbg = jnp.sum(cm * bm, axis=-1, keepdims=True)
        parts.append(cbg * xdt[:, g * hw:(g + 1) * hw])
    yloc_ref[...] = jnp.concatenate(parts, axis=-1) + dsk_ref[...] * xs


def _ssd_sample_state_body(da_ref, s_ref, xdtt_ref, bc_ref, so_ref, yi_ref, *, bb, heads):
    i = pl.program_id(0)
    bs = bc_ref.shape[0]
    inner = heads * SSD_HEADDIM
    hw = inner // SSD_GROUPS
    hpg = heads // SSD_GROUPS
    gw = SSD_STATE
    rid = lax.broadcasted_iota(jnp.int32, (bs, gw), 0)
    for t in range(bb):
        b = i * bb + t
        parts = []
        for g in range(SSD_GROUPS):
            mg = jnp.where(rid == b, bc_ref[:, g * gw:(g + 1) * gw], 0.0)
            sl = _dot(xdtt_ref[g * hw:(g + 1) * hw, :], mg)
            crow = _row_at(bc_ref, b, slice(SSD_GROUPS * gw + g * gw, SSD_GROUPS * gw + (g + 1) * gw))
            sg = s_ref[t, g * hw:(g + 1) * hw, :]
            parts.append(_dot_nt(jnp.broadcast_to(crow, (SUBLANE, gw)), sg)[0:1, :])
            for hg in range(hpg):
                h = g * hpg + hg
                lo = hg * SSD_HEADDIM
                so_ref[t, h * SSD_HEADDIM:(h + 1) * SSD_HEADDIM, :] = (
                    sg[lo:lo + SSD_HEADDIM, :] * da_ref[b, h] + sl[lo:lo + SSD_HEADDIM, :])
        yi_ref[t] = jnp.concatenate(parts, axis=-1)


def _ssd_sample_finish_body(yloc_ref, yi_ref, dae_ref, z_ref, nw_ref, y_ref):
    y = (yloc_ref[...] + yi_ref[...] * dae_ref[...]) * _silu(z_ref[...])
    y_ref[...] = _rms(y, nw_ref[...])


def _ssd_sample(u, cst, state, cw, cb, dtb, alog, aloge, dsk, nw, *, heads, col_z, col_xbc, col_dt, bb):
    bs = u.shape[0]
    inner = heads * SSD_HEADDIM
    cch = cw.shape[1]
    ncs = (CONV_K - 1) * cch
    bcw = 2 * SSD_GROUPS * SSD_STATE
    full = lambda a: pl.BlockSpec(a.shape, lambda i: (0,) * a.ndim)
    conv, yloc, xdtt, bc, da, dae = pl.pallas_call(
        functools.partial(_ssd_sample_prep_body, heads=heads),
        grid=(1,),
        in_specs=[pl.BlockSpec((bs, cch), lambda i: (0, col_xbc // cch)),
                  pl.BlockSpec((bs, LANE), lambda i: (0, col_dt // LANE)),
                  full(cst), full(cw), full(cb), full(dtb), full(alog), full(aloge), full(dsk)],
        out_specs=[pl.BlockSpec((bs, ncs), lambda i: (0, 0)),
                   pl.BlockSpec((bs, inner), lambda i: (0, 0)),
                   pl.BlockSpec((inner, bs), lambda i: (0, 0)),
                   pl.BlockSpec((bs, bcw), lambda i: (0, 0)),
                   pl.BlockSpec((bs, LANE), lambda i: (0, 0)),
                   pl.BlockSpec((bs, inner), lambda i: (0, 0))],
        out_shape=[jax.ShapeDtypeStruct((bs, ncs), F32),
                   jax.ShapeDtypeStruct((bs, inner), F32),
                   jax.ShapeDtypeStruct((inner, bs), F32),
                   jax.ShapeDtypeStruct((bs, bcw), F32),
                   jax.ShapeDtypeStruct((bs, LANE), F32),
                   jax.ShapeDtypeStruct((bs, inner), F32)],
        compiler_params=_params("arbitrary"),
        name="ssd_sample_prep")(u, u, cst, cw, cb, dtb, alog, aloge, dsk)
    new_state, yi = pl.pallas_call(
        functools.partial(_ssd_sample_state_body, bb=bb, heads=heads),
        grid=(bs // bb,),
        in_specs=[pl.BlockSpec(memory_space=pltpu.SMEM),
                  pl.BlockSpec((bb, inner, SSD_STATE), lambda i: (i, 0, 0)),
                  full(xdtt), full(bc)],
        out_specs=[pl.BlockSpec((bb, inner, SSD_STATE), lambda i: (i, 0, 0)),
                   pl.BlockSpec((bb, 1, inner), lambda i: (i, 0, 0))],
        out_shape=[jax.ShapeDtypeStruct((bs, inner, SSD_STATE), F32),
                   jax.ShapeDtypeStruct((bs, 1, inner), F32)],
        compiler_params=_params("arbitrary"),
        name="ssd_sample_state")(da[:, :heads], state, xdtt, bc)
    y = pl.pallas_call(
        _ssd_sample_finish_body,
        grid=(1,),
        in_specs=[full(yloc), pl.BlockSpec((bs, inner), lambda i: (0, 0)), full(dae),
                  pl.BlockSpec((bs, inner), lambda i: (0, col_z // inner)), full(nw)],
        out_specs=pl.BlockSpec((bs, inner), lambda i: (0, 0)),
        out_shape=jax.ShapeDtypeStruct((bs, inner), F32),
        compiler_params=_params("arbitrary"),
        name="ssd_sample_finish")(yloc, yi.reshape(bs, inner), dae, u, nw)
    return y, conv, new_state


def _mlstm_sample_prep_body(q_ref, k_ref, gi_ref, gf_ref, cst_ref, cw_ref, cb_ref, bi_ref, bf_ref, m_ref,
                            conv_ref, qk_ref, kwt_ref, wi_ref, wf_ref, mt_ref, *, heads, hd):
    inner = heads * hd
    cch = 2 * inner
    x = jnp.concatenate([q_ref[...], k_ref[...]], axis=-1)
    acc = cb_ref[...] + x * cw_ref[CONV_K - 1:CONV_K, :]
    for j in range(CONV_K - 1):
        acc = acc + cst_ref[:, j * cch:(j + 1) * cch] * cw_ref[j:j + 1, :]
    for j in range(CONV_K - 2):
        conv_ref[:, j * cch:(j + 1) * cch] = cst_ref[:, (j + 1) * cch:(j + 2) * cch]
    conv_ref[:, (CONV_K - 2) * cch:(CONV_K - 1) * cch] = x
    act = _silu(acc)
    q = act[:, 0:inner]
    k = act[:, inner:] * (hd ** -0.5)
    it = gi_ref[...] + bi_ref[...]
    inter = -_softplus(-(gf_ref[...] + bf_ref[...])) + m_ref[...]
    mt = jnp.maximum(inter, it)
    wi = jnp.exp(it - mt)
    wf = jnp.exp(inter - mt)
    wi_ref[...] = wi
    wf_ref[...] = wf
    mt_ref[...] = mt
    qk_ref[:, 0:inner] = q
    qk_ref[:, inner:] = k
    kw = jnp.concatenate([k[:, h * hd:(h + 1) * hd] * wi[:, h:h + 1] for h in range(heads)], axis=-1)
    kwt_ref[...] = kw.T


def _mlstm_sample_state_body(wf_ref, c_ref, qk_ref, v_ref, kwt_ref, co_ref, qc_ref, *, bb, heads, hd):
    i = pl.program_id(0)
    bs = v_ref.shape[0]
    rid = lax.broadcasted_iota(jnp.int32, (bs, hd), 0)
    for t in range(bb):
        b = i * bb + t
        parts = []
        for h in range(heads):
            cp = c_ref[t, h]
            qrow = _row_at(qk_ref, b, slice(h * hd, (h + 1) * hd))
            parts.append(_dot(jnp.broadcast_to(qrow, (SUBLANE, hd)), cp)[0:1, :])
            mh = jnp.where(rid == b, v_ref[:, h * hd:(h + 1) * hd], 0.0)
            co_ref[t, h] = cp * wf_ref[b, h] + _dot(kwt_ref[h * hd:(h + 1) * hd, :], mh)
        qc_ref[t] = jnp.concatenate(parts, axis=-1)


def _mlstm_sample_finish_body(qk_ref, v_ref, o_ref, z_ref, qc_ref, n_ref, wi_ref, wf_ref, mt_ref, nw_ref,
                              hz_ref, no_ref, *, heads, hd):
    inner = heads * hd
    for h in range(heads):
        sl = slice(h * hd, (h + 1) * hd)
        q = qk_ref[:, sl]
        k = qk_ref[:, inner + h * hd:inner + (h + 1) * hd]
        wi = wi_ref[:, h:h + 1]
        wf = wf_ref[:, h:h + 1]
        mt = mt_ref[:, h:h + 1]
        n_prev = n_ref[:, sl]
        att = wi * jnp.sum(q * k, axis=-1, keepdims=True)
        num = att * v_ref[:, sl] + wf * qc_ref[:, sl]
        den = att + wf * jnp.sum(q * n_prev, axis=-1, keepdims=True)
        hh = num / jnp.maximum(jnp.abs(den), jnp.exp(-mt))
        hg = _sigmoid(o_ref[:, sl]) * hh
        mu = jnp.mean(hg, axis=-1, keepdims=True)
        var = jnp.mean(jnp.square(hg - mu), axis=-1, keepdims=True)
        hn = (hg - mu) * lax.rsqrt(var + EPS) * nw_ref[:, sl]
        hz_ref[:, sl] = hn * _silu(z_ref[:, sl])
        no_ref[:, sl] = wf * n_prev + wi * k


def _mlstm_sample(u, cst, c0, n0, m0p, cw, cb, bi, bf, nw, *, heads, hd, col_gates, bb):
    bs = u.shape[0]
    inner = heads * hd
    ncs = (CONV_K - 1) * 2 * inner
    full = lambda a: pl.BlockSpec(a.shape, lambda i: (0,) * a.ndim)
    ucol = lambda j: pl.BlockSpec((bs, inner), lambda i, j=j: (0, j))
    gcol = lambda j: pl.BlockSpec((bs, LANE), lambda i, j=j: (0, col_gates // LANE + j))
    tile = jax.ShapeDtypeStruct((bs, LANE), F32)
    conv, qk, kwt, wi, wf, mt = pl.pallas_call(
        functools.partial(_mlstm_sample_prep_body, heads=heads, hd=hd),
        grid=(1,),
        in_specs=[ucol(0), ucol(1), gcol(0), gcol(1), full(cst), full(cw), full(cb), full(bi), full(bf),
                  full(m0p)],
        out_specs=[pl.BlockSpec((bs, ncs), lambda i: (0, 0)),
                   pl.BlockSpec((bs, 2 * inner), lambda i: (0, 0)),
                   pl.BlockSpec((inner, bs), lambda i: (0, 0)),
                   pl.BlockSpec((bs, LANE), lambda i: (0, 0)),
                   pl.BlockSpec((bs, LANE), lambda i: (0, 0)),
                   pl.BlockSpec((bs, LANE), lambda i: (0, 0))],
        out_shape=[jax.ShapeDtypeStruct((bs, ncs), F32),
                   jax.ShapeDtypeStruct((bs, 2 * inner), F32),
                   jax.ShapeDtypeStruct((inner, bs), F32),
                   tile, tile, tile],
        compiler_params=_params("arbitrary"),
        name="mlstm_sample_prep")(u, u, u, u, cst, cw, cb, bi, bf, m0p)
    c_new, qc = pl.pallas_call(
        functools.partial(_mlstm_sample_state_body, bb=bb, heads=heads, hd=hd),
        grid=(bs // bb,),
        in_specs=[pl.BlockSpec(memory_space=pltpu.SMEM),
                  pl.BlockSpec((bb, heads, hd, hd), lambda i: (i, 0, 0, 0)),
                  full(qk), pl.BlockSpec((bs, inner), lambda i: (0, 2)), full(kwt)],
        out_specs=[pl.BlockSpec((bb, heads, hd, hd), lambda i: (i, 0, 0, 0)),
                   pl.BlockSpec((bb, 1, inner), lambda i: (i, 0, 0))],
        out_shape=[jax.ShapeDtypeStruct((bs, heads, hd, hd), F32),
                   jax.ShapeDtypeStruct((bs, 1, inner), F32)],
        compiler_params=_params("arbitrary"),
        name="mlstm_sample_state")(wf[:, :heads], c0, qk, u, kwt)
    hz, n_new = pl.pallas_call(
        functools.partial(_mlstm_sample_finish_body, heads=heads, hd=hd),
        grid=(1,),
        in_specs=[full(qk), ucol(2), ucol(3), ucol(4), pl.BlockSpec((bs, inner), lambda i: (0, 0)),
                  full(n0), full(wi), full(wf), full(mt), full(nw)],
        out_specs=[pl.BlockSpec((bs, inner), lambda i: (0, 0)),
                   pl.BlockSpec((bs, inner), lambda i: (0, 0))],
        out_shape=[jax.ShapeDtypeStruct((bs, inner), F32)] * 2,
        compiler_params=_params("arbitrary"),
        name="mlstm_sample_finish")(qk, u, u, u, qc.reshape(bs, inner), n0, wi, wf, mt, nw)
    return hz, conv, c_new, n_new, mt[:, :heads]


def _pad_cols(w, n):
    return jnp.pad(w, ((0, 0), (0, n - w.shape[1])))


def _row(v, n=None):
    v = v.reshape(1, -1)
    return v if n is None else _pad_cols(v, n)


def kernel(x_prompt, x_sample, cache_attn_k, cache_attn_v, state_ssd_conv, state_ssd, state_mlstm_conv, state_mlstm_c, state_mlstm_n, state_mlstm_m, norm_w, final_norm_w, w_in_even, w_out_even, ssd_conv_w, ssd_conv_b, ssd_dt_bias, ssd_a_log, ssd_d, ssd_norm_w, w_in_odd, w_out_odd, mlstm_conv_w, mlstm_conv_b, mlstm_igate_b, mlstm_fgate_b, mlstm_norm_w):
    batch, seq, d_model = x_prompt.shape
    bs = x_sample.shape[0]
    ssd_heads = ssd_a_log.shape[1]
    ssd_inner = ssd_heads * SSD_HEADDIM
    ssd_cch = ssd_conv_w.shape[2]
    m_inner = mlstm_norm_w.shape[1]
    m_hd = m_inner // MLSTM_HEADS
    mp = batch * seq

    col_z = 4 * ATT_W
    col_xbc = col_z + ssd_inner
    col_dt = col_xbc + ssd_cch
    n_even = col_dt + LANE
    w_in0 = _pad_cols(w_in_even[0], n_even).astype(BF16)
    w_out0 = w_out_even[0].astype(BF16)
    nw0 = _row(norm_w[0])
    cw0, cb0 = ssd_conv_w[0], _row(ssd_conv_b[0])
    dtb = _row(ssd_dt_bias[0], LANE)
    alog = _row(ssd_a_log[0], LANE)
    aloge = _row(jnp.repeat(ssd_a_log[0], SSD_HEADDIM))
    dsk = _row(jnp.repeat(ssd_d[0], SSD_HEADDIM))
    snw = _row(ssd_norm_w[0])

    hp = x_prompt.reshape(mp, d_model)
    hs = x_sample.reshape(bs, d_model)

    up = _norm_matmul(hp, nw0, w_in0, tm=512, panels=1)
    cos_p, sa_p, sb_p = _rope_tables(jnp.arange(seq), LANE)
    att_p, k_p, v_p = _attn_prompt(up, cos_p, sa_p, sb_p, batch=batch, seq=seq)
    y_p, conv_p, st_p = _ssd_prompt(up, cw0, cb0, dtb, alog, dsk, snw, batch=batch, seq=seq,
                                    heads=ssd_heads, col_z=col_z, col_xbc=col_xbc, col_dt=col_dt)
    hp = _out_proj(hp, [att_p, y_p], [w_out0[:ATT_W], w_out0[ATT_W:]], tm=1024)

    us = _norm_matmul(hs, nw0, w_in0, tm=bs, panels=1)
    cos_s, sa_s, sb_s = _rope_tables(PAST_LEN + jnp.arange(1), ATT_W)
    xt_s = _rot_sample(us, cos_s, sa_s, sb_s)
    k_s = xt_s[ATT_W:2 * ATT_W].T
    v_s = xt_s[2 * ATT_W:3 * ATT_W].T
    pos_minor = lambda cache: jnp.transpose(cache, (0, 2, 3, 1))
    att_s = _attn_sample(xt_s, pos_minor(cache_attn_k[0]), pos_minor(cache_attn_v[0]), bb=2).T
    y_s, conv_s, st_s = _ssd_sample(us, state_ssd_conv[0].reshape(bs, -1),
                                    state_ssd[0].reshape(bs, ssd_inner, SSD_STATE),
                                    cw0, cb0, dtb, alog, aloge, dsk, snw, heads=ssd_heads,
                                    col_z=col_z, col_xbc=col_xbc, col_dt=col_dt, bb=8)
    hs = _out_proj(hs, [att_s, y_s], [w_out0[:ATT_W], w_out0[ATT_W:]], tm=bs)

    wo = w_in_odd[0]
    gates_at = 4 * m_inner
    zcol = gates_at + 2 * MLSTM_HEADS
    col_gates = 5 * m_inner
    zpad = jnp.zeros((d_model, LANE - MLSTM_HEADS), wo.dtype)
    w_in1 = jnp.concatenate([wo[:, :gates_at], wo[:, zcol:],
                             wo[:, gates_at:gates_at + MLSTM_HEADS], zpad,
                             wo[:, gates_at + MLSTM_HEADS:zcol], zpad], axis=1).astype(BF16)
    w_out1 = w_out_odd[0].astype(BF16)
    nw1 = _row(norm_w[1])
    cw1, cb1 = mlstm_conv_w[0], _row(mlstm_conv_b[0])
    bi = _row(mlstm_igate_b[0], LANE)
    bf = _row(mlstm_fgate_b[0], LANE)
    mnw = _row(mlstm_norm_w[0])
    fnw = _row(final_norm_w)

    up1 = _norm_matmul(hp, nw1, w_in1, tm=512, panels=2)
    hz_p, mconv_p, c_p, n_p, m_p = _mlstm_prompt(up1, cw1, cb1, bi, bf, mnw, batch=batch, seq=seq,
                                                   heads=MLSTM_HEADS, hd=m_hd, col_gates=col_gates)
    y_prompt = _out_proj(hp, [hz_p], [w_out1], fnw, tm=1024)

    us1 = _norm_matmul(hs, nw1, w_in1, tm=bs, panels=2)
    m0p = _pad_cols(state_mlstm_m[0], LANE)
    hz_s, mconv_s, c_s, n_s, m_s = _mlstm_sample(us1, state_mlstm_conv[0].reshape(bs, -1),
                                                 state_mlstm_c[0], state_mlstm_n[0].reshape(bs, m_inner),
                                                 m0p, cw1, cb1, bi, bf, mnw,
                                                 heads=MLSTM_HEADS, hd=m_hd, col_gates=col_gates, bb=4)
    y_sample = _out_proj(hs, [hz_s], [w_out1], fnw, tm=bs)

    tmax = min(seq, CHUNK * max(DILATIONS))
    kv_out = lambda t: jnp.transpose(t.reshape(batch, ATT_HEADS, ATT_HD, seq), (0, 3, 1, 2))[None, :, seq - tmax:]
    return (
        y_prompt.reshape(batch, seq, d_model),
        y_sample.reshape(bs, 1, d_model),
        kv_out(k_p),
        kv_out(v_p),
        conv_p[None],
        st_p.reshape(1, batch, ssd_heads, SSD_HEADDIM, SSD_STATE),
        mconv_p[None],
        c_p[None],
        n_p.reshape(1, batch, MLSTM_HEADS, m_hd),
        m_p.reshape(1, batch, MLSTM_HEADS),
        k_s.reshape(1, bs, 1, ATT_HEADS, ATT_HD),
        v_s.reshape(1, bs, 1, ATT_HEADS, ATT_HD),
        conv_s.reshape(1, bs, CONV_K - 1, ssd_cch),
        st_s.reshape(1, bs, ssd_heads, SSD_HEADDIM, SSD_STATE),
        mconv_s.reshape(1, bs, CONV_K - 1, 2 * m_inner),
        c_s[None],
        n_s.reshape(1, bs, MLSTM_HEADS, m_hd),
        m_s.reshape(1, bs, MLSTM_HEADS),
    )
```

```python
import functools

import jax
import jax.numpy as jnp
from jax import lax
from jax.experimental import pallas as pl
from jax.experimental.pallas import tpu as pltpu

F32 = jnp.float32
BF16 = jnp.bfloat16
NEG_INF = float("-inf")
EPS = 1e-6
LANE = 128
SUBLANE = 8
CHUNK = 128
VMEM_LIMIT = 56 * 1024 * 1024

CONV_K = 4
HIST = SUBLANE - (CONV_K - 1)
ATT_HEADS = 8
ATT_HD = 64
ATT_W = ATT_HEADS * ATT_HD
ROT_DIM = ATT_HD // 4
ROPE_THETA = 500000.0
DILATIONS = (1, 4, 16)
PAST_LEN = 2048
SSD_HEADDIM = 64
SSD_GROUPS = 2
SSD_STATE = 128
MLSTM_HEADS = 8


def _params(*sem):
    return pltpu.CompilerParams(dimension_semantics=sem, vmem_limit_bytes=VMEM_LIMIT)


def _chunks(n, w):
    out, c = [], 0
    while c < n:
        out.append((c, min(w, n - c)))
        c += w
    return out


def _dot(a, b):
    return jnp.dot(a.astype(BF16), b.astype(BF16), preferred_element_type=F32)


def _dot_nt(a, b):
    return lax.dot_general(a.astype(BF16), b.astype(BF16), (((1,), (1,)), ((), ())),
                           preferred_element_type=F32)


def _dot_tn(a, b):
    return lax.dot_general(a.astype(BF16), b.astype(BF16), (((0,), (0,)), ((), ())),
                           preferred_element_type=F32)


def _split3(x):
    hi = x.astype(BF16)
    r1 = x - hi.astype(F32)
    mid = r1.astype(BF16)
    lo = (r1 - mid.astype(F32)).astype(BF16)
    return hi, mid, lo


def _sel_dot(sel, x):
    hi, mid, lo = _split3(x)
    d = lambda p: jnp.dot(sel, p, preferred_element_type=F32)
    return d(hi) + d(mid) + d(lo)


def _dot_sel(x, sel):
    hi, mid, lo = _split3(x)
    d = lambda p: jnp.dot(p, sel, preferred_element_type=F32)
    return d(hi) + d(mid) + d(lo)


def _tril(n):
    r = lax.broadcasted_iota(jnp.int32, (n, n), 0)
    c = lax.broadcasted_iota(jnp.int32, (n, n), 1)
    return r >= c


def _seg_matrix(rows, cols, seg, along_rows):
    r = lax.broadcasted_iota(jnp.int32, (rows, cols), 0)
    c = lax.broadcasted_iota(jnp.int32, (rows, cols), 1)
    m = (r // seg == c) if along_rows else (c // seg == r)
    return m.astype(BF16)


def _row_at(ref, b, cols=slice(None)):
    base = pl.multiple_of((b // SUBLANE) * SUBLANE, SUBLANE)
    tile = ref[pl.ds(base, SUBLANE), cols]
    sub = lax.broadcasted_iota(jnp.int32, tile.shape, 0)
    return jnp.sum(jnp.where(sub == b % SUBLANE, tile, 0.0), axis=0, keepdims=True)


def _silu(x):
    h = 0.5 * x
    return h + h * jnp.tanh(h)


def _sigmoid(x):
    return 0.5 + 0.5 * jnp.tanh(0.5 * x)


def _softplus(x):
    return jnp.maximum(x, 0.0) + jnp.log1p(jnp.exp(-jnp.abs(x)))


def _rms(x, w):
    return x * lax.rsqrt(jnp.mean(x * x, axis=-1, keepdims=True) + EPS) * w


def _norm_matmul_body(x_ref, nw_ref, w_ref, o_ref, *, chunks):
    xn = _rms(x_ref[...], nw_ref[...]).astype(BF16)
    for c0, cw in chunks:
        o_ref[:, c0:c0 + cw] = jnp.dot(xn, w_ref[:, c0:c0 + cw], preferred_element_type=F32)


def _norm_matmul(x, nw, w, *, tm, panels):
    m, d = x.shape
    n = w.shape[1]
    pn = n // panels
    return pl.pallas_call(
        functools.partial(_norm_matmul_body, chunks=_chunks(pn, 512)),
        grid=(panels, m // tm),
        in_specs=[pl.BlockSpec((tm, d), lambda p, i: (i, 0)),
                  pl.BlockSpec((1, d), lambda p, i: (0, 0)),
                  pl.BlockSpec((d, pn), lambda p, i: (0, p))],
        out_specs=pl.BlockSpec((tm, pn), lambda p, i: (i, p)),
        out_shape=jax.ShapeDtypeStruct((m, n), F32),
        compiler_params=_params("arbitrary", "arbitrary"),
        name="norm_matmul")(x, nw, w)


def _out_proj_body(*refs, n_in, final):
    h_ref = refs[0]
    xs = refs[1:1 + n_in]
    ws = refs[1 + n_in:1 + 2 * n_in]
    rest = refs[1 + 2 * n_in:]
    acc = h_ref[...]
    for x_ref, w_ref in zip(xs, ws):
        acc = acc + jnp.dot(x_ref[...].astype(BF16), w_ref[...], preferred_element_type=F32)
    if final:
        fw_ref, o_ref = rest
        o_ref[...] = _rms(acc, fw_ref[...])
    else:
        (o_ref,) = rest
        o_ref[...] = acc


def _out_proj(h, xs, ws, fw=None, *, tm):
    m, d = h.shape
    n_in = len(xs)
    in_specs = [pl.BlockSpec((tm, d), lambda i: (i, 0))]
    in_specs += [pl.BlockSpec((tm, x.shape[1]), lambda i: (i, 0)) for x in xs]
    in_specs += [pl.BlockSpec(w.shape, lambda i: (0, 0)) for w in ws]
    args = [h, *xs, *ws]
    if fw is not None:
        in_specs.append(pl.BlockSpec((1, d), lambda i: (0, 0)))
        args.append(fw)
    return pl.pallas_call(
        functools.partial(_out_proj_body, n_in=n_in, final=fw is not None),
        grid=(m // tm,),
        in_specs=in_specs,
        out_specs=pl.BlockSpec((tm, d), lambda i: (i, 0)),
        out_shape=jax.ShapeDtypeStruct((m, d), F32),
        compiler_params=_params("arbitrary"),
        name="out_proj")(*args)


def _rope_tables(pos, width):
    half = ROT_DIM // 2
    inv = jnp.power(F32(ROPE_THETA), -jnp.arange(half, dtype=F32) * (2.0 / ROT_DIM))
    ang = pos.astype(F32)[:, None] * inv[None, :]
    cos, sin = jnp.cos(ang), jnp.sin(ang)
    n = pos.shape[0]
    one = jnp.ones((n, ATT_HD - ROT_DIM), F32)
    z8 = jnp.zeros((n, half), F32)
    z48 = jnp.zeros((n, ATT_HD - ROT_DIM), F32)
    c = jnp.concatenate([cos, cos, one], axis=-1)
    sa = jnp.concatenate([-sin, z8, z48], axis=-1)
    sb = jnp.concatenate([z8, sin, z48], axis=-1)
    rep = width // ATT_HD
    return tuple(jnp.tile(t, (1, rep)) for t in (c, sa, sb))


def _rotary(x, c, sa, sb):
    w = x.shape[-1]
    half = ROT_DIM // 2
    return x * c + pltpu.roll(x, w - half, 1) * sa + pltpu.roll(x, half, 1) * sb


def _attn_prompt_body(q_ref, k_ref, v_ref, g_ref, c_ref, sa_ref, sb_ref,
                      att_ref, ko_ref, vo_ref,
                      nat, qd, kd, vd, od, std, o_s, st_s, *, seq):
    nblk = seq // CHUNK
    c, sa, sb = c_ref[...], sa_ref[...], sb_ref[...]
    q = _rotary(q_ref[...], c, sa, sb) * (ATT_HD ** -0.5)
    k = _rotary(k_ref[...], c, sa, sb)
    ko_ref[...] = k.T
    vo_ref[...] = v_ref[...].T
    head_lane = lax.broadcasted_iota(jnp.int32, (1, LANE), 1) // ATT_HD
    nat[0] = q
    nat[1] = k
    nat[2] = v_ref[...]
    zero = jnp.zeros((CHUNK, LANE), BF16)
    for p, d in enumerate(DILATIONS):
        kd[p, 0:CHUNK, :] = zero
        vd[p, 0:CHUNK, :] = zero
        ln = seq // d
        for r in range(d):
            rows = slice(r * ln, (r + 1) * ln)
            krows = slice(CHUNK + r * ln, CHUNK + (r + 1) * ln)
            src = pl.ds(r, ln, stride=d) if d > 1 else slice(None)
            qr = nat[0, src, :]
            qd[p, 0, rows, :] = jnp.where(head_lane == 0, qr, 0.0).astype(BF16)
            qd[p, 1, rows, :] = jnp.where(head_lane == 1, qr, 0.0).astype(BF16)
            kd[p, krows, :] = nat[1, src, :].astype(BF16)
            vd[p, krows, :] = nat[2, src, :].astype(BF16)

    row = lax.broadcasted_iota(jnp.int32, (CHUNK, 2 * CHUNK), 0)
    col = lax.broadcasted_iota(jnp.int32, (CHUNK, 2 * CHUNK), 1)
    band = (col >= row) & (col <= row + CHUNK)
    first_head = lax.broadcasted_iota(jnp.int32, (CHUNK, LANE), 1) < ATT_HD

    for p, d in enumerate(DILATIONS):
        nb = nblk // d

        def block(t, carry, p=p, nb=nb):
            base = pl.multiple_of(t * CHUNK, CHUNK)
            rows = pl.ds(base, CHUNK)
            first = (t % nb) == 0
            valid = band & (col >= jnp.where(first, CHUNK, 0))
            kw = kd[p, pl.ds(base, 2 * CHUNK), :]
            vw = vd[p, pl.ds(base, 2 * CHUNK), :]
            parts = []
            for hh in range(2):
                s = lax.dot_general(qd[p, hh, rows, :], kw, (((1,), (1,)), ((), ())),
                                    preferred_element_type=F32)
                s = jnp.where(valid, s, NEG_INF)
                m = jnp.max(s, axis=-1, keepdims=True)
                e = jnp.exp(s - m)
                o = jnp.dot(e.astype(BF16), vw, preferred_element_type=F32)
                parts.append((o, m, jnp.sum(e, axis=-1, keepdims=True)))
            (o0, m0, l0), (o1, m1, l1) = parts
            wide = lambda t: jnp.broadcast_to(t, (CHUNK, LANE))
            o_dst, st_dst = (o_s.at[p], st_s.at[p]) if nb == nblk else (od, std)
            o_dst[rows, :] = jnp.where(first_head, o0, o1)
            st_dst[0, rows, :] = jnp.where(first_head, wide(m0), wide(m1))
            st_dst[1, rows, :] = jnp.where(first_head, wide(l0), wide(l1))
            return carry

        lax.fori_loop(0, nblk, block, 0, unroll=True)
        if d > 1:
            ln = seq // d
            for r in range(d):
                o_s[p, pl.ds(r, ln, stride=d), :] = od[r * ln:(r + 1) * ln, :]
                st_s[p, 0, pl.ds(r, ln, stride=d), :] = std[0, r * ln:(r + 1) * ln, :]
                st_s[p, 1, pl.ds(r, ln, stride=d), :] = std[1, r * ln:(r + 1) * ln, :]

    np_ = len(DILATIONS)

    def combine(t, carry):
        base = pl.multiple_of(t * CHUNK, CHUNK)
        rows = pl.ds(base, CHUNK)
        ms = [st_s[p, 0, rows, :] for p in range(np_)]
        mx = functools.reduce(jnp.maximum, ms)
        ws = [jnp.exp(mm - mx) for mm in ms]
        num = sum(w * o_s[p, rows, :] for p, w in enumerate(ws))
        den = sum(w * st_s[p, 1, rows, :] for p, w in enumerate(ws))
        att_ref[rows, :] = ((num / den) * _silu(g_ref[rows, :])).astype(att_ref.dtype)
        return carry

    lax.fori_loop(0, nblk, combine, 0, unroll=2)


def _attn_prompt(u, cos, sa, sb, *, batch, seq):
    m = batch * seq
    npair = ATT_W // LANE
    blk = lambda off: pl.BlockSpec((seq, LANE), lambda b, hp, off=off: (b, off + hp))
    tab = pl.BlockSpec((seq, LANE), lambda b, hp: (0, 0))
    out = pl.BlockSpec((seq, LANE), lambda b, hp: (b, hp))
    out_t = pl.BlockSpec((None, LANE, seq), lambda b, hp: (b, hp, 0))
    np_ = len(DILATIONS)
    return pl.pallas_call(
        functools.partial(_attn_prompt_body, seq=seq),
        grid=(batch, npair),
        in_specs=[blk(0), blk(npair), blk(2 * npair), blk(3 * npair), tab, tab, tab],
        out_specs=[out, out_t, out_t],
        out_shape=[jax.ShapeDtypeStruct((m, ATT_W), BF16),
                   jax.ShapeDtypeStruct((batch, ATT_W, seq), F32),
                   jax.ShapeDtypeStruct((batch, ATT_W, seq), F32)],
        scratch_shapes=[pltpu.VMEM((3, seq, LANE), F32),
                        pltpu.VMEM((np_, 2, seq, LANE), BF16),
                        pltpu.VMEM((np_, seq + CHUNK, LANE), BF16),
                        pltpu.VMEM((np_, seq + CHUNK, LANE), BF16),
                        pltpu.VMEM((seq, LANE), F32),
                        pltpu.VMEM((2, seq, LANE), F32),
                        pltpu.VMEM((np_, seq, LANE), F32),
                        pltpu.VMEM((np_, 2, seq, LANE), F32)],
        compiler_params=_params("arbitrary", "arbitrary"),
        name="attn_prompt")(u, u, u, u, cos, sa, sb)


def _ssd_prompt_body(xbc_ref, z_ref, dt_ref, cw_ref, cb_ref, dtb_ref, alog_ref, dsk_ref, nw_ref,
                     y_ref, conv_ref, st_ref, ext, st, ys, *, heads, nchunk):
    c = pl.program_id(1)
    inner = heads * SSD_HEADDIM
    gw = SSD_STATE
    hpg = heads // SSD_GROUPS

    @pl.when(c == 0)
    def _():
        ext[0:SUBLANE, :] = jnp.zeros((SUBLANE, ext.shape[1]), F32)
        st[...] = jnp.zeros(st.shape, F32)

    ext[SUBLANE:SUBLANE + CHUNK, :] = xbc_ref[...]
    xp = ext[...]
    conv = cb_ref[...] + xp[SUBLANE:, :] * cw_ref[CONV_K - 1:CONV_K, :]
    for j in range(CONV_K - 1):
        conv = conv + pltpu.roll(xp, CONV_K - 1 - j, 0)[SUBLANE:, :] * cw_ref[j:j + 1, :]
    act = _silu(conv)

    @pl.when(c == nchunk - 1)
    def _():
        conv_ref[...] = ext[CHUNK + HIST:CHUNK + SUBLANE, :]

    ext[0:SUBLANE, :] = ext[CHUNK:CHUNK + SUBLANE, :]

    dt = _softplus(dt_ref[...] + dtb_ref[...])
    a = -jnp.exp(alog_ref[...])
    tril = _tril(CHUNK)
    acum = _sel_dot(jnp.where(tril, 1.0, 0.0).astype(BF16), dt * a)
    acum_t = acum.T
    last = acum[CHUNK - 1:CHUNK, :]
    elast = jnp.exp(last)
    expand = _seg_matrix(LANE, inner, SSD_HEADDIM, False)
    dt_e = _dot_sel(dt, expand)
    eacum_e = _dot_sel(jnp.exp(acum), expand)
    wend_e = _dot_sel(jnp.exp(last - acum) * dt, expand)
    xs = act[:, 0:inner]
    xdt = xs * dt_e
    xw = xs * wend_e
    hw = hpg * SSD_HEADDIM
    first_half = lax.broadcasted_iota(jnp.int32, (CHUNK, LANE), 1) < SSD_HEADDIM

    for g in range(SSD_GROUPS):
        bm = act[:, inner + g * gw:inner + (g + 1) * gw]
        cm = act[:, inner + SSD_GROUPS * gw + g * gw:inner + SSD_GROUPS * gw + (g + 1) * gw]
        cb = _dot_nt(cm, bm)
        sg = st[g * hw:(g + 1) * hw, :]
        y_inter = _dot_nt(cm, sg) * eacum_e[:, g * hw:(g + 1) * hw]
        s_local = _dot_tn(xw[:, g * hw:(g + 1) * hw], bm)
        for hg in range(hpg):
            h = g * hpg + hg
            rows = slice(hg * SSD_HEADDIM, (hg + 1) * SSD_HEADDIM)
            st[g * hw + hg * SSD_HEADDIM:g * hw + (hg + 1) * SSD_HEADDIM, :] = (
                sg[rows, :] * elast[:, h:h + 1] + s_local[rows, :])
        for pr in range(hpg // 2):
            lo = g * hw + pr * LANE
            xp = xdt[:, lo:lo + LANE].astype(BF16)
            halves = []
            for hh in range(2):
                h = g * hpg + 2 * pr + hh
                seg = acum[:, h:h + 1] - acum_t[h:h + 1, :]
                mh = cb * jnp.exp(jnp.where(tril, seg, NEG_INF))
                halves.append(jnp.dot(mh.astype(BF16), xp, preferred_element_type=F32))
            ys[:, lo:lo + LANE] = (jnp.where(first_half, halves[0], halves[1])
                                   + y_inter[:, pr * LANE:(pr + 1) * LANE])

    yt = (ys[...] + dsk_ref[...] * xs) * _silu(z_ref[...])
    y_ref[...] = _rms(yt, nw_ref[...]).astype(y_ref.dtype)

    @pl.when(c == nchunk - 1)
    def _():
        st_ref[...] = st[...]


def _ssd_prompt(u, cw, cb, dtb, alog, dsk, nw, *, batch, seq, heads, col_z, col_xbc, col_dt):
    nchunk = seq // CHUNK
    inner = heads * SSD_HEADDIM
    cch = cw.shape[1]
    row = lambda b, c: b * nchunk + c
    full = lambda a: pl.BlockSpec(a.shape, lambda b, c: (0, 0))
    return pl.pallas_call(
        functools.partial(_ssd_prompt_body, heads=heads, nchunk=nchunk),
        grid=(batch, nchunk),
        in_specs=[pl.BlockSpec((CHUNK, cch), lambda b, c: (row(b, c), col_xbc // cch)),
                  pl.BlockSpec((CHUNK, inner), lambda b, c: (row(b, c), col_z // inner)),
                  pl.BlockSpec((CHUNK, LANE), lambda b, c: (row(b, c), col_dt // LANE)),
                  full(cw), full(cb), full(dtb), full(alog), full(dsk), full(nw)],
        out_specs=[pl.BlockSpec((CHUNK, inner), lambda b, c: (row(b, c), 0)),
                   pl.BlockSpec((None, CONV_K - 1, cch), lambda b, c: (b, 0, 0)),
                   pl.BlockSpec((None, inner, SSD_STATE), lambda b, c: (b, 0, 0))],
        out_shape=[jax.ShapeDtypeStruct((batch * seq, inner), BF16),
                   jax.ShapeDtypeStruct((batch, CONV_K - 1, cch), F32),
                   jax.ShapeDtypeStruct((batch, inner, SSD_STATE), F32)],
        scratch_shapes=[pltpu.VMEM((CHUNK + SUBLANE, cch), F32),
                        pltpu.VMEM((inner, SSD_STATE), F32),
                        pltpu.VMEM((CHUNK, inner), F32)],
        compiler_params=_params("arbitrary", "arbitrary"),
        name="ssd_prompt")(u, u, u, cw, cb, dtb, alog, dsk, nw)


def _mlstm_prompt_body(q_ref, k_ref, v_ref, o_ref, z_ref, gi_ref, gf_ref,
                       cw_ref, cb_ref, bi_ref, bf_ref, nw_ref,
                       hz_ref, conv_ref, c_out, n_out, m_out,
                       histq, histk, c_s, n_s, m_s, gate_s, gate_r, *, chunk, nchunk, heads, hd):
    CHUNK = chunk
    c = pl.program_id(1)
    inner = heads * hd

    @pl.when(c == 0)
    def _():
        histq[...] = jnp.zeros((SUBLANE, inner), F32)
        histk[...] = jnp.zeros((SUBLANE, inner), F32)
        c_s[...] = jnp.zeros(c_s.shape, F32)
        n_s[...] = jnp.zeros(n_s.shape, F32)
        m_s[...] = jnp.full(m_s.shape, NEG_INF, F32)

    def conv_head(hist, x_ref, col0, h):
        sl = slice(h * hd, (h + 1) * hd)
        cols = slice(col0 + h * hd, col0 + (h + 1) * hd)
        xp = jnp.concatenate([hist[:, sl], x_ref[:, sl]], axis=0)
        acc = cb_ref[:, cols] + xp[SUBLANE:, :] * cw_ref[CONV_K - 1:CONV_K, cols]
        for j in range(CONV_K - 1):
            acc = acc + pltpu.roll(xp, CONV_K - 1 - j, 0)[SUBLANE:, :] * cw_ref[j:j + 1, cols]
        return _silu(acc)

    @pl.when(c == nchunk - 1)
    def _():
        conv_ref[:, 0:inner] = q_ref[CHUNK - (CONV_K - 1):CHUNK, :]
        conv_ref[:, inner:] = k_ref[CHUNK - (CONV_K - 1):CHUNK, :]

    it = gi_ref[...] + bi_ref[...]
    logf = -_softplus(-(gf_ref[...] + bf_ref[...]))
    tril = _tril(CHUNK)
    bc = _sel_dot(jnp.where(tril, 1.0, 0.0).astype(BF16), logf)
    gate_s[0] = it
    gate_s[1] = bc
    gate_r[0] = it.T
    gate_r[1] = bc.T

    for h in range(heads):
        sl = slice(h * hd, (h + 1) * hd)
        q = conv_head(histq, q_ref, 0, h)
        k = conv_head(histk, k_ref, inner, h) * (hd ** -0.5)
        v = v_ref[:, sl]
        i_col, b_col = gate_s[0, :, h:h + 1], gate_s[1, :, h:h + 1]
        i_row, b_row = gate_r[0, h:h + 1, :], gate_r[1, h:h + 1, :]
        m_prev = m_s[h]
        dmat = jnp.where(tril, b_col - b_row + i_row, NEG_INF)
        inter = b_col + m_prev
        m_t = jnp.maximum(inter, jnp.max(dmat, axis=-1, keepdims=True))
        w_intra = jnp.exp(dmat - m_t)
        w_inter = jnp.exp(inter - m_t)
        att = w_intra * _dot_nt(q, k)
        c_prev = c_s[h]
        n_prev = n_s[h]
        num = _dot(att, v) + w_inter * _dot(q, c_prev)
        qn = _dot_nt(q, jnp.broadcast_to(n_prev, (SUBLANE, hd)))[:, 0:1]
        den = jnp.sum(att, axis=-1, keepdims=True) + w_inter * qn
        hh = num / jnp.maximum(jnp.abs(den), jnp.exp(-m_t))

        b_last = b_col[CHUNK - 1:CHUNK, :]
        logw = b_last - b_col + i_col
        m_new = jnp.maximum(b_last + m_prev, jnp.max(logw, axis=0, keepdims=True))
        ws = jnp.exp(logw - m_new)
        ws_row = jnp.exp(b_last - b_row + i_row - m_new)
        scale = jnp.exp(b_last + m_prev - m_new)
        c_s[h] = scale * c_prev + _dot_tn(k, ws * v)
        n_s[h] = scale * n_prev + _dot(jnp.broadcast_to(ws_row, (SUBLANE, CHUNK)), k)[0:1, :]
        m_s[h] = m_new

        hg = _sigmoid(o_ref[:, sl]) * hh
        mu = jnp.mean(hg, axis=-1, keepdims=True)
        var = jnp.mean(jnp.square(hg - mu), axis=-1, keepdims=True)
        hn = (hg - mu) * lax.rsqrt(var + EPS) * nw_ref[:, sl]
        hz_ref[:, sl] = (hn * _silu(z_ref[:, sl])).astype(hz_ref.dtype)

    histq[...] = q_ref[CHUNK - SUBLANE:CHUNK, :]
    histk[...] = k_ref[CHUNK - SUBLANE:CHUNK, :]

    @pl.when(c == nchunk - 1)
    def _():
        c_out[...] = c_s[...]
        n_out[...] = n_s[...]
        m_out[...] = m_s[...]


def _mlstm_prompt(u, cw, cb, bi, bf, nw, *, batch, seq, heads, hd, col_gates, chunk):
    CHUNK = chunk
    nchunk = seq // CHUNK
    inner = heads * hd
    row = lambda b, c: b * nchunk + c
    ublk = lambda j: pl.BlockSpec((CHUNK, inner), lambda b, c, j=j: (row(b, c), j))
    gblk = lambda off: pl.BlockSpec((CHUNK, LANE), lambda b, c, off=off: (row(b, c), col_gates // LANE + off))
    full = lambda a: pl.BlockSpec(a.shape, lambda b, c: (0, 0))
    return pl.pallas_call(
        functools.partial(_mlstm_prompt_body, chunk=chunk, nchunk=nchunk, heads=heads, hd=hd),
        grid=(batch, nchunk),
        in_specs=[ublk(0), ublk(1), ublk(2), ublk(3), ublk(4), gblk(0), gblk(1),
                  full(cw), full(cb), full(bi), full(bf), full(nw)],
        out_specs=[pl.BlockSpec((CHUNK, inner), lambda b, c: (row(b, c), 0)),
                   pl.BlockSpec((None, CONV_K - 1, 2 * inner), lambda b, c: (b, 0, 0)),
                   pl.BlockSpec((None, heads, hd, hd), lambda b, c: (b, 0, 0, 0)),
                   pl.BlockSpec((None, heads, 1, hd), lambda b, c: (b, 0, 0, 0)),
                   pl.BlockSpec((None, heads, 1, 1), lambda b, c: (b, 0, 0, 0))],
        out_shape=[jax.ShapeDtypeStruct((batch * seq, inner), BF16),
                   jax.ShapeDtypeStruct((batch, CONV_K - 1, 2 * inner), F32),
                   jax.ShapeDtypeStruct((batch, heads, hd, hd), F32),
                   jax.ShapeDtypeStruct((batch, heads, 1, hd), F32),
                   jax.ShapeDtypeStruct((batch, heads, 1, 1), F32)],
        scratch_shapes=[pltpu.VMEM((SUBLANE, inner), F32),
                        pltpu.VMEM((SUBLANE, inner), F32),
                        pltpu.VMEM((heads, hd, hd), F32),
                        pltpu.VMEM((heads, 1, hd), F32),
                        pltpu.VMEM((heads, 1, 1), F32),
                        pltpu.VMEM((2, CHUNK, LANE), F32),
                        pltpu.VMEM((2, LANE, CHUNK), F32)],
        compiler_params=_params("arbitrary", "arbitrary"),
        name="mlstm_prompt")(u, u, u, u, u, u, u, cw, cb, bi, bf, nw)


def _rot_sample_body(q_ref, k_ref, v_ref, g_ref, c_ref, sa_ref, sb_ref, xt_ref):
    c, sa, sb = c_ref[...], sa_ref[...], sb_ref[...]
    q = _rotary(q_ref[...], c, sa, sb) * (ATT_HD ** -0.5)
    k = _rotary(k_ref[...], c, sa, sb)
    xt_ref[...] = jnp.concatenate([q, k, v_ref[...], g_ref[...]], axis=-1).T


def _rot_sample(u, cos, sa, sb):
    bs = u.shape[0]
    ublk = lambda off: pl.BlockSpec((bs, ATT_W), lambda i, off=off: (0, off))
    tab = pl.BlockSpec((1, ATT_W), lambda i: (0, 0))
    return pl.pallas_call(
        _rot_sample_body,
        grid=(1,),
        in_specs=[ublk(0), ublk(1), ublk(2), ublk(3), tab, tab, tab],
        out_specs=pl.BlockSpec((4 * ATT_W, bs), lambda i: (0, 0)),
        out_shape=jax.ShapeDtypeStruct((4 * ATT_W, bs), F32),
        compiler_params=_params("arbitrary"),
        name="rot_sample")(u, u, u, u, cos, sa, sb)


def _key_multiplicity(wb):
    back = wb - lax.broadcasted_iota(jnp.int32, (1, wb), 1)
    cnt = jnp.zeros((1, wb), F32)
    for d in DILATIONS:
        cnt = cnt + ((back <= CHUNK * d) & (lax.rem(back, d) == 0)).astype(F32)
    return cnt


def _attn_sample_body(x_ref, kc_ref, vc_ref, att_ref, *, bb, wb):
    i = pl.program_id(0)
    bs = x_ref.shape[1]
    nt = wb // LANE

    @pl.when(i == 0)
    def _():
        att_ref[...] = jnp.zeros(att_ref.shape, F32)

    cnt = _key_multiplicity(wb)
    valid = cnt > 0.0
    npat = float(len(DILATIONS))
    rid = lax.broadcasted_iota(jnp.int32, (bs, LANE), 0)
    lid = lax.broadcasted_iota(jnp.int32, (ATT_HD, bs), 1)
    for t in range(bb):
        b = i * bb + t
        cols = _dot_sel(x_ref[...], (rid == b).astype(BF16))
        for h in range(ATT_HEADS):
            lo = h * ATT_HD
            qc = cols[lo:lo + ATT_HD]
            kc = cols[ATT_W + lo:ATT_W + lo + ATT_HD]
            vc = cols[2 * ATT_W + lo:2 * ATT_W + lo + ATT_HD]
            gc = cols[3 * ATT_W + lo:3 * ATT_W + lo + ATT_HD]
            kt = kc_ref[t, h]
            vt = vc_ref[t, h]
            s = jnp.sum(kt * jnp.tile(qc, (1, nt)), axis=0, keepdims=True)
            s = jnp.where(valid, s, NEG_INF)
            s_self = jnp.sum(qc * kc, axis=0, keepdims=True)[:, 0:1]
            m = jnp.maximum(jnp.max(s, axis=1, keepdims=True), s_self)
            p = cnt * jnp.exp(s - m)
            p_self = npat * jnp.exp(s_self - m)
            den = jnp.sum(p, axis=1, keepdims=True) + p_self
            acc = vt[:, 0:LANE] * p[:, 0:LANE]
            for j in range(1, nt):
                acc = acc + vt[:, j * LANE:(j + 1) * LANE] * p[:, j * LANE:(j + 1) * LANE]
            o = jnp.sum(acc, axis=1, keepdims=True) + p_self * vc[:, 0:1]
            o = o / den * _silu(gc[:, 0:1])
            att_ref[lo:lo + ATT_HD, :] = jnp.where(lid == b, o, att_ref[lo:lo + ATT_HD, :])


def _attn_sample(xt, ck, cv, *, bb):
    bs, wb = ck.shape[0], ck.shape[3]
    cache = pl.BlockSpec((bb, ATT_HEADS, ATT_HD, wb), lambda i: (i, 0, 0, 0))
    return pl.pallas_call(
        functools.partial(_attn_sample_body, bb=bb, wb=wb),
        grid=(bs // bb,),
        in_specs=[pl.BlockSpec(xt.shape, lambda i: (0, 0)), cache, cache],
        out_specs=pl.BlockSpec((ATT_W, bs), lambda i: (0, 0)),
        out_shape=jax.ShapeDtypeStruct((ATT_W, bs), F32),
        compiler_params=_params("arbitrary"),
        name="attn_sample")(xt, ck, cv)


def _ssd_sample_prep_body(xbc_ref, dt_ref, cst_ref, cw_ref, cb_ref, dtb_ref, alog_ref, aloge_ref, dsk_ref,
                          conv_ref, yloc_ref, xdtt_ref, bc_ref, da_ref, dae_ref, *, heads):
    inner = heads * SSD_HEADDIM
    cch = xbc_ref.shape[1]
    gw = SSD_STATE
    x = xbc_ref[...]
    acc = cb_ref[...] + x * cw_ref[CONV_K - 1:CONV_K, :]
    for j in range(CONV_K - 1):
        acc = acc + cst_ref[:, j * cch:(j + 1) * cch] * cw_ref[j:j + 1, :]
    for j in range(CONV_K - 2):
        conv_ref[:, j * cch:(j + 1) * cch] = cst_ref[:, (j + 1) * cch:(j + 2) * cch]
    conv_ref[:, (CONV_K - 2) * cch:(CONV_K - 1) * cch] = x
    act = _silu(acc)
    xs = act[:, 0:inner]
    bc_ref[...] = act[:, inner:]
    dt = _softplus(dt_ref[...] + dtb_ref[...])
    da_ref[...] = jnp.exp(dt * (-jnp.exp(alog_ref[...])))
    expand = _seg_matrix(LANE, inner, SSD_HEADDIM, False)
    dte = _dot_sel(dt, expand)
    dae_ref[...] = jnp.exp(dte * (-jnp.exp(aloge_ref[...])))
    xdt = xs * dte
    xdtt_ref[...] = xdt.T
    hw = inner // SSD_GROUPS
    parts = []
    for g in range(SSD_GROUPS):
        bm = act[:, inner + g * gw:inner + (g + 1) * gw]
        cm = act[:, inner + SSD_GROUPS * gw + g * gw:inner + SSD_GROUPS * gw + (g + 1) * gw]
        cbg = jnp.sum(cm * bm, axis=-1, keepdims=True)
        parts.append(cbg * xdt[:, g * hw:(g + 1) * hw])
    yloc_ref[...] = jnp.concatenate(parts, axis=-1) + dsk_ref[...] * xs


def _ssd_sample_state_body(da_ref, s_ref, xdtt_ref, bc_ref, so_ref, yi_ref, *, bb, heads):
    i = pl.program_id(0)
    bs = bc_ref.shape[0]
    inner = heads * SSD_HEADDIM
    hw = inner // SSD_GROUPS
    hpg = heads // SSD_GROUPS
    gw = SSD_STATE
    rid = lax.broadcasted_iota(jnp.int32, (bs, gw), 0)
    for t in range(bb):
        b = i * bb + t
        parts = []
        for g in range(SSD_GROUPS):
            mg = jnp.where(rid == b, bc_ref[:, g * gw:(g + 1) * gw], 0.0)
            sl = _dot(xdtt_ref[g * hw:(g + 1) * hw, :], mg)
            crow = _row_at(bc_ref, b, slice(SSD_GROUPS * gw + g * gw, SSD_GROUPS * gw + (g + 1) * gw))
            sg = s_ref[t, g * hw:(g + 1) * hw, :]
            parts.append(_dot_nt(jnp.broadcast_to(crow, (SUBLANE, gw)), sg)[0:1, :])
            for hg in range(hpg):
                h = g * hpg + hg
                lo = hg * SSD_HEADDIM
                so_ref[t, h * SSD_HEADDIM:(h + 1) * SSD_HEADDIM, :] = (
                    sg[lo:lo + SSD_HEADDIM, :] * da_ref[b, h] + sl[lo:lo + SSD_HEADDIM, :])
        yi_ref[t] = jnp.concatenate(parts, axis=-1)


def _ssd_sample_finish_body(yloc_ref, yi_ref, dae_ref, z_ref, nw_ref, y_ref):
    y = (yloc_ref[...] + yi_ref[...] * dae_ref[...]) * _silu(z_ref[...])
    y_ref[...] = _rms(y, nw_ref[...])


def _ssd_sample(u, cst, state, cw, cb, dtb, alog, aloge, dsk, nw, *, heads, col_z, col_xbc, col_dt, bb):
    bs = u.shape[0]
    inner = heads * SSD_HEADDIM
    cch = cw.shape[1]
    ncs = (CONV_K - 1) * cch
    bcw = 2 * SSD_GROUPS * SSD_STATE
    full = lambda a: pl.BlockSpec(a.shape, lambda i: (0,) * a.ndim)
    conv, yloc, xdtt, bc, da, dae = pl.pallas_call(
        functools.partial(_ssd_sample_prep_body, heads=heads),
        grid=(1,),
        in_specs=[pl.BlockSpec((bs, cch), lambda i: (0, col_xbc // cch)),
                  pl.BlockSpec((bs, LANE), lambda i: (0, col_dt // LANE)),
                  full(cst), full(cw), full(cb), full(dtb), full(alog), full(aloge), full(dsk)],
        out_specs=[pl.BlockSpec((bs, ncs), lambda i: (0, 0)),
                   pl.BlockSpec((bs, inner), lambda i: (0, 0)),
                   pl.BlockSpec((inner, bs), lambda i: (0, 0)),
                   pl.BlockSpec((bs, bcw), lambda i: (0, 0)),
                   pl.BlockSpec((bs, LANE), lambda i: (0, 0)),
                   pl.BlockSpec((bs, inner), lambda i: (0, 0))],
        out_shape=[jax.ShapeDtypeStruct((bs, ncs), F32),
                   jax.ShapeDtypeStruct((bs, inner), F32),
                   jax.ShapeDtypeStruct((inner, bs), F32),
                   jax.ShapeDtypeStruct((bs, bcw), F32),
                   jax.ShapeDtypeStruct((bs, LANE), F32),
                   jax.ShapeDtypeStruct((bs, inner), F32)],
        compiler_params=_params("arbitrary"),
        name="ssd_sample_prep")(u, u, cst, cw, cb, dtb, alog, aloge, dsk)
    new_state, yi = pl.pallas_call(
        functools.partial(_ssd_sample_state_body, bb=bb, heads=heads),
        grid=(bs // bb,),
        in_specs=[pl.BlockSpec(memory_space=pltpu.SMEM),
                  pl.BlockSpec((bb, inner, SSD_STATE), lambda i: (i, 0, 0)),
                  full(xdtt), full(bc)],
        out_specs=[pl.BlockSpec((bb, inner, SSD_STATE), lambda i: (i, 0, 0)),
                   pl.BlockSpec((bb, 1, inner), lambda i: (i, 0, 0))],
        out_shape=[jax.ShapeDtypeStruct((bs, inner, SSD_STATE), F32),
                   jax.ShapeDtypeStruct((bs, 1, inner), F32)],
        compiler_params=_params("arbitrary"),
        name="ssd_sample_state")(da[:, :heads], state, xdtt, bc)
    y = pl.pallas_call(
        _ssd_sample_finish_body,
        grid=(1,),
        in_specs=[full(yloc), pl.BlockSpec((bs, inner), lambda i: (0, 0)), full(dae),
                  pl.BlockSpec((bs, inner), lambda i: (0, col_z // inner)), full(nw)],
        out_specs=pl.BlockSpec((bs, inner), lambda i: (0, 0)),
        out_shape=jax.ShapeDtypeStruct((bs, inner), F32),
        compiler_params=_params("arbitrary"),
        name="ssd_sample_finish")(yloc, yi.reshape(bs, inner), dae, u, nw)
    return y, conv, new_state


def _mlstm_sample_prep_body(q_ref, k_ref, gi_ref, gf_ref, cst_ref, cw_ref, cb_ref, bi_ref, bf_ref, m_ref,
                            conv_ref, qk_ref, kwt_ref, wi_ref, wf_ref, mt_ref, *, heads, hd):
    inner = heads * hd
    cch = 2 * inner
    x = jnp.concatenate([q_ref[...], k_ref[...]], axis=-1)
    acc = cb_ref[...] + x * cw_ref[CONV_K - 1:CONV_K, :]
    for j in range(CONV_K - 1):
        acc = acc + cst_ref[:, j * cch:(j + 1) * cch] * cw_ref[j:j + 1, :]
    for j in range(CONV_K - 2):
        conv_ref[:, j * cch:(j + 1) * cch] = cst_ref[:, (j + 1) * cch:(j + 2) * cch]
    conv_ref[:, (CONV_K - 2) * cch:(CONV_K - 1) * cch] = x
    act = _silu(acc)
    q = act[:, 0:inner]
    k = act[:, inner:] * (hd ** -0.5)
    it = gi_ref[...] + bi_ref[...]
    inter = -_softplus(-(gf_ref[...] + bf_ref[...])) + m_ref[...]
    mt = jnp.maximum(inter, it)
    wi = jnp.exp(it - mt)
    wf = jnp.exp(inter - mt)
    wi_ref[...] = wi
    wf_ref[...] = wf
    mt_ref[...] = mt
    qk_ref[:, 0:inner] = q
    qk_ref[:, inner:] = k
    kw = jnp.concatenate([k[:, h * hd:(h + 1) * hd] * wi[:, h:h + 1] for h in range(heads)], axis=-1)
    kwt_ref[...] = kw.T


def _mlstm_sample_state_body(wf_ref, c_ref, qk_ref, v_ref, kwt_ref, co_ref, qc_ref, *, bb, heads, hd):
    i = pl.program_id(0)
    bs = v_ref.shape[0]
    rid = lax.broadcasted_iota(jnp.int32, (bs, hd), 0)
    for t in range(bb):
        b = i * bb + t
        parts = []
        for h in range(heads):
            cp = c_ref[t, h]
            qrow = _row_at(qk_ref, b, slice(h * hd, (h + 1) * hd))
            parts.append(_dot(jnp.broadcast_to(qrow, (SUBLANE, hd)), cp)[0:1, :])
            mh = jnp.where(rid == b, v_ref[:, h * hd:(h + 1) * hd], 0.0)
            co_ref[t, h] = cp * wf_ref[b, h] + _dot(kwt_ref[h * hd:(h + 1) * hd, :], mh)
        qc_ref[t] = jnp.concatenate(parts, axis=-1)


def _mlstm_sample_finish_body(qk_ref, v_ref, o_ref, z_ref, qc_ref, n_ref, wi_ref, wf_ref, mt_ref, nw_ref,
                              hz_ref, no_ref, *, heads, hd):
    inner = heads * hd
    for h in range(heads):
        sl = slice(h * hd, (h + 1) * hd)
        q = qk_ref[:, sl]
        k = qk_ref[:, inner + h * hd:inner + (h + 1) * hd]
        wi = wi_ref[:, h:h + 1]
        wf = wf_ref[:, h:h + 1]
        mt = mt_ref[:, h:h + 1]
        n_prev = n_ref[:, sl]
        att = wi * jnp.sum(q * k, axis=-1, keepdims=True)
        num = att * v_ref[:, sl] + wf * qc_ref[:, sl]
        den = att + wf * jnp.sum(q * n_prev, axis=-1, keepdims=True)
        hh = num / jnp.maximum(jnp.abs(den), jnp.exp(-mt))
        hg = _sigmoid(o_ref[:, sl]) * hh
        mu = jnp.mean(hg, axis=-1, keepdims=True)
        var = jnp.mean(jnp.square(hg - mu), axis=-1, keepdims=True)
        hn = (hg - mu) * lax.rsqrt(var + EPS) * nw_ref[:, sl]
        hz_ref[:, sl] = hn * _silu(z_ref[:, sl])
        no_ref[:, sl] = wf * n_prev + wi * k


def _mlstm_sample(u, cst, c0, n0, m0p, cw, cb, bi, bf, nw, *, heads, hd, col_gates, bb):
    bs = u.shape[0]
    inner = heads * hd
    ncs = (CONV_K - 1) * 2 * inner
    full = lambda a: pl.BlockSpec(a.shape, lambda i: (0,) * a.ndim)
    ucol = lambda j: pl.BlockSpec((bs, inner), lambda i, j=j: (0, j))
    gcol = lambda j: pl.BlockSpec((bs, LANE), lambda i, j=j: (0, col_gates // LANE + j))
    tile = jax.ShapeDtypeStruct((bs, LANE), F32)
    conv, qk, kwt, wi, wf, mt = pl.pallas_call(
        functools.partial(_mlstm_sample_prep_body, heads=heads, hd=hd),
        grid=(1,),
        in_specs=[ucol(0), ucol(1), gcol(0), gcol(1), full(cst), full(cw), full(cb), full(bi), full(bf),
                  full(m0p)],
        out_specs=[pl.BlockSpec((bs, ncs), lambda i: (0, 0)),
                   pl.BlockSpec((bs, 2 * inner), lambda i: (0, 0)),
                   pl.BlockSpec((inner, bs), lambda i: (0, 0)),
                   pl.BlockSpec((bs, LANE), lambda i: (0, 0)),
                   pl.BlockSpec((bs, LANE), lambda i: (0, 0)),
                   pl.BlockSpec((bs, LANE), lambda i: (0, 0))],
        out_shape=[jax.ShapeDtypeStruct((bs, ncs), F32),
                   jax.ShapeDtypeStruct((bs, 2 * inner), F32),
                   jax.ShapeDtypeStruct((inner, bs), F32),
                   tile, tile, tile],
        compiler_params=_params("arbitrary"),
        name="mlstm_sample_prep")(u, u, u, u, cst, cw, cb, bi, bf, m0p)
    c_new, qc = pl.pallas_call(
        functools.partial(_mlstm_sample_state_body, bb=bb, heads=heads, hd=hd),
        grid=(bs // bb,),
        in_specs=[pl.BlockSpec(memory_space=pltpu.SMEM),
                  pl.BlockSpec((bb, heads, hd, hd), lambda i: (i, 0, 0, 0)),
                  full(qk), pl.BlockSpec((bs, inner), lambda i: (0, 2)), full(kwt)],
        out_specs=[pl.BlockSpec((bb, heads, hd, hd), lambda i: (i, 0, 0, 0)),
                   pl.BlockSpec((bb, 1, inner), lambda i: (i, 0, 0))],
        out_shape=[jax.ShapeDtypeStruct((bs, heads, hd, hd), F32),
                   jax.ShapeDtypeStruct((bs, 1, inner), F32)],
        compiler_params=_params("arbitrary"),
        name="mlstm_sample_state")(wf[:, :heads], c0, qk, u, kwt)
    hz, n_new = pl.pallas_call(
        functools.partial(_mlstm_sample_finish_body, heads=heads, hd=hd),
        grid=(1,),
        in_specs=[full(qk), ucol(2), ucol(3), ucol(4), pl.BlockSpec((bs, inner), lambda i: (0, 0)),
                  full(n0), full(wi), full(wf), full(mt), full(nw)],
        out_specs=[pl.BlockSpec((bs, inner), lambda i: (0, 0)),
                   pl.BlockSpec((bs, inner), lambda i: (0, 0))],
        out_shape=[jax.ShapeDtypeStruct((bs, inner), F32)] * 2,
        compiler_params=_params("arbitrary"),
        name="mlstm_sample_finish")(qk, u, u, u, qc.reshape(bs, inner), n0, wi, wf, mt, nw)
    return hz, conv, c_new, n_new, mt[:, :heads]


def _pad_cols(w, n):
    return jnp.pad(w, ((0, 0), (0, n - w.shape[1])))


def _row(v, n=None):
    v = v.reshape(1, -1)
    return v if n is None else _pad_cols(v, n)


def kernel(x_prompt, x_sample, cache_attn_k, cache_attn_v, state_ssd_conv, state_ssd, state_mlstm_conv, state_mlstm_c, state_mlstm_n, state_mlstm_m, norm_w, final_norm_w, w_in_even, w_out_even, ssd_conv_w, ssd_conv_b, ssd_dt_bias, ssd_a_log, ssd_d, ssd_norm_w, w_in_odd, w_out_odd, mlstm_conv_w, mlstm_conv_b, mlstm_igate_b, mlstm_fgate_b, mlstm_norm_w):
    batch, seq, d_model = x_prompt.shape
    bs = x_sample.shape[0]
    ssd_heads = ssd_a_log.shape[1]
    ssd_inner = ssd_heads * SSD_HEADDIM
    ssd_cch = ssd_conv_w.shape[2]
    m_inner = mlstm_norm_w.shape[1]
    m_hd = m_inner // MLSTM_HEADS
    mp = batch * seq

    col_z = 4 * ATT_W
    col_xbc = col_z + ssd_inner
    col_dt = col_xbc + ssd_cch
    n_even = col_dt + LANE
    w_in0 = _pad_cols(w_in_even[0], n_even).astype(BF16)
    w_out0 = w_out_even[0].astype(BF16)
    nw0 = _row(norm_w[0])
    cw0, cb0 = ssd_conv_w[0], _row(ssd_conv_b[0])
    dtb = _row(ssd_dt_bias[0], LANE)
    alog = _row(ssd_a_log[0], LANE)
    aloge = _row(jnp.repeat(ssd_a_log[0], SSD_HEADDIM))
    dsk = _row(jnp.repeat(ssd_d[0], SSD_HEADDIM))
    snw = _row(ssd_norm_w[0])

    hp = x_prompt.reshape(mp, d_model)
    hs = x_sample.reshape(bs, d_model)

    up = _norm_matmul(hp, nw0, w_in0, tm=512, panels=1)
    cos_p, sa_p, sb_p = _rope_tables(jnp.arange(seq), LANE)
    att_p, k_p, v_p = _attn_prompt(up, cos_p, sa_p, sb_p, batch=batch, seq=seq)
    y_p, conv_p, st_p = _ssd_prompt(up, cw0, cb0, dtb, alog, dsk, snw, batch=batch, seq=seq,
                                    heads=ssd_heads, col_z=col_z, col_xbc=col_xbc, col_dt=col_dt)
    hp = _out_proj(hp, [att_p, y_p], [w_out0[:ATT_W], w_out0[ATT_W:]], tm=1024)

    us = _norm_matmul(hs, nw0, w_in0, tm=bs, panels=1)
    cos_s, sa_s, sb_s = _rope_tables(PAST_LEN + jnp.arange(1), ATT_W)
    xt_s = _rot_sample(us, cos_s, sa_s, sb_s)
    k_s = xt_s[ATT_W:2 * ATT_W].T
    v_s = xt_s[2 * ATT_W:3 * ATT_W].T
    pos_minor = lambda cache: jnp.transpose(cache, (0, 2, 3, 1))
    att_s = _attn_sample(xt_s, pos_minor(cache_attn_k[0]), pos_minor(cache_attn_v[0]), bb=2).T
    y_s, conv_s, st_s = _ssd_sample(us, state_ssd_conv[0].reshape(bs, -1),
                                    state_ssd[0].reshape(bs, ssd_inner, SSD_STATE),
                                    cw0, cb0, dtb, alog, aloge, dsk, snw, heads=ssd_heads,
                                    col_z=col_z, col_xbc=col_xbc, col_dt=col_dt, bb=8)
    hs = _out_proj(hs, [att_s, y_s], [w_out0[:ATT_W], w_out0[ATT_W:]], tm=bs)

    wo = w_in_odd[0]
    gates_at = 4 * m_inner
    zcol = gates_at + 2 * MLSTM_HEADS
    col_gates = 5 * m_inner
    zpad = jnp.zeros((d_model, LANE - MLSTM_HEADS), wo.dtype)
    w_in1 = jnp.concatenate([wo[:, :gates_at], wo[:, zcol:],
                             wo[:, gates_at:gates_at + MLSTM_HEADS], zpad,
                             wo[:, gates_at + MLSTM_HEADS:zcol], zpad], axis=1).astype(BF16)
    w_out1 = w_out_odd[0].astype(BF16)
    nw1 = _row(norm_w[1])
    cw1, cb1 = mlstm_conv_w[0], _row(mlstm_conv_b[0])
    bi = _row(mlstm_igate_b[0], LANE)
    bf = _row(mlstm_fgate_b[0], LANE)
    mnw = _row(mlstm_norm_w[0])
    fnw = _row(final_norm_w)

    up1 = _norm_matmul(hp, nw1, w_in1, tm=512, panels=2)
    hz_p, mconv_p, c_p, n_p, m_p = _mlstm_prompt(up1, cw1, cb1, bi, bf, mnw, batch=batch, seq=seq,
                                                   heads=MLSTM_HEADS, hd=m_hd, col_gates=col_gates,
                                                   chunk=2 * CHUNK)
    y_prompt = _out_proj(hp, [hz_p], [w_out1], fnw, tm=1024)

    us1 = _norm_matmul(hs, nw1, w_in1, tm=bs, panels=2)
    m0p = _pad_cols(state_mlstm_m[0], LANE)
    hz_s, mconv_s, c_s, n_s, m_s = _mlstm_sample(us1, state_mlstm_conv[0].reshape(bs, -1),
                                                 state_mlstm_c[0], state_mlstm_n[0].reshape(bs, m_inner),
                                                 m0p, cw1, cb1, bi, bf, mnw,
                                                 heads=MLSTM_HEADS, hd=m_hd, col_gates=col_gates, bb=4)
    y_sample = _out_proj(hs, [hz_s], [w_out1], fnw, tm=bs)

    tmax = min(seq, CHUNK * max(DILATIONS))
    kv_out = lambda t: jnp.transpose(t.reshape(batch, ATT_HEADS, ATT_HD, seq), (0, 3, 1, 2))[None, :, seq - tmax:]
    return (
        y_prompt.reshape(batch, seq, d_model),
        y_sample.reshape(bs, 1, d_model),
        kv_out(k_p),
        kv_out(v_p),
        conv_p[None],
        st_p.reshape(1, batch, ssd_heads, SSD_HEADDIM, SSD_STATE),
        mconv_p[None],
        c_p[None],
        n_p.reshape(1, batch, MLSTM_HEADS, m_hd),
        m_p.reshape(1, batch, MLSTM_HEADS),
        k_s.reshape(1, bs, 1, ATT_HEADS, ATT_HD),
        v_s.reshape(1, bs, 1, ATT_HEADS, ATT_HD),
        conv_s.reshape(1, bs, CONV_K - 1, ssd_cch),
        st_s.reshape(1, bs, ssd_heads, SSD_HEADDIM, SSD_STATE),
        mconv_s.reshape(1, bs, CONV_K - 1, 2 * m_inner),
        c_s[None],
        n_s.reshape(1, bs, MLSTM_HEADS, m_hd),
        m_s.reshape(1, bs, MLSTM_HEADS),
    )
```
